```python
import math
import jax, jax.numpy as jnp
from jax import lax
import numpy as np

D_MODEL = 2048
BATCH = 2
SEQ = 4096
DEPTH = 1

GDN_HEADS = 8
GDN_DK = 128
GDN_DV = 128
GDN_CONV = 5
GDN_CHUNK = 64
DIFF_HEADS = 8
DIFF_DH = 64
DIFF_DV = 2 * DIFF_DH
Q_BLOCK = 128
N_GROUPS = 4
EXPERTS_PER_GROUP = 8
N_EXPERTS = N_GROUPS * EXPERTS_PER_GROUP
TOP_K = 2
D_EXPERT = 512
MOE_TOKEN_BLOCK = 1024
EPS = 1e-6

GDN_QK = GDN_HEADS * GDN_DK
GDN_V = GDN_HEADS * GDN_DV
DIFF_QK = DIFF_HEADS * 2 * DIFF_DH
DIFF_V = DIFF_HEADS * DIFF_DV
SPLITS = [GDN_QK, GDN_QK, GDN_V, GDN_V, 2 * GDN_HEADS, 2 * GDN_HEADS,
          DIFF_QK, DIFF_QK, DIFF_V, D_MODEL, D_MODEL]
D_IN = sum(SPLITS)

kernel_name = "hybrid_gdn_diffattn_hiermoe_encoder"

F32 = jnp.float32


def rms_norm(x, g):
    xf = x.astype(F32)
    y = xf * lax.rsqrt(jnp.mean(xf * xf, axis=-1, keepdims=True) + EPS)
    return (y * g.astype(F32)).astype(x.dtype)


def l2norm(t):
    return t * lax.rsqrt(jnp.sum(t * t, axis=-1, keepdims=True) + EPS)


def centred_depthwise_conv(x, w):
    K, C = w.shape
    pad = K // 2
    return lax.conv_general_dilated(
        x, w[:, None, :].astype(x.dtype), window_strides=(1,), padding=[(pad, pad)],
        dimension_numbers=("NWC", "WIO", "NWC"), feature_group_count=C)


def gated_delta_chunked(q, k, v, g, beta):
    B_, H, S, dk = q.shape
    dv = v.shape[-1]
    C = GDN_CHUNK
    N = S // C
    q = q.reshape(B_, H, N, C, dk)
    k = k.reshape(B_, H, N, C, dk)
    v = v.reshape(B_, H, N, C, dv)
    g_cum = jnp.cumsum(g.reshape(B_, H, N, C), axis=-1)
    beta = beta.reshape(B_, H, N, C)
    lower = jnp.tril(jnp.ones((C, C), bool))
    strict = jnp.tril(jnp.ones((C, C), bool), -1)
    decay = jnp.exp(jnp.where(lower, g_cum[..., :, None] - g_cum[..., None, :], -jnp.inf))
    k_beta = k * beta[..., None]
    L = jnp.where(strict, jnp.einsum("bhncd,bhnsd->bhncs", k_beta, k) * decay, 0.0)
    A = L + jnp.eye(C, dtype=F32)
    u = lax.linalg.triangular_solve(A, v * beta[..., None], left_side=True, lower=True, unit_diagonal=True)
    w = lax.linalg.triangular_solve(A, k_beta * jnp.exp(g_cum)[..., None], left_side=True, lower=True,
                                    unit_diagonal=True)
    attn = jnp.einsum("bhncd,bhnsd->bhncs", q, k) * decay

    def step(state, inp):
        q_c, k_c, u_c, w_c, attn_c, gc = inp
        v_new = u_c - jnp.einsum("bhcd,bhde->bhce", w_c, state)
        o = (jnp.einsum("bhcd,bhde->bhce", q_c * jnp.exp(gc)[..., None], state)
             + jnp.einsum("bhcs,bhse->bhce", attn_c, v_new))
        g_last = gc[..., -1]
        state = (state * jnp.exp(g_last)[..., None, None]
                 + jnp.einsum("bhcd,bhce->bhde", k_c * jnp.exp(g_last[..., None] - gc)[..., None], v_new))
        return state, o

    xs = tuple(jnp.moveaxis(t, 2, 0) for t in (q, k, u, w, attn, g_cum))
    state0 = jnp.zeros((B_, H, dk, dv), F32)
    _, o = lax.scan(step, state0, xs)
    return jnp.moveaxis(o, 0, 2).reshape(B_, H, S, dv)


def diff_attention(q, k, v, lam):
    B_, H, _, S, dh = q.shape
    dv = v.shape[-1]
    nb = S // Q_BLOCK
    scale = dh ** -0.5
    slopes = jnp.exp2(-8.0 * (jnp.arange(H, dtype=F32) + 1.0) / H)
    kpos = jnp.arange(S)
    qb = jnp.moveaxis(q.reshape(B_, H, 2, nb, Q_BLOCK, dh), 3, 0)

    def block(args):
        qi, start = args
        s = jnp.einsum("bhmqd,bhmkd->bhmqk", qi, k).astype(F32) * scale
        qpos = start + jnp.arange(Q_BLOCK)
        dist = jnp.abs(qpos[:, None] - kpos[None, :]).astype(F32)
        s = s - slopes[None, :, None, None, None] * dist
        p = jax.nn.softmax(s, axis=-1)
        pd = p[:, :, 0] - lam * p[:, :, 1]
        return jnp.einsum("bhqk,bhkd->bhqd", pd.astype(v.dtype), v)

    out = lax.map(block, (qb, jnp.arange(nb) * Q_BLOCK))
    return out.transpose(1, 2, 0, 3, 4).reshape(B_, H, S, dv)


def token_mixers(h, layer, w_in, conv_w, a_log, dt_bias, gdn_norm_w, diff_lambda, diff_norm_w,
                 w_branch_a, w_branch_d, w_out):
    B_, S, _ = h.shape
    p = h @ w_in
    idx = np.cumsum(SPLITS)[:-1].tolist()
    q_a, k_a, v_a, z_a, a_a, b_a, q_d, k_d, v_d, gate_a, gate_d = jnp.split(p, idx, axis=-1)

    qkv = jax.nn.silu(centred_depthwise_conv(jnp.concatenate([q_a, k_a, v_a], axis=-1), conv_w))
    q_a, k_a, v_a = jnp.split(qkv.astype(F32), [GDN_QK, 2 * GDN_QK], axis=-1)
    to_heads = lambda t, d: t.reshape(B_, S, GDN_HEADS, d).transpose(0, 2, 1, 3)
    q_a = l2norm(to_heads(q_a, GDN_DK)) * (GDN_DK ** -0.5)
    k_a = l2norm(to_heads(k_a, GDN_DK))
    v_a = to_heads(v_a, GDN_DV)
    a_a = a_a.astype(F32).reshape(B_, S, 2, GDN_HEADS).transpose(2, 0, 3, 1)
    b_a = b_a.astype(F32).reshape(B_, S, 2, GDN_HEADS).transpose(2, 0, 3, 1)
    g = -jnp.exp(a_log.astype(F32))[:, None, :, None] * jax.nn.softplus(
        a_a + dt_bias.astype(F32)[:, None, :, None])
    beta = jax.nn.sigmoid(b_a)
    flip = lambda t: jnp.flip(t, axis=2)
    o_fwd = gated_delta_chunked(q_a, k_a, v_a, g[0], beta[0])
    o_bwd = flip(gated_delta_chunked(flip(q_a), flip(k_a), flip(v_a), flip(g[1]), flip(beta[1])))
    o = (o_fwd + o_bwd).transpose(0, 2, 1, 3)
    z = z_a.astype(F32).reshape(B_, S, GDN_HEADS, GDN_DV)
    y_a = (rms_norm(o, gdn_norm_w) * jax.nn.silu(z)).reshape(B_, S, GDN_V).astype(h.dtype)

    lam_init = 0.8 - 0.6 * math.exp(-0.3 * layer)
    lf = diff_lambda.astype(F32)
    lam = jnp.exp(jnp.sum(lf[0] * lf[1])) - jnp.exp(jnp.sum(lf[2] * lf[3])) + lam_init
    qd = q_d.reshape(B_, S, DIFF_HEADS, 2, DIFF_DH).transpose(0, 2, 3, 1, 4)
    kd = k_d.reshape(B_, S, DIFF_HEADS, 2, DIFF_DH).transpose(0, 2, 3, 1, 4)
    vd = v_d.reshape(B_, S, DIFF_HEADS, DIFF_DV).transpose(0, 2, 1, 3)
    od = diff_attention(qd, kd, vd, lam).transpose(0, 2, 1, 3)
    y_d = (rms_norm(od, diff_norm_w) * (1.0 - lam_init)).reshape(B_, S, DIFF_V).astype(h.dtype)

    merged = (jax.nn.sigmoid(gate_a) * (y_a @ w_branch_a)
              + jax.nn.sigmoid(gate_d) * (y_d @ w_branch_d))
    return merged @ w_out


def hier_moe(h, w_group, b_group, w_router, b_router, w_gate, w_up, w_down):
    B_, S, D = h.shape
    t = h.reshape(-1, D)
    T = t.shape[0]
    g_logits = (t @ w_group).astype(F32) + b_group.astype(F32)
    g_probs = jax.nn.softmax(g_logits, axis=-1)
    g_idx = jnp.argmax(g_logits, axis=-1)
    g_w = jnp.take_along_axis(g_probs, g_idx[:, None], axis=-1)
    e_logits = ((t @ w_router).astype(F32) + b_router.astype(F32)).reshape(T, N_GROUPS, EXPERTS_PER_GROUP)
    e_sel = jnp.take_along_axis(e_logits, g_idx[:, None, None], axis=1)[:, 0]
    top_v, top_i = lax.top_k(jax.nn.softmax(e_sel, axis=-1), TOP_K)
    weights = top_v / jnp.sum(top_v, axis=-1, keepdims=True) * g_w
    expert_id = g_idx[:, None] * EXPERTS_PER_GROUP + top_i
    combine = jnp.sum(jax.nn.one_hot(expert_id, N_EXPERTS, dtype=F32) * weights[..., None], axis=1)

    blk = math.gcd(T, MOE_TOKEN_BLOCK)
    nb = T // blk

    def expert_block(args):
        tb, cb = args
        hid = jax.nn.silu(jnp.einsum("td,edf->tef", tb, w_gate)) * jnp.einsum("td,edf->tef", tb, w_up)
        return jnp.einsum("tef,efd->td", hid * cb[..., None].astype(hid.dtype), w_down)

    y = lax.map(expert_block, (t.reshape(nb, blk, D), combine.reshape(nb, blk, N_EXPERTS)))
    return y.reshape(B_, S, D)


def setup_inputs(seed: int = 0) -> dict:
    key = jax.random.key(seed)
    ks = jax.random.split(key, 24)
    nrm = lambda k, shape, scale: jax.random.normal(k, shape, F32) * scale
    gain = lambda k, shape: 1.0 + 0.01 * jax.random.normal(k, shape, F32)
    dt = jnp.exp(jax.random.uniform(ks[5], (DEPTH, 2, GDN_HEADS), F32, math.log(1e-3), math.log(1e-1)))
    return {
        "x": nrm(ks[0], (BATCH, SEQ, D_MODEL), 1.0),
        "g_mix": gain(ks[1], (DEPTH, D_MODEL)),
        "w_in": nrm(ks[2], (DEPTH, D_MODEL, D_IN), D_MODEL ** -0.5),
        "conv_w": nrm(ks[3], (DEPTH, GDN_CONV, 2 * GDN_QK + GDN_V), GDN_CONV ** -0.5),
        "a_log": jnp.log(jax.random.uniform(ks[4], (DEPTH, 2, GDN_HEADS), F32, 1.0, 16.0)),
        "dt_bias": dt + jnp.log(-jnp.expm1(-dt)),
        "gdn_norm_w": gain(ks[6], (DEPTH, GDN_DV)),
        "diff_lambda": nrm(ks[7], (DEPTH, 4, DIFF_DH), 0.1),
        "diff_norm_w": gain(ks[8], (DEPTH, DIFF_DV)),
        "w_branch_a": nrm(ks[9], (DEPTH, GDN_V, D_MODEL), GDN_V ** -0.5),
        "w_branch_d": nrm(ks[10], (DEPTH, DIFF_V, D_MODEL), DIFF_V ** -0.5),
        "w_out": nrm(ks[11], (DEPTH, D_MODEL, D_MODEL), D_MODEL ** -0.5),
        "g_ffn": gain(ks[12], (DEPTH, D_MODEL)),
        "w_group": nrm(ks[13], (DEPTH, D_MODEL, N_GROUPS), D_MODEL ** -0.5),
        "b_group": nrm(ks[14], (DEPTH, N_GROUPS), 0.01),
        "w_router": nrm(ks[15], (DEPTH, D_MODEL, N_EXPERTS), D_MODEL ** -0.5),
        "b_router": nrm(ks[16], (DEPTH, N_EXPERTS), 0.01),
        "w_exp_gate": nrm(ks[17], (DEPTH, N_EXPERTS, D_MODEL, D_EXPERT), D_MODEL ** -0.5),
        "w_exp_up": nrm(ks[18], (DEPTH, N_EXPERTS, D_MODEL, D_EXPERT), D_MODEL ** -0.5),
        "w_exp_down": nrm(ks[19], (DEPTH, N_EXPERTS, D_EXPERT, D_MODEL), D_EXPERT ** -0.5),
        "g_final": gain(ks[20], (D_MODEL,)),
    }


def reference(x, g_mix, w_in, conv_w, a_log, dt_bias, gdn_norm_w, diff_lambda, diff_norm_w,
              w_branch_a, w_branch_d, w_out, g_ffn, w_group, b_group, w_router, b_router,
              w_exp_gate, w_exp_up, w_exp_down, g_final):
    for layer in range(DEPTH):
        h = rms_norm(x, g_mix[layer])
        x = x + token_mixers(h, layer, w_in[layer], conv_w[layer], a_log[layer], dt_bias[layer],
                             gdn_norm_w[layer], diff_lambda[layer], diff_norm_w[layer],
                             w_branch_a[layer], w_branch_d[layer], w_out[layer])
        h = rms_norm(x, g_ffn[layer])
        x = x + hier_moe(h, w_group[layer], b_group[layer], w_router[layer], b_router[layer],
                         w_exp_gate[layer], w_exp_up[layer], w_exp_down[layer])
    return rms_norm(x, g_final)
```

```python
import functools
import math

import jax
import jax.numpy as jnp
from jax import lax
from jax.experimental import pallas as pl
from jax.experimental.pallas import tpu as pltpu

F32 = jnp.float32
BF16 = jnp.bfloat16
EPS = 1e-6
LANE = 128
GDN_DK = 128
GDN_DV = 128
GDN_CONV = 5
GDN_CHUNK = 64
DIFF_DH = 64
DIFF_DV = 2 * DIFF_DH
TOP_K = 2
LOG2E = 1.4426950408889634
NEG_BIG = -1e30
VMEM_LIMIT_BYTES = 56 * 1024 * 1024

_HI = lax.Precision.HIGHEST


def _cparams(*sem):
    return pltpu.CompilerParams(dimension_semantics=sem, vmem_limit_bytes=VMEM_LIMIT_BYTES)


def _const_spec(shape):
    nd = len(shape)
    return pl.BlockSpec(shape, lambda *_: (0,) * nd, pipeline_mode=pl.Buffered(1))


def _mm(a, b):
    return jnp.dot(a.astype(BF16), b.astype(BF16), preferred_element_type=F32)


def _split3(x):
    hi = x.astype(BF16)
    r = x - hi.astype(F32)
    mid = r.astype(BF16)
    lo = (r - mid.astype(F32)).astype(BF16)
    return hi, mid, lo


def _softplus(x):
    return jnp.maximum(x, 0.0) + jnp.log(1.0 + jnp.exp(-jnp.abs(x)))


def _sigmoid(x):
    return 1.0 / (1.0 + jnp.exp(-x))


def _norm_proj_kernel(x_ref, g_ref, wab_ref, alog_ref, dtb_ref, h_ref, gc_ref, beta_ref, *, n_dir_heads):
    x = x_ref[...]
    h = x * lax.rsqrt(jnp.mean(x * x, axis=-1, keepdims=True) + EPS) * g_ref[...]
    hb = h.astype(BF16)
    h_ref[...] = hb
    ab = jnp.dot(hb, wab_ref[...], preferred_element_type=F32)
    g = -jnp.exp(alog_ref[...]) * _softplus(ab + dtb_ref[...])
    beta_ref[...] = _sigmoid(ab)
    tm = x.shape[0]
    row = lax.broadcasted_iota(jnp.int32, (tm, tm), 0)
    col = lax.broadcasted_iota(jnp.int32, (tm, tm), 1)
    same = (row // GDN_CHUNK) == (col // GDN_CHUNK)
    prefix = jnp.where(same & (col <= row), 1.0, 0.0).astype(BF16)
    suffix = jnp.where(same & (col >= row), 1.0, 0.0).astype(BF16)
    pieces = _split3(g)
    cs_f = sum(jnp.dot(prefix, p, preferred_element_type=F32) for p in pieces)
    cs_b = sum(jnp.dot(suffix, p, preferred_element_type=F32) for p in pieces)
    lane = lax.broadcasted_iota(jnp.int32, g.shape, 1)
    gc_ref[...] = jnp.where(lane < n_dir_heads // 2, cs_f, cs_b)


def _norm_proj(x2, g_mix, w_ab, alog_pad, dtb_pad, n_dir_heads, tm):
    T, D = x2.shape
    return pl.pallas_call(
        functools.partial(_norm_proj_kernel, n_dir_heads=n_dir_heads),
        grid=(T // tm,),
        in_specs=[pl.BlockSpec((tm, D), lambda i: (i, 0)),
                  _const_spec((1, D)), _const_spec((D, LANE)), _const_spec((1, LANE)), _const_spec((1, LANE))],
        out_specs=[pl.BlockSpec((tm, D), lambda i: (i, 0)),
                   pl.BlockSpec((tm, LANE), lambda i: (i, 0)),
                   pl.BlockSpec((tm, LANE), lambda i: (i, 0))],
        out_shape=[jax.ShapeDtypeStruct((T, D), BF16),
                   jax.ShapeDtypeStruct((T, LANE), F32),
                   jax.ShapeDtypeStruct((T, LANE), F32)],
        compiler_params=_cparams("parallel"),
        name="norm_proj",
    )(x2, g_mix, w_ab, alog_pad, dtb_pad)


def _matmul_kernel(a_ref, b_ref, o_ref):
    o_ref[...] = jnp.dot(a_ref[...], b_ref[...], preferred_element_type=F32).astype(o_ref.dtype)


def _matmul(a, b, tm, tn, out_dtype):
    M, K = a.shape
    N = b.shape[1]
    return pl.pallas_call(
        _matmul_kernel,
        grid=(N // tn, M // tm),
        in_specs=[pl.BlockSpec((tm, K), lambda j, i: (i, 0)),
                  pl.BlockSpec((K, tn), lambda j, i: (0, j))],
        out_specs=pl.BlockSpec((tm, tn), lambda j, i: (i, j)),
        out_shape=jax.ShapeDtypeStruct((M, N), out_dtype),
        compiler_params=_cparams("parallel", "parallel"),
        name="in_proj",
    )(a, b)


def _gdn_kernel(q_ref, k_ref, v_ref, z_ref, cwq_ref, cwk_ref, cwv_ref, gc_ref, grow_ref, beta_ref, nw_ref,
                y_ref,
                xpad, qn, kn, vn, u_s, w_s, qg_s, kg_s, at_s, egl_s, o_s,
                *, n_heads, seq, conv_rows, prep_chunks):
    C = GDN_CHUNK
    n_chunks = seq // C
    head = pl.program_id(1)
    pad = 8
    half = GDN_CONV // 2

    zeros_pad = jnp.zeros((pad, LANE), F32)
    xpad[0:pad, :] = zeros_pad
    xpad[pad + seq:pad + seq + pad, :] = zeros_pad
    for src, cw_ref, dst, mode in ((q_ref, cwq_ref, qn, "q"), (k_ref, cwk_ref, kn, "k"), (v_ref, cwv_ref, vn, "v")):
        xpad[pad:pad + seq, :] = src[...].astype(F32)
        for t in range(seq // conv_rows):
            r0 = t * conv_rows
            acc = jnp.zeros((conv_rows, LANE), F32)
            for j in range(GDN_CONV):
                lo = pad + r0 + j - half
                acc = acc + xpad[lo:lo + conv_rows, :] * cw_ref[j:j + 1, :]
            y = acc * _sigmoid(acc)
            if mode != "v":
                y = y * lax.rsqrt(jnp.sum(y * y, axis=-1, keepdims=True) + EPS)
            if mode == "q":
                y = y * (GDN_DK ** -0.5)
            dst[r0:r0 + conv_rows, :] = y

    nb = prep_chunks
    R = nb * C
    ii = lax.broadcasted_iota(jnp.int32, (nb, C, C), 1)
    jj = lax.broadcasted_iota(jnp.int32, (nb, C, C), 2)
    eye = jnp.where(ii == jj, 1.0, 0.0).astype(F32)
    lane = lax.broadcasted_iota(jnp.int32, (R, LANE), 1)

    def prep(t, carry):
        r0 = pl.multiple_of(t * R, R)
        n0 = pl.multiple_of(t * nb, nb)
        q3 = qn[pl.ds(r0, R), :].reshape(nb, C, LANE)
        k3 = kn[pl.ds(r0, R), :].reshape(nb, C, LANE)
        v3 = vn[pl.ds(r0, R), :].reshape(nb, C, LANE)
        gc_blk = gc_ref[pl.ds(r0, R), :]
        beta_blk = beta_ref[pl.ds(r0, R), :]
        qk = jnp.einsum("nik,njk->nij", q3.astype(BF16), k3.astype(BF16), preferred_element_type=F32)
        for d in range(2):
            colidx = d * n_heads + head
            gcb = jnp.sum(jnp.where(lane == colidx, gc_blk, 0.0), axis=-1, keepdims=True).reshape(nb, C, 1)
            bt = jnp.sum(jnp.where(lane == 2 * n_heads + colidx, beta_blk, 0.0), axis=-1,
                         keepdims=True).reshape(nb, C, 1)
            grow = grow_ref[0, colidx, pl.ds(n0, nb), :].reshape(nb, 1, C)
            if d == 0:
                incl, strict = ii >= jj, ii > jj
                gl = gcb[:, C - 1:C, :]
            else:
                incl, strict = ii <= jj, ii < jj
                gl = gcb[:, 0:1, :]
            decay = jnp.exp(jnp.where(incl, gcb - grow, -jnp.inf))
            kb = k3 * bt
            kk = jnp.einsum("nik,njk->nij", kb.astype(BF16), k3.astype(BF16), preferred_element_type=F32)
            L = jnp.where(strict, kk * decay, 0.0)
            attn = qk * decay
            ainv = eye - L
            P = L
            for _ in range(int(math.log2(C)) - 1):
                Pb = P.astype(BF16)
                P = jnp.einsum("nij,njk->nik", Pb, Pb, preferred_element_type=F32)
                ainv = ainv + jnp.einsum("nij,njk->nik", ainv.astype(BF16), P.astype(BF16),
                                         preferred_element_type=F32)
            eg = jnp.exp(gcb)
            rhs = jnp.concatenate([v3 * bt, kb * eg], axis=-1).astype(BF16)
            uw = jnp.einsum("nij,njd->nid", ainv.astype(BF16), rhs, preferred_element_type=F32)
            u_s[d, pl.ds(r0, R), :] = uw[:, :, :LANE].reshape(R, LANE)
            w_s[d, pl.ds(r0, R), :] = uw[:, :, LANE:].reshape(R, LANE).astype(BF16)
            qg_s[d, pl.ds(r0, R), :] = (q3 * eg).reshape(R, LANE).astype(BF16)
            kg_s[d, pl.ds(r0, R), :] = (k3 * jnp.exp(gl - gcb)).reshape(R, LANE).astype(BF16)
            at_s[d, pl.ds(r0, R), :] = attn.reshape(R, C).astype(BF16)
            egl_s[d, pl.ds(n0, nb), :] = jnp.broadcast_to(jnp.exp(gl).reshape(nb, 1), (nb, LANE))
        return carry

    lax.fori_loop(0, n_chunks // nb, prep, 0)

    def chunk_step(d, n, state):
        r = pl.multiple_of(n * C, C)
        sb = state.astype(BF16)
        w = w_s[d, pl.ds(r, C), :]
        vnew = u_s[d, pl.ds(r, C), :] - jnp.dot(w, sb, preferred_element_type=F32)
        vb = vnew.astype(BF16)
        o = (jnp.dot(qg_s[d, pl.ds(r, C), :], sb, preferred_element_type=F32)
             + jnp.dot(at_s[d, pl.ds(r, C), :], vb, preferred_element_type=F32))
        upd = lax.dot_general(kg_s[d, pl.ds(r, C), :], vb, (((0,), (0,)), ((), ())),
                              preferred_element_type=F32)
        state = state * egl_s[d, pl.ds(n, 1), :] + upd
        o_s[d, pl.ds(r, C), :] = o
        return state

    def scan(n, carry):
        sf, sbw = carry
        sf = chunk_step(0, n, sf)
        sbw = chunk_step(1, n_chunks - 1 - n, sbw)
        return sf, sbw

    zero_state = jnp.zeros((GDN_DK, GDN_DV), F32)
    lax.fori_loop(0, n_chunks, scan, (zero_state, zero_state))

    for t in range(seq // conv_rows):
        r0 = t * conv_rows
        o = o_s[0, r0:r0 + conv_rows, :] + o_s[1, r0:r0 + conv_rows, :]
        on = o * lax.rsqrt(jnp.mean(o * o, axis=-1, keepdims=True) + EPS) * nw_ref[...]
        z = z_ref[r0:r0 + conv_rows, :].astype(F32)
        y_ref[r0:r0 + conv_rows, :] = (on * (z * _sigmoid(z))).astype(y_ref.dtype)


def _gdn(p, p_off, conv_w8, gc, grow, beta, norm_w, batch, seq, n_heads):
    T = batch * seq
    qk_blocks = n_heads
    n_chunks = seq // GDN_CHUNK
    conv_rows = min(512, seq)
    prep_chunks = min(4, n_chunks)
    blk = lambda off: pl.BlockSpec((seq, LANE), lambda b, h: (b, p_off + off + h))
    cw = lambda off: pl.BlockSpec((8, LANE), lambda b, h: (0, off + h))
    return pl.pallas_call(
        functools.partial(_gdn_kernel, n_heads=n_heads, seq=seq, conv_rows=conv_rows, prep_chunks=prep_chunks),
        grid=(batch, n_heads),
        in_specs=[blk(0), blk(qk_blocks), blk(2 * qk_blocks), blk(3 * qk_blocks),
                  cw(0), cw(qk_blocks), cw(2 * qk_blocks),
                  pl.BlockSpec((seq, LANE), lambda b, h: (b, 0)),
                  pl.BlockSpec((1, 2 * n_heads, n_chunks, GDN_CHUNK), lambda b, h: (b, 0, 0, 0)),
                  pl.BlockSpec((seq, LANE), lambda b, h: (b, 0)),
                  _const_spec((1, LANE))],
        out_specs=pl.BlockSpec((seq, LANE), lambda b, h: (b, h)),
        out_shape=jax.ShapeDtypeStruct((T, n_heads * GDN_DV), BF16),
        scratch_shapes=[
            pltpu.VMEM((seq + 16, LANE), F32),
            pltpu.VMEM((seq, LANE), F32),
            pltpu.VMEM((seq, LANE), F32),
            pltpu.VMEM((seq, LANE), F32),
            pltpu.VMEM((2, seq, LANE), F32),
            pltpu.VMEM((2, seq, LANE), BF16),
            pltpu.VMEM((2, seq, LANE), BF16),
            pltpu.VMEM((2, seq, LANE), BF16),
            pltpu.VMEM((2, seq, GDN_CHUNK), BF16),
            pltpu.VMEM((2, max(n_chunks, 8), LANE), F32),
            pltpu.VMEM((2, seq, LANE), F32),
        ],
        compiler_params=_cparams("parallel", "parallel"),
        name="gdn",
    )(p, p, p, p, conv_w8, conv_w8, conv_w8, gc, grow, beta, norm_w)


def _diff_attn_kernel(slopes_ref, q_ref, k_ref, v_ref, lam_ref, nw_ref, o_ref,
                      qT_s, vT_s, klo_s, khi_s, rel_s, acc1_s, acc2_s,
                      *, seq, tq, tk, lam_init):
    head = pl.program_id(1)
    slope2 = slopes_ref[head] * LOG2E
    scale2 = (DIFF_DH ** -0.5) * LOG2E

    rows = min(512, seq)
    for t in range(seq // rows):
        r0 = t * rows
        qT_s[:, r0:r0 + rows] = (q_ref[r0:r0 + rows, :].astype(F32) * scale2).T.astype(BF16)
        vT_s[:, r0:r0 + rows] = v_ref[r0:r0 + rows, :].astype(F32).T.astype(BF16)
        kt = k_ref[r0:r0 + rows, :]
        lane = lax.broadcasted_iota(jnp.int32, kt.shape, 1)
        klo_s[r0:r0 + rows, :] = jnp.where(lane < DIFF_DH, kt, jnp.zeros_like(kt))
        khi_s[r0:r0 + rows, :] = jnp.where(lane >= DIFF_DH, kt, jnp.zeros_like(kt))
    kr = lax.broadcasted_iota(jnp.int32, (tk, tq), 0)
    qc = lax.broadcasted_iota(jnp.int32, (tk, tq), 1)
    rel_s[...] = (qc - kr).astype(F32) * slope2

    lf = lam_ref[...]
    lam = (jnp.exp(jnp.sum(lf[0:1, :] * lf[1:2, :], axis=-1, keepdims=True))
           - jnp.exp(jnp.sum(lf[2:3, :] * lf[3:4, :], axis=-1, keepdims=True)) + lam_init)

    def q_tile(i, carry):
        c_q = pl.multiple_of(i * tq, tq)
        qT = qT_s[:, pl.ds(c_q, tq)]
        acc1_s[...] = jnp.zeros_like(acc1_s)
        acc2_s[...] = jnp.zeros_like(acc2_s)

        def kv_tile(j, st):
            m1, l1, m2, l2 = st
            r_k = pl.multiple_of(j * tk, tk)
            off = (i * tq - j * tk).astype(F32) * slope2
            bias = jnp.abs(rel_s[...] + off)
            vT = vT_s[:, pl.ds(r_k, tk)]
            out = []
            for k_s, acc_s, m, l in ((klo_s, acc1_s, m1, l1), (khi_s, acc2_s, m2, l2)):
                s = jnp.dot(k_s[pl.ds(r_k, tk), :], qT, preferred_element_type=F32) - bias
                m_new = jnp.maximum(m, jnp.max(s, axis=0, keepdims=True))
                alpha = jnp.exp2(m - m_new)
                p = jnp.exp2(s - m_new)
                l = alpha * l + jnp.sum(p, axis=0, keepdims=True)
                acc_s[...] = acc_s[...] * alpha + jnp.dot(vT, p.astype(BF16), preferred_element_type=F32)
                out += [m_new, l]
            return tuple(out)

        init = (jnp.full((1, tq), NEG_BIG, F32), jnp.zeros((1, tq), F32),
                jnp.full((1, tq), NEG_BIG, F32), jnp.zeros((1, tq), F32))
        m1, l1, m2, l2 = lax.fori_loop(0, seq // tk, kv_tile, init)
        od = acc1_s[...] / l1 - lam * (acc2_s[...] / l2)
        yn = od * lax.rsqrt(jnp.mean(od * od, axis=0, keepdims=True) + EPS) * nw_ref[...] * (1.0 - lam_init)
        o_ref[pl.ds(c_q, tq), :] = yn.T.astype(o_ref.dtype)
        return carry

    lax.fori_loop(0, seq // tq, q_tile, 0)


def _diff_attn(p, slopes, diff_lambda, norm_w_col, batch, seq, n_heads, q_off, k_off, v_off, lam_init):
    T = batch * seq
    tq = min(256, seq)
    tk = min(512, seq)
    blk = lambda off: pl.BlockSpec((seq, LANE), lambda b, h: (b, off + h))
    return pl.pallas_call(
        functools.partial(_diff_attn_kernel, seq=seq, tq=tq, tk=tk, lam_init=lam_init),
        grid=(batch, n_heads),
        in_specs=[pl.BlockSpec(memory_space=pltpu.SMEM),
                  blk(q_off), blk(k_off), blk(v_off),
                  _const_spec((4, DIFF_DH)), _const_spec((DIFF_DV, 1))],
        out_specs=pl.BlockSpec((seq, LANE), lambda b, h: (b, h)),
        out_shape=jax.ShapeDtypeStruct((T, n_heads * DIFF_DV), BF16),
        scratch_shapes=[
            pltpu.VMEM((2 * DIFF_DH, seq), BF16),
            pltpu.VMEM((DIFF_DV, seq), BF16),
            pltpu.VMEM((seq, LANE), BF16),
            pltpu.VMEM((seq, LANE), BF16),
            pltpu.VMEM((tk, tq), F32),
            pltpu.VMEM((DIFF_DV, tq), F32),
            pltpu.VMEM((DIFF_DV, tq), F32),
        ],
        compiler_params=_cparams("parallel", "parallel"),
        name="diff_attn",
    )(slopes, p, p, p, diff_lambda, norm_w_col)


def _merge_kernel(ya_ref, yd_ref, ga_ref, gd_ref, x_ref, wa_ref, wd_ref, wo_ref, gffn_ref,
                  wr_hi_ref, wr_lo_ref, br_ref, x1_ref, h2_ref, route_ref, *, n_groups, per_group):
    ma = jnp.dot(ya_ref[...], wa_ref[...], preferred_element_type=F32)
    md = jnp.dot(yd_ref[...], wd_ref[...], preferred_element_type=F32)
    merged = _sigmoid(ga_ref[...].astype(F32)) * ma + _sigmoid(gd_ref[...].astype(F32)) * md
    x1 = x_ref[...] + jnp.dot(merged.astype(BF16), wo_ref[...], preferred_element_type=F32)
    x1_ref[...] = x1
    h2 = x1 * lax.rsqrt(jnp.mean(x1 * x1, axis=-1, keepdims=True) + EPS) * gffn_ref[...]
    h2_ref[...] = h2.astype(BF16)

    h_hi = h2.astype(BF16)
    h_lo = (h2 - h_hi.astype(F32)).astype(BF16)
    logits = (jnp.dot(h_hi, wr_hi_ref[...], preferred_element_type=F32)
              + jnp.dot(h_hi, wr_lo_ref[...], preferred_element_type=F32)
              + jnp.dot(h_lo, wr_hi_ref[...], preferred_element_type=F32)) + br_ref[...]
    lane = lax.broadcasted_iota(jnp.int32, logits.shape, 1)
    lane_f = lane.astype(F32)
    big = float(LANE)

    def first_argmax(vals, vmax):
        return jnp.min(jnp.where(vals == vmax, lane_f, big), axis=-1, keepdims=True)

    gl = jnp.where(lane < n_groups, logits, -jnp.inf)
    gmax = jnp.max(gl, axis=-1, keepdims=True)
    g_idx = first_argmax(gl, gmax)
    g_w = 1.0 / jnp.sum(jnp.exp(gl - gmax), axis=-1, keepdims=True)
    e_lo = n_groups + g_idx * per_group
    in_group = (lane_f >= e_lo) & (lane_f < e_lo + per_group)
    el = jnp.where(in_group, logits, -jnp.inf)
    emax = jnp.max(el, axis=-1, keepdims=True)
    pe = jnp.exp(el - emax)
    pe = pe / jnp.sum(pe, axis=-1, keepdims=True)
    p1 = jnp.max(pe, axis=-1, keepdims=True)
    i1 = first_argmax(jnp.where(in_group, pe, -1.0), p1)
    rest = jnp.where(in_group & (lane_f != i1), pe, -1.0)
    p2 = jnp.max(rest, axis=-1, keepdims=True)
    i2 = first_argmax(rest, p2)
    denom = p1 + p2
    w1 = p1 / denom * g_w
    w2 = p2 / denom * g_w
    route = jnp.where(lane == 0, i1 - n_groups, 0.0)
    route = jnp.where(lane == 1, i2 - n_groups, route)
    route = jnp.where(lane == 2, w1, route)
    route = jnp.where(lane == 3, w2, route)
    route_ref[...] = route


def _merge(y_a, y_d, p, x2, wa, wd, wo, g_ffn, wr_hi, wr_lo, b_r, n_groups, per_group, tm):
    T, D = x2.shape
    va, vd = y_a.shape[1], y_d.shape[1]
    row = lambda w: pl.BlockSpec((tm, w), lambda i: (i, 0))
    return pl.pallas_call(
        functools.partial(_merge_kernel, n_groups=n_groups, per_group=per_group),
        grid=(T // tm,),
        in_specs=[row(va), row(vd),
                  pl.BlockSpec((tm, D), lambda i: (i, 0)),
                  pl.BlockSpec((tm, D), lambda i: (i, 1)),
                  row(D),
                  _const_spec((va, D)), _const_spec((vd, D)), _const_spec((D, D)), _const_spec((1, D)),
                  _const_spec((D, LANE)), _const_spec((D, LANE)), _const_spec((1, LANE))],
        out_specs=[row(D), row(D), row(LANE)],
        out_shape=[jax.ShapeDtypeStruct((T, D), F32),
                   jax.ShapeDtypeStruct((T, D), BF16),
                   jax.ShapeDtypeStruct((T, LANE), F32)],
        compiler_params=_cparams("parallel"),
        name="merge",
    )(y_a, y_d, p, p, x2, wa, wd, wo, g_ffn, wr_hi, wr_lo, b_r)


def _expert_kernel(tile_expert_ref, tile_flag_ref, xs_ref, cw_ref, wg_ref, wu_ref, wd_ref, o_ref,
                   wg_s, wu_s, wd_s):
    i = pl.program_id(0)
    flag = tile_flag_ref[i]

    @pl.when(flag == 2)
    def _():
        wg_s[...] = wg_ref[0].astype(BF16)
        wu_s[...] = wu_ref[0].astype(BF16)
        wd_s[...] = wd_ref[0].astype(BF16)

    @pl.when(flag > 0)
    def _():
        x = xs_ref[...]
        g = jnp.dot(x, wg_s[...], preferred_element_type=F32)
        u = jnp.dot(x, wu_s[...], preferred_element_type=F32)
        hid = g * _sigmoid(g) * u * cw_ref[...]
        o_ref[...] = jnp.dot(hid.astype(BF16), wd_s[...], preferred_element_type=F32).astype(o_ref.dtype)

    @pl.when(flag == 0)
    def _():
        o_ref[...] = jnp.zeros_like(o_ref)


def _experts(tile_expert, tile_flag, xs, cw, w_gate, w_up, w_down, tm):
    n_rows, D = xs.shape
    E, _, Fd = w_gate.shape
    n_tiles = n_rows // tm
    grid_spec = pltpu.PrefetchScalarGridSpec(
        num_scalar_prefetch=2,
        grid=(n_tiles,),
        in_specs=[pl.BlockSpec((tm, D), lambda i, te, tf: (i, 0)),
                  pl.BlockSpec((tm, 1), lambda i, te, tf: (i, 0)),
                  pl.BlockSpec((1, D, Fd), lambda i, te, tf: (te[i], 0, 0)),
                  pl.BlockSpec((1, D, Fd), lambda i, te, tf: (te[i], 0, 0)),
                  pl.BlockSpec((1, Fd, D), lambda i, te, tf: (te[i], 0, 0))],
        out_specs=pl.BlockSpec((tm, D), lambda i, te, tf: (i, 0)),
        scratch_shapes=[pltpu.VMEM((D, Fd), BF16), pltpu.VMEM((D, Fd), BF16), pltpu.VMEM((Fd, D), BF16)],
    )
    return pl.pallas_call(
        _expert_kernel,
        grid_spec=grid_spec,
        out_shape=jax.ShapeDtypeStruct((n_rows, D), BF16),
        compiler_params=_cparams("arbitrary"),
        name="experts",
    )(tile_expert, tile_flag, xs, cw, w_gate, w_up, w_down)


def _final_kernel(x1_ref, y0_ref, y1_ref, g_ref, o_ref):
    x = x1_ref[...] + (y0_ref[...].astype(F32) + y1_ref[...].astype(F32))
    o_ref[...] = x * lax.rsqrt(jnp.mean(x * x, axis=-1, keepdims=True) + EPS) * g_ref[...]


def _final(x1, y0, y1, g_final, tm):
    T, D = x1.shape
    row = pl.BlockSpec((tm, D), lambda i: (i, 0))
    return pl.pallas_call(
        _final_kernel,
        grid=(T // tm,),
        in_specs=[row, row, row, _const_spec((1, D))],
        out_specs=row,
        out_shape=jax.ShapeDtypeStruct((T, D), F32),
        compiler_params=_cparams("parallel"),
        name="final",
    )(x1, y0, y1, g_final)


def _route_tables(ids, weights, n_experts, tm):
    T = ids.shape[0]
    A = T * TOP_K
    flat_e = ids.reshape(A)
    onehot = (flat_e[:, None] == jnp.arange(n_experts, dtype=jnp.int32)[None, :]).astype(jnp.int32)
    rank = jnp.sum((jnp.cumsum(onehot, axis=0) - onehot) * onehot, axis=1)
    counts = jnp.sum(onehot, axis=0)
    padded = ((counts + tm - 1) // tm) * tm
    ends = jnp.cumsum(padded)
    starts = ends - padded
    pos = starts[flat_e] + rank
    n_tiles = A // tm + n_experts
    n_rows = n_tiles * tm
    slot_token = jnp.zeros((n_rows,), jnp.int32).at[pos].set(jnp.arange(A, dtype=jnp.int32) // TOP_K)
    slot_w = jnp.zeros((n_rows,), F32).at[pos].set(weights.reshape(A))
    tile_start = jnp.arange(n_tiles, dtype=jnp.int32) * tm
    tile_expert_raw = jnp.searchsorted(ends, tile_start, side="right").astype(jnp.int32)
    valid = tile_start < ends[-1]
    last_expert = jnp.max(jnp.where(counts > 0, jnp.arange(n_experts, dtype=jnp.int32), 0))
    tile_expert = jnp.where(valid, jnp.minimum(tile_expert_raw, n_experts - 1), last_expert)
    first = jnp.concatenate([jnp.ones((1,), bool), tile_expert[1:] != tile_expert[:-1]])
    tile_flag = jnp.where(valid, jnp.where(first, 2, 1), 0).astype(jnp.int32)
    return pos.reshape(T, TOP_K), slot_token, slot_w, tile_expert, tile_flag


def _largest_tile(n, cap):
    t = min(n, cap)
    while n % t:
        t //= 2
    return t


def kernel(x, g_mix, w_in, conv_w, a_log, dt_bias, gdn_norm_w, diff_lambda, diff_norm_w, w_branch_a, w_branch_d,
           w_out, g_ffn, w_group, b_group, w_router, b_router, w_exp_gate, w_exp_up, w_exp_down, g_final):
    B, S, D = x.shape
    T = B * S
    depth = g_mix.shape[0]
    Hg = a_log.shape[-1]
    gdn_qk = Hg * GDN_DK
    gdn_v = Hg * GDN_DV
    diff_v = w_branch_d.shape[1]
    Hd = diff_v // DIFF_DV
    diff_qk = Hd * 2 * DIFF_DH
    n_groups = w_group.shape[-1]
    n_experts = w_router.shape[-1]
    per_group = n_experts // n_groups
    assert 4 * Hg <= LANE and n_groups + n_experts <= LANE
    assert S % GDN_CHUNK == 0 and D % LANE == 0

    ab_lo = 2 * gdn_qk + 2 * gdn_v
    ab_hi = ab_lo + 4 * Hg
    gates_lo = ab_hi + 2 * diff_qk + diff_v
    main_cols = 2 * D + ab_lo + 2 * diff_qk + diff_v
    gdn_off = 2 * D
    q_d_off = gdn_off + ab_lo
    k_d_off = q_d_off + diff_qk
    v_d_off = k_d_off + diff_qk

    tm_norm = _largest_tile(T, 512)
    tm_proj = _largest_tile(T, 1024)
    tn_proj = _largest_tile(main_cols, 1024)
    tm_merge = _largest_tile(T, 256)
    tm_exp = _largest_tile(T * TOP_K, 256)
    tm_final = _largest_tile(T, 512)

    slopes = jnp.exp2(-8.0 * (jnp.arange(Hd, dtype=F32) + 1.0) / Hd)
    x2 = x.reshape(T, D)
    for layer in range(depth):
        lam_init = 0.8 - 0.6 * math.exp(-0.3 * layer)
        w_in_l = w_in[layer]
        w_main = jnp.concatenate([w_in_l[:, gates_lo:], w_in_l[:, :ab_lo], w_in_l[:, ab_hi:gates_lo]],
                                 axis=1).astype(BF16)
        w_ab = jnp.pad(w_in_l[:, ab_lo:ab_hi], ((0, 0), (0, LANE - 4 * Hg))).astype(BF16)
        alog_pad = jnp.pad(a_log[layer].reshape(1, 2 * Hg), ((0, 0), (0, LANE - 2 * Hg)))
        dtb_pad = jnp.pad(dt_bias[layer].reshape(1, 2 * Hg), ((0, 0), (0, LANE - 2 * Hg)))

        h, gc, beta = _norm_proj(x2, g_mix[layer].reshape(1, D), w_ab, alog_pad, dtb_pad, 2 * Hg, tm_norm)
        p = _matmul(h, w_main, tm_proj, tn_proj, BF16)

        grow = gc[:, :2 * Hg].reshape(B, S // GDN_CHUNK, GDN_CHUNK, 2 * Hg).transpose(0, 3, 1, 2)
        conv_w8 = jnp.pad(conv_w[layer], ((0, 8 - GDN_CONV), (0, 0)))
        y_a = _gdn(p, gdn_off // LANE, conv_w8, gc, grow, beta, gdn_norm_w[layer].reshape(1, GDN_DV), B, S, Hg)
        y_d = _diff_attn(p, slopes, diff_lambda[layer], diff_norm_w[layer].reshape(DIFF_DV, 1), B, S, Hd,
                         q_d_off // LANE, k_d_off // LANE, v_d_off // LANE, lam_init)

        w_r = jnp.pad(jnp.concatenate([w_group[layer], w_router[layer]], axis=1),
                      ((0, 0), (0, LANE - n_groups - n_experts)))
        wr_hi = w_r.astype(BF16)
        wr_lo = (w_r - wr_hi.astype(F32)).astype(BF16)
        b_r = jnp.pad(jnp.concatenate([b_group[layer], b_router[layer]]).reshape(1, -1),
                      ((0, 0), (0, LANE - n_groups - n_experts)))
        x1, h2, route = _merge(y_a, y_d, p, x2,
                               w_branch_a[layer].astype(BF16), w_branch_d[layer].astype(BF16),
                               w_out[layer].astype(BF16), g_ffn[layer].reshape(1, D),
                               wr_hi, wr_lo, b_r, n_groups, per_group, tm_merge)

        ids = route[:, 0:TOP_K].astype(jnp.int32)
        wts = route[:, TOP_K:2 * TOP_K]
        pos, slot_token, slot_w, tile_expert, tile_flag = _route_tables(ids, wts, n_experts, tm_exp)
        xs = jnp.take(h2, slot_token, axis=0)
        ys = _experts(tile_expert, tile_flag, xs, slot_w[:, None], w_exp_gate[layer], w_exp_up[layer],
                      w_exp_down[layer], tm_exp)
        y0 = jnp.take(ys, pos[:, 0], axis=0)
        y1 = jnp.take(ys, pos[:, 1], axis=0)
        if layer + 1 < depth:
            x2 = x1 + (y0.astype(F32) + y1.astype(F32))
    out = _final(x1, y0, y1, g_final.reshape(1, D), tm_final)
    return out.reshape(B, S, D)
```

```python
import functools
import math

import jax
import jax.numpy as jnp
from jax import lax
from jax.experimental import pallas as pl
from jax.experimental.pallas import tpu as pltpu

F32 = jnp.float32
BF16 = jnp.bfloat16
EPS = 1e-6
LANE = 128
GDN_DK = 128
GDN_DV = 128
GDN_CONV = 5
GDN_CHUNK = 64
DIFF_DH = 64
DIFF_DV = 2 * DIFF_DH
TOP_K = 2
LOG2E = 1.4426950408889634
NEG_BIG = -1e30
VMEM_LIMIT_BYTES = 56 * 1024 * 1024

_HI = lax.Precision.HIGHEST


def _cparams(*sem):
    return pltpu.CompilerParams(dimension_semantics=sem, vmem_limit_bytes=VMEM_LIMIT_BYTES)


def _const_spec(shape):
    nd = len(shape)
    return pl.BlockSpec(shape, lambda *_: (0,) * nd, pipeline_mode=pl.Buffered(1))


def _mm(a, b):
    return jnp.dot(a.astype(BF16), b.astype(BF16), preferred_element_type=F32)


def _split3(x):
    hi = x.astype(BF16)
    r = x - hi.astype(F32)
    mid = r.astype(BF16)
    lo = (r - mid.astype(F32)).astype(BF16)
    return hi, mid, lo


def _softplus(x):
    return jnp.maximum(x, 0.0) + jnp.log(1.0 + jnp.exp(-jnp.abs(x)))


def _sigmoid(x):
    return 1.0 / (1.0 + jnp.exp(-x))


def _norm_proj_kernel(x_ref, g_ref, wab_ref, alog_ref, dtb_ref, h_ref, gc_ref, beta_ref, *, n_dir_heads):
    x = x_ref[...]
    h = x * lax.rsqrt(jnp.mean(x * x, axis=-1, keepdims=True) + EPS) * g_ref[...]
    hb = h.astype(BF16)
    h_ref[...] = hb
    ab = jnp.dot(hb, wab_ref[...], preferred_element_type=F32)
    g = -jnp.exp(alog_ref[...]) * _softplus(ab + dtb_ref[...])
    beta_ref[...] = _sigmoid(ab)
    tm = x.shape[0]
    row = lax.broadcasted_iota(jnp.int32, (tm, tm), 0)
    col = lax.broadcasted_iota(jnp.int32, (tm, tm), 1)
    same = (row // GDN_CHUNK) == (col // GDN_CHUNK)
    prefix = jnp.where(same & (col <= row), 1.0, 0.0).astype(BF16)
    suffix = jnp.where(same & (col >= row), 1.0, 0.0).astype(BF16)
    pieces = _split3(g)
    cs_f = sum(jnp.dot(prefix, p, preferred_element_type=F32) for p in pieces)
    cs_b = sum(jnp.dot(suffix, p, preferred_element_type=F32) for p in pieces)
    lane = lax.broadcasted_iota(jnp.int32, g.shape, 1)
    gc_ref[...] = jnp.where(lane < n_dir_heads // 2, cs_f, cs_b)


def _norm_proj(x2, g_mix, w_ab, alog_pad, dtb_pad, n_dir_heads, tm):
    T, D = x2.shape
    return pl.pallas_call(
        functools.partial(_norm_proj_kernel, n_dir_heads=n_dir_heads),
        grid=(T // tm,),
        in_specs=[pl.BlockSpec((tm, D), lambda i: (i, 0)),
                  _const_spec((1, D)), _const_spec((D, LANE)), _const_spec((1, LANE)), _const_spec((1, LANE))],
        out_specs=[pl.BlockSpec((tm, D), lambda i: (i, 0)),
                   pl.BlockSpec((tm, LANE), lambda i: (i, 0)),
                   pl.BlockSpec((tm, LANE), lambda i: (i, 0))],
        out_shape=[jax.ShapeDtypeStruct((T, D), BF16),
                   jax.ShapeDtypeStruct((T, LANE), F32),
                   jax.ShapeDtypeStruct((T, LANE), F32)],
        compiler_params=_cparams("parallel"),
        name="norm_proj",
    )(x2, g_mix, w_ab, alog_pad, dtb_pad)


def _matmul_kernel(a_ref, b_ref, o_ref):
    o_ref[...] = jnp.dot(a_ref[...], b_ref[...], preferred_element_type=F32).astype(o_ref.dtype)


def _matmul(a, b, tm, tn, out_dtype):
    M, K = a.shape
    N = b.shape[1]
    return pl.pallas_call(
        _matmul_kernel,
        grid=(N // tn, M // tm),
        in_specs=[pl.BlockSpec((tm, K), lambda j, i: (i, 0)),
                  pl.BlockSpec((K, tn), lambda j, i: (0, j))],
        out_specs=pl.BlockSpec((tm, tn), lambda j, i: (i, j)),
        out_shape=jax.ShapeDtypeStruct((M, N), out_dtype),
        compiler_params=_cparams("parallel", "parallel"),
        name="in_proj",
    )(a, b)


def _gdn_kernel(q_ref, k_ref, v_ref, z_ref, cwq_ref, cwk_ref, cwv_ref, gc_ref, grow_ref, beta_ref, nw_ref,
                y_ref,
                xpad, qn, kn, vn, u_s, w_s, qg_s, kg_s, at_s, egl_s, o_s,
                *, n_heads, seq, conv_rows, prep_chunks):
    C = GDN_CHUNK
    n_chunks = seq // C
    head = pl.program_id(1)
    pad = 8
    half = GDN_CONV // 2

    zeros_pad = jnp.zeros((pad, LANE), F32)
    xpad[0:pad, :] = zeros_pad
    xpad[pad + seq:pad + seq + pad, :] = zeros_pad
    for src, cw_ref, dst, mode in ((q_ref, cwq_ref, qn, "q"), (k_ref, cwk_ref, kn, "k"), (v_ref, cwv_ref, vn, "v")):
        xpad[pad:pad + seq, :] = src[...].astype(F32)
        for t in range(seq // conv_rows):
            r0 = t * conv_rows
            acc = jnp.zeros((conv_rows, LANE), F32)
            for j in range(GDN_CONV):
                lo = pad + r0 + j - half
                acc = acc + xpad[lo:lo + conv_rows, :] * cw_ref[j:j + 1, :]
            y = acc * _sigmoid(acc)
            if mode != "v":
                y = y * lax.rsqrt(jnp.sum(y * y, axis=-1, keepdims=True) + EPS)
            if mode == "q":
                y = y * (GDN_DK ** -0.5)
            dst[r0:r0 + conv_rows, :] = y

    nb = prep_chunks
    R = nb * C
    ii = lax.broadcasted_iota(jnp.int32, (nb, C, C), 1)
    jj = lax.broadcasted_iota(jnp.int32, (nb, C, C), 2)
    eye = jnp.where(ii == jj, 1.0, 0.0).astype(F32)
    lane = lax.broadcasted_iota(jnp.int32, (R, LANE), 1)

    def prep(t, carry):
        r0 = pl.multiple_of(t * R, R)
        n0 = pl.multiple_of(t * nb, nb)
        q3 = qn[pl.ds(r0, R), :].reshape(nb, C, LANE)
        k3 = kn[pl.ds(r0, R), :].reshape(nb, C, LANE)
        v3 = vn[pl.ds(r0, R), :].reshape(nb, C, LANE)
        gc_blk = gc_ref[pl.ds(r0, R), :]
        beta_blk = beta_ref[pl.ds(r0, R), :]
        qk = jnp.einsum("nik,njk->nij", q3.astype(BF16), k3.astype(BF16), preferred_element_type=F32)
        for d in range(2):
            colidx = d * n_heads + head
            gcb = jnp.sum(jnp.where(lane == colidx, gc_blk, 0.0), axis=-1, keepdims=True).reshape(nb, C, 1)
            bt = jnp.sum(jnp.where(lane == 2 * n_heads + colidx, beta_blk, 0.0), axis=-1,
                         keepdims=True).reshape(nb, C, 1)
            grow = grow_ref[0, colidx, pl.ds(n0, nb), :].reshape(nb, 1, C)
            if d == 0:
                incl, strict = ii >= jj, ii > jj
                gl = gcb[:, C - 1:C, :]
            else:
                incl, strict = ii <= jj, ii < jj
                gl = gcb[:, 0:1, :]
            decay = jnp.exp(jnp.where(incl, gcb - grow, -jnp.inf))
            kb = k3 * bt
            kk = jnp.einsum("nik,njk->nij", kb.astype(BF16), k3.astype(BF16), preferred_element_type=F32)
            L = jnp.where(strict, kk * decay, 0.0)
            attn = qk * decay
            ainv = eye - L
            P = L
            for _ in range(int(math.log2(C)) - 1):
                Pb = P.astype(BF16)
                P = jnp.einsum("nij,njk->nik", Pb, Pb, preferred_element_type=F32)
                ainv = ainv + jnp.einsum("nij,njk->nik", ainv.astype(BF16), P.astype(BF16),
                                         preferred_element_type=F32)
            eg = jnp.exp(gcb)
            rhs = jnp.concatenate([v3 * bt, kb * eg], axis=-1).astype(BF16)
            uw = jnp.einsum("nij,njd->nid", ainv.astype(BF16), rhs, preferred_element_type=F32)
            u_s[d, pl.ds(r0, R), :] = uw[:, :, :LANE].reshape(R, LANE)
            w_s[d, pl.ds(r0, R), :] = uw[:, :, LANE:].reshape(R, LANE).astype(BF16)
            qg_s[d, pl.ds(r0, R), :] = (q3 * eg).reshape(R, LANE).astype(BF16)
            kg_s[d, pl.ds(r0, R), :] = (k3 * jnp.exp(gl - gcb)).reshape(R, LANE).astype(BF16)
            at_s[d, pl.ds(r0, R), :] = attn.reshape(R, C).astype(BF16)
            egl_s[d, pl.ds(n0, nb), :] = jnp.broadcast_to(jnp.exp(gl).reshape(nb, 1), (nb, LANE))
        return carry

    lax.fori_loop(0, n_chunks // nb, prep, 0)

    def chunk_step(d, n, state):
        r = pl.multiple_of(n * C, C)
        sb = state.astype(BF16)
        w = w_s[d, pl.ds(r, C), :]
        vnew = u_s[d, pl.ds(r, C), :] - jnp.dot(w, sb, preferred_element_type=F32)
        vb = vnew.astype(BF16)
        o = (jnp.dot(qg_s[d, pl.ds(r, C), :], sb, preferred_element_type=F32)
             + jnp.dot(at_s[d, pl.ds(r, C), :], vb, preferred_element_type=F32))
        upd = lax.dot_general(kg_s[d, pl.ds(r, C), :], vb, (((0,), (0,)), ((), ())),
                              preferred_element_type=F32)
        state = state * egl_s[d, pl.ds(n, 1), :] + upd
        o_s[d, pl.ds(r, C), :] = o
        return state

    def scan(n, carry):
        sf, sbw = carry
        sf = chunk_step(0, n, sf)
        sbw = chunk_step(1, n_chunks - 1 - n, sbw)
        return sf, sbw

    zero_state = jnp.zeros((GDN_DK, GDN_DV), F32)
    lax.fori_loop(0, n_chunks, scan, (zero_state, zero_state))

    for t in range(seq // conv_rows):
        r0 = t * conv_rows
        o = o_s[0, r0:r0 + conv_rows, :] + o_s[1, r0:r0 + conv_rows, :]
        on = o * lax.rsqrt(jnp.mean(o * o, axis=-1, keepdims=True) + EPS) * nw_ref[...]
        z = z_ref[r0:r0 + conv_rows, :].astype(F32)
        y_ref[r0:r0 + conv_rows, :] = (on * (z * _sigmoid(z))).astype(y_ref.dtype)


def _gdn(p, p_off, conv_w8, gc, grow, beta, norm_w, batch, seq, n_heads):
    T = batch * seq
    qk_blocks = n_heads
    n_chunks = seq // GDN_CHUNK
    conv_rows = min(512, seq)
    prep_chunks = min(4, n_chunks)
    blk = lambda off: pl.BlockSpec((seq, LANE), lambda b, h: (b, p_off + off + h))
    cw = lambda off: pl.BlockSpec((8, LANE), lambda b, h: (0, off + h))
    return pl.pallas_call(
        functools.partial(_gdn_kernel, n_heads=n_heads, seq=seq, conv_rows=conv_rows, prep_chunks=prep_chunks),
        grid=(batch, n_heads),
        in_specs=[blk(0), blk(qk_blocks), blk(2 * qk_blocks), blk(3 * qk_blocks),
                  cw(0), cw(qk_blocks), cw(2 * qk_blocks),
                  pl.BlockSpec((seq, LANE), lambda b, h: (b, 0)),
                  pl.BlockSpec((1, 2 * n_heads, n_chunks, GDN_CHUNK), lambda b, h: (b, 0, 0, 0)),
                  pl.BlockSpec((seq, LANE), lambda b, h: (b, 0)),
                  _const_spec((1, LANE))],
        out_specs=pl.BlockSpec((seq, LANE), lambda b, h: (b, h)),
        out_shape=jax.ShapeDtypeStruct((T, n_heads * GDN_DV), BF16),
        scratch_shapes=[
            pltpu.VMEM((seq + 16, LANE), F32),
            pltpu.VMEM((seq, LANE), F32),
            pltpu.VMEM((seq, LANE), F32),
            pltpu.VMEM((seq, LANE), F32),
            pltpu.VMEM((2, seq, LANE), F32),
            pltpu.VMEM((2, seq, LANE), BF16),
            pltpu.VMEM((2, seq, LANE), BF16),
            pltpu.VMEM((2, seq, LANE), BF16),
            pltpu.VMEM((2, seq, GDN_CHUNK), BF16),
            pltpu.VMEM((2, max(n_chunks, 8), LANE), F32),
            pltpu.VMEM((2, seq, LANE), F32),
        ],
        compiler_params=_cparams("parallel", "parallel"),
        name="gdn",
    )(p, p, p, p, conv_w8, conv_w8, conv_w8, gc, grow, beta, norm_w)


def _diff_attn_kernel(slopes_ref, q_ref, k_ref, v_ref, lam_ref, nw_ref, o_ref,
                      qT_s, vT_s, klo_s, khi_s, rel_s, s_s, p_s, acc_s,
                      *, seq, tq, tk, lam_init):
    head = pl.program_id(1)
    slope2 = slopes_ref[head] * LOG2E
    scale2 = (DIFF_DH ** -0.5) * LOG2E

    rows = min(512, seq)
    for t in range(seq // rows):
        r0 = t * rows
        qT_s[:, r0:r0 + rows] = (q_ref[r0:r0 + rows, :].astype(F32) * scale2).T.astype(BF16)
        vT_s[:, r0:r0 + rows] = v_ref[r0:r0 + rows, :].astype(F32).T.astype(BF16)
        kt = k_ref[r0:r0 + rows, :]
        lane = lax.broadcasted_iota(jnp.int32, kt.shape, 1)
        klo_s[r0:r0 + rows, :] = jnp.where(lane < DIFF_DH, kt, jnp.zeros_like(kt))
        khi_s[r0:r0 + rows, :] = jnp.where(lane >= DIFF_DH, kt, jnp.zeros_like(kt))
    kr = lax.broadcasted_iota(jnp.int32, (tk, tq), 0)
    qc = lax.broadcasted_iota(jnp.int32, (tk, tq), 1)
    rel_s[...] = (qc - kr).astype(F32) * slope2

    lf = lam_ref[...]
    lam = (jnp.exp(jnp.sum(lf[0:1, :] * lf[1:2, :], axis=-1, keepdims=True))
           - jnp.exp(jnp.sum(lf[2:3, :] * lf[3:4, :], axis=-1, keepdims=True)) + lam_init)

    n_kv = seq // tk
    maps = (klo_s, khi_s)

    def q_tile(i, carry):
        c_q = pl.multiple_of(i * tq, tq)
        qT = qT_s[:, pl.ds(c_q, tq)]
        q0 = (i * tq).astype(F32) * slope2

        def scores(j):
            bias = jnp.abs(rel_s[...] + (q0 - (j * tk) * slope2))
            for mp, k_s in enumerate(maps):
                s_s[j % 2, mp] = jnp.dot(k_s[j * tk:(j + 1) * tk, :], qT, preferred_element_type=F32) - bias

        def weighted_values(j, alphas):
            vT = vT_s[:, j * tk:(j + 1) * tk]
            for mp in range(len(maps)):
                upd = jnp.dot(vT, p_s[j % 2, mp], preferred_element_type=F32)
                acc_s[mp] = upd if j == 0 else acc_s[mp] * alphas[mp] + upd

        m = [jnp.full((1, tq), NEG_BIG, F32) for _ in maps]
        l = [jnp.zeros((1, tq), F32) for _ in maps]
        scores(0)
        alphas = None
        for j in range(n_kv):
            if j + 1 < n_kv:
                scores(j + 1)
            new_alphas = []
            for mp in range(len(maps)):
                s = s_s[j % 2, mp]
                m_new = jnp.maximum(m[mp], jnp.max(s, axis=0, keepdims=True))
                alpha = jnp.exp2(m[mp] - m_new)
                p = jnp.exp2(s - m_new)
                l[mp] = alpha * l[mp] + jnp.sum(p, axis=0, keepdims=True)
                m[mp] = m_new
                p_s[j % 2, mp] = p.astype(BF16)
                new_alphas.append(alpha)
            if j > 0:
                weighted_values(j - 1, alphas)
            alphas = new_alphas
        weighted_values(n_kv - 1, alphas)

        od = acc_s[0] / l[0] - lam * (acc_s[1] / l[1])
        yn = od * lax.rsqrt(jnp.mean(od * od, axis=0, keepdims=True) + EPS) * nw_ref[...] * (1.0 - lam_init)
        o_ref[pl.ds(c_q, tq), :] = yn.T.astype(o_ref.dtype)
        return carry

    lax.fori_loop(0, seq // tq, q_tile, 0)


def _diff_attn(p, slopes, diff_lambda, norm_w_col, batch, seq, n_heads, q_off, k_off, v_off, lam_init):
    T = batch * seq
    tq = min(256, seq)
    tk = min(512, seq)
    blk = lambda off: pl.BlockSpec((seq, LANE), lambda b, h: (b, off + h))
    return pl.pallas_call(
        functools.partial(_diff_attn_kernel, seq=seq, tq=tq, tk=tk, lam_init=lam_init),
        grid=(batch, n_heads),
        in_specs=[pl.BlockSpec(memory_space=pltpu.SMEM),
                  blk(q_off), blk(k_off), blk(v_off),
                  _const_spec((4, DIFF_DH)), _const_spec((DIFF_DV, 1))],
        out_specs=pl.BlockSpec((seq, LANE), lambda b, h: (b, h)),
        out_shape=jax.ShapeDtypeStruct((T, n_heads * DIFF_DV), BF16),
        scratch_shapes=[
            pltpu.VMEM((2 * DIFF_DH, seq), BF16),
            pltpu.VMEM((DIFF_DV, seq), BF16),
            pltpu.VMEM((seq, LANE), BF16),
            pltpu.VMEM((seq, LANE), BF16),
            pltpu.VMEM((tk, tq), F32),
            pltpu.VMEM((2, 2, tk, tq), F32),
            pltpu.VMEM((2, 2, tk, tq), BF16),
            pltpu.VMEM((2, DIFF_DV, tq), F32),
        ],
        compiler_params=_cparams("parallel", "parallel"),
        name="diff_attn",
    )(slopes, p, p, p, diff_lambda, norm_w_col)


def _merge_kernel(ya_ref, yd_ref, ga_ref, gd_ref, x_ref, wa_ref, wd_ref, wo_ref, gffn_ref,
                  wr_hi_ref, wr_lo_ref, br_ref, x1_ref, h2_ref, route_ref, *, n_groups, per_group):
    ma = jnp.dot(ya_ref[...], wa_ref[...], preferred_element_type=F32)
    md = jnp.dot(yd_ref[...], wd_ref[...], preferred_element_type=F32)
    merged = _sigmoid(ga_ref[...].astype(F32)) * ma + _sigmoid(gd_ref[...].astype(F32)) * md
    x1 = x_ref[...] + jnp.dot(merged.astype(BF16), wo_ref[...], preferred_element_type=F32)
    x1_ref[...] = x1
    h2 = x1 * lax.rsqrt(jnp.mean(x1 * x1, axis=-1, keepdims=True) + EPS) * gffn_ref[...]
    h2_ref[...] = h2.astype(BF16)

    h_hi = h2.astype(BF16)
    h_lo = (h2 - h_hi.astype(F32)).astype(BF16)
    logits = (jnp.dot(h_hi, wr_hi_ref[...], preferred_element_type=F32)
              + jnp.dot(h_hi, wr_lo_ref[...], preferred_element_type=F32)
              + jnp.dot(h_lo, wr_hi_ref[...], preferred_element_type=F32)) + br_ref[...]
    lane = lax.broadcasted_iota(jnp.int32, logits.shape, 1)
    lane_f = lane.astype(F32)
    big = float(LANE)

    def first_argmax(vals, vmax):
        return jnp.min(jnp.where(vals == vmax, lane_f, big), axis=-1, keepdims=True)

    gl = jnp.where(lane < n_groups, logits, -jnp.inf)
    gmax = jnp.max(gl, axis=-1, keepdims=True)
    g_idx = first_argmax(gl, gmax)
    g_w = 1.0 / jnp.sum(jnp.exp(gl - gmax), axis=-1, keepdims=True)
    e_lo = n_groups + g_idx * per_group
    in_group = (lane_f >= e_lo) & (lane_f < e_lo + per_group)
    el = jnp.where(in_group, logits, -jnp.inf)
    emax = jnp.max(el, axis=-1, keepdims=True)
    pe = jnp.exp(el - emax)
    pe = pe / jnp.sum(pe, axis=-1, keepdims=True)
    p1 = jnp.max(pe, axis=-1, keepdims=True)
    i1 = first_argmax(jnp.where(in_group, pe, -1.0), p1)
    rest = jnp.where(in_group & (lane_f != i1), pe, -1.0)
    p2 = jnp.max(rest, axis=-1, keepdims=True)
    i2 = first_argmax(rest, p2)
    denom = p1 + p2
    w1 = p1 / denom * g_w
    w2 = p2 / denom * g_w
    route = jnp.where(lane == 0, i1 - n_groups, 0.0)
    route = jnp.where(lane == 1, i2 - n_groups, route)
    route = jnp.where(lane == 2, w1, route)
    route = jnp.where(lane == 3, w2, route)
    route_ref[...] = route


def _merge(y_a, y_d, p, x2, wa, wd, wo, g_ffn, wr_hi, wr_lo, b_r, n_groups, per_group, tm):
    T, D = x2.shape
    va, vd = y_a.shape[1], y_d.shape[1]
    row = lambda w: pl.BlockSpec((tm, w), lambda i: (i, 0))
    return pl.pallas_call(
        functools.partial(_merge_kernel, n_groups=n_groups, per_group=per_group),
        grid=(T // tm,),
        in_specs=[row(va), row(vd),
                  pl.BlockSpec((tm, D), lambda i: (i, 0)),
                  pl.BlockSpec((tm, D), lambda i: (i, 1)),
                  row(D),
                  _const_spec((va, D)), _const_spec((vd, D)), _const_spec((D, D)), _const_spec((1, D)),
                  _const_spec((D, LANE)), _const_spec((D, LANE)), _const_spec((1, LANE))],
        out_specs=[row(D), row(D), row(LANE)],
        out_shape=[jax.ShapeDtypeStruct((T, D), F32),
                   jax.ShapeDtypeStruct((T, D), BF16),
                   jax.ShapeDtypeStruct((T, LANE), F32)],
        compiler_params=_cparams("parallel"),
        name="merge",
    )(y_a, y_d, p, p, x2, wa, wd, wo, g_ffn, wr_hi, wr_lo, b_r)


def _expert_kernel(tile_expert_ref, tile_flag_ref, *refs, n_parts, tiles_per_part):
    xs_refs = refs[:n_parts]
    cw_ref, wg_ref, wu_ref, wd_ref, o_ref, wg_s, wu_s, wd_s, x_s = refs[n_parts:]
    i = pl.program_id(0)
    flag = tile_flag_ref[i]

    @pl.when(flag == 2)
    def _():
        wg_s[...] = wg_ref[0].astype(BF16)
        wu_s[...] = wu_ref[0].astype(BF16)
        wd_s[...] = wd_ref[0].astype(BF16)

    for part in range(n_parts):
        @pl.when((flag > 0) & (i // tiles_per_part == part))
        def _(part=part):
            x_s[...] = xs_refs[part][...]

    @pl.when(flag > 0)
    def _():
        x = x_s[...]
        g = jnp.dot(x, wg_s[...], preferred_element_type=F32)
        u = jnp.dot(x, wu_s[...], preferred_element_type=F32)
        hid = g * _sigmoid(g) * u * cw_ref[...]
        o_ref[...] = jnp.dot(hid.astype(BF16), wd_s[...], preferred_element_type=F32).astype(o_ref.dtype)

    @pl.when(flag == 0)
    def _():
        o_ref[...] = jnp.zeros_like(o_ref)


def _experts(tile_expert, tile_flag, xs_parts, cw, w_gate, w_up, w_down, tm):
    n_parts = len(xs_parts)
    part_rows, D = xs_parts[0].shape
    n_rows = n_parts * part_rows
    E, _, Fd = w_gate.shape
    n_tiles = n_rows // tm
    tiles_per_part = part_rows // tm

    def part_spec(part):
        def index_map(i, te, tf):
            return (jnp.clip(i - part * tiles_per_part, 0, tiles_per_part - 1), 0)
        return pl.BlockSpec((tm, D), index_map)

    grid_spec = pltpu.PrefetchScalarGridSpec(
        num_scalar_prefetch=2,
        grid=(n_tiles,),
        in_specs=[part_spec(part) for part in range(n_parts)] + [
                  pl.BlockSpec((tm, 1), lambda i, te, tf: (i, 0)),
                  pl.BlockSpec((1, D, Fd), lambda i, te, tf: (te[i], 0, 0)),
                  pl.BlockSpec((1, D, Fd), lambda i, te, tf: (te[i], 0, 0)),
                  pl.BlockSpec((1, Fd, D), lambda i, te, tf: (te[i], 0, 0))],
        out_specs=pl.BlockSpec((tm, D), lambda i, te, tf: (i, 0)),
        scratch_shapes=[pltpu.VMEM((D, Fd), BF16), pltpu.VMEM((D, Fd), BF16), pltpu.VMEM((Fd, D), BF16),
                        pltpu.VMEM((tm, D), BF16)],
    )
    return pl.pallas_call(
        functools.partial(_expert_kernel, n_parts=n_parts, tiles_per_part=tiles_per_part),
        grid_spec=grid_spec,
        out_shape=jax.ShapeDtypeStruct((n_rows, D), BF16),
        compiler_params=_cparams("arbitrary"),
        name="experts",
    )(tile_expert, tile_flag, *xs_parts, cw, w_gate, w_up, w_down)


def _final_kernel(x1_ref, y0_ref, y1_ref, g_ref, o_ref):
    x = x1_ref[...] + (y0_ref[...].astype(F32) + y1_ref[...].astype(F32))
    o_ref[...] = x * lax.rsqrt(jnp.mean(x * x, axis=-1, keepdims=True) + EPS) * g_ref[...]


def _final(x1, y0, y1, g_final, tm):
    T, D = x1.shape
    row = pl.BlockSpec((tm, D), lambda i: (i, 0))
    return pl.pallas_call(
        _final_kernel,
        grid=(T // tm,),
        in_specs=[row, row, row, _const_spec((1, D))],
        out_specs=row,
        out_shape=jax.ShapeDtypeStruct((T, D), F32),
        compiler_params=_cparams("parallel"),
        name="final",
    )(x1, y0, y1, g_final)


def _route_tables(ids, weights, n_experts, tm):
    T = ids.shape[0]
    A = T * TOP_K
    flat_e = ids.reshape(A)
    onehot = (flat_e[:, None] == jnp.arange(n_experts, dtype=jnp.int32)[None, :]).astype(jnp.int32)
    rank = jnp.sum((jnp.cumsum(onehot, axis=0) - onehot) * onehot, axis=1)
    counts = jnp.sum(onehot, axis=0)
    padded = ((counts + tm - 1) // tm) * tm
    ends = jnp.cumsum(padded)
    starts = ends - padded
    pos = starts[flat_e] + rank
    n_tiles = A // tm + n_experts
    n_rows = n_tiles * tm
    slot_token = jnp.zeros((n_rows,), jnp.int32).at[pos].set(jnp.arange(A, dtype=jnp.int32) // TOP_K)
    slot_w = jnp.zeros((n_rows,), F32).at[pos].set(weights.reshape(A))
    tile_start = jnp.arange(n_tiles, dtype=jnp.int32) * tm
    tile_expert_raw = jnp.sum((ends[None, :] <= tile_start[:, None]).astype(jnp.int32), axis=1)
    valid = tile_start < ends[-1]
    last_expert = jnp.max(jnp.where(counts > 0, jnp.arange(n_experts, dtype=jnp.int32), 0))
    tile_expert = jnp.where(valid, jnp.minimum(tile_expert_raw, n_experts - 1), last_expert)
    first = jnp.concatenate([jnp.ones((1,), bool), tile_expert[1:] != tile_expert[:-1]])
    tile_flag = jnp.where(valid, jnp.where(first, 2, 1), 0).astype(jnp.int32)
    return pos.reshape(T, TOP_K), slot_token, slot_w, tile_expert, tile_flag


def _largest_tile(n, cap):
    t = min(n, cap)
    while n % t:
        t //= 2
    return t


def kernel(x, g_mix, w_in, conv_w, a_log, dt_bias, gdn_norm_w, diff_lambda, diff_norm_w, w_branch_a, w_branch_d,
           w_out, g_ffn, w_group, b_group, w_router, b_router, w_exp_gate, w_exp_up, w_exp_down, g_final):
    B, S, D = x.shape
    T = B * S
    depth = g_mix.shape[0]
    Hg = a_log.shape[-1]
    gdn_qk = Hg * GDN_DK
    gdn_v = Hg * GDN_DV
    diff_v = w_branch_d.shape[1]
    Hd = diff_v // DIFF_DV
    diff_qk = Hd * 2 * DIFF_DH
    n_groups = w_group.shape[-1]
    n_experts = w_router.shape[-1]
    per_group = n_experts // n_groups
    assert 4 * Hg <= LANE and n_groups + n_experts <= LANE
    assert S % GDN_CHUNK == 0 and D % LANE == 0

    ab_lo = 2 * gdn_qk + 2 * gdn_v
    ab_hi = ab_lo + 4 * Hg
    gates_lo = ab_hi + 2 * diff_qk + diff_v
    main_cols = 2 * D + ab_lo + 2 * diff_qk + diff_v
    gdn_off = 2 * D
    q_d_off = gdn_off + ab_lo
    k_d_off = q_d_off + diff_qk
    v_d_off = k_d_off + diff_qk

    tm_norm = _largest_tile(T, 512)
    tm_proj = _largest_tile(T, 1024)
    tn_proj = _largest_tile(main_cols, 1024)
    tm_merge = _largest_tile(T, 256)
    tm_exp = _largest_tile(T * TOP_K, 256)
    tm_final = _largest_tile(T, 512)

    slopes = jnp.exp2(-8.0 * (jnp.arange(Hd, dtype=F32) + 1.0) / Hd)
    x2 = x.reshape(T, D)
    for layer in range(depth):
        lam_init = 0.8 - 0.6 * math.exp(-0.3 * layer)
        w_in_l = w_in[layer]
        w_main = jnp.concatenate([w_in_l[:, gates_lo:], w_in_l[:, :ab_lo], w_in_l[:, ab_hi:gates_lo]],
                                 axis=1).astype(BF16)
        w_ab = jnp.pad(w_in_l[:, ab_lo:ab_hi], ((0, 0), (0, LANE - 4 * Hg))).astype(BF16)
        alog_pad = jnp.pad(a_log[layer].reshape(1, 2 * Hg), ((0, 0), (0, LANE - 2 * Hg)))
        dtb_pad = jnp.pad(dt_bias[layer].reshape(1, 2 * Hg), ((0, 0), (0, LANE - 2 * Hg)))

        h, gc, beta = _norm_proj(x2, g_mix[layer].reshape(1, D), w_ab, alog_pad, dtb_pad, 2 * Hg, tm_norm)
        p = _matmul(h, w_main, tm_proj, tn_proj, BF16)

        grow = gc[:, :2 * Hg].reshape(B, S // GDN_CHUNK, GDN_CHUNK, 2 * Hg).transpose(0, 3, 1, 2)
        conv_w8 = jnp.pad(conv_w[layer], ((0, 8 - GDN_CONV), (0, 0)))
        y_a = _gdn(p, gdn_off // LANE, conv_w8, gc, grow, beta, gdn_norm_w[layer].reshape(1, GDN_DV), B, S, Hg)
        y_d = _diff_attn(p, slopes, diff_lambda[layer], diff_norm_w[layer].reshape(DIFF_DV, 1), B, S, Hd,
                         q_d_off // LANE, k_d_off // LANE, v_d_off // LANE, lam_init)

        w_r = jnp.pad(jnp.concatenate([w_group[layer], w_router[layer]], axis=1),
                      ((0, 0), (0, LANE - n_groups - n_experts)))
        wr_hi = w_r.astype(BF16)
        wr_lo = (w_r - wr_hi.astype(F32)).astype(BF16)
        b_r = jnp.pad(jnp.concatenate([b_group[layer], b_router[layer]]).reshape(1, -1),
                      ((0, 0), (0, LANE - n_groups - n_experts)))
        x1, h2, route = _merge(y_a, y_d, p, x2,
                               w_branch_a[layer].astype(BF16), w_branch_d[layer].astype(BF16),
                               w_out[layer].astype(BF16), g_ffn[layer].reshape(1, D),
                               wr_hi, wr_lo, b_r, n_groups, per_group, tm_merge)

        ids = route[:, 0:TOP_K].astype(jnp.int32)
        wts = route[:, TOP_K:2 * TOP_K]
        pos, slot_token, slot_w, tile_expert, tile_flag = _route_tables(ids, wts, n_experts, tm_exp)
        n_rows = slot_token.shape[0]
        part_rows = T if n_rows % T == 0 else n_rows
        xs_parts = [jnp.take(h2, slot_token[r:r + part_rows], axis=0) for r in range(0, n_rows, part_rows)]
        ys = _experts(tile_expert, tile_flag, xs_parts, slot_w[:, None], w_exp_gate[layer], w_exp_up[layer],
                      w_exp_down[layer], tm_exp)
        y0 = jnp.take(ys, pos[:, 0], axis=0)
        y1 = jnp.take(ys, pos[:, 1], axis=0)
        if layer + 1 < depth:
            x2 = x1 + (y0.astype(F32) + y1.astype(F32))
    out = _final(x1, y0, y1, g_final.reshape(1, D), tm_final)
    return out.reshape(B, S, D)
```

```python
import functools
import math

import jax
import jax.numpy as jnp
from jax import lax
from jax.experimental import pallas as pl
from jax.experimental.pallas import tpu as pltpu
from jax.experimental.pallas import tpu_sc as plsc

F32 = jnp.float32
BF16 = jnp.bfloat16
EPS = 1e-6
LANE = 128
GDN_DK = 128
GDN_DV = 128
GDN_CONV = 5
GDN_CHUNK = 64
DIFF_DH = 64
DIFF_DV = 2 * DIFF_DH
TOP_K = 2
LOG2E = 1.4426950408889634
NEG_BIG = -1e30
VMEM_LIMIT_BYTES = 56 * 1024 * 1024

_HI = lax.Precision.HIGHEST


def _cparams(*sem):
    return pltpu.CompilerParams(dimension_semantics=sem, vmem_limit_bytes=VMEM_LIMIT_BYTES)


def _const_spec(shape):
    nd = len(shape)
    return pl.BlockSpec(shape, lambda *_: (0,) * nd, pipeline_mode=pl.Buffered(1))


def _mm(a, b):
    return jnp.dot(a.astype(BF16), b.astype(BF16), preferred_element_type=F32)


def _split3(x):
    hi = x.astype(BF16)
    r = x - hi.astype(F32)
    mid = r.astype(BF16)
    lo = (r - mid.astype(F32)).astype(BF16)
    return hi, mid, lo


def _softplus(x):
    return jnp.maximum(x, 0.0) + jnp.log(1.0 + jnp.exp(-jnp.abs(x)))


def _sigmoid(x):
    return 1.0 / (1.0 + jnp.exp(-x))


def _norm_proj_kernel(x_ref, g_ref, wab_ref, alog_ref, dtb_ref, h_ref, gb_ref, *, n_dir_heads):
    x = x_ref[...]
    h = x * lax.rsqrt(jnp.mean(x * x, axis=-1, keepdims=True) + EPS) * g_ref[...]
    hb = h.astype(BF16)
    h_ref[...] = hb
    ab = jnp.dot(hb, wab_ref[...], preferred_element_type=F32)
    g = -jnp.exp(alog_ref[...]) * _softplus(ab + dtb_ref[...])
    beta = _sigmoid(ab)
    tm = x.shape[0]
    row = lax.broadcasted_iota(jnp.int32, (tm, tm), 0)
    col = lax.broadcasted_iota(jnp.int32, (tm, tm), 1)
    same = (row // GDN_CHUNK) == (col // GDN_CHUNK)
    prefix = jnp.where(same & (col <= row), 1.0, 0.0).astype(BF16)
    suffix = jnp.where(same & (col >= row), 1.0, 0.0).astype(BF16)
    pieces = _split3(g)
    cs_f = sum(jnp.dot(prefix, p, preferred_element_type=F32) for p in pieces)
    cs_b = sum(jnp.dot(suffix, p, preferred_element_type=F32) for p in pieces)
    lane = lax.broadcasted_iota(jnp.int32, g.shape, 1)
    gb_ref[...] = jnp.where(lane < n_dir_heads // 2, cs_f, jnp.where(lane < n_dir_heads, cs_b, beta))


def _norm_proj(x2, g_mix, w_ab, alog_pad, dtb_pad, n_dir_heads, tm):
    T, D = x2.shape
    return pl.pallas_call(
        functools.partial(_norm_proj_kernel, n_dir_heads=n_dir_heads),
        grid=(T // tm,),
        in_specs=[pl.BlockSpec((tm, D), lambda i: (i, 0)),
                  _const_spec((1, D)), _const_spec((D, LANE)), _const_spec((1, LANE)), _const_spec((1, LANE))],
        out_specs=[pl.BlockSpec((tm, D), lambda i: (i, 0)),
                   pl.BlockSpec((tm, LANE), lambda i: (i, 0))],
        out_shape=[jax.ShapeDtypeStruct((T, D), BF16),
                   jax.ShapeDtypeStruct((T, LANE), F32)],
        compiler_params=_cparams("parallel"),
        name="norm_proj",
    )(x2, g_mix, w_ab, alog_pad, dtb_pad)


def _matmul_kernel(a_ref, b_ref, o_ref):
    o_ref[...] = jnp.dot(a_ref[...], b_ref[...], preferred_element_type=F32).astype(o_ref.dtype)


def _matmul(a, b, tm, tn, out_dtype):
    M, K = a.shape
    N = b.shape[1]
    return pl.pallas_call(
        _matmul_kernel,
        grid=(N // tn, M // tm),
        in_specs=[pl.BlockSpec((tm, K), lambda j, i: (i, 0)),
                  pl.BlockSpec((K, tn), lambda j, i: (0, j))],
        out_specs=pl.BlockSpec((tm, tn), lambda j, i: (i, j)),
        out_shape=jax.ShapeDtypeStruct((M, N), out_dtype),
        compiler_params=_cparams("parallel", "parallel"),
        name="in_proj",
    )(a, b)


def _bmm_tn(a, b):
    return jnp.stack([lax.dot_general(a[n], b[n], (((0,), (0,)), ((), ())), preferred_element_type=F32)
                      for n in range(a.shape[0])])


def _gdn_kernel(q_ref, k_ref, v_ref, z_ref, cwq_ref, cwk_ref, cwv_ref, gb_ref, grow_ref, nw_ref,
                y_ref,
                xpad, qn, kn, vn, u_s, w_s, qg_s, at_s, tm_s, tn_s, egl_s, st_s,
                *, n_heads, seq, conv_rows, prep_chunks):
    C = GDN_CHUNK
    n_chunks = seq // C
    head = pl.program_id(1)
    pad = 8
    half = GDN_CONV // 2

    zeros_pad = jnp.zeros((pad, LANE), F32)
    xpad[0:pad, :] = zeros_pad
    xpad[pad + seq:pad + seq + pad, :] = zeros_pad
    for src, cw_ref, dst, mode in ((q_ref, cwq_ref, qn, "q"), (k_ref, cwk_ref, kn, "k"), (v_ref, cwv_ref, vn, "v")):
        xpad[pad:pad + seq, :] = src[...].astype(F32)
        for t in range(seq // conv_rows):
            r0 = t * conv_rows
            acc = jnp.zeros((conv_rows, LANE), F32)
            for j in range(GDN_CONV):
                lo = pad + r0 + j - half
                acc = acc + xpad[lo:lo + conv_rows, :] * cw_ref[j:j + 1, :]
            y = acc * _sigmoid(acc)
            if mode != "v":
                y = y * lax.rsqrt(jnp.sum(y * y, axis=-1, keepdims=True) + EPS)
            if mode == "q":
                y = y * (GDN_DK ** -0.5)
            dst[r0:r0 + conv_rows, :] = y

    nb = prep_chunks
    R = nb * C
    nb2 = 2 * nb
    bi = lax.broadcasted_iota(jnp.int32, (nb2, C, C), 0)
    ii = lax.broadcasted_iota(jnp.int32, (nb2, C, C), 1)
    jj = lax.broadcasted_iota(jnp.int32, (nb2, C, C), 2)
    fwd = bi < nb
    incl = (fwd & (ii >= jj)) | (~fwd & (ii <= jj))
    strict = (fwd & (ii > jj)) | (~fwd & (ii < jj))
    eye = jnp.where(ii == jj, 1.0, 0.0).astype(F32)
    lane = lax.broadcasted_iota(jnp.int32, (R, LANE), 1)
    both = lambda x: jnp.concatenate([x, x], axis=0)

    def prep(t, carry):
        r0 = pl.multiple_of(t * R, R)
        n0 = pl.multiple_of(t * nb, nb)
        q3 = qn[pl.ds(r0, R), :].reshape(nb, C, LANE)
        k3 = kn[pl.ds(r0, R), :].reshape(nb, C, LANE)
        v3 = vn[pl.ds(r0, R), :].reshape(nb, C, LANE)
        gb_blk = gb_ref[pl.ds(r0, R), :]
        kq = jnp.concatenate([k3, q3], axis=1).astype(BF16)
        gram = jnp.einsum("nik,njk->nij", kq, k3.astype(BF16), preferred_element_type=F32)
        kk, qk = both(gram[:, :C, :]), both(gram[:, C:, :])
        gcb, bt, grow, gl = [], [], [], []
        for d in range(2):
            colidx = d * n_heads + head
            gcb_d = jnp.sum(jnp.where(lane == colidx, gb_blk, 0.0), axis=-1, keepdims=True).reshape(nb, C, 1)
            gcb.append(gcb_d)
            bt.append(jnp.sum(jnp.where(lane == 2 * n_heads + colidx, gb_blk, 0.0), axis=-1,
                              keepdims=True).reshape(nb, C, 1))
            grow.append(grow_ref[0, colidx, pl.ds(n0, nb), :].reshape(nb, 1, C))
            gl.append(gcb_d[:, C - 1:C, :] if d == 0 else gcb_d[:, 0:1, :])
        gcb, bt, grow, gl = (jnp.concatenate(x, axis=0) for x in (gcb, bt, grow, gl))
        k2, q2, v2 = both(k3), both(q3), both(v3)

        decay = jnp.exp(jnp.where(incl, gcb - grow, -jnp.inf))
        L = jnp.where(strict, bt * kk * decay, 0.0)
        attn = qk * decay
        ainv = eye - L
        P = L
        for _ in range(int(math.log2(C)) - 1):
            Pb = P.astype(BF16)
            P = jnp.einsum("nij,njk->nik", Pb, Pb, preferred_element_type=F32)
            ainv = ainv + jnp.einsum("nij,njk->nik", ainv.astype(BF16), P.astype(BF16),
                                     preferred_element_type=F32)
        eg = jnp.exp(gcb)
        rhs = jnp.concatenate([v2 * bt, k2 * (bt * eg)], axis=-1).astype(BF16)
        uw = jnp.einsum("nij,njd->nid", ainv.astype(BF16), rhs, preferred_element_type=F32).astype(BF16)
        kg = (k2 * jnp.exp(gl - gcb)).astype(BF16)
        trans = _bmm_tn(kg, uw)
        qg = (q2 * eg).astype(BF16)
        egl = jnp.broadcast_to(jnp.exp(gl).reshape(nb2, 1), (nb2, LANE))
        for d in range(2):
            sl = slice(d * nb, (d + 1) * nb)
            u_s[d, pl.ds(r0, R), :] = uw[sl, :, :LANE].reshape(R, LANE)
            w_s[d, pl.ds(r0, R), :] = uw[sl, :, LANE:].reshape(R, LANE)
            qg_s[d, pl.ds(r0, R), :] = qg[sl].reshape(R, LANE)
            at_s[d, pl.ds(r0, R), :] = attn[sl].reshape(R, C).astype(BF16)
            tn_s[d, pl.ds(n0 * GDN_DK, nb * GDN_DK), :] = trans[sl, :, :LANE].reshape(nb * GDN_DK, LANE).astype(BF16)
            tm_s[d, pl.ds(n0 * GDN_DK, nb * GDN_DK), :] = (-trans[sl, :, LANE:]).reshape(nb * GDN_DK, LANE).astype(BF16)
            egl_s[d, pl.ds(n0, nb), :] = egl[sl]
        return carry

    lax.fori_loop(0, n_chunks // nb, prep, 0)

    def chunk_step(d, n, state):
        r = pl.multiple_of(n * GDN_DK, GDN_DK)
        sb = state.astype(BF16)
        st_s[d, pl.ds(r, GDN_DK), :] = sb
        return (state * egl_s[d, pl.ds(n, 1), :]
                + jnp.dot(tm_s[d, pl.ds(r, GDN_DK), :], sb, preferred_element_type=F32)
                + tn_s[d, pl.ds(r, GDN_DK), :].astype(F32))

    def scan(n, carry):
        sf, sbw = carry
        return chunk_step(0, n, sf), chunk_step(1, n_chunks - 1 - n, sbw)

    zero_state = jnp.zeros((GDN_DK, GDN_DV), F32)
    lax.fori_loop(0, n_chunks, scan, (zero_state, zero_state))

    def emit(t, carry):
        r0 = pl.multiple_of(t * R, R)
        s0 = pl.multiple_of(t * nb * GDN_DK, nb * GDN_DK)
        two = lambda ref, start, rows: jnp.concatenate([ref[0, pl.ds(start, rows), :], ref[1, pl.ds(start, rows), :]])
        st = two(st_s, s0, nb * GDN_DK).reshape(nb2, GDN_DK, GDN_DV)
        wq = jnp.concatenate([two(w_s, r0, R).reshape(nb2, C, LANE), two(qg_s, r0, R).reshape(nb2, C, LANE)], axis=1)
        ws_qs = jnp.einsum("ncd,nde->nce", wq, st, preferred_element_type=F32)
        vnew = two(u_s, r0, R).reshape(nb2, C, LANE).astype(F32) - ws_qs[:, :C, :]
        o = ws_qs[:, C:, :] + jnp.einsum("ncs,nse->nce", two(at_s, r0, R).reshape(nb2, C, C), vnew.astype(BF16),
                                         preferred_element_type=F32)
        o = (o[:nb] + o[nb:]).reshape(R, LANE)
        on = o * lax.rsqrt(jnp.mean(o * o, axis=-1, keepdims=True) + EPS) * nw_ref[...]
        z = z_ref[pl.ds(r0, R), :].astype(F32)
        y_ref[pl.ds(r0, R), :] = (on * (z * _sigmoid(z))).astype(y_ref.dtype)
        return carry

    lax.fori_loop(0, n_chunks // nb, emit, 0)


def _gdn(p, p_off, conv_w8, gb, grow, norm_w, batch, seq, n_heads):
    T = batch * seq
    qk_blocks = n_heads
    n_chunks = seq // GDN_CHUNK
    conv_rows = min(512, seq)
    prep_chunks = min(16, n_chunks)
    blk = lambda off: pl.BlockSpec((seq, LANE), lambda b, h: (b, p_off + off + h))
    cw = lambda off: pl.BlockSpec((8, LANE), lambda b, h: (0, off + h))
    return pl.pallas_call(
        functools.partial(_gdn_kernel, n_heads=n_heads, seq=seq, conv_rows=conv_rows, prep_chunks=prep_chunks),
        grid=(batch, n_heads),
        in_specs=[blk(0), blk(qk_blocks), blk(2 * qk_blocks), blk(3 * qk_blocks),
                  cw(0), cw(qk_blocks), cw(2 * qk_blocks),
                  pl.BlockSpec((seq, LANE), lambda b, h: (b, 0), pipeline_mode=pl.Buffered(1)),
                  pl.BlockSpec((1, 2 * n_heads, n_chunks, GDN_CHUNK), lambda b, h: (b, 0, 0, 0),
                               pipeline_mode=pl.Buffered(1)),
                  _const_spec((1, LANE))],
        out_specs=pl.BlockSpec((seq, LANE), lambda b, h: (b, h)),
        out_shape=jax.ShapeDtypeStruct((T, n_heads * GDN_DV), BF16),
        scratch_shapes=[
            pltpu.VMEM((seq + 16, LANE), F32),
            pltpu.VMEM((seq, LANE), F32),
            pltpu.VMEM((seq, LANE), F32),
            pltpu.VMEM((seq, LANE), F32),
            pltpu.VMEM((2, seq, LANE), BF16),
            pltpu.VMEM((2, seq, LANE), BF16),
            pltpu.VMEM((2, seq, LANE), BF16),
            pltpu.VMEM((2, seq, GDN_CHUNK), BF16),
            pltpu.VMEM((2, n_chunks * GDN_DK, GDN_DV), BF16),
            pltpu.VMEM((2, n_chunks * GDN_DK, GDN_DV), BF16),
            pltpu.VMEM((2, max(n_chunks, 8), LANE), F32),
            pltpu.VMEM((2, n_chunks * GDN_DK, GDN_DV), BF16),
        ],
        compiler_params=_cparams("parallel", "parallel"),
        name="gdn",
    )(p, p, p, p, conv_w8, conv_w8, conv_w8, gb, grow, norm_w)


def _diff_attn_kernel(slopes_ref, q_ref, k_ref, v_ref, lam_ref, nw_ref, o_ref,
                      qT_s, vT_s, klo_s, khi_s, rel_s, s_s, p_s, acc_s,
                      *, seq, tq, tk, lam_init):
    head = pl.program_id(1)
    slope2 = slopes_ref[head] * LOG2E
    scale2 = (DIFF_DH ** -0.5) * LOG2E

    rows = min(512, seq)
    for t in range(seq // rows):
        r0 = t * rows
        qT_s[:, r0:r0 + rows] = (q_ref[r0:r0 + rows, :].astype(F32) * scale2).T.astype(BF16)
        vT_s[:, r0:r0 + rows] = v_ref[r0:r0 + rows, :].astype(F32).T.astype(BF16)
        kt = k_ref[r0:r0 + rows, :]
        lane = lax.broadcasted_iota(jnp.int32, kt.shape, 1)
        klo_s[r0:r0 + rows, :] = jnp.where(lane < DIFF_DH, kt, jnp.zeros_like(kt))
        khi_s[r0:r0 + rows, :] = jnp.where(lane >= DIFF_DH, kt, jnp.zeros_like(kt))
    kr = lax.broadcasted_iota(jnp.int32, (tk, tq), 0)
    qc = lax.broadcasted_iota(jnp.int32, (tk, tq), 1)
    rel_s[...] = (qc - kr).astype(F32) * slope2

    lf = lam_ref[...]
    lam = (jnp.exp(jnp.sum(lf[0:1, :] * lf[1:2, :], axis=-1, keepdims=True))
           - jnp.exp(jnp.sum(lf[2:3, :] * lf[3:4, :], axis=-1, keepdims=True)) + lam_init)

    n_kv = seq // tk
    maps = (klo_s, khi_s)

    def q_tile(i, carry):
        c_q = pl.multiple_of(i * tq, tq)
        qT = qT_s[:, pl.ds(c_q, tq)]
        q0 = jnp.asarray(i * tq).astype(F32) * slope2

        def scores(j):
            bias = jnp.abs(rel_s[...] + (q0 - (j * tk) * slope2))
            for mp, k_s in enumerate(maps):
                s_s[j % 2, mp] = jnp.dot(k_s[j * tk:(j + 1) * tk, :], qT, preferred_element_type=F32) - bias

        def weighted_values(j, alphas):
            vT = vT_s[:, j * tk:(j + 1) * tk]
            for mp in range(len(maps)):
                upd = jnp.dot(vT, p_s[j % 2, mp], preferred_element_type=F32)
                acc_s[mp] = upd if j == 0 else acc_s[mp] * alphas[mp] + upd

        m = [jnp.full((1, tq), NEG_BIG, F32) for _ in maps]
        l = [jnp.zeros((1, tq), F32) for _ in maps]
        scores(0)
        alphas = None
        for j in range(n_kv):
            if j + 1 < n_kv:
                scores(j + 1)
            new_alphas = []
            for mp in range(len(maps)):
                s = s_s[j % 2, mp]
                m_new = jnp.maximum(m[mp], jnp.max(s, axis=0, keepdims=True))
                alpha = jnp.exp2(m[mp] - m_new)
                p = jnp.exp2(s - m_new)
                l[mp] = alpha * l[mp] + jnp.sum(p, axis=0, keepdims=True)
                m[mp] = m_new
                p_s[j % 2, mp] = p.astype(BF16)
                new_alphas.append(alpha)
            if j > 0:
                weighted_values(j - 1, alphas)
            alphas = new_alphas
        weighted_values(n_kv - 1, alphas)

        od = acc_s[0] / l[0] - lam * (acc_s[1] / l[1])
        yn = od * lax.rsqrt(jnp.mean(od * od, axis=0, keepdims=True) + EPS) * nw_ref[...] * (1.0 - lam_init)
        o_ref[pl.ds(c_q, tq), :] = yn.T.astype(o_ref.dtype)
        return carry

    lax.fori_loop(0, seq // tq, q_tile, 0)


def _diff_attn(p, slopes, diff_lambda, norm_w_col, batch, seq, n_heads, q_off, k_off, v_off, lam_init):
    T = batch * seq
    tq = min(256, seq)
    tk = min(512, seq)
    blk = lambda off: pl.BlockSpec((seq, LANE), lambda b, h: (b, off + h))
    return pl.pallas_call(
        functools.partial(_diff_attn_kernel, seq=seq, tq=tq, tk=tk, lam_init=lam_init),
        grid=(batch, n_heads),
        in_specs=[pl.BlockSpec(memory_space=pltpu.SMEM),
                  blk(q_off), blk(k_off), blk(v_off),
                  _const_spec((4, DIFF_DH)), _const_spec((DIFF_DV, 1))],
        out_specs=pl.BlockSpec((seq, LANE), lambda b, h: (b, h)),
        out_shape=jax.ShapeDtypeStruct((T, n_heads * DIFF_DV), BF16),
        scratch_shapes=[
            pltpu.VMEM((2 * DIFF_DH, seq), BF16),
            pltpu.VMEM((DIFF_DV, seq), BF16),
            pltpu.VMEM((seq, LANE), BF16),
            pltpu.VMEM((seq, LANE), BF16),
            pltpu.VMEM((tk, tq), F32),
            pltpu.VMEM((2, 2, tk, tq), F32),
            pltpu.VMEM((2, 2, tk, tq), BF16),
            pltpu.VMEM((2, DIFF_DV, tq), F32),
        ],
        compiler_params=_cparams("parallel", "parallel"),
        name="diff_attn",
    )(slopes, p, p, p, diff_lambda, norm_w_col)


def _merge_kernel(ya_ref, yd_ref, ga_ref, gd_ref, x_ref, wa_ref, wd_ref, wo_ref, gffn_ref,
                  wr_hi_ref, wr_lo_ref, br_ref, x1_ref, h2_ref, route_ref, *, n_groups, per_group):
    ma = jnp.dot(ya_ref[...], wa_ref[...], preferred_element_type=F32)
    md = jnp.dot(yd_ref[...], wd_ref[...], preferred_element_type=F32)
    merged = _sigmoid(ga_ref[...].astype(F32)) * ma + _sigmoid(gd_ref[...].astype(F32)) * md
    x1 = x_ref[...] + jnp.dot(merged.astype(BF16), wo_ref[...], preferred_element_type=F32)
    x1_ref[...] = x1
    h2 = x1 * lax.rsqrt(jnp.mean(x1 * x1, axis=-1, keepdims=True) + EPS) * gffn_ref[...]
    bits = lax.bitcast_convert_type(h2.astype(BF16).astype(F32), jnp.uint32)
    half = bits.shape[1] // 2
    h2_ref[...] = bits[:, :half] | (bits[:, half:] >> 16)

    h_hi = h2.astype(BF16)
    h_lo = (h2 - h_hi.astype(F32)).astype(BF16)
    logits = (jnp.dot(h_hi, wr_hi_ref[...], preferred_element_type=F32)
              + jnp.dot(h_hi, wr_lo_ref[...], preferred_element_type=F32)
              + jnp.dot(h_lo, wr_hi_ref[...], preferred_element_type=F32)) + br_ref[...]
    lane = lax.broadcasted_iota(jnp.int32, logits.shape, 1)
    lane_f = lane.astype(F32)
    big = float(LANE)

    def first_argmax(vals, vmax):
        return jnp.min(jnp.where(vals == vmax, lane_f, big), axis=-1, keepdims=True)

    gl = jnp.where(lane < n_groups, logits, -jnp.inf)
    gmax = jnp.max(gl, axis=-1, keepdims=True)
    g_idx = first_argmax(gl, gmax)
    g_w = 1.0 / jnp.sum(jnp.exp(gl - gmax), axis=-1, keepdims=True)
    e_lo = n_groups + g_idx * per_group
    in_group = (lane_f >= e_lo) & (lane_f < e_lo + per_group)
    el = jnp.where(in_group, logits, -jnp.inf)
    emax = jnp.max(el, axis=-1, keepdims=True)
    pe = jnp.exp(el - emax)
    pe = pe / jnp.sum(pe, axis=-1, keepdims=True)
    p1 = jnp.max(pe, axis=-1, keepdims=True)
    i1 = first_argmax(jnp.where(in_group, pe, -1.0), p1)
    rest = jnp.where(in_group & (lane_f != i1), pe, -1.0)
    p2 = jnp.max(rest, axis=-1, keepdims=True)
    i2 = first_argmax(rest, p2)
    denom = p1 + p2
    w1 = p1 / denom * g_w
    w2 = p2 / denom * g_w
    route = jnp.where(lane == 0, i1 - n_groups, 0.0)
    route = jnp.where(lane == 1, i2 - n_groups, route)
    route = jnp.where(lane == 2, w1, route)
    route = jnp.where(lane == 3, w2, route)
    route_ref[...] = route


def _merge(y_a, y_d, p, x2, wa, wd, wo, g_ffn, wr_hi, wr_lo, b_r, n_groups, per_group, tm):
    T, D = x2.shape
    va, vd = y_a.shape[1], y_d.shape[1]
    row = lambda w: pl.BlockSpec((tm, w), lambda i: (i, 0))
    return pl.pallas_call(
        functools.partial(_merge_kernel, n_groups=n_groups, per_group=per_group),
        grid=(T // tm,),
        in_specs=[row(va), row(vd),
                  pl.BlockSpec((tm, D), lambda i: (i, 0)),
                  pl.BlockSpec((tm, D), lambda i: (i, 1)),
                  row(D),
                  _const_spec((va, D)), _const_spec((vd, D)), _const_spec((D, D)), _const_spec((1, D)),
                  _const_spec((D, LANE)), _const_spec((D, LANE)), _const_spec((1, LANE))],
        out_specs=[row(D), row(D // 2), row(LANE)],
        out_shape=[jax.ShapeDtypeStruct((T, D), F32),
                   jax.ShapeDtypeStruct((T, D // 2), jnp.uint32),
                   jax.ShapeDtypeStruct((T, LANE), F32)],
        compiler_params=_cparams("parallel"),
        name="merge",
    )(y_a, y_d, p, p, x2, wa, wd, wo, g_ffn, wr_hi, wr_lo, b_r)


def _expert_kernel(tile_expert_ref, tile_flag_ref, xs_ref, cw_ref, wg_ref, wu_ref, wd_ref, o_ref,
                   wg_s, wu_s, wd_s):
    i = pl.program_id(0)
    flag = tile_flag_ref[i]

    @pl.when(flag == 2)
    def _():
        wg_s[...] = wg_ref[0].astype(BF16)
        wu_s[...] = wu_ref[0].astype(BF16)
        wd_s[...] = wd_ref[0].astype(BF16)

    @pl.when(flag > 0)
    def _():
        packed = xs_ref[...]
        left = lax.bitcast_convert_type(packed & jnp.uint32(0xFFFF0000), F32).astype(BF16)
        right = lax.bitcast_convert_type(packed << 16, F32).astype(BF16)
        x = jnp.concatenate([left, right], axis=1)
        g = jnp.dot(x, wg_s[...], preferred_element_type=F32)
        u = jnp.dot(x, wu_s[...], preferred_element_type=F32)
        hid = g * _sigmoid(g) * u * cw_ref[...]
        o_ref[...] = jnp.dot(hid.astype(BF16), wd_s[...], preferred_element_type=F32).astype(o_ref.dtype)

    @pl.when(flag == 0)
    def _():
        o_ref[...] = jnp.zeros_like(o_ref)


def _experts(tile_expert, tile_flag, xs, cw, w_gate, w_up, w_down, tm):
    n_rows, half = xs.shape
    E, D, Fd = w_gate.shape
    assert D == 2 * half
    n_tiles = n_rows // tm
    grid_spec = pltpu.PrefetchScalarGridSpec(
        num_scalar_prefetch=2,
        grid=(n_tiles,),
        in_specs=[pl.BlockSpec((tm, half), lambda i, te, tf: (i, 0)),
                  pl.BlockSpec((tm, 1), lambda i, te, tf: (i, 0)),
                  pl.BlockSpec((1, D, Fd), lambda i, te, tf: (te[i], 0, 0)),
                  pl.BlockSpec((1, D, Fd), lambda i, te, tf: (te[i], 0, 0)),
                  pl.BlockSpec((1, Fd, D), lambda i, te, tf: (te[i], 0, 0))],
        out_specs=pl.BlockSpec((tm, D), lambda i, te, tf: (i, 0)),
        scratch_shapes=[pltpu.VMEM((D, Fd), BF16), pltpu.VMEM((D, Fd), BF16), pltpu.VMEM((Fd, D), BF16)],
    )
    return pl.pallas_call(
        _expert_kernel,
        grid_spec=grid_spec,
        out_shape=jax.ShapeDtypeStruct((n_rows, D), BF16),
        compiler_params=_cparams("arbitrary"),
        name="experts",
    )(tile_expert, tile_flag, xs, cw, w_gate, w_up, w_down)


SC_CORES = 2
SC_SUBCORES = 16
SC_GATHER_ROWS = 32


def _row_gather(table, idx):
    V, W = table.shape
    n_rows = idx.shape[0]
    workers = SC_CORES * SC_SUBCORES
    rows_per_worker = n_rows // workers
    chunks = rows_per_worker // SC_GATHER_ROWS
    assert workers * chunks * SC_GATHER_ROWS == n_rows
    idx3 = idx.reshape(workers, chunks, SC_GATHER_ROWS)
    mesh = plsc.VectorSubcoreMesh(core_axis_name="c", subcore_axis_name="s")

    @functools.partial(
        pl.kernel, mesh=mesh,
        out_type=jax.ShapeDtypeStruct((n_rows, W), table.dtype),
        scratch_types=[pltpu.VMEM((chunks, SC_GATHER_ROWS), jnp.int32),
                       pltpu.VMEM((SC_GATHER_ROWS, W), table.dtype),
                       pltpu.SemaphoreType.DMA],
        name="row_gather",
    )
    def gather(table_hbm, idx_hbm, out_hbm, idx_v, rows_v, sem):
        wid = lax.axis_index("s") * SC_CORES + lax.axis_index("c")
        base = wid * rows_per_worker
        pltpu.sync_copy(idx_hbm.at[wid], idx_v)

        @pl.loop(0, chunks)
        def _(c):
            pltpu.async_copy(table_hbm.at[idx_v.at[c]], rows_v, sem).wait()
            pltpu.sync_copy(rows_v, out_hbm.at[pl.ds(base + c * SC_GATHER_ROWS, SC_GATHER_ROWS)])

    return gather(table, idx3)


def _final_kernel(x1_ref, y0_ref, y1_ref, g_ref, o_ref):
    x = x1_ref[...] + (y0_ref[...].astype(F32) + y1_ref[...].astype(F32))
    o_ref[...] = x * lax.rsqrt(jnp.mean(x * x, axis=-1, keepdims=True) + EPS) * g_ref[...]


def _final(x1, y0, y1, g_final, tm):
    T, D = x1.shape
    row = pl.BlockSpec((tm, D), lambda i: (i, 0))
    return pl.pallas_call(
        _final_kernel,
        grid=(T // tm,),
        in_specs=[row, row, row, _const_spec((1, D))],
        out_specs=row,
        out_shape=jax.ShapeDtypeStruct((T, D), F32),
        compiler_params=_cparams("parallel"),
        name="final",
    )(x1, y0, y1, g_final)


def _route_tables(ids, weights, n_experts, tm):
    T = ids.shape[0]
    A = T * TOP_K
    flat_e = ids.reshape(A)
    onehot = (flat_e[:, None] == jnp.arange(n_experts, dtype=jnp.int32)[None, :]).astype(jnp.int32)
    rank = jnp.sum((jnp.cumsum(onehot, axis=0) - onehot) * onehot, axis=1)
    counts = jnp.sum(onehot, axis=0)
    padded = ((counts + tm - 1) // tm) * tm
    ends = jnp.cumsum(padded)
    starts = ends - padded
    pos = starts[flat_e] + rank
    n_tiles = A // tm + n_experts
    n_rows = n_tiles * tm
    slot_token = jnp.zeros((n_rows,), jnp.int32).at[pos].set(jnp.arange(A, dtype=jnp.int32) // TOP_K)
    slot_w = jnp.zeros((n_rows,), F32).at[pos].set(weights.reshape(A))
    tile_start = jnp.arange(n_tiles, dtype=jnp.int32) * tm
    tile_expert_raw = jnp.sum((ends[None, :] <= tile_start[:, None]).astype(jnp.int32), axis=1)
    valid = tile_start < ends[-1]
    last_expert = jnp.max(jnp.where(counts > 0, jnp.arange(n_experts, dtype=jnp.int32), 0))
    tile_expert = jnp.where(valid, jnp.minimum(tile_expert_raw, n_experts - 1), last_expert)
    first = jnp.concatenate([jnp.ones((1,), bool), tile_expert[1:] != tile_expert[:-1]])
    tile_flag = jnp.where(valid, jnp.where(first, 2, 1), 0).astype(jnp.int32)
    return pos.reshape(T, TOP_K), slot_token, slot_w, tile_expert, tile_flag


def _largest_tile(n, cap):
    t = min(n, cap)
    while n % t:
        t //= 2
    return t


def kernel(x, g_mix, w_in, conv_w, a_log, dt_bias, gdn_norm_w, diff_lambda, diff_norm_w, w_branch_a, w_branch_d,
           w_out, g_ffn, w_group, b_group, w_router, b_router, w_exp_gate, w_exp_up, w_exp_down, g_final):
    B, S, D = x.shape
    T = B * S
    depth = g_mix.shape[0]
    Hg = a_log.shape[-1]
    gdn_qk = Hg * GDN_DK
    gdn_v = Hg * GDN_DV
    diff_v = w_branch_d.shape[1]
    Hd = diff_v // DIFF_DV
    diff_qk = Hd * 2 * DIFF_DH
    n_groups = w_group.shape[-1]
    n_experts = w_router.shape[-1]
    per_group = n_experts // n_groups
    assert 4 * Hg <= LANE and n_groups + n_experts <= LANE
    assert S % GDN_CHUNK == 0 and D % LANE == 0

    ab_lo = 2 * gdn_qk + 2 * gdn_v
    ab_hi = ab_lo + 4 * Hg
    gates_lo = ab_hi + 2 * diff_qk + diff_v
    main_cols = 2 * D + ab_lo + 2 * diff_qk + diff_v
    gdn_off = 2 * D
    q_d_off = gdn_off + ab_lo
    k_d_off = q_d_off + diff_qk
    v_d_off = k_d_off + diff_qk

    tm_norm = _largest_tile(T, 512)
    tm_proj = _largest_tile(T, 1024)
    tn_proj = _largest_tile(main_cols, 1024)
    tm_merge = _largest_tile(T, 256)
    tm_exp = _largest_tile(T * TOP_K, 256)
    tm_final = _largest_tile(T, 512)

    slopes = jnp.exp2(-8.0 * (jnp.arange(Hd, dtype=F32) + 1.0) / Hd)
    x2 = x.reshape(T, D)
    for layer in range(depth):
        lam_init = 0.8 - 0.6 * math.exp(-0.3 * layer)
        w_in_l = w_in[layer]
        w_main = jnp.concatenate([w_in_l[:, gates_lo:], w_in_l[:, :ab_lo], w_in_l[:, ab_hi:gates_lo]],
                                 axis=1).astype(BF16)
        w_ab = jnp.pad(w_in_l[:, ab_lo:ab_hi], ((0, 0), (0, LANE - 4 * Hg))).astype(BF16)
        alog_pad = jnp.pad(a_log[layer].reshape(1, 2 * Hg), ((0, 0), (0, LANE - 2 * Hg)))
        dtb_pad = jnp.pad(dt_bias[layer].reshape(1, 2 * Hg), ((0, 0), (0, LANE - 2 * Hg)))

        h, gb = _norm_proj(x2, g_mix[layer].reshape(1, D), w_ab, alog_pad, dtb_pad, 2 * Hg, tm_norm)
        p = _matmul(h, w_main, tm_proj, tn_proj, BF16)

        grow = gb[:, :2 * Hg].reshape(B, S // GDN_CHUNK, GDN_CHUNK, 2 * Hg).transpose(0, 3, 1, 2)
        conv_w8 = jnp.pad(conv_w[layer], ((0, 8 - GDN_CONV), (0, 0)))
        y_a = _gdn(p, gdn_off // LANE, conv_w8, gb, grow, gdn_norm_w[layer].reshape(1, GDN_DV), B, S, Hg)
        y_d = _diff_attn(p, slopes, diff_lambda[layer], diff_norm_w[layer].reshape(DIFF_DV, 1), B, S, Hd,
                         q_d_off // LANE, k_d_off // LANE, v_d_off // LANE, lam_init)

        w_r = jnp.pad(jnp.concatenate([w_group[layer], w_router[layer]], axis=1),
                      ((0, 0), (0, LANE - n_groups - n_experts)))
        wr_hi = w_r.astype(BF16)
        wr_lo = (w_r - wr_hi.astype(F32)).astype(BF16)
        b_r = jnp.pad(jnp.concatenate([b_group[layer], b_router[layer]]).reshape(1, -1),
                      ((0, 0), (0, LANE - n_groups - n_experts)))
        x1, h2, route = _merge(y_a, y_d, p, x2,
                               w_branch_a[layer].astype(BF16), w_branch_d[layer].astype(BF16),
                               w_out[layer].astype(BF16), g_ffn[layer].reshape(1, D),
                               wr_hi, wr_lo, b_r, n_groups, per_group, tm_merge)

        ids = route[:, 0:TOP_K].astype(jnp.int32)
        wts = route[:, TOP_K:2 * TOP_K]
        pos, slot_token, slot_w, tile_expert, tile_flag = _route_tables(ids, wts, n_experts, tm_exp)
        xs = _row_gather(h2, slot_token)
        ys = _experts(tile_expert, tile_flag, xs, slot_w[:, None], w_exp_gate[layer], w_exp_up[layer],
                      w_exp_down[layer], tm_exp)
        y0 = jnp.take(ys, pos[:, 0], axis=0)
        y1 = jnp.take(ys, pos[:, 1], axis=0)
        if layer + 1 < depth:
            x2 = x1 + (y0.astype(F32) + y1.astype(F32))
    out = _final(x1, y0, y1, g_final.reshape(1, D), tm_final)
    return out.reshape(B, S, D)
```

```python
import functools
import math

import jax
import jax.numpy as jnp
from jax import lax
from jax.experimental import pallas as pl
from jax.experimental.pallas import tpu as pltpu
from jax.experimental.pallas import tpu_sc as plsc

F32 = jnp.float32
BF16 = jnp.bfloat16
EPS = 1e-6
LANE = 128
GDN_DK = 128
GDN_DV = 128
GDN_CONV = 5
GDN_CHUNK = 64
DIFF_DH = 64
DIFF_DV = 2 * DIFF_DH
TOP_K = 2
LOG2E = 1.4426950408889634
NEG_BIG = -1e30
VMEM_LIMIT_BYTES = 56 * 1024 * 1024

_HI = lax.Precision.HIGHEST


def _cparams(*sem):
    return pltpu.CompilerParams(dimension_semantics=sem, vmem_limit_bytes=VMEM_LIMIT_BYTES)


def _const_spec(shape):
    nd = len(shape)
    return pl.BlockSpec(shape, lambda *_: (0,) * nd, pipeline_mode=pl.Buffered(1))


def _mm(a, b):
    return jnp.dot(a.astype(BF16), b.astype(BF16), preferred_element_type=F32)


def _split3(x):
    hi = x.astype(BF16)
    r = x - hi.astype(F32)
    mid = r.astype(BF16)
    lo = (r - mid.astype(F32)).astype(BF16)
    return hi, mid, lo


def _softplus(x):
    return jnp.maximum(x, 0.0) + jnp.log(1.0 + jnp.exp(-jnp.abs(x)))


def _sigmoid(x):
    return 1.0 / (1.0 + jnp.exp(-x))


def _norm_proj_kernel(x_ref, g_ref, wab_ref, alog_ref, dtb_ref, h_ref, gb_ref, *, n_dir_heads):
    x = x_ref[...]
    h = x * lax.rsqrt(jnp.mean(x * x, axis=-1, keepdims=True) + EPS) * g_ref[...]
    hb = h.astype(BF16)
    h_ref[...] = hb
    ab = jnp.dot(hb, wab_ref[...], preferred_element_type=F32)
    g = -jnp.exp(alog_ref[...]) * _softplus(ab + dtb_ref[...])
    beta = _sigmoid(ab)
    tm = x.shape[0]
    row = lax.broadcasted_iota(jnp.int32, (tm, tm), 0)
    col = lax.broadcasted_iota(jnp.int32, (tm, tm), 1)
    same = (row // GDN_CHUNK) == (col // GDN_CHUNK)
    prefix = jnp.where(same & (col <= row), 1.0, 0.0).astype(BF16)
    suffix = jnp.where(same & (col >= row), 1.0, 0.0).astype(BF16)
    pieces = _split3(g)
    cs_f = sum(jnp.dot(prefix, p, preferred_element_type=F32) for p in pieces)
    cs_b = sum(jnp.dot(suffix, p, preferred_element_type=F32) for p in pieces)
    lane = lax.broadcasted_iota(jnp.int32, g.shape, 1)
    gb_ref[...] = jnp.where(lane < n_dir_heads // 2, cs_f, jnp.where(lane < n_dir_heads, cs_b, beta))


def _norm_proj(x2, g_mix, w_ab, alog_pad, dtb_pad, n_dir_heads, tm):
    T, D = x2.shape
    return pl.pallas_call(
        functools.partial(_norm_proj_kernel, n_dir_heads=n_dir_heads),
        grid=(T // tm,),
        in_specs=[pl.BlockSpec((tm, D), lambda i: (i, 0)),
                  _const_spec((1, D)), _const_spec((D, LANE)), _const_spec((1, LANE)), _const_spec((1, LANE))],
        out_specs=[pl.BlockSpec((tm, D), lambda i: (i, 0)),
                   pl.BlockSpec((tm, LANE), lambda i: (i, 0))],
        out_shape=[jax.ShapeDtypeStruct((T, D), BF16),
                   jax.ShapeDtypeStruct((T, LANE), F32)],
        compiler_params=_cparams("parallel"),
        name="norm_proj",
    )(x2, g_mix, w_ab, alog_pad, dtb_pad)


def _matmul_kernel(a_ref, b_ref, o_ref):
    o_ref[...] = jnp.dot(a_ref[...], b_ref[...], preferred_element_type=F32).astype(o_ref.dtype)


def _matmul(a, b, tm, tn, out_dtype):
    M, K = a.shape
    N = b.shape[1]
    return pl.pallas_call(
        _matmul_kernel,
        grid=(N // tn, M // tm),
        in_specs=[pl.BlockSpec((tm, K), lambda j, i: (i, 0)),
                  pl.BlockSpec((K, tn), lambda j, i: (0, j))],
        out_specs=pl.BlockSpec((tm, tn), lambda j, i: (i, j)),
        out_shape=jax.ShapeDtypeStruct((M, N), out_dtype),
        compiler_params=_cparams("parallel", "parallel"),
        name="in_proj",
    )(a, b)


def _bmm_tn(a, b):
    return jnp.stack([lax.dot_general(a[n], b[n], (((0,), (0,)), ((), ())), preferred_element_type=F32)
                      for n in range(a.shape[0])])


def _gdn_kernel(q_ref, k_ref, v_ref, z_ref, cwq_ref, cwk_ref, cwv_ref, gb_ref, grow_ref, nw_ref,
                y_ref,
                xpad, qn, kn, vn, u_s, w_s, qg_s, at_s, tm_s, tn_s, egl_s, st_s,
                *, n_heads, seq, conv_rows, prep_chunks):
    C = GDN_CHUNK
    n_chunks = seq // C
    head = pl.program_id(1)
    pad = 8
    half = GDN_CONV // 2

    zeros_pad = jnp.zeros((pad, LANE), F32)
    xpad[0:pad, :] = zeros_pad
    xpad[pad + seq:pad + seq + pad, :] = zeros_pad
    for src, cw_ref, dst, mode in ((q_ref, cwq_ref, qn, "q"), (k_ref, cwk_ref, kn, "k"), (v_ref, cwv_ref, vn, "v")):
        xpad[pad:pad + seq, :] = src[...].astype(F32)
        for t in range(seq // conv_rows):
            r0 = t * conv_rows
            acc = jnp.zeros((conv_rows, LANE), F32)
            for j in range(GDN_CONV):
                lo = pad + r0 + j - half
                acc = acc + xpad[lo:lo + conv_rows, :] * cw_ref[j:j + 1, :]
            y = acc * _sigmoid(acc)
            if mode != "v":
                y = y * lax.rsqrt(jnp.sum(y * y, axis=-1, keepdims=True) + EPS)
            if mode == "q":
                y = y * (GDN_DK ** -0.5)
            dst[r0:r0 + conv_rows, :] = y

    nb = prep_chunks
    R = nb * C
    nb2 = 2 * nb
    bi = lax.broadcasted_iota(jnp.int32, (nb2, C, C), 0)
    ii = lax.broadcasted_iota(jnp.int32, (nb2, C, C), 1)
    jj = lax.broadcasted_iota(jnp.int32, (nb2, C, C), 2)
    fwd = bi < nb
    incl = (fwd & (ii >= jj)) | (~fwd & (ii <= jj))
    strict = (fwd & (ii > jj)) | (~fwd & (ii < jj))
    eye = jnp.where(ii == jj, 1.0, 0.0).astype(F32)
    lane = lax.broadcasted_iota(jnp.int32, (R, LANE), 1)
    both = lambda x: jnp.concatenate([x, x], axis=0)

    def prep(t, carry):
        r0 = pl.multiple_of(t * R, R)
        n0 = pl.multiple_of(t * nb, nb)
        q3 = qn[pl.ds(r0, R), :].reshape(nb, C, LANE)
        k3 = kn[pl.ds(r0, R), :].reshape(nb, C, LANE)
        v3 = vn[pl.ds(r0, R), :].reshape(nb, C, LANE)
        gb_blk = gb_ref[pl.ds(r0, R), :]
        kq = jnp.concatenate([k3, q3], axis=1).astype(BF16)
        gram = jnp.einsum("nik,njk->nij", kq, k3.astype(BF16), preferred_element_type=F32)
        kk, qk = both(gram[:, :C, :]), both(gram[:, C:, :])
        gcb, bt, grow, gl = [], [], [], []
        for d in range(2):
            colidx = d * n_heads + head
            gcb_d = jnp.sum(jnp.where(lane == colidx, gb_blk, 0.0), axis=-1, keepdims=True).reshape(nb, C, 1)
            gcb.append(gcb_d)
            bt.append(jnp.sum(jnp.where(lane == 2 * n_heads + colidx, gb_blk, 0.0), axis=-1,
                              keepdims=True).reshape(nb, C, 1))
            grow.append(grow_ref[0, colidx, pl.ds(n0, nb), :].reshape(nb, 1, C))
            gl.append(gcb_d[:, C - 1:C, :] if d == 0 else gcb_d[:, 0:1, :])
        gcb, bt, grow, gl = (jnp.concatenate(x, axis=0) for x in (gcb, bt, grow, gl))
        k2, q2, v2 = both(k3), both(q3), both(v3)

        decay = jnp.exp(jnp.where(incl, gcb - grow, -jnp.inf))
        L = jnp.where(strict, bt * kk * decay, 0.0)
        attn = qk * decay
        ainv = eye - L
        P = L
        for _ in range(int(math.log2(C)) - 1):
            Pb = P.astype(BF16)
            P = jnp.einsum("nij,njk->nik", Pb, Pb, preferred_element_type=F32)
            ainv = ainv + jnp.einsum("nij,njk->nik", ainv.astype(BF16), P.astype(BF16),
                                     preferred_element_type=F32)
        eg = jnp.exp(gcb)
        rhs = jnp.concatenate([v2 * bt, k2 * (bt * eg)], axis=-1).astype(BF16)
        uw = jnp.einsum("nij,njd->nid", ainv.astype(BF16), rhs, preferred_element_type=F32).astype(BF16)
        kg = (k2 * jnp.exp(gl - gcb)).astype(BF16)
        trans = _bmm_tn(kg, uw)
        qg = (q2 * eg).astype(BF16)
        egl = jnp.broadcast_to(jnp.exp(gl).reshape(nb2, 1), (nb2, LANE))
        for d in range(2):
            sl = slice(d * nb, (d + 1) * nb)
            u_s[d, pl.ds(r0, R), :] = uw[sl, :, :LANE].reshape(R, LANE)
            w_s[d, pl.ds(r0, R), :] = uw[sl, :, LANE:].reshape(R, LANE)
            qg_s[d, pl.ds(r0, R), :] = qg[sl].reshape(R, LANE)
            at_s[d, pl.ds(r0, R), :] = attn[sl].reshape(R, C).astype(BF16)
            tn_s[d, pl.ds(n0 * GDN_DK, nb * GDN_DK), :] = trans[sl, :, :LANE].reshape(nb * GDN_DK, LANE).astype(BF16)
            tm_s[d, pl.ds(n0 * GDN_DK, nb * GDN_DK), :] = (-trans[sl, :, LANE:]).reshape(nb * GDN_DK, LANE).astype(BF16)
            egl_s[d, pl.ds(n0, nb), :] = egl[sl]
        return carry

    lax.fori_loop(0, n_chunks // nb, prep, 0)

    def chunk_step(d, n, state):
        r = pl.multiple_of(n * GDN_DK, GDN_DK)
        sb = state.astype(BF16)
        st_s[d, pl.ds(r, GDN_DK), :] = sb
        return (state * egl_s[d, pl.ds(n, 1), :]
                + jnp.dot(tm_s[d, pl.ds(r, GDN_DK), :], sb, preferred_element_type=F32)
                + tn_s[d, pl.ds(r, GDN_DK), :].astype(F32))

    def scan(n, carry):
        sf, sbw = carry
        return chunk_step(0, n, sf), chunk_step(1, n_chunks - 1 - n, sbw)

    zero_state = jnp.zeros((GDN_DK, GDN_DV), F32)
    lax.fori_loop(0, n_chunks, scan, (zero_state, zero_state))

    def emit(t, carry):
        r0 = pl.multiple_of(t * R, R)
        s0 = pl.multiple_of(t * nb * GDN_DK, nb * GDN_DK)
        two = lambda ref, start, rows: jnp.concatenate([ref[0, pl.ds(start, rows), :], ref[1, pl.ds(start, rows), :]])
        st = two(st_s, s0, nb * GDN_DK).reshape(nb2, GDN_DK, GDN_DV)
        wq = jnp.concatenate([two(w_s, r0, R).reshape(nb2, C, LANE), two(qg_s, r0, R).reshape(nb2, C, LANE)], axis=1)
        ws_qs = jnp.einsum("ncd,nde->nce", wq, st, preferred_element_type=F32)
        vnew = two(u_s, r0, R).reshape(nb2, C, LANE).astype(F32) - ws_qs[:, :C, :]
        o = ws_qs[:, C:, :] + jnp.einsum("ncs,nse->nce", two(at_s, r0, R).reshape(nb2, C, C), vnew.astype(BF16),
                                         preferred_element_type=F32)
        o = (o[:nb] + o[nb:]).reshape(R, LANE)
        on = o * lax.rsqrt(jnp.mean(o * o, axis=-1, keepdims=True) + EPS) * nw_ref[...]
        z = z_ref[pl.ds(r0, R), :].astype(F32)
        y_ref[pl.ds(r0, R), :] = (on * (z * _sigmoid(z))).astype(y_ref.dtype)
        return carry

    lax.fori_loop(0, n_chunks // nb, emit, 0)


def _gdn(p, p_off, conv_w8, gb, grow, norm_w, batch, seq, n_heads):
    T = batch * seq
    qk_blocks = n_heads
    n_chunks = seq // GDN_CHUNK
    conv_rows = min(512, seq)
    prep_chunks = min(16, n_chunks)
    blk = lambda off: pl.BlockSpec((seq, LANE), lambda b, h: (b, p_off + off + h))
    cw = lambda off: pl.BlockSpec((8, LANE), lambda b, h: (0, off + h))
    return pl.pallas_call(
        functools.partial(_gdn_kernel, n_heads=n_heads, seq=seq, conv_rows=conv_rows, prep_chunks=prep_chunks),
        grid=(batch, n_heads),
        in_specs=[blk(0), blk(qk_blocks), blk(2 * qk_blocks), blk(3 * qk_blocks),
                  cw(0), cw(qk_blocks), cw(2 * qk_blocks),
                  pl.BlockSpec((seq, LANE), lambda b, h: (b, 0), pipeline_mode=pl.Buffered(1)),
                  pl.BlockSpec((1, 2 * n_heads, n_chunks, GDN_CHUNK), lambda b, h: (b, 0, 0, 0),
                               pipeline_mode=pl.Buffered(1)),
                  _const_spec((1, LANE))],
        out_specs=pl.BlockSpec((seq, LANE), lambda b, h: (b, h)),
        out_shape=jax.ShapeDtypeStruct((T, n_heads * GDN_DV), BF16),
        scratch_shapes=[
            pltpu.VMEM((seq + 16, LANE), F32),
            pltpu.VMEM((seq, LANE), F32),
            pltpu.VMEM((seq, LANE), F32),
            pltpu.VMEM((seq, LANE), F32),
            pltpu.VMEM((2, seq, LANE), BF16),
            pltpu.VMEM((2, seq, LANE), BF16),
            pltpu.VMEM((2, seq, LANE), BF16),
            pltpu.VMEM((2, seq, GDN_CHUNK), BF16),
            pltpu.VMEM((2, n_chunks * GDN_DK, GDN_DV), BF16),
            pltpu.VMEM((2, n_chunks * GDN_DK, GDN_DV), BF16),
            pltpu.VMEM((2, max(n_chunks, 8), LANE), F32),
            pltpu.VMEM((2, n_chunks * GDN_DK, GDN_DV), BF16),
        ],
        compiler_params=_cparams("parallel", "parallel"),
        name="gdn",
    )(p, p, p, p, conv_w8, conv_w8, conv_w8, gb, grow, norm_w)


def _diff_attn_kernel(slopes_ref, q_ref, k_ref, v_ref, lam_ref, nw_ref, o_ref,
                      qT_s, vT_s, klo_s, khi_s, rel_s, s_s, p_s, acc_s,
                      *, seq, tq, tk, lam_init):
    head = pl.program_id(1)
    slope2 = slopes_ref[head] * LOG2E
    scale2 = (DIFF_DH ** -0.5) * LOG2E

    rows = min(512, seq)
    for t in range(seq // rows):
        r0 = t * rows
        qT_s[:, r0:r0 + rows] = (q_ref[r0:r0 + rows, :].astype(F32) * scale2).T.astype(BF16)
        vT_s[:, r0:r0 + rows] = v_ref[r0:r0 + rows, :].astype(F32).T.astype(BF16)
        kt = k_ref[r0:r0 + rows, :]
        lane = lax.broadcasted_iota(jnp.int32, kt.shape, 1)
        klo_s[r0:r0 + rows, :] = jnp.where(lane < DIFF_DH, kt, jnp.zeros_like(kt))
        khi_s[r0:r0 + rows, :] = jnp.where(lane >= DIFF_DH, kt, jnp.zeros_like(kt))
    kr = lax.broadcasted_iota(jnp.int32, (tk, tq), 0)
    qc = lax.broadcasted_iota(jnp.int32, (tk, tq), 1)
    rel_s[...] = (qc - kr).astype(F32) * slope2

    lf = lam_ref[...]
    lam = (jnp.exp(jnp.sum(lf[0:1, :] * lf[1:2, :], axis=-1, keepdims=True))
           - jnp.exp(jnp.sum(lf[2:3, :] * lf[3:4, :], axis=-1, keepdims=True)) + lam_init)

    n_kv = seq // tk
    maps = (klo_s, khi_s)

    def q_tile(i, carry):
        c_q = pl.multiple_of(i * tq, tq)
        qT = qT_s[:, pl.ds(c_q, tq)]
        q0 = jnp.asarray(i * tq).astype(F32) * slope2

        def scores(j):
            bias = jnp.abs(rel_s[...] + (q0 - (j * tk) * slope2))
            for mp, k_s in enumerate(maps):
                s_s[j % 2, mp] = jnp.dot(k_s[j * tk:(j + 1) * tk, :], qT, preferred_element_type=F32) - bias

        def weighted_values(j, alphas):
            vT = vT_s[:, j * tk:(j + 1) * tk]
            for mp in range(len(maps)):
                upd = jnp.dot(vT, p_s[j % 2, mp], preferred_element_type=F32)
                acc_s[mp] = upd if j == 0 else acc_s[mp] * alphas[mp] + upd

        m = [jnp.full((1, tq), NEG_BIG, F32) for _ in maps]
        l = [jnp.zeros((1, tq), F32) for _ in maps]
        scores(0)
        alphas = None
        for j in range(n_kv):
            if j + 1 < n_kv:
                scores(j + 1)
            new_alphas = []
            for mp in range(len(maps)):
                s = s_s[j % 2, mp]
                m_new = jnp.maximum(m[mp], jnp.max(s, axis=0, keepdims=True))
                alpha = jnp.exp2(m[mp] - m_new)
                p = jnp.exp2(s - m_new)
                l[mp] = alpha * l[mp] + jnp.sum(p, axis=0, keepdims=True)
                m[mp] = m_new
                p_s[j % 2, mp] = p.astype(BF16)
                new_alphas.append(alpha)
            if j > 0:
                weighted_values(j - 1, alphas)
            alphas = new_alphas
        weighted_values(n_kv - 1, alphas)

        od = acc_s[0] / l[0] - lam * (acc_s[1] / l[1])
        yn = od * lax.rsqrt(jnp.mean(od * od, axis=0, keepdims=True) + EPS) * nw_ref[...] * (1.0 - lam_init)
        o_ref[pl.ds(c_q, tq), :] = yn.T.astype(o_ref.dtype)
        return carry

    lax.fori_loop(0, seq // tq, q_tile, 0)


def _diff_attn(p, slopes, diff_lambda, norm_w_col, batch, seq, n_heads, q_off, k_off, v_off, lam_init):
    T = batch * seq
    tq = min(256, seq)
    tk = min(512, seq)
    blk = lambda off: pl.BlockSpec((seq, LANE), lambda b, h: (b, off + h))
    return pl.pallas_call(
        functools.partial(_diff_attn_kernel, seq=seq, tq=tq, tk=tk, lam_init=lam_init),
        grid=(batch, n_heads),
        in_specs=[pl.BlockSpec(memory_space=pltpu.SMEM),
                  blk(q_off), blk(k_off), blk(v_off),
                  _const_spec((4, DIFF_DH)), _const_spec((DIFF_DV, 1))],
        out_specs=pl.BlockSpec((seq, LANE), lambda b, h: (b, h)),
        out_shape=jax.ShapeDtypeStruct((T, n_heads * DIFF_DV), BF16),
        scratch_shapes=[
            pltpu.VMEM((2 * DIFF_DH, seq), BF16),
            pltpu.VMEM((DIFF_DV, seq), BF16),
            pltpu.VMEM((seq, LANE), BF16),
            pltpu.VMEM((seq, LANE), BF16),
            pltpu.VMEM((tk, tq), F32),
            pltpu.VMEM((2, 2, tk, tq), F32),
            pltpu.VMEM((2, 2, tk, tq), BF16),
            pltpu.VMEM((2, DIFF_DV, tq), F32),
        ],
        compiler_params=_cparams("parallel", "parallel"),
        name="diff_attn",
    )(slopes, p, p, p, diff_lambda, norm_w_col)


def _merge_kernel(ya_ref, yd_ref, ga_ref, gd_ref, x_ref, wa_ref, wd_ref, wo_ref, gffn_ref,
                  wr_hi_ref, wr_lo_ref, br_ref, x1_ref, h2_ref, route_ref, *, n_groups, per_group):
    ma = jnp.dot(ya_ref[...], wa_ref[...], preferred_element_type=F32)
    md = jnp.dot(yd_ref[...], wd_ref[...], preferred_element_type=F32)
    merged = _sigmoid(ga_ref[...].astype(F32)) * ma + _sigmoid(gd_ref[...].astype(F32)) * md
    x1 = x_ref[...] + jnp.dot(merged.astype(BF16), wo_ref[...], preferred_element_type=F32)
    x1_ref[...] = x1
    h2 = x1 * lax.rsqrt(jnp.mean(x1 * x1, axis=-1, keepdims=True) + EPS) * gffn_ref[...]
    bits = lax.bitcast_convert_type(h2.astype(BF16).astype(F32), jnp.uint32)
    half = bits.shape[1] // 2
    h2_ref[...] = bits[:, :half] | (bits[:, half:] >> 16)

    h_hi = h2.astype(BF16)
    h_lo = (h2 - h_hi.astype(F32)).astype(BF16)
    logits = (jnp.dot(h_hi, wr_hi_ref[...], preferred_element_type=F32)
              + jnp.dot(h_hi, wr_lo_ref[...], preferred_element_type=F32)
              + jnp.dot(h_lo, wr_hi_ref[...], preferred_element_type=F32)) + br_ref[...]
    lane = lax.broadcasted_iota(jnp.int32, logits.shape, 1)
    lane_f = lane.astype(F32)
    big = float(LANE)

    def first_argmax(vals, vmax):
        return jnp.min(jnp.where(vals == vmax, lane_f, big), axis=-1, keepdims=True)

    gl = jnp.where(lane < n_groups, logits, -jnp.inf)
    gmax = jnp.max(gl, axis=-1, keepdims=True)
    g_idx = first_argmax(gl, gmax)
    g_w = 1.0 / jnp.sum(jnp.exp(gl - gmax), axis=-1, keepdims=True)
    e_lo = n_groups + g_idx * per_group
    in_group = (lane_f >= e_lo) & (lane_f < e_lo + per_group)
    el = jnp.where(in_group, logits, -jnp.inf)
    emax = jnp.max(el, axis=-1, keepdims=True)
    pe = jnp.exp(el - emax)
    pe = pe / jnp.sum(pe, axis=-1, keepdims=True)
    p1 = jnp.max(pe, axis=-1, keepdims=True)
    i1 = first_argmax(jnp.where(in_group, pe, -1.0), p1)
    rest = jnp.where(in_group & (lane_f != i1), pe, -1.0)
    p2 = jnp.max(rest, axis=-1, keepdims=True)
    i2 = first_argmax(rest, p2)
    denom = p1 + p2
    w1 = p1 / denom * g_w
    w2 = p2 / denom * g_w
    route = jnp.where(lane == 0, i1 - n_groups, 0.0)
    route = jnp.where(lane == 1, i2 - n_groups, route)
    route = jnp.where(lane == 2, w1, route)
    route = jnp.where(lane == 3, w2, route)
    route_ref[...] = route


def _merge(y_a, y_d, p, x2, wa, wd, wo, g_ffn, wr_hi, wr_lo, b_r, n_groups, per_group, tm):
    T, D = x2.shape
    va, vd = y_a.shape[1], y_d.shape[1]
    row = lambda w: pl.BlockSpec((tm, w), lambda i: (i, 0))
    return pl.pallas_call(
        functools.partial(_merge_kernel, n_groups=n_groups, per_group=per_group),
        grid=(T // tm,),
        in_specs=[row(va), row(vd),
                  pl.BlockSpec((tm, D), lambda i: (i, 0)),
                  pl.BlockSpec((tm, D), lambda i: (i, 1)),
                  row(D),
                  _const_spec((va, D)), _const_spec((vd, D)), _const_spec((D, D)), _const_spec((1, D)),
                  _const_spec((D, LANE)), _const_spec((D, LANE)), _const_spec((1, LANE))],
        out_specs=[row(D), row(D // 2), row(LANE)],
        out_shape=[jax.ShapeDtypeStruct((T, D), F32),
                   jax.ShapeDtypeStruct((T, D // 2), jnp.uint32),
                   jax.ShapeDtypeStruct((T, LANE), F32)],
        compiler_params=_cparams("parallel"),
        name="merge",
    )(y_a, y_d, p, p, x2, wa, wd, wo, g_ffn, wr_hi, wr_lo, b_r)


def _expert_kernel(tile_expert_ref, tile_flag_ref, tile_rows_ref, xs_ref, wg_ref, wu_ref, wd_ref, o_ref,
                   wg_s, wu_s, wd_s):
    i = pl.program_id(0)
    flag = tile_flag_ref[i]

    @pl.when(flag == 2)
    def _():
        wg_s[...] = wg_ref[0].astype(BF16)
        wu_s[...] = wu_ref[0].astype(BF16)
        wd_s[...] = wd_ref[0].astype(BF16)

    @pl.when(flag > 0)
    def _():
        row = lax.broadcasted_iota(jnp.int32, xs_ref.shape, 0)
        packed = jnp.where(row < tile_rows_ref[i], xs_ref[...], jnp.uint32(0))
        left = lax.bitcast_convert_type(packed & jnp.uint32(0xFFFF0000), F32).astype(BF16)
        right = lax.bitcast_convert_type(packed << 16, F32).astype(BF16)
        x = jnp.concatenate([left, right], axis=1)
        g = jnp.dot(x, wg_s[...], preferred_element_type=F32)
        u = jnp.dot(x, wu_s[...], preferred_element_type=F32)
        hid = g * _sigmoid(g) * u
        o_ref[...] = jnp.dot(hid.astype(BF16), wd_s[...], preferred_element_type=F32).astype(o_ref.dtype)

    @pl.when(flag == 0)
    def _():
        o_ref[...] = jnp.zeros_like(o_ref)


def _experts(tile_expert, tile_flag, tile_rows, xs, w_gate, w_up, w_down, tm):
    n_rows, half = xs.shape
    E, D, Fd = w_gate.shape
    assert D == 2 * half
    n_tiles = n_rows // tm
    grid_spec = pltpu.PrefetchScalarGridSpec(
        num_scalar_prefetch=3,
        grid=(n_tiles,),
        in_specs=[pl.BlockSpec((tm, half), lambda i, te, tf, tr: (i, 0)),
                  pl.BlockSpec((1, D, Fd), lambda i, te, tf, tr: (te[i], 0, 0)),
                  pl.BlockSpec((1, D, Fd), lambda i, te, tf, tr: (te[i], 0, 0)),
                  pl.BlockSpec((1, Fd, D), lambda i, te, tf, tr: (te[i], 0, 0))],
        out_specs=pl.BlockSpec((tm, D), lambda i, te, tf, tr: (i, 0)),
        scratch_shapes=[pltpu.VMEM((D, Fd), BF16), pltpu.VMEM((D, Fd), BF16), pltpu.VMEM((Fd, D), BF16)],
    )
    return pl.pallas_call(
        _expert_kernel,
        grid_spec=grid_spec,
        out_shape=jax.ShapeDtypeStruct((n_rows, D), BF16),
        compiler_params=_cparams("arbitrary"),
        name="experts",
    )(tile_expert, tile_flag, tile_rows, xs, w_gate, w_up, w_down)


SC_CORES = 2
SC_SUBCORES = 16
SC_ROWS = 32


def _dispatch_rows(table, pos, n_rows):
    T, W = table.shape
    workers = SC_CORES * SC_SUBCORES
    chunks = T // (workers * SC_ROWS)
    assert workers * chunks * SC_ROWS == T
    pos4 = pos.T.reshape(TOP_K, workers, chunks, SC_ROWS).transpose(1, 0, 2, 3)
    mesh = plsc.VectorSubcoreMesh(core_axis_name="c", subcore_axis_name="s")

    @functools.partial(
        pl.kernel, mesh=mesh,
        out_type=jax.ShapeDtypeStruct((n_rows, W), table.dtype),
        scratch_types=[pltpu.VMEM((TOP_K, chunks, SC_ROWS), jnp.int32),
                       pltpu.VMEM((2, SC_ROWS, W), table.dtype),
                       pltpu.SemaphoreType.DMA, pltpu.SemaphoreType.DMA, pltpu.SemaphoreType.DMA],
        name="dispatch_rows",
    )
    def dispatch(table_hbm, pos_hbm, out_hbm, pos_v, rows_v, sem_in, sem_out0, sem_out1):
        wid = lax.axis_index("s") * SC_CORES + lax.axis_index("c")
        base = wid * (chunks * SC_ROWS)
        pltpu.sync_copy(pos_hbm.at[wid], pos_v)

        def load(c):
            return pltpu.async_copy(table_hbm.at[pl.ds(base + c * SC_ROWS, SC_ROWS)], rows_v.at[c % 2], sem_in)

        pending = load(0)
        for c in range(chunks):
            pending.wait()
            if c + 1 < chunks:
                pending = load(c + 1)
            out0 = pltpu.async_copy(rows_v.at[c % 2], out_hbm.at[pos_v.at[0, c]], sem_out0)
            out1 = pltpu.async_copy(rows_v.at[c % 2], out_hbm.at[pos_v.at[1, c]], sem_out1)
            out0.wait()
            out1.wait()

    return dispatch(table, pos4)


def _final_kernel(x1_ref, y0_ref, y1_ref, route_ref, g_ref, o_ref):
    route = route_ref[...]
    lane = lax.broadcasted_iota(jnp.int32, route.shape, 1)
    w0 = jnp.sum(jnp.where(lane == TOP_K, route, 0.0), axis=-1, keepdims=True)
    w1 = jnp.sum(jnp.where(lane == TOP_K + 1, route, 0.0), axis=-1, keepdims=True)
    x = x1_ref[...] + (w0 * y0_ref[...].astype(F32) + w1 * y1_ref[...].astype(F32))
    o_ref[...] = x * lax.rsqrt(jnp.mean(x * x, axis=-1, keepdims=True) + EPS) * g_ref[...]


def _final(x1, y0, y1, route, g_final, tm):
    T, D = x1.shape
    row = pl.BlockSpec((tm, D), lambda i: (i, 0))
    return pl.pallas_call(
        _final_kernel,
        grid=(T // tm,),
        in_specs=[row, row, row, pl.BlockSpec((tm, LANE), lambda i: (i, 0)), _const_spec((1, D))],
        out_specs=row,
        out_shape=jax.ShapeDtypeStruct((T, D), F32),
        compiler_params=_cparams("parallel"),
        name="final",
    )(x1, y0, y1, route, g_final)


def _route_tables(ids, n_experts, tm):
    T = ids.shape[0]
    A = T * TOP_K
    flat_e = ids.reshape(A)
    onehot = (flat_e[:, None] == jnp.arange(n_experts, dtype=jnp.int32)[None, :]).astype(jnp.int32)
    rank = jnp.sum((jnp.cumsum(onehot, axis=0) - onehot) * onehot, axis=1)
    counts = jnp.sum(onehot, axis=0)
    padded = ((counts + tm - 1) // tm) * tm
    ends = jnp.cumsum(padded)
    starts = ends - padded
    pos = starts[flat_e] + rank
    n_tiles = A // tm + n_experts
    tile_start = jnp.arange(n_tiles, dtype=jnp.int32) * tm
    tile_expert_raw = jnp.sum((ends[None, :] <= tile_start[:, None]).astype(jnp.int32), axis=1)
    valid = tile_start < ends[-1]
    last_expert = jnp.max(jnp.where(counts > 0, jnp.arange(n_experts, dtype=jnp.int32), 0))
    tile_expert = jnp.where(valid, jnp.minimum(tile_expert_raw, n_experts - 1), last_expert)
    first = jnp.concatenate([jnp.ones((1,), bool), tile_expert[1:] != tile_expert[:-1]])
    tile_flag = jnp.where(valid, jnp.where(first, 2, 1), 0).astype(jnp.int32)
    tile_rows = jnp.clip((starts + counts)[tile_expert] - tile_start, 0, tm)
    tile_rows = jnp.where(valid, tile_rows, 0).astype(jnp.int32)
    return pos.reshape(T, TOP_K), tile_expert, tile_flag, tile_rows


def _largest_tile(n, cap):
    t = min(n, cap)
    while n % t:
        t //= 2
    return t


def kernel(x, g_mix, w_in, conv_w, a_log, dt_bias, gdn_norm_w, diff_lambda, diff_norm_w, w_branch_a, w_branch_d,
           w_out, g_ffn, w_group, b_group, w_router, b_router, w_exp_gate, w_exp_up, w_exp_down, g_final):
    B, S, D = x.shape
    T = B * S
    depth = g_mix.shape[0]
    Hg = a_log.shape[-1]
    gdn_qk = Hg * GDN_DK
    gdn_v = Hg * GDN_DV
    diff_v = w_branch_d.shape[1]
    Hd = diff_v // DIFF_DV
    diff_qk = Hd * 2 * DIFF_DH
    n_groups = w_group.shape[-1]
    n_experts = w_router.shape[-1]
    per_group = n_experts // n_groups
    assert 4 * Hg <= LANE and n_groups + n_experts <= LANE
    assert S % GDN_CHUNK == 0 and D % LANE == 0

    ab_lo = 2 * gdn_qk + 2 * gdn_v
    ab_hi = ab_lo + 4 * Hg
    gates_lo = ab_hi + 2 * diff_qk + diff_v
    main_cols = 2 * D + ab_lo + 2 * diff_qk + diff_v
    gdn_off = 2 * D
    q_d_off = gdn_off + ab_lo
    k_d_off = q_d_off + diff_qk
    v_d_off = k_d_off + diff_qk

    tm_norm = _largest_tile(T, 512)
    tm_proj = _largest_tile(T, 1024)
    tn_proj = _largest_tile(main_cols, 1024)
    tm_merge = _largest_tile(T, 256)
    tm_exp = _largest_tile(T * TOP_K, 256)
    tm_final = _largest_tile(T, 512)

    slopes = jnp.exp2(-8.0 * (jnp.arange(Hd, dtype=F32) + 1.0) / Hd)
    x2 = x.reshape(T, D)
    for layer in range(depth):
        lam_init = 0.8 - 0.6 * math.exp(-0.3 * layer)
        w_in_l = w_in[layer]
        w_main = jnp.concatenate([w_in_l[:, gates_lo:], w_in_l[:, :ab_lo], w_in_l[:, ab_hi:gates_lo]],
                                 axis=1).astype(BF16)
        w_ab = jnp.pad(w_in_l[:, ab_lo:ab_hi], ((0, 0), (0, LANE - 4 * Hg))).astype(BF16)
        alog_pad = jnp.pad(a_log[layer].reshape(1, 2 * Hg), ((0, 0), (0, LANE - 2 * Hg)))
        dtb_pad = jnp.pad(dt_bias[layer].reshape(1, 2 * Hg), ((0, 0), (0, LANE - 2 * Hg)))

        h, gb = _norm_proj(x2, g_mix[layer].reshape(1, D), w_ab, alog_pad, dtb_pad, 2 * Hg, tm_norm)
        p = _matmul(h, w_main, tm_proj, tn_proj, BF16)

        grow = gb[:, :2 * Hg].reshape(B, S // GDN_CHUNK, GDN_CHUNK, 2 * Hg).transpose(0, 3, 1, 2)
        conv_w8 = jnp.pad(conv_w[layer], ((0, 8 - GDN_CONV), (0, 0)))
        y_a = _gdn(p, gdn_off // LANE, conv_w8, gb, grow, gdn_norm_w[layer].reshape(1, GDN_DV), B, S, Hg)
        y_d = _diff_attn(p, slopes, diff_lambda[layer], diff_norm_w[layer].reshape(DIFF_DV, 1), B, S, Hd,
                         q_d_off // LANE, k_d_off // LANE, v_d_off // LANE, lam_init)

        w_r = jnp.pad(jnp.concatenate([w_group[layer], w_router[layer]], axis=1),
                      ((0, 0), (0, LANE - n_groups - n_experts)))
        wr_hi = w_r.astype(BF16)
        wr_lo = (w_r - wr_hi.astype(F32)).astype(BF16)
        b_r = jnp.pad(jnp.concatenate([b_group[layer], b_router[layer]]).reshape(1, -1),
                      ((0, 0), (0, LANE - n_groups - n_experts)))
        x1, h2, route = _merge(y_a, y_d, p, x2,
                               w_branch_a[layer].astype(BF16), w_branch_d[layer].astype(BF16),
                               w_out[layer].astype(BF16), g_ffn[layer].reshape(1, D),
                               wr_hi, wr_lo, b_r, n_groups, per_group, tm_merge)

        ids = route[:, 0:TOP_K].astype(jnp.int32)
        pos, tile_expert, tile_flag, tile_rows = _route_tables(ids, n_experts, tm_exp)
        xs = _dispatch_rows(h2, pos, tile_expert.shape[0] * tm_exp)
        ys = _experts(tile_expert, tile_flag, tile_rows, xs, w_exp_gate[layer], w_exp_up[layer],
                      w_exp_down[layer], tm_exp)
        y0 = jnp.take(ys, pos[:, 0], axis=0)
        y1 = jnp.take(ys, pos[:, 1], axis=0)
        if layer + 1 < depth:
            wts = route[:, TOP_K:2 * TOP_K]
            x2 = x1 + (wts[:, 0:1] * y0.astype(F32) + wts[:, 1:2] * y1.astype(F32))
    out = _final(x1, y0, y1, route, g_final.reshape(1, D), tm_final)
    return out.reshape(B, S, D)
```

```python
import functools
import math

import jax
import jax.numpy as jnp
from jax import lax
from jax.experimental import pallas as pl
from jax.experimental.pallas import tpu as pltpu
from jax.experimental.pallas import tpu_sc as plsc

F32 = jnp.float32
BF16 = jnp.bfloat16
EPS = 1e-6
LANE = 128
GDN_DK = 128
GDN_DV = 128
GDN_CONV = 5
GDN_CHUNK = 64
DIFF_DH = 64
DIFF_DV = 2 * DIFF_DH
TOP_K = 2
LOG2E = 1.4426950408889634
NEG_BIG = -1e30
VMEM_LIMIT_BYTES = 56 * 1024 * 1024

_HI = lax.Precision.HIGHEST


def _cparams(*sem):
    return pltpu.CompilerParams(dimension_semantics=sem, vmem_limit_bytes=VMEM_LIMIT_BYTES)


def _const_spec(shape):
    nd = len(shape)
    return pl.BlockSpec(shape, lambda *_: (0,) * nd, pipeline_mode=pl.Buffered(1))


def _mm(a, b):
    return jnp.dot(a.astype(BF16), b.astype(BF16), preferred_element_type=F32)


def _split3(x):
    hi = x.astype(BF16)
    r = x - hi.astype(F32)
    mid = r.astype(BF16)
    lo = (r - mid.astype(F32)).astype(BF16)
    return hi, mid, lo


def _softplus(x):
    return jnp.maximum(x, 0.0) + jnp.log(1.0 + jnp.exp(-jnp.abs(x)))


def _sigmoid(x):
    return 1.0 / (1.0 + jnp.exp(-x))


def _norm_proj_kernel(x_ref, g_ref, wab_ref, alog_ref, dtb_ref, h_ref, gb_ref, *, n_dir_heads):
    x = x_ref[...]
    h = x * lax.rsqrt(jnp.mean(x * x, axis=-1, keepdims=True) + EPS) * g_ref[...]
    hb = h.astype(BF16)
    h_ref[...] = hb
    ab = jnp.dot(hb, wab_ref[...], preferred_element_type=F32)
    g = -jnp.exp(alog_ref[...]) * _softplus(ab + dtb_ref[...])
    beta = _sigmoid(ab)
    tm = x.shape[0]
    row = lax.broadcasted_iota(jnp.int32, (tm, tm), 0)
    col = lax.broadcasted_iota(jnp.int32, (tm, tm), 1)
    same = (row // GDN_CHUNK) == (col // GDN_CHUNK)
    prefix = jnp.where(same & (col <= row), 1.0, 0.0).astype(BF16)
    suffix = jnp.where(same & (col >= row), 1.0, 0.0).astype(BF16)
    pieces = _split3(g)
    cs_f = sum(jnp.dot(prefix, p, preferred_element_type=F32) for p in pieces)
    cs_b = sum(jnp.dot(suffix, p, preferred_element_type=F32) for p in pieces)
    lane = lax.broadcasted_iota(jnp.int32, g.shape, 1)
    gb_ref[...] = jnp.where(lane < n_dir_heads // 2, cs_f, jnp.where(lane < n_dir_heads, cs_b, beta))


def _norm_proj(x2, g_mix, w_ab, alog_pad, dtb_pad, n_dir_heads, tm):
    T, D = x2.shape
    return pl.pallas_call(
        functools.partial(_norm_proj_kernel, n_dir_heads=n_dir_heads),
        grid=(T // tm,),
        in_specs=[pl.BlockSpec((tm, D), lambda i: (i, 0)),
                  _const_spec((1, D)), _const_spec((D, LANE)), _const_spec((1, LANE)), _const_spec((1, LANE))],
        out_specs=[pl.BlockSpec((tm, D), lambda i: (i, 0)),
                   pl.BlockSpec((tm, LANE), lambda i: (i, 0))],
        out_shape=[jax.ShapeDtypeStruct((T, D), BF16),
                   jax.ShapeDtypeStruct((T, LANE), F32)],
        compiler_params=_cparams("parallel"),
        name="norm_proj",
    )(x2, g_mix, w_ab, alog_pad, dtb_pad)


def _matmul_kernel(a_ref, b_ref, o_ref):
    o_ref[...] = jnp.dot(a_ref[...], b_ref[...], preferred_element_type=F32).astype(o_ref.dtype)


def _matmul(a, b, tm, tn, out_dtype):
    M, K = a.shape
    N = b.shape[1]
    return pl.pallas_call(
        _matmul_kernel,
        grid=(N // tn, M // tm),
        in_specs=[pl.BlockSpec((tm, K), lambda j, i: (i, 0)),
                  pl.BlockSpec((K, tn), lambda j, i: (0, j))],
        out_specs=pl.BlockSpec((tm, tn), lambda j, i: (i, j)),
        out_shape=jax.ShapeDtypeStruct((M, N), out_dtype),
        compiler_params=_cparams("parallel", "parallel"),
        name="in_proj",
    )(a, b)


def _bmm_tn(a, b):
    return jnp.stack([lax.dot_general(a[n], b[n], (((0,), (0,)), ((), ())), preferred_element_type=F32)
                      for n in range(a.shape[0])])


def _gdn_kernel(q_ref, k_ref, v_ref, z_ref, cwq_ref, cwk_ref, cwv_ref, gb_ref, grow_ref, nw_ref,
                y_ref,
                xpad, qn, kn, vn, u_s, w_s, qg_s, at_s, tm_s, tn_s, egl_s, st_s,
                *, n_heads, seq, conv_rows, prep_chunks):
    C = GDN_CHUNK
    n_chunks = seq // C
    head = pl.program_id(1)
    pad = 8
    half = GDN_CONV // 2

    zeros_pad = jnp.zeros((pad, LANE), F32)
    xpad[0:pad, :] = zeros_pad
    xpad[pad + seq:pad + seq + pad, :] = zeros_pad
    for src, cw_ref, dst, mode in ((q_ref, cwq_ref, qn, "q"), (k_ref, cwk_ref, kn, "k"), (v_ref, cwv_ref, vn, "v")):
        xpad[pad:pad + seq, :] = src[...].astype(F32)
        for t in range(seq // conv_rows):
            r0 = t * conv_rows
            acc = jnp.zeros((conv_rows, LANE), F32)
            for j in range(GDN_CONV):
                lo = pad + r0 + j - half
                acc = acc + xpad[lo:lo + conv_rows, :] * cw_ref[j:j + 1, :]
            y = acc * _sigmoid(acc)
            if mode != "v":
                y = y * lax.rsqrt(jnp.sum(y * y, axis=-1, keepdims=True) + EPS)
            if mode == "q":
                y = y * (GDN_DK ** -0.5)
            dst[r0:r0 + conv_rows, :] = y

    nb = prep_chunks
    R = nb * C
    nb2 = 2 * nb
    bi = lax.broadcasted_iota(jnp.int32, (nb2, C, C), 0)
    ii = lax.broadcasted_iota(jnp.int32, (nb2, C, C), 1)
    jj = lax.broadcasted_iota(jnp.int32, (nb2, C, C), 2)
    fwd = bi < nb
    incl = (fwd & (ii >= jj)) | (~fwd & (ii <= jj))
    strict = (fwd & (ii > jj)) | (~fwd & (ii < jj))
    eye = jnp.where(ii == jj, 1.0, 0.0).astype(F32)
    lane = lax.broadcasted_iota(jnp.int32, (R, LANE), 1)
    both = lambda x: jnp.concatenate([x, x], axis=0)

    def prep(t, carry):
        r0 = pl.multiple_of(t * R, R)
        n0 = pl.multiple_of(t * nb, nb)
        q3 = qn[pl.ds(r0, R), :].reshape(nb, C, LANE)
        k3 = kn[pl.ds(r0, R), :].reshape(nb, C, LANE)
        v3 = vn[pl.ds(r0, R), :].reshape(nb, C, LANE)
        gb_blk = gb_ref[pl.ds(r0, R), :]
        kq = jnp.concatenate([k3, q3], axis=1).astype(BF16)
        gram = jnp.einsum("nik,njk->nij", kq, k3.astype(BF16), preferred_element_type=F32)
        kk, qk = both(gram[:, :C, :]), both(gram[:, C:, :])
        gcb, bt, grow, gl = [], [], [], []
        for d in range(2):
            colidx = d * n_heads + head
            gcb_d = jnp.sum(jnp.where(lane == colidx, gb_blk, 0.0), axis=-1, keepdims=True).reshape(nb, C, 1)
            gcb.append(gcb_d)
            bt.append(jnp.sum(jnp.where(lane == 2 * n_heads + colidx, gb_blk, 0.0), axis=-1,
                              keepdims=True).reshape(nb, C, 1))
            grow.append(grow_ref[0, colidx, pl.ds(n0, nb), :].reshape(nb, 1, C))
            gl.append(gcb_d[:, C - 1:C, :] if d == 0 else gcb_d[:, 0:1, :])
        gcb, bt, grow, gl = (jnp.concatenate(x, axis=0) for x in (gcb, bt, grow, gl))
        k2, q2, v2 = both(k3), both(q3), both(v3)

        decay = jnp.exp(jnp.where(incl, gcb - grow, -jnp.inf))
        L = jnp.where(strict, bt * kk * decay, 0.0)
        attn = qk * decay
        ainv = eye - L
        P = L
        for _ in range(int(math.log2(C)) - 1):
            Pb = P.astype(BF16)
            P = jnp.einsum("nij,njk->nik", Pb, Pb, preferred_element_type=F32)
            ainv = ainv + jnp.einsum("nij,njk->nik", ainv.astype(BF16), P.astype(BF16),
                                     preferred_element_type=F32)
        eg = jnp.exp(gcb)
        rhs = jnp.concatenate([v2 * bt, k2 * (bt * eg)], axis=-1).astype(BF16)
        uw = jnp.einsum("nij,njd->nid", ainv.astype(BF16), rhs, preferred_element_type=F32).astype(BF16)
        kg = (k2 * jnp.exp(gl - gcb)).astype(BF16)
        trans = _bmm_tn(kg, uw)
        qg = (q2 * eg).astype(BF16)
        egl = jnp.broadcast_to(jnp.exp(gl).reshape(nb2, 1), (nb2, LANE))
        for d in range(2):
            sl = slice(d * nb, (d + 1) * nb)
            u_s[d, pl.ds(r0, R), :] = uw[sl, :, :LANE].reshape(R, LANE)
            w_s[d, pl.ds(r0, R), :] = uw[sl, :, LANE:].reshape(R, LANE)
            qg_s[d, pl.ds(r0, R), :] = qg[sl].reshape(R, LANE)
            at_s[d, pl.ds(r0, R), :] = attn[sl].reshape(R, C).astype(BF16)
            tn_s[d, pl.ds(n0 * GDN_DK, nb * GDN_DK), :] = trans[sl, :, :LANE].reshape(nb * GDN_DK, LANE).astype(BF16)
            tm_s[d, pl.ds(n0 * GDN_DK, nb * GDN_DK), :] = (-trans[sl, :, LANE:]).reshape(nb * GDN_DK, LANE).astype(BF16)
            egl_s[d, pl.ds(n0, nb), :] = egl[sl]
        return carry

    lax.fori_loop(0, n_chunks // nb, prep, 0)

    def chunk_step(d, n, state):
        r = pl.multiple_of(n * GDN_DK, GDN_DK)
        sb = state.astype(BF16)
        st_s[d, pl.ds(r, GDN_DK), :] = sb
        return (state * egl_s[d, pl.ds(n, 1), :]
                + jnp.dot(tm_s[d, pl.ds(r, GDN_DK), :], sb, preferred_element_type=F32)
                + tn_s[d, pl.ds(r, GDN_DK), :].astype(F32))

    def scan(n, carry):
        sf, sbw = carry
        return chunk_step(0, n, sf), chunk_step(1, n_chunks - 1 - n, sbw)

    zero_state = jnp.zeros((GDN_DK, GDN_DV), F32)
    lax.fori_loop(0, n_chunks, scan, (zero_state, zero_state))

    def emit(t, carry):
        r0 = pl.multiple_of(t * R, R)
        s0 = pl.multiple_of(t * nb * GDN_DK, nb * GDN_DK)
        two = lambda ref, start, rows: jnp.concatenate([ref[0, pl.ds(start, rows), :], ref[1, pl.ds(start, rows), :]])
        st = two(st_s, s0, nb * GDN_DK).reshape(nb2, GDN_DK, GDN_DV)
        wq = jnp.concatenate([two(w_s, r0, R).reshape(nb2, C, LANE), two(qg_s, r0, R).reshape(nb2, C, LANE)], axis=1)
        ws_qs = jnp.einsum("ncd,nde->nce", wq, st, preferred_element_type=F32)
        vnew = two(u_s, r0, R).reshape(nb2, C, LANE).astype(F32) - ws_qs[:, :C, :]
        o = ws_qs[:, C:, :] + jnp.einsum("ncs,nse->nce", two(at_s, r0, R).reshape(nb2, C, C), vnew.astype(BF16),
                                         preferred_element_type=F32)
        o = (o[:nb] + o[nb:]).reshape(R, LANE)
        on = o * lax.rsqrt(jnp.mean(o * o, axis=-1, keepdims=True) + EPS) * nw_ref[...]
        z = z_ref[pl.ds(r0, R), :].astype(F32)
        y_ref[pl.ds(r0, R), :] = (on * (z * _sigmoid(z))).astype(y_ref.dtype)
        return carry

    lax.fori_loop(0, n_chunks // nb, emit, 0)


def _gdn(p, p_off, conv_w8, gb, grow, norm_w, batch, seq, n_heads):
    T = batch * seq
    qk_blocks = n_heads
    n_chunks = seq // GDN_CHUNK
    conv_rows = min(512, seq)
    prep_chunks = min(16, n_chunks)
    blk = lambda off: pl.BlockSpec((seq, LANE), lambda b, h: (b, p_off + off + h))
    cw = lambda off: pl.BlockSpec((8, LANE), lambda b, h: (0, off + h))
    return pl.pallas_call(
        functools.partial(_gdn_kernel, n_heads=n_heads, seq=seq, conv_rows=conv_rows, prep_chunks=prep_chunks),
        grid=(batch, n_heads),
        in_specs=[blk(0), blk(qk_blocks), blk(2 * qk_blocks), blk(3 * qk_blocks),
                  cw(0), cw(qk_blocks), cw(2 * qk_blocks),
                  pl.BlockSpec((seq, LANE), lambda b, h: (b, 0), pipeline_mode=pl.Buffered(1)),
                  pl.BlockSpec((1, 2 * n_heads, n_chunks, GDN_CHUNK), lambda b, h: (b, 0, 0, 0),
                               pipeline_mode=pl.Buffered(1)),
                  _const_spec((1, LANE))],
        out_specs=pl.BlockSpec((seq, LANE), lambda b, h: (b, h)),
        out_shape=jax.ShapeDtypeStruct((T, n_heads * GDN_DV), BF16),
        scratch_shapes=[
            pltpu.VMEM((seq + 16, LANE), F32),
            pltpu.VMEM((seq, LANE), F32),
            pltpu.VMEM((seq, LANE), F32),
            pltpu.VMEM((seq, LANE), F32),
            pltpu.VMEM((2, seq, LANE), BF16),
            pltpu.VMEM((2, seq, LANE), BF16),
            pltpu.VMEM((2, seq, LANE), BF16),
            pltpu.VMEM((2, seq, GDN_CHUNK), BF16),
            pltpu.VMEM((2, n_chunks * GDN_DK, GDN_DV), BF16),
            pltpu.VMEM((2, n_chunks * GDN_DK, GDN_DV), BF16),
            pltpu.VMEM((2, max(n_chunks, 8), LANE), F32),
            pltpu.VMEM((2, n_chunks * GDN_DK, GDN_DV), BF16),
        ],
        compiler_params=_cparams("parallel", "parallel"),
        name="gdn",
    )(p, p, p, p, conv_w8, conv_w8, conv_w8, gb, grow, norm_w)


def _diff_attn_kernel(slopes_ref, q_ref, k_ref, v_ref, lam_ref, nw_ref, o_ref,
                      qT_s, vT_s, klo_s, khi_s, rel_s, s_s, p_s, acc_s,
                      *, seq, tq, tk, lam_init):
    head = pl.program_id(1)
    slope2 = slopes_ref[head] * LOG2E
    scale2 = (DIFF_DH ** -0.5) * LOG2E
    n_kv = seq // tk
    maps = (klo_s, khi_s)

    rows = min(512, seq)
    for t in range(seq // rows):
        r0 = t * rows
        qt = (q_ref[r0:r0 + rows, :].astype(F32) * scale2).T.astype(BF16)
        frow = lax.broadcasted_iota(jnp.int32, (LANE, rows), 0)
        fcol = lax.broadcasted_iota(jnp.int32, (LANE, rows), 1)
        qa, qb, qc = (x.astype(F32) for x in _split3(((r0 + fcol) % tq).astype(F32) * slope2))
        qfeat = jnp.where(frow < 3, 1.0, jnp.where(frow == 3, -qa, jnp.where(frow == 4, -qb,
                                                   jnp.where(frow == 5, -qc, 0.0)))).astype(BF16)
        for sgn, feat in enumerate((qfeat, -qfeat)):
            qT_s[sgn, 0:LANE, r0:r0 + rows] = qt
            qT_s[sgn, LANE:2 * LANE, r0:r0 + rows] = feat
        vT_s[0:LANE, r0:r0 + rows] = v_ref[r0:r0 + rows, :].astype(F32).T.astype(BF16)
        vT_s[LANE:LANE + 16, r0:r0 + rows] = jnp.ones((16, rows), BF16)
        kt = k_ref[r0:r0 + rows, :]
        krow = lax.broadcasted_iota(jnp.int32, (rows, LANE), 0)
        lane = lax.broadcasted_iota(jnp.int32, (rows, LANE), 1)
        ka, kb, kc = (x.astype(F32) for x in _split3(((r0 + krow) % tk).astype(F32) * slope2))
        kfeat = jnp.where(lane == 0, ka, jnp.where(lane == 1, kb, jnp.where(lane == 2, kc,
                                                   jnp.where(lane < 6, 1.0, 0.0)))).astype(BF16)
        klo_s[r0:r0 + rows, 0:LANE] = jnp.where(lane < DIFF_DH, kt, jnp.zeros_like(kt))
        khi_s[r0:r0 + rows, 0:LANE] = jnp.where(lane >= DIFF_DH, kt, jnp.zeros_like(kt))
        klo_s[r0:r0 + rows, LANE:2 * LANE] = kfeat
        khi_s[r0:r0 + rows, LANE:2 * LANE] = kfeat
    kr = lax.broadcasted_iota(jnp.int32, (tk, tq), 0)
    qcol = lax.broadcasted_iota(jnp.int32, (tk, tq), 1)
    rel_s[...] = (qcol - kr).astype(F32) * slope2

    lf = lam_ref[...]
    lam = (jnp.exp(jnp.sum(lf[0:1, :] * lf[1:2, :], axis=-1, keepdims=True))
           - jnp.exp(jnp.sum(lf[2:3, :] * lf[3:4, :], axis=-1, keepdims=True)) + lam_init)

    def q_tile(i, carry):
        c_q = pl.multiple_of(i * tq, tq)
        qT_pos = qT_s[0, :, pl.ds(c_q, tq)]
        qT_neg = qT_s[1, :, pl.ds(c_q, tq)]
        j_diag = (i * tq) // tk

        def scores(t):
            j = j_diag if t == 0 else lax.rem(j_diag + t, n_kv)
            r_k = pl.multiple_of(j * tk, tk)
            d0 = jnp.asarray(i * tq - j * tk)
            if t == 0:
                bias = jnp.abs(rel_s[...] + d0.astype(F32) * slope2)
                for mp, k_s in enumerate(maps):
                    s_s[t % 2, mp] = jnp.dot(k_s[pl.ds(r_k, tk), 0:LANE], qT_pos[0:LANE, :],
                                             preferred_element_type=F32) - bias
                return r_k, 0.0
            qT = jnp.where(j > j_diag, qT_neg, qT_pos)
            for mp, k_s in enumerate(maps):
                s_s[t % 2, mp] = jnp.dot(k_s[pl.ds(r_k, tk), :], qT, preferred_element_type=F32)
            return r_k, jnp.abs(d0).astype(F32) * slope2

        def weighted_values(t, r_k, alphas):
            vT = vT_s[:, pl.ds(r_k, tk)]
            for mp in range(len(maps)):
                upd = jnp.dot(vT, p_s[t % 2, mp], preferred_element_type=F32)
                acc_s[mp] = upd if t == 0 else acc_s[mp] * alphas[mp] + upd

        m = [jnp.full((1, tq), NEG_BIG, F32) for _ in maps]
        nxt = scores(0)
        prev = None
        for t in range(n_kv):
            r_k, const = nxt
            if t + 1 < n_kv:
                nxt = scores(t + 1)
            alphas = []
            for mp in range(len(maps)):
                m_new = jnp.maximum(m[mp], jnp.max(s_s[t % 2, mp], axis=0, keepdims=True) - const)
                alphas.append(jnp.exp2(m[mp] - m_new))
                p_s[t % 2, mp] = jnp.exp2(s_s[t % 2, mp] - (m_new + const)).astype(BF16)
                m[mp] = m_new
            if prev is not None:
                weighted_values(t - 1, *prev)
            prev = (r_k, alphas)
        weighted_values(n_kv - 1, *prev)

        o1, o2 = acc_s[0], acc_s[1]
        od = o1[0:DIFF_DV] / o1[DIFF_DV:DIFF_DV + 1] - lam * (o2[0:DIFF_DV] / o2[DIFF_DV:DIFF_DV + 1])
        yn = od * lax.rsqrt(jnp.mean(od * od, axis=0, keepdims=True) + EPS) * nw_ref[...] * (1.0 - lam_init)
        o_ref[pl.ds(c_q, tq), :] = yn.T.astype(o_ref.dtype)
        return carry

    lax.fori_loop(0, seq // tq, q_tile, 0)


def _diff_attn(p, slopes, diff_lambda, norm_w_col, batch, seq, n_heads, q_off, k_off, v_off, lam_init):
    T = batch * seq
    tq = min(256, seq)
    tk = min(512, seq)
    blk = lambda off: pl.BlockSpec((seq, LANE), lambda b, h: (b, off + h))
    return pl.pallas_call(
        functools.partial(_diff_attn_kernel, seq=seq, tq=tq, tk=tk, lam_init=lam_init),
        grid=(batch, n_heads),
        in_specs=[pl.BlockSpec(memory_space=pltpu.SMEM),
                  blk(q_off), blk(k_off), blk(v_off),
                  _const_spec((4, DIFF_DH)), _const_spec((DIFF_DV, 1))],
        out_specs=pl.BlockSpec((seq, LANE), lambda b, h: (b, h)),
        out_shape=jax.ShapeDtypeStruct((T, n_heads * DIFF_DV), BF16),
        scratch_shapes=[
            pltpu.VMEM((2, 2 * LANE, seq), BF16),
            pltpu.VMEM((DIFF_DV + 16, seq), BF16),
            pltpu.VMEM((seq, 2 * LANE), BF16),
            pltpu.VMEM((seq, 2 * LANE), BF16),
            pltpu.VMEM((tk, tq), F32),
            pltpu.VMEM((2, 2, tk, tq), F32),
            pltpu.VMEM((2, 2, tk, tq), BF16),
            pltpu.VMEM((2, DIFF_DV + 16, tq), F32),
        ],
        compiler_params=_cparams("parallel", "parallel"),
        name="diff_attn",
    )(slopes, p, p, p, diff_lambda, norm_w_col)


def _merge_kernel(ya_ref, yd_ref, ga_ref, gd_ref, x_ref, wa_ref, wd_ref, wo_ref, gffn_ref,
                  wr_hi_ref, wr_lo_ref, br_ref, x1_ref, h2_ref, route_ref, count_ref, *, n_groups, per_group):
    ma = jnp.dot(ya_ref[...], wa_ref[...], preferred_element_type=F32)
    md = jnp.dot(yd_ref[...], wd_ref[...], preferred_element_type=F32)
    merged = _sigmoid(ga_ref[...].astype(F32)) * ma + _sigmoid(gd_ref[...].astype(F32)) * md
    x1 = x_ref[...] + jnp.dot(merged.astype(BF16), wo_ref[...], preferred_element_type=F32)
    x1_ref[...] = x1
    h2 = x1 * lax.rsqrt(jnp.mean(x1 * x1, axis=-1, keepdims=True) + EPS) * gffn_ref[...]
    bits = lax.bitcast_convert_type(h2.astype(BF16).astype(F32), jnp.uint32)
    half = bits.shape[1] // 2
    h2_ref[...] = bits[:, :half] | (bits[:, half:] >> 16)

    h_hi = h2.astype(BF16)
    h_lo = (h2 - h_hi.astype(F32)).astype(BF16)
    logits = (jnp.dot(h_hi, wr_hi_ref[...], preferred_element_type=F32)
              + jnp.dot(h_hi, wr_lo_ref[...], preferred_element_type=F32)
              + jnp.dot(h_lo, wr_hi_ref[...], preferred_element_type=F32)) + br_ref[...]
    lane = lax.broadcasted_iota(jnp.int32, logits.shape, 1)
    lane_f = lane.astype(F32)
    big = float(LANE)

    def first_argmax(vals, vmax):
        return jnp.min(jnp.where(vals == vmax, lane_f, big), axis=-1, keepdims=True)

    gl = jnp.where(lane < n_groups, logits, -jnp.inf)
    gmax = jnp.max(gl, axis=-1, keepdims=True)
    g_idx = first_argmax(gl, gmax)
    g_w = 1.0 / jnp.sum(jnp.exp(gl - gmax), axis=-1, keepdims=True)
    e_lo = n_groups + g_idx * per_group
    in_group = (lane_f >= e_lo) & (lane_f < e_lo + per_group)
    el = jnp.where(in_group, logits, -jnp.inf)
    emax = jnp.max(el, axis=-1, keepdims=True)
    pe = jnp.exp(el - emax)
    pe = pe / jnp.sum(pe, axis=-1, keepdims=True)
    p1 = jnp.max(pe, axis=-1, keepdims=True)
    i1 = first_argmax(jnp.where(in_group, pe, -1.0), p1)
    rest = jnp.where(in_group & (lane_f != i1), pe, -1.0)
    p2 = jnp.max(rest, axis=-1, keepdims=True)
    i2 = first_argmax(rest, p2)
    denom = p1 + p2
    w1 = p1 / denom * g_w
    w2 = p2 / denom * g_w
    @pl.when(pl.program_id(0) == 0)
    def _():
        count_ref[...] = jnp.zeros_like(count_ref)

    hit1 = lane_f == i1
    hit2 = lane_f == i2
    hits = jnp.where(hit1 | hit2, 1.0, 0.0)
    tm = hits.shape[0]
    earlier = (lax.broadcasted_iota(jnp.int32, (tm, tm), 1) < lax.broadcasted_iota(jnp.int32, (tm, tm), 0))
    before = count_ref[0:1, :] + jnp.dot(jnp.where(earlier, 1.0, 0.0).astype(BF16), hits.astype(BF16),
                                         preferred_element_type=F32)
    rank1 = jnp.sum(jnp.where(hit1, before, 0.0), axis=-1, keepdims=True)
    rank2 = jnp.sum(jnp.where(hit2, before, 0.0), axis=-1, keepdims=True)
    count_ref[...] = count_ref[...] + jnp.sum(hits, axis=0, keepdims=True)

    route = jnp.where(lane == 0, i1 - n_groups, 0.0)
    route = jnp.where(lane == 1, i2 - n_groups, route)
    route = jnp.where(lane == 2, w1, route)
    route = jnp.where(lane == 3, w2, route)
    route = jnp.where(lane == 4, rank1, route)
    route = jnp.where(lane == 5, rank2, route)
    route_ref[...] = route


def _merge(y_a, y_d, p, x2, wa, wd, wo, g_ffn, wr_hi, wr_lo, b_r, n_groups, per_group, tm):
    T, D = x2.shape
    va, vd = y_a.shape[1], y_d.shape[1]
    row = lambda w: pl.BlockSpec((tm, w), lambda i: (i, 0))
    return pl.pallas_call(
        functools.partial(_merge_kernel, n_groups=n_groups, per_group=per_group),
        grid=(T // tm,),
        in_specs=[row(va), row(vd),
                  pl.BlockSpec((tm, D), lambda i: (i, 0)),
                  pl.BlockSpec((tm, D), lambda i: (i, 1)),
                  row(D),
                  _const_spec((va, D)), _const_spec((vd, D)), _const_spec((D, D)), _const_spec((1, D)),
                  _const_spec((D, LANE)), _const_spec((D, LANE)), _const_spec((1, LANE))],
        out_specs=[row(D), row(D // 2), row(LANE), pl.BlockSpec((8, LANE), lambda i: (0, 0))],
        out_shape=[jax.ShapeDtypeStruct((T, D), F32),
                   jax.ShapeDtypeStruct((T, D // 2), jnp.uint32),
                   jax.ShapeDtypeStruct((T, LANE), F32),
                   jax.ShapeDtypeStruct((8, LANE), F32)],
        compiler_params=_cparams("arbitrary"),
        name="merge",
    )(y_a, y_d, p, p, x2, wa, wd, wo, g_ffn, wr_hi, wr_lo, b_r)


def _expert_kernel(tile_expert_ref, tile_flag_ref, tile_rows_ref, next_expert_ref, slot_ref,
                   xs_ref, wg_hbm, wu_hbm, wd_hbm, o_ref,
                   wg_f, wu_f, wd_f, wg_s, wu_s, wd_s, sems):
    i = pl.program_id(0)
    flag = tile_flag_ref[i]

    def weight_copies(expert, slot):
        return [pltpu.make_async_copy(hbm.at[expert], buf.at[slot], sems.at[slot, n])
                for n, (hbm, buf) in enumerate(((wg_hbm, wg_f), (wu_hbm, wu_f), (wd_hbm, wd_f)))]

    @pl.when(flag == 2)
    def _():
        slot = slot_ref[i]

        @pl.when(i == 0)
        def _():
            for copy in weight_copies(tile_expert_ref[i], slot):
                copy.start()

        for copy in weight_copies(tile_expert_ref[i], slot):
            copy.wait()
        nxt = next_expert_ref[i]

        @pl.when(nxt >= 0)
        def _():
            for copy in weight_copies(nxt, 1 - slot):
                copy.start()

        wg_s[...] = wg_f[slot].astype(BF16)
        wu_s[...] = wu_f[slot].astype(BF16)
        wd_s[...] = wd_f[slot].astype(BF16)

    @pl.when(flag > 0)
    def _():
        row = lax.broadcasted_iota(jnp.int32, xs_ref.shape, 0)
        packed = jnp.where(row < tile_rows_ref[i], xs_ref[...], jnp.uint32(0))
        left = lax.bitcast_convert_type(packed & jnp.uint32(0xFFFF0000), F32).astype(BF16)
        right = lax.bitcast_convert_type(packed << 16, F32).astype(BF16)
        x = jnp.concatenate([left, right], axis=1)
        g = jnp.dot(x, wg_s[...], preferred_element_type=F32)
        u = jnp.dot(x, wu_s[...], preferred_element_type=F32)
        hid = g * _sigmoid(g) * u
        o_ref[...] = jnp.dot(hid.astype(BF16), wd_s[...], preferred_element_type=F32).astype(o_ref.dtype)

    @pl.when(flag == 0)
    def _():
        o_ref[...] = jnp.zeros_like(o_ref)


def _experts(tile_expert, tile_flag, tile_rows, next_expert, slot, xs, w_gate, w_up, w_down, tm):
    n_rows, half = xs.shape
    E, D, Fd = w_gate.shape
    assert D == 2 * half
    n_tiles = n_rows // tm
    hbm = pl.BlockSpec(memory_space=pl.ANY)
    grid_spec = pltpu.PrefetchScalarGridSpec(
        num_scalar_prefetch=5,
        grid=(n_tiles,),
        in_specs=[pl.BlockSpec((tm, half), lambda i, *_: (i, 0)), hbm, hbm, hbm],
        out_specs=pl.BlockSpec((tm, D), lambda i, *_: (i, 0)),
        scratch_shapes=[pltpu.VMEM((2, D, Fd), F32), pltpu.VMEM((2, D, Fd), F32), pltpu.VMEM((2, Fd, D), F32),
                        pltpu.VMEM((D, Fd), BF16), pltpu.VMEM((D, Fd), BF16), pltpu.VMEM((Fd, D), BF16),
                        pltpu.SemaphoreType.DMA((2, 3))],
    )
    return pl.pallas_call(
        _expert_kernel,
        grid_spec=grid_spec,
        out_shape=jax.ShapeDtypeStruct((n_rows, D), BF16),
        compiler_params=_cparams("arbitrary"),
        name="experts",
    )(tile_expert, tile_flag, tile_rows, next_expert, slot, xs, w_gate, w_up, w_down)


SC_CORES = 2
SC_SUBCORES = 16
SC_ROWS = 32


def _dispatch_rows(table, pos, n_rows):
    T, W = table.shape
    workers = SC_CORES * SC_SUBCORES
    chunks = T // (workers * SC_ROWS)
    assert workers * chunks * SC_ROWS == T
    pos4 = pos.T.reshape(TOP_K, workers, chunks, SC_ROWS).transpose(1, 0, 2, 3)
    mesh = plsc.VectorSubcoreMesh(core_axis_name="c", subcore_axis_name="s")

    @functools.partial(
        pl.kernel, mesh=mesh,
        out_type=jax.ShapeDtypeStruct((n_rows, W), table.dtype),
        scratch_types=[pltpu.VMEM((TOP_K, chunks, SC_ROWS), jnp.int32),
                       pltpu.VMEM((2, SC_ROWS, W), table.dtype),
                       pltpu.SemaphoreType.DMA, pltpu.SemaphoreType.DMA, pltpu.SemaphoreType.DMA],
        name="dispatch_rows",
    )
    def dispatch(table_hbm, pos_hbm, out_hbm, pos_v, rows_v, sem_in, sem_out0, sem_out1):
        wid = lax.axis_index("s") * SC_CORES + lax.axis_index("c")
        base = wid * (chunks * SC_ROWS)
        pltpu.sync_copy(pos_hbm.at[wid], pos_v)

        def load(c):
            return pltpu.async_copy(table_hbm.at[pl.ds(base + c * SC_ROWS, SC_ROWS)], rows_v.at[c % 2], sem_in)

        pending = load(0)
        for c in range(chunks):
            pending.wait()
            if c + 1 < chunks:
                pending = load(c + 1)
            out0 = pltpu.async_copy(rows_v.at[c % 2], out_hbm.at[pos_v.at[0, c]], sem_out0)
            out1 = pltpu.async_copy(rows_v.at[c % 2], out_hbm.at[pos_v.at[1, c]], sem_out1)
            out0.wait()
            out1.wait()

    return dispatch(table, pos4)


def _final_kernel(x1_ref, y0_ref, y1_ref, route_ref, g_ref, o_ref):
    route = route_ref[...]
    lane = lax.broadcasted_iota(jnp.int32, route.shape, 1)
    w0 = jnp.sum(jnp.where(lane == TOP_K, route, 0.0), axis=-1, keepdims=True)
    w1 = jnp.sum(jnp.where(lane == TOP_K + 1, route, 0.0), axis=-1, keepdims=True)
    x = x1_ref[...] + (w0 * y0_ref[...].astype(F32) + w1 * y1_ref[...].astype(F32))
    o_ref[...] = x * lax.rsqrt(jnp.mean(x * x, axis=-1, keepdims=True) + EPS) * g_ref[...]


def _final(x1, y0, y1, route, g_final, tm):
    T, D = x1.shape
    row = pl.BlockSpec((tm, D), lambda i: (i, 0))
    return pl.pallas_call(
        _final_kernel,
        grid=(T // tm,),
        in_specs=[row, row, row, pl.BlockSpec((tm, LANE), lambda i: (i, 0)), _const_spec((1, D))],
        out_specs=row,
        out_shape=jax.ShapeDtypeStruct((T, D), F32),
        compiler_params=_cparams("parallel"),
        name="final",
    )(x1, y0, y1, route, g_final)


def _route_tables(ids, rank, counts, tm):
    T = ids.shape[0]
    A = T * TOP_K
    n_experts = counts.shape[0]
    padded = ((counts + tm - 1) // tm) * tm
    ends = jnp.cumsum(padded)
    starts = ends - padded
    onehot = ids[:, :, None] == jnp.arange(n_experts, dtype=jnp.int32)[None, None, :]
    pos = jnp.sum(jnp.where(onehot, starts[None, None, :], 0), axis=-1) + rank
    n_tiles = A // tm + n_experts
    tile_start = jnp.arange(n_tiles, dtype=jnp.int32) * tm
    tile_expert_raw = jnp.sum((ends[None, :] <= tile_start[:, None]).astype(jnp.int32), axis=1)
    valid = tile_start < ends[-1]
    last_expert = jnp.max(jnp.where(counts > 0, jnp.arange(n_experts, dtype=jnp.int32), 0))
    tile_expert = jnp.where(valid, jnp.minimum(tile_expert_raw, n_experts - 1), last_expert)
    first = jnp.concatenate([jnp.ones((1,), bool), tile_expert[1:] != tile_expert[:-1]])
    tile_flag = jnp.where(valid, jnp.where(first, 2, 1), 0).astype(jnp.int32)
    tile_rows = jnp.clip((starts + counts)[tile_expert] - tile_start, 0, tm)
    tile_rows = jnp.where(valid, tile_rows, 0).astype(jnp.int32)
    e_idx = jnp.arange(n_experts, dtype=jnp.int32)
    nonempty = counts > 0
    later = (e_idx[None, :] > e_idx[:, None]) & nonempty[None, :]
    next_nonempty = jnp.min(jnp.where(later, e_idx[None, :], n_experts), axis=1)
    next_nonempty = jnp.where(next_nonempty < n_experts, next_nonempty, -1).astype(jnp.int32)
    order = jnp.sum(((e_idx[None, :] < e_idx[:, None]) & nonempty[None, :]).astype(jnp.int32), axis=1)
    next_expert = next_nonempty[tile_expert]
    slot = (order[tile_expert] % 2).astype(jnp.int32)
    return pos, tile_expert, tile_flag, tile_rows, next_expert, slot


def _largest_tile(n, cap):
    t = min(n, cap)
    while n % t:
        t //= 2
    return t


def kernel(x, g_mix, w_in, conv_w, a_log, dt_bias, gdn_norm_w, diff_lambda, diff_norm_w, w_branch_a, w_branch_d,
           w_out, g_ffn, w_group, b_group, w_router, b_router, w_exp_gate, w_exp_up, w_exp_down, g_final):
    B, S, D = x.shape
    T = B * S
    depth = g_mix.shape[0]
    Hg = a_log.shape[-1]
    gdn_qk = Hg * GDN_DK
    gdn_v = Hg * GDN_DV
    diff_v = w_branch_d.shape[1]
    Hd = diff_v // DIFF_DV
    diff_qk = Hd * 2 * DIFF_DH
    n_groups = w_group.shape[-1]
    n_experts = w_router.shape[-1]
    per_group = n_experts // n_groups
    assert 4 * Hg <= LANE and n_groups + n_experts <= LANE
    assert S % GDN_CHUNK == 0 and D % LANE == 0

    ab_lo = 2 * gdn_qk + 2 * gdn_v
    ab_hi = ab_lo + 4 * Hg
    gates_lo = ab_hi + 2 * diff_qk + diff_v
    main_cols = 2 * D + ab_lo + 2 * diff_qk + diff_v
    gdn_off = 2 * D
    q_d_off = gdn_off + ab_lo
    k_d_off = q_d_off + diff_qk
    v_d_off = k_d_off + diff_qk

    tm_norm = _largest_tile(T, 512)
    tm_proj = _largest_tile(T, 1024)
    tn_proj = _largest_tile(main_cols, 1024)
    tm_merge = _largest_tile(T, 256)
    tm_exp = _largest_tile(T * TOP_K, 256)
    tm_final = _largest_tile(T, 512)

    slopes = jnp.exp2(-8.0 * (jnp.arange(Hd, dtype=F32) + 1.0) / Hd)
    x2 = x.reshape(T, D)
    for layer in range(depth):
        lam_init = 0.8 - 0.6 * math.exp(-0.3 * layer)
        w_in_l = w_in[layer]
        w_main = jnp.concatenate([w_in_l[:, gates_lo:], w_in_l[:, :ab_lo], w_in_l[:, ab_hi:gates_lo]],
                                 axis=1).astype(BF16)
        w_ab = jnp.pad(w_in_l[:, ab_lo:ab_hi], ((0, 0), (0, LANE - 4 * Hg))).astype(BF16)
        alog_pad = jnp.pad(a_log[layer].reshape(1, 2 * Hg), ((0, 0), (0, LANE - 2 * Hg)))
        dtb_pad = jnp.pad(dt_bias[layer].reshape(1, 2 * Hg), ((0, 0), (0, LANE - 2 * Hg)))

        h, gb = _norm_proj(x2, g_mix[layer].reshape(1, D), w_ab, alog_pad, dtb_pad, 2 * Hg, tm_norm)
        p = _matmul(h, w_main, tm_proj, tn_proj, BF16)

        grow = gb[:, :2 * Hg].reshape(B, S // GDN_CHUNK, GDN_CHUNK, 2 * Hg).transpose(0, 3, 1, 2)
        conv_w8 = jnp.pad(conv_w[layer], ((0, 8 - GDN_CONV), (0, 0)))
        y_a = _gdn(p, gdn_off // LANE, conv_w8, gb, grow, gdn_norm_w[layer].reshape(1, GDN_DV), B, S, Hg)
        y_d = _diff_attn(p, slopes, diff_lambda[layer], diff_norm_w[layer].reshape(DIFF_DV, 1), B, S, Hd,
                         q_d_off // LANE, k_d_off // LANE, v_d_off // LANE, lam_init)

        w_r = jnp.pad(jnp.concatenate([w_group[layer], w_router[layer]], axis=1),
                      ((0, 0), (0, LANE - n_groups - n_experts)))
        wr_hi = w_r.astype(BF16)
        wr_lo = (w_r - wr_hi.astype(F32)).astype(BF16)
        b_r = jnp.pad(jnp.concatenate([b_group[layer], b_router[layer]]).reshape(1, -1),
                      ((0, 0), (0, LANE - n_groups - n_experts)))
        x1, h2, route, counts = _merge(y_a, y_d, p, x2,
                                       w_branch_a[layer].astype(BF16), w_branch_d[layer].astype(BF16),
                                       w_out[layer].astype(BF16), g_ffn[layer].reshape(1, D),
                                       wr_hi, wr_lo, b_r, n_groups, per_group, tm_merge)

        ids = route[:, 0:TOP_K].astype(jnp.int32)
        rank = route[:, 2 * TOP_K:3 * TOP_K].astype(jnp.int32)
        counts = counts[0, n_groups:n_groups + n_experts].astype(jnp.int32)
        pos, tile_expert, tile_flag, tile_rows, next_expert, slot = _route_tables(ids, rank, counts, tm_exp)
        xs = _dispatch_rows(h2, pos, tile_expert.shape[0] * tm_exp)
        ys = _experts(tile_expert, tile_flag, tile_rows, next_expert, slot, xs, w_exp_gate[layer],
                      w_exp_up[layer], w_exp_down[layer], tm_exp)
        y0 = jnp.take(ys, pos[:, 0], axis=0)
        y1 = jnp.take(ys, pos[:, 1], axis=0)
        if layer + 1 < depth:
            wts = route[:, TOP_K:2 * TOP_K]
            x2 = x1 + (wts[:, 0:1] * y0.astype(F32) + wts[:, 1:2] * y1.astype(F32))
    out = _final(x1, y0, y1, route, g_final.reshape(1, D), tm_final)
    return out.reshape(B, S, D)
```

```python
import functools
import math

import jax
import jax.numpy as jnp
from jax import lax
from jax.experimental import pallas as pl
from jax.experimental.pallas import tpu as pltpu
from jax.experimental.pallas import tpu_sc as plsc

F32 = jnp.float32
BF16 = jnp.bfloat16
EPS = 1e-6
LANE = 128
GDN_DK = 128
GDN_DV = 128
GDN_CONV = 5
GDN_CHUNK = 64
DIFF_DH = 64
DIFF_DV = 2 * DIFF_DH
TOP_K = 2
LOG2E = 1.4426950408889634
NEG_BIG = -1e30
VMEM_LIMIT_BYTES = 56 * 1024 * 1024

_HI = lax.Precision.HIGHEST


def _cparams(*sem):
    return pltpu.CompilerParams(dimension_semantics=sem, vmem_limit_bytes=VMEM_LIMIT_BYTES)


def _const_spec(shape):
    nd = len(shape)
    return pl.BlockSpec(shape, lambda *_: (0,) * nd, pipeline_mode=pl.Buffered(1))


def _mm(a, b):
    return jnp.dot(a.astype(BF16), b.astype(BF16), preferred_element_type=F32)


def _split3(x):
    hi = x.astype(BF16)
    r = x - hi.astype(F32)
    mid = r.astype(BF16)
    lo = (r - mid.astype(F32)).astype(BF16)
    return hi, mid, lo


def _softplus(x):
    return jnp.maximum(x, 0.0) + jnp.log(1.0 + jnp.exp(-jnp.abs(x)))


def _sigmoid(x):
    return 1.0 / (1.0 + jnp.exp(-x))


def _norm_proj_kernel(x_ref, g_ref, wab_ref, alog_ref, dtb_ref, h_ref, gb_ref, *, n_dir_heads):
    x = x_ref[...]
    h = x * lax.rsqrt(jnp.mean(x * x, axis=-1, keepdims=True) + EPS) * g_ref[...]
    hb = h.astype(BF16)
    h_ref[...] = hb
    ab = jnp.dot(hb, wab_ref[...], preferred_element_type=F32)
    g = -jnp.exp(alog_ref[...]) * _softplus(ab + dtb_ref[...])
    beta = _sigmoid(ab)
    tm = x.shape[0]
    row = lax.broadcasted_iota(jnp.int32, (tm, tm), 0)
    col = lax.broadcasted_iota(jnp.int32, (tm, tm), 1)
    same = (row // GDN_CHUNK) == (col // GDN_CHUNK)
    prefix = jnp.where(same & (col <= row), 1.0, 0.0).astype(BF16)
    suffix = jnp.where(same & (col >= row), 1.0, 0.0).astype(BF16)
    pieces = _split3(g)
    cs_f = sum(jnp.dot(prefix, p, preferred_element_type=F32) for p in pieces)
    cs_b = sum(jnp.dot(suffix, p, preferred_element_type=F32) for p in pieces)
    lane = lax.broadcasted_iota(jnp.int32, g.shape, 1)
    gb_ref[...] = jnp.where(lane < n_dir_heads // 2, cs_f, jnp.where(lane < n_dir_heads, cs_b, beta))


def _norm_proj(x2, g_mix, w_ab, alog_pad, dtb_pad, n_dir_heads, tm):
    T, D = x2.shape
    return pl.pallas_call(
        functools.partial(_norm_proj_kernel, n_dir_heads=n_dir_heads),
        grid=(T // tm,),
        in_specs=[pl.BlockSpec((tm, D), lambda i: (i, 0)),
                  _const_spec((1, D)), _const_spec((D, LANE)), _const_spec((1, LANE)), _const_spec((1, LANE))],
        out_specs=[pl.BlockSpec((tm, D), lambda i: (i, 0)),
                   pl.BlockSpec((tm, LANE), lambda i: (i, 0))],
        out_shape=[jax.ShapeDtypeStruct((T, D), BF16),
                   jax.ShapeDtypeStruct((T, LANE), F32)],
        compiler_params=_cparams("parallel"),
        name="norm_proj",
    )(x2, g_mix, w_ab, alog_pad, dtb_pad)


REORDER_COLS = 512


def _reorder_cast_kernel(w_ref, o_ref, *, segments):
    total = w_ref.shape[1]
    off = 0
    for start, width in segments:
        for c in range(0, width, REORDER_COLS):
            n = min(REORDER_COLS, width - c)
            lo = ((start + c) // LANE) * LANE
            shift = start + c - lo
            hi = min(total, lo + shift + n + (LANE - (shift + n) % LANE) % LANE)
            o_ref[:, off:off + n] = w_ref[:, lo:hi][:, shift:shift + n].astype(o_ref.dtype)
            off += n


def _reorder_cast(w, segments, tr):
    rows, cols = w.shape
    out_cols = sum(width for _, width in segments)
    return pl.pallas_call(
        functools.partial(_reorder_cast_kernel, segments=segments),
        grid=(rows // tr,),
        in_specs=[pl.BlockSpec((tr, cols), lambda i: (i, 0))],
        out_specs=pl.BlockSpec((tr, out_cols), lambda i: (i, 0)),
        out_shape=jax.ShapeDtypeStruct((rows, out_cols), BF16),
        compiler_params=_cparams("parallel"),
        name="reorder_cast",
    )(w)


def _matmul_kernel(a_ref, b_ref, o_ref):
    o_ref[...] = jnp.dot(a_ref[...], b_ref[...], preferred_element_type=F32).astype(o_ref.dtype)


def _matmul(a, b, tm, tn, out_dtype):
    M, K = a.shape
    N = b.shape[1]
    return pl.pallas_call(
        _matmul_kernel,
        grid=(N // tn, M // tm),
        in_specs=[pl.BlockSpec((tm, K), lambda j, i: (i, 0)),
                  pl.BlockSpec((K, tn), lambda j, i: (0, j))],
        out_specs=pl.BlockSpec((tm, tn), lambda j, i: (i, j)),
        out_shape=jax.ShapeDtypeStruct((M, N), out_dtype),
        compiler_params=_cparams("parallel", "parallel"),
        name="in_proj",
    )(a, b)


def _bmm_tn(a, b):
    return jnp.stack([lax.dot_general(a[n], b[n], (((0,), (0,)), ((), ())), preferred_element_type=F32)
                      for n in range(a.shape[0])])


def _gdn_kernel(q_ref, k_ref, v_ref, z_ref, cwq_ref, cwk_ref, cwv_ref, gb_ref, grow_ref, nw_ref,
                y_ref,
                xpad, qn, kn, vn, u_s, w_s, qg_s, at_s, tm_s, tn_s, egl_s, st_s,
                *, n_heads, seq, conv_rows, prep_chunks):
    C = GDN_CHUNK
    n_chunks = seq // C
    head = pl.program_id(1)
    pad = 8
    half = GDN_CONV // 2

    zeros_pad = jnp.zeros((pad, LANE), F32)
    xpad[0:pad, :] = zeros_pad
    xpad[pad + seq:pad + seq + pad, :] = zeros_pad
    for src, cw_ref, dst, mode in ((q_ref, cwq_ref, qn, "q"), (k_ref, cwk_ref, kn, "k"), (v_ref, cwv_ref, vn, "v")):
        xpad[pad:pad + seq, :] = src[...].astype(F32)
        for t in range(seq // conv_rows):
            r0 = t * conv_rows
            acc = jnp.zeros((conv_rows, LANE), F32)
            for j in range(GDN_CONV):
                lo = pad + r0 + j - half
                acc = acc + xpad[lo:lo + conv_rows, :] * cw_ref[j:j + 1, :]
            y = acc * _sigmoid(acc)
            if mode != "v":
                y = y * lax.rsqrt(jnp.sum(y * y, axis=-1, keepdims=True) + EPS)
            if mode == "q":
                y = y * (GDN_DK ** -0.5)
            dst[r0:r0 + conv_rows, :] = y

    nb = prep_chunks
    R = nb * C
    nb2 = 2 * nb
    bi = lax.broadcasted_iota(jnp.int32, (nb2, C, C), 0)
    ii = lax.broadcasted_iota(jnp.int32, (nb2, C, C), 1)
    jj = lax.broadcasted_iota(jnp.int32, (nb2, C, C), 2)
    fwd = bi < nb
    incl = (fwd & (ii >= jj)) | (~fwd & (ii <= jj))
    strict = (fwd & (ii > jj)) | (~fwd & (ii < jj))
    eye = jnp.where(ii == jj, 1.0, 0.0).astype(F32)
    lane = lax.broadcasted_iota(jnp.int32, (R, LANE), 1)
    both = lambda x: jnp.concatenate([x, x], axis=0)

    def prep(t, carry):
        r0 = pl.multiple_of(t * R, R)
        n0 = pl.multiple_of(t * nb, nb)
        q3 = qn[pl.ds(r0, R), :].reshape(nb, C, LANE)
        k3 = kn[pl.ds(r0, R), :].reshape(nb, C, LANE)
        v3 = vn[pl.ds(r0, R), :].reshape(nb, C, LANE)
        gb_blk = gb_ref[pl.ds(r0, R), :]
        kq = jnp.concatenate([k3, q3], axis=1).astype(BF16)
        gram = jnp.einsum("nik,njk->nij", kq, k3.astype(BF16), preferred_element_type=F32)
        kk, qk = both(gram[:, :C, :]), both(gram[:, C:, :])
        gcb, bt, grow, gl = [], [], [], []
        for d in range(2):
            colidx = d * n_heads + head
            gcb_d = jnp.sum(jnp.where(lane == colidx, gb_blk, 0.0), axis=-1, keepdims=True).reshape(nb, C, 1)
            gcb.append(gcb_d)
            bt.append(jnp.sum(jnp.where(lane == 2 * n_heads + colidx, gb_blk, 0.0), axis=-1,
                              keepdims=True).reshape(nb, C, 1))
            grow.append(grow_ref[0, colidx, pl.ds(n0, nb), :].reshape(nb, 1, C))
            gl.append(gcb_d[:, C - 1:C, :] if d == 0 else gcb_d[:, 0:1, :])
        gcb, bt, grow, gl = (jnp.concatenate(x, axis=0) for x in (gcb, bt, grow, gl))
        k2, q2, v2 = both(k3), both(q3), both(v3)

        decay = jnp.exp(jnp.where(incl, gcb - grow, -jnp.inf))
        L = jnp.where(strict, bt * kk * decay, 0.0)
        attn = qk * decay
        ainv = eye - L
        P = L
        for _ in range(int(math.log2(C)) - 1):
            Pb = P.astype(BF16)
            P = jnp.einsum("nij,njk->nik", Pb, Pb, preferred_element_type=F32)
            ainv = ainv + jnp.einsum("nij,njk->nik", ainv.astype(BF16), P.astype(BF16),
                                     preferred_element_type=F32)
        eg = jnp.exp(gcb)
        rhs = jnp.concatenate([v2 * bt, k2 * (bt * eg)], axis=-1).astype(BF16)
        uw = jnp.einsum("nij,njd->nid", ainv.astype(BF16), rhs, preferred_element_type=F32).astype(BF16)
        kg = (k2 * jnp.exp(gl - gcb)).astype(BF16)
        trans = _bmm_tn(kg, uw)
        qg = (q2 * eg).astype(BF16)
        egl = jnp.broadcast_to(jnp.exp(gl).reshape(nb2, 1), (nb2, LANE))
        for d in range(2):
            sl = slice(d * nb, (d + 1) * nb)
            u_s[d, pl.ds(r0, R), :] = uw[sl, :, :LANE].reshape(R, LANE)
            w_s[d, pl.ds(r0, R), :] = uw[sl, :, LANE:].reshape(R, LANE)
            qg_s[d, pl.ds(r0, R), :] = qg[sl].reshape(R, LANE)
            at_s[d, pl.ds(r0, R), :] = attn[sl].reshape(R, C).astype(BF16)
            tn_s[d, pl.ds(n0 * GDN_DK, nb * GDN_DK), :] = trans[sl, :, :LANE].reshape(nb * GDN_DK, LANE).astype(BF16)
            tm_s[d, pl.ds(n0 * GDN_DK, nb * GDN_DK), :] = (-trans[sl, :, LANE:]).reshape(nb * GDN_DK, LANE).astype(BF16)
            egl_s[d, pl.ds(n0, nb), :] = egl[sl]
        return carry

    lax.fori_loop(0, n_chunks // nb, prep, 0)

    def chunk_step(d, n, state):
        r = pl.multiple_of(n * GDN_DK, GDN_DK)
        sb = state.astype(BF16)
        st_s[d, pl.ds(r, GDN_DK), :] = sb
        return (state * egl_s[d, pl.ds(n, 1), :]
                + jnp.dot(tm_s[d, pl.ds(r, GDN_DK), :], sb, preferred_element_type=F32)
                + tn_s[d, pl.ds(r, GDN_DK), :].astype(F32))

    def scan(n, carry):
        sf, sbw = carry
        return chunk_step(0, n, sf), chunk_step(1, n_chunks - 1 - n, sbw)

    zero_state = jnp.zeros((GDN_DK, GDN_DV), F32)
    lax.fori_loop(0, n_chunks, scan, (zero_state, zero_state))

    def emit(t, carry):
        r0 = pl.multiple_of(t * R, R)
        s0 = pl.multiple_of(t * nb * GDN_DK, nb * GDN_DK)
        two = lambda ref, start, rows: jnp.concatenate([ref[0, pl.ds(start, rows), :], ref[1, pl.ds(start, rows), :]])
        st = two(st_s, s0, nb * GDN_DK).reshape(nb2, GDN_DK, GDN_DV)
        wq = jnp.concatenate([two(w_s, r0, R).reshape(nb2, C, LANE), two(qg_s, r0, R).reshape(nb2, C, LANE)], axis=1)
        ws_qs = jnp.einsum("ncd,nde->nce", wq, st, preferred_element_type=F32)
        vnew = two(u_s, r0, R).reshape(nb2, C, LANE).astype(F32) - ws_qs[:, :C, :]
        o = ws_qs[:, C:, :] + jnp.einsum("ncs,nse->nce", two(at_s, r0, R).reshape(nb2, C, C), vnew.astype(BF16),
                                         preferred_element_type=F32)
        o = (o[:nb] + o[nb:]).reshape(R, LANE)
        on = o * lax.rsqrt(jnp.mean(o * o, axis=-1, keepdims=True) + EPS) * nw_ref[...]
        z = z_ref[pl.ds(r0, R), :].astype(F32)
        y_ref[pl.ds(r0, R), :] = (on * (z * _sigmoid(z))).astype(y_ref.dtype)
        return carry

    lax.fori_loop(0, n_chunks // nb, emit, 0)


def _gdn(p, p_off, conv_w8, gb, grow, norm_w, batch, seq, n_heads):
    T = batch * seq
    qk_blocks = n_heads
    n_chunks = seq // GDN_CHUNK
    conv_rows = min(512, seq)
    prep_chunks = min(16, n_chunks)
    blk = lambda off: pl.BlockSpec((seq, LANE), lambda b, h: (b, p_off + off + h))
    cw = lambda off: pl.BlockSpec((8, LANE), lambda b, h: (0, off + h))
    return pl.pallas_call(
        functools.partial(_gdn_kernel, n_heads=n_heads, seq=seq, conv_rows=conv_rows, prep_chunks=prep_chunks),
        grid=(batch, n_heads),
        in_specs=[blk(0), blk(qk_blocks), blk(2 * qk_blocks), blk(3 * qk_blocks),
                  cw(0), cw(qk_blocks), cw(2 * qk_blocks),
                  pl.BlockSpec((seq, LANE), lambda b, h: (b, 0), pipeline_mode=pl.Buffered(1)),
                  pl.BlockSpec((1, 2 * n_heads, n_chunks, GDN_CHUNK), lambda b, h: (b, 0, 0, 0),
                               pipeline_mode=pl.Buffered(1)),
                  _const_spec((1, LANE))],
        out_specs=pl.BlockSpec((seq, LANE), lambda b, h: (b, h)),
        out_shape=jax.ShapeDtypeStruct((T, n_heads * GDN_DV), BF16),
        scratch_shapes=[
            pltpu.VMEM((seq + 16, LANE), F32),
            pltpu.VMEM((seq, LANE), F32),
            pltpu.VMEM((seq, LANE), F32),
            pltpu.VMEM((seq, LANE), F32),
            pltpu.VMEM((2, seq, LANE), BF16),
            pltpu.VMEM((2, seq, LANE), BF16),
            pltpu.VMEM((2, seq, LANE), BF16),
            pltpu.VMEM((2, seq, GDN_CHUNK), BF16),
            pltpu.VMEM((2, n_chunks * GDN_DK, GDN_DV), BF16),
            pltpu.VMEM((2, n_chunks * GDN_DK, GDN_DV), BF16),
            pltpu.VMEM((2, max(n_chunks, 8), LANE), F32),
            pltpu.VMEM((2, n_chunks * GDN_DK, GDN_DV), BF16),
        ],
        compiler_params=_cparams("parallel", "parallel"),
        name="gdn",
    )(p, p, p, p, conv_w8, conv_w8, conv_w8, gb, grow, norm_w)


def _diff_attn_kernel(slopes_ref, q_ref, k_ref, v_ref, lam_ref, nw_ref, o_ref,
                      qT_s, vT_s, klo_s, khi_s, rel_s, s_s, p_s, acc_s,
                      *, seq, tq, tk, lam_init):
    head = pl.program_id(1)
    slope2 = slopes_ref[head] * LOG2E
    scale2 = (DIFF_DH ** -0.5) * LOG2E
    n_kv = seq // tk
    maps = (klo_s, khi_s)

    rows = min(512, seq)
    for t in range(seq // rows):
        r0 = t * rows
        qt = (q_ref[r0:r0 + rows, :].astype(F32) * scale2).T.astype(BF16)
        frow = lax.broadcasted_iota(jnp.int32, (LANE, rows), 0)
        fcol = lax.broadcasted_iota(jnp.int32, (LANE, rows), 1)
        qa, qb, qc = (x.astype(F32) for x in _split3(((r0 + fcol) % tq).astype(F32) * slope2))
        qfeat = jnp.where(frow < 3, 1.0, jnp.where(frow == 3, -qa, jnp.where(frow == 4, -qb,
                                                   jnp.where(frow == 5, -qc, 0.0)))).astype(BF16)
        for sgn, feat in enumerate((qfeat, -qfeat)):
            qT_s[sgn, 0:LANE, r0:r0 + rows] = qt
            qT_s[sgn, LANE:2 * LANE, r0:r0 + rows] = feat
        vT_s[0:LANE, r0:r0 + rows] = v_ref[r0:r0 + rows, :].astype(F32).T.astype(BF16)
        vT_s[LANE:LANE + 16, r0:r0 + rows] = jnp.ones((16, rows), BF16)
        kt = k_ref[r0:r0 + rows, :]
        krow = lax.broadcasted_iota(jnp.int32, (rows, LANE), 0)
        lane = lax.broadcasted_iota(jnp.int32, (rows, LANE), 1)
        ka, kb, kc = (x.astype(F32) for x in _split3(((r0 + krow) % tk).astype(F32) * slope2))
        kfeat = jnp.where(lane == 0, ka, jnp.where(lane == 1, kb, jnp.where(lane == 2, kc,
                                                   jnp.where(lane < 6, 1.0, 0.0)))).astype(BF16)
        klo_s[r0:r0 + rows, 0:LANE] = jnp.where(lane < DIFF_DH, kt, jnp.zeros_like(kt))
        khi_s[r0:r0 + rows, 0:LANE] = jnp.where(lane >= DIFF_DH, kt, jnp.zeros_like(kt))
        klo_s[r0:r0 + rows, LANE:2 * LANE] = kfeat
        khi_s[r0:r0 + rows, LANE:2 * LANE] = kfeat
    kr = lax.broadcasted_iota(jnp.int32, (tk, tq), 0)
    qcol = lax.broadcasted_iota(jnp.int32, (tk, tq), 1)
    rel_s[...] = (qcol - kr).astype(F32) * slope2

    lf = lam_ref[...]
    lam = (jnp.exp(jnp.sum(lf[0:1, :] * lf[1:2, :], axis=-1, keepdims=True))
           - jnp.exp(jnp.sum(lf[2:3, :] * lf[3:4, :], axis=-1, keepdims=True)) + lam_init)

    def q_tile(i, carry):
        c_q = pl.multiple_of(i * tq, tq)
        j_diag = (i * tq) // tk

        def scores(t):
            j = j_diag if t == 0 else lax.rem(j_diag + t, n_kv)
            r_k = pl.multiple_of(j * tk, tk)
            d0 = jnp.asarray(i * tq - j * tk)
            if t == 0:
                bias = jnp.abs(rel_s[...] + d0.astype(F32) * slope2)
                for mp, k_s in enumerate(maps):
                    s_s[t % 2, mp] = jnp.dot(k_s[pl.ds(r_k, tk), 0:LANE], qT_s[0, 0:LANE, pl.ds(c_q, tq)],
                                             preferred_element_type=F32) - bias
                return r_k, 0.0
            sgn = jnp.asarray(j > j_diag).astype(jnp.int32)
            for mp, k_s in enumerate(maps):
                s_s[t % 2, mp] = jnp.dot(k_s[pl.ds(r_k, tk), :], qT_s[sgn, :, pl.ds(c_q, tq)],
                                         preferred_element_type=F32)
            return r_k, jnp.abs(d0).astype(F32) * slope2

        def weighted_values(t, r_k, alphas):
            vT = vT_s[:, pl.ds(r_k, tk)]
            for mp in range(len(maps)):
                upd = jnp.dot(vT, p_s[t % 2, mp], preferred_element_type=F32)
                acc_s[mp] = upd if t == 0 else acc_s[mp] * alphas[mp] + upd

        m = [jnp.full((1, tq), NEG_BIG, F32) for _ in maps]
        nxt = scores(0)
        prev = None
        for t in range(n_kv):
            r_k, const = nxt
            if t + 1 < n_kv:
                nxt = scores(t + 1)
            alphas = []
            for mp in range(len(maps)):
                m_new = jnp.maximum(m[mp], jnp.max(s_s[t % 2, mp], axis=0, keepdims=True) - const)
                alphas.append(jnp.exp2(m[mp] - m_new))
                p_s[t % 2, mp] = jnp.exp2(s_s[t % 2, mp] - (m_new + const)).astype(BF16)
                m[mp] = m_new
            if prev is not None:
                weighted_values(t - 1, *prev)
            prev = (r_k, alphas)
        weighted_values(n_kv - 1, *prev)

        o1, o2 = acc_s[0], acc_s[1]
        od = o1[0:DIFF_DV] / o1[DIFF_DV:DIFF_DV + 1] - lam * (o2[0:DIFF_DV] / o2[DIFF_DV:DIFF_DV + 1])
        yn = od * lax.rsqrt(jnp.mean(od * od, axis=0, keepdims=True) + EPS) * nw_ref[...] * (1.0 - lam_init)
        o_ref[pl.ds(c_q, tq), :] = yn.T.astype(o_ref.dtype)
        return carry

    lax.fori_loop(0, seq // tq, q_tile, 0)


def _diff_attn(p, slopes, diff_lambda, norm_w_col, batch, seq, n_heads, q_off, k_off, v_off, lam_init):
    T = batch * seq
    tq = min(256, seq)
    tk = min(512, seq)
    blk = lambda off: pl.BlockSpec((seq, LANE), lambda b, h: (b, off + h))
    return pl.pallas_call(
        functools.partial(_diff_attn_kernel, seq=seq, tq=tq, tk=tk, lam_init=lam_init),
        grid=(batch, n_heads),
        in_specs=[pl.BlockSpec(memory_space=pltpu.SMEM),
                  blk(q_off), blk(k_off), blk(v_off),
                  _const_spec((4, DIFF_DH)), _const_spec((DIFF_DV, 1))],
        out_specs=pl.BlockSpec((seq, LANE), lambda b, h: (b, h)),
        out_shape=jax.ShapeDtypeStruct((T, n_heads * DIFF_DV), BF16),
        scratch_shapes=[
            pltpu.VMEM((2, 2 * LANE, seq), BF16),
            pltpu.VMEM((DIFF_DV + 16, seq), BF16),
            pltpu.VMEM((seq, 2 * LANE), BF16),
            pltpu.VMEM((seq, 2 * LANE), BF16),
            pltpu.VMEM((tk, tq), F32),
            pltpu.VMEM((2, 2, tk, tq), F32),
            pltpu.VMEM((2, 2, tk, tq), BF16),
            pltpu.VMEM((2, DIFF_DV + 16, tq), F32),
        ],
        compiler_params=_cparams("parallel", "parallel"),
        name="diff_attn",
    )(slopes, p, p, p, diff_lambda, norm_w_col)


def _merge_kernel(ya_ref, yd_ref, ga_ref, gd_ref, x_ref, wa_ref, wd_ref, wo_ref, gffn_ref,
                  wr_hi_ref, wr_lo_ref, br_ref, x1_ref, h2_ref, route_ref, count_ref, *, n_groups, per_group):
    ma = jnp.dot(ya_ref[...], wa_ref[...], preferred_element_type=F32)
    md = jnp.dot(yd_ref[...], wd_ref[...], preferred_element_type=F32)
    merged = _sigmoid(ga_ref[...].astype(F32)) * ma + _sigmoid(gd_ref[...].astype(F32)) * md
    x1 = x_ref[...] + jnp.dot(merged.astype(BF16), wo_ref[...], preferred_element_type=F32)
    x1_ref[...] = x1
    h2 = x1 * lax.rsqrt(jnp.mean(x1 * x1, axis=-1, keepdims=True) + EPS) * gffn_ref[...]
    bits = lax.bitcast_convert_type(h2.astype(BF16).astype(F32), jnp.uint32)
    half = bits.shape[1] // 2
    h2_ref[...] = bits[:, :half] | (bits[:, half:] >> 16)

    h_hi = h2.astype(BF16)
    h_lo = (h2 - h_hi.astype(F32)).astype(BF16)
    logits = (jnp.dot(h_hi, wr_hi_ref[...], preferred_element_type=F32)
              + jnp.dot(h_hi, wr_lo_ref[...], preferred_element_type=F32)
              + jnp.dot(h_lo, wr_hi_ref[...], preferred_element_type=F32)) + br_ref[...]
    lane = lax.broadcasted_iota(jnp.int32, logits.shape, 1)
    lane_f = lane.astype(F32)
    big = float(LANE)

    def first_argmax(vals, vmax):
        return jnp.min(jnp.where(vals == vmax, lane_f, big), axis=-1, keepdims=True)

    gl = jnp.where(lane < n_groups, logits, -jnp.inf)
    gmax = jnp.max(gl, axis=-1, keepdims=True)
    g_idx = first_argmax(gl, gmax)
    g_w = 1.0 / jnp.sum(jnp.exp(gl - gmax), axis=-1, keepdims=True)
    e_lo = n_groups + g_idx * per_group
    in_group = (lane_f >= e_lo) & (lane_f < e_lo + per_group)
    el = jnp.where(in_group, logits, -jnp.inf)
    emax = jnp.max(el, axis=-1, keepdims=True)
    pe = jnp.exp(el - emax)
    pe = pe / jnp.sum(pe, axis=-1, keepdims=True)
    p1 = jnp.max(pe, axis=-1, keepdims=True)
    i1 = first_argmax(jnp.where(in_group, pe, -1.0), p1)
    rest = jnp.where(in_group & (lane_f != i1), pe, -1.0)
    p2 = jnp.max(rest, axis=-1, keepdims=True)
    i2 = first_argmax(rest, p2)
    denom = p1 + p2
    w1 = p1 / denom * g_w
    w2 = p2 / denom * g_w
    @pl.when(pl.program_id(0) == 0)
    def _():
        count_ref[...] = jnp.zeros_like(count_ref)

    hit1 = lane_f == i1
    hit2 = lane_f == i2
    hits = jnp.where(hit1 | hit2, 1.0, 0.0)
    tm = hits.shape[0]
    earlier = (lax.broadcasted_iota(jnp.int32, (tm, tm), 1) < lax.broadcasted_iota(jnp.int32, (tm, tm), 0))
    before = count_ref[0:1, :] + jnp.dot(jnp.where(earlier, 1.0, 0.0).astype(BF16), hits.astype(BF16),
                                         preferred_element_type=F32)
    rank1 = jnp.sum(jnp.where(hit1, before, 0.0), axis=-1, keepdims=True)
    rank2 = jnp.sum(jnp.where(hit2, before, 0.0), axis=-1, keepdims=True)
    count_ref[...] = count_ref[...] + jnp.sum(hits, axis=0, keepdims=True)

    route = jnp.where(lane == 0, i1 - n_groups, 0.0)
    route = jnp.where(lane == 1, i2 - n_groups, route)
    route = jnp.where(lane == 2, w1, route)
    route = jnp.where(lane == 3, w2, route)
    route = jnp.where(lane == 4, rank1, route)
    route = jnp.where(lane == 5, rank2, route)
    route_ref[...] = route


def _merge(y_a, y_d, p, x2, wa, wd, wo, g_ffn, wr_hi, wr_lo, b_r, n_groups, per_group, tm):
    T, D = x2.shape
    va, vd = y_a.shape[1], y_d.shape[1]
    row = lambda w: pl.BlockSpec((tm, w), lambda i: (i, 0))
    return pl.pallas_call(
        functools.partial(_merge_kernel, n_groups=n_groups, per_group=per_group),
        grid=(T // tm,),
        in_specs=[row(va), row(vd),
                  pl.BlockSpec((tm, D), lambda i: (i, 0)),
                  pl.BlockSpec((tm, D), lambda i: (i, 1)),
                  row(D),
                  _const_spec((va, D)), _const_spec((vd, D)), _const_spec((D, D)), _const_spec((1, D)),
                  _const_spec((D, LANE)), _const_spec((D, LANE)), _const_spec((1, LANE))],
        out_specs=[row(D), row(D // 2), row(LANE), pl.BlockSpec((8, LANE), lambda i: (0, 0))],
        out_shape=[jax.ShapeDtypeStruct((T, D), F32),
                   jax.ShapeDtypeStruct((T, D // 2), jnp.uint32),
                   jax.ShapeDtypeStruct((T, LANE), F32),
                   jax.ShapeDtypeStruct((8, LANE), F32)],
        compiler_params=_cparams("arbitrary"),
        name="merge",
    )(y_a, y_d, p, p, x2, wa, wd, wo, g_ffn, wr_hi, wr_lo, b_r)


def _expert_kernel(tile_expert_ref, tile_flag_ref, tile_rows_ref, next_expert_ref, slot_ref,
                   xs_ref, wg_hbm, wu_hbm, wd_hbm, o_ref,
                   wg_f, wu_f, wd_f, wg_s, wu_s, wd_s, sems):
    i = pl.program_id(0)
    flag = tile_flag_ref[i]

    def weight_copies(expert, slot):
        return [pltpu.make_async_copy(hbm.at[expert], buf.at[slot], sems.at[slot, n])
                for n, (hbm, buf) in enumerate(((wg_hbm, wg_f), (wu_hbm, wu_f), (wd_hbm, wd_f)))]

    @pl.when(flag == 2)
    def _():
        slot = slot_ref[i]

        @pl.when(i == 0)
        def _():
            for copy in weight_copies(tile_expert_ref[i], slot):
                copy.start()

        for copy in weight_copies(tile_expert_ref[i], slot):
            copy.wait()
        nxt = next_expert_ref[i]

        @pl.when(nxt >= 0)
        def _():
            for copy in weight_copies(nxt, 1 - slot):
                copy.start()

        wg_s[...] = wg_f[slot].astype(BF16)
        wu_s[...] = wu_f[slot].astype(BF16)
        wd_s[...] = wd_f[slot].astype(BF16)

    @pl.when(flag > 0)
    def _():
        row = lax.broadcasted_iota(jnp.int32, xs_ref.shape, 0)
        packed = jnp.where(row < tile_rows_ref[i], xs_ref[...], jnp.uint32(0))
        left = lax.bitcast_convert_type(packed & jnp.uint32(0xFFFF0000), F32).astype(BF16)
        right = lax.bitcast_convert_type(packed << 16, F32).astype(BF16)
        x = jnp.concatenate([left, right], axis=1)
        g = jnp.dot(x, wg_s[...], preferred_element_type=F32)
        u = jnp.dot(x, wu_s[...], preferred_element_type=F32)
        hid = g * _sigmoid(g) * u
        o_ref[...] = jnp.dot(hid.astype(BF16), wd_s[...], preferred_element_type=F32).astype(o_ref.dtype)

    @pl.when(flag == 0)
    def _():
        o_ref[...] = jnp.zeros_like(o_ref)


def _experts(tile_expert, tile_flag, tile_rows, next_expert, slot, xs, w_gate, w_up, w_down, tm):
    n_rows, half = xs.shape
    E, D, Fd = w_gate.shape
    assert D == 2 * half
    n_tiles = n_rows // tm
    hbm = pl.BlockSpec(memory_space=pl.ANY)
    grid_spec = pltpu.PrefetchScalarGridSpec(
        num_scalar_prefetch=5,
        grid=(n_tiles,),
        in_specs=[pl.BlockSpec((tm, half), lambda i, *_: (i, 0)), hbm, hbm, hbm],
        out_specs=pl.BlockSpec((tm, D), lambda i, *_: (i, 0)),
        scratch_shapes=[pltpu.VMEM((2, D, Fd), F32), pltpu.VMEM((2, D, Fd), F32), pltpu.VMEM((2, Fd, D), F32),
                        pltpu.VMEM((D, Fd), BF16), pltpu.VMEM((D, Fd), BF16), pltpu.VMEM((Fd, D), BF16),
                        pltpu.SemaphoreType.DMA((2, 3))],
    )
    return pl.pallas_call(
        _expert_kernel,
        grid_spec=grid_spec,
        out_shape=jax.ShapeDtypeStruct((n_rows, D), BF16),
        compiler_params=_cparams("arbitrary"),
        name="experts",
    )(tile_expert, tile_flag, tile_rows, next_expert, slot, xs, w_gate, w_up, w_down)


SC_CORES = 2
SC_SUBCORES = 16
SC_ROWS = 32


def _dispatch_rows(table, pos, n_rows):
    T, W = table.shape
    workers = SC_CORES * SC_SUBCORES
    chunks = T // (workers * SC_ROWS)
    assert workers * chunks * SC_ROWS == T
    pos4 = pos.T.reshape(TOP_K, workers, chunks, SC_ROWS).transpose(1, 0, 2, 3)
    mesh = plsc.VectorSubcoreMesh(core_axis_name="c", subcore_axis_name="s")

    @functools.partial(
        pl.kernel, mesh=mesh,
        out_type=jax.ShapeDtypeStruct((n_rows, W), table.dtype),
        scratch_types=[pltpu.VMEM((TOP_K, chunks, SC_ROWS), jnp.int32),
                       pltpu.VMEM((2, SC_ROWS, W), table.dtype),
                       pltpu.SemaphoreType.DMA, pltpu.SemaphoreType.DMA, pltpu.SemaphoreType.DMA],
        name="dispatch_rows",
    )
    def dispatch(table_hbm, pos_hbm, out_hbm, pos_v, rows_v, sem_in, sem_out0, sem_out1):
        wid = lax.axis_index("s") * SC_CORES + lax.axis_index("c")
        base = wid * (chunks * SC_ROWS)
        pltpu.sync_copy(pos_hbm.at[wid], pos_v)

        def load(c):
            return pltpu.async_copy(table_hbm.at[pl.ds(base + c * SC_ROWS, SC_ROWS)], rows_v.at[c % 2], sem_in)

        pending = load(0)
        for c in range(chunks):
            pending.wait()
            if c + 1 < chunks:
                pending = load(c + 1)
            out0 = pltpu.async_copy(rows_v.at[c % 2], out_hbm.at[pos_v.at[0, c]], sem_out0)
            out1 = pltpu.async_copy(rows_v.at[c % 2], out_hbm.at[pos_v.at[1, c]], sem_out1)
            out0.wait()
            out1.wait()

    return dispatch(table, pos4)


def _final_kernel(x1_ref, y0_ref, y1_ref, route_ref, g_ref, o_ref):
    route = route_ref[...]
    lane = lax.broadcasted_iota(jnp.int32, route.shape, 1)
    w0 = jnp.sum(jnp.where(lane == TOP_K, route, 0.0), axis=-1, keepdims=True)
    w1 = jnp.sum(jnp.where(lane == TOP_K + 1, route, 0.0), axis=-1, keepdims=True)
    x = x1_ref[...] + (w0 * y0_ref[...].astype(F32) + w1 * y1_ref[...].astype(F32))
    o_ref[...] = x * lax.rsqrt(jnp.mean(x * x, axis=-1, keepdims=True) + EPS) * g_ref[...]


def _final(x1, y0, y1, route, g_final, tm):
    T, D = x1.shape
    row = pl.BlockSpec((tm, D), lambda i: (i, 0))
    return pl.pallas_call(
        _final_kernel,
        grid=(T // tm,),
        in_specs=[row, row, row, pl.BlockSpec((tm, LANE), lambda i: (i, 0)), _const_spec((1, D))],
        out_specs=row,
        out_shape=jax.ShapeDtypeStruct((T, D), F32),
        compiler_params=_cparams("parallel"),
        name="final",
    )(x1, y0, y1, route, g_final)


def _route_tables(ids, rank, counts, tm):
    T = ids.shape[0]
    A = T * TOP_K
    n_experts = counts.shape[0]
    padded = ((counts + tm - 1) // tm) * tm
    ends = jnp.cumsum(padded)
    starts = ends - padded
    onehot = ids[:, :, None] == jnp.arange(n_experts, dtype=jnp.int32)[None, None, :]
    pos = jnp.sum(jnp.where(onehot, starts[None, None, :], 0), axis=-1) + rank
    n_tiles = A // tm + n_experts
    tile_start = jnp.arange(n_tiles, dtype=jnp.int32) * tm
    tile_expert_raw = jnp.sum((ends[None, :] <= tile_start[:, None]).astype(jnp.int32), axis=1)
    valid = tile_start < ends[-1]
    last_expert = jnp.max(jnp.where(counts > 0, jnp.arange(n_experts, dtype=jnp.int32), 0))
    tile_expert = jnp.where(valid, jnp.minimum(tile_expert_raw, n_experts - 1), last_expert)
    first = jnp.concatenate([jnp.ones((1,), bool), tile_expert[1:] != tile_expert[:-1]])
    tile_flag = jnp.where(valid, jnp.where(first, 2, 1), 0).astype(jnp.int32)
    tile_rows = jnp.clip((starts + counts)[tile_expert] - tile_start, 0, tm)
    tile_rows = jnp.where(valid, tile_rows, 0).astype(jnp.int32)
    e_idx = jnp.arange(n_experts, dtype=jnp.int32)
    nonempty = counts > 0
    later = (e_idx[None, :] > e_idx[:, None]) & nonempty[None, :]
    next_nonempty = jnp.min(jnp.where(later, e_idx[None, :], n_experts), axis=1)
    next_nonempty = jnp.where(next_nonempty < n_experts, next_nonempty, -1).astype(jnp.int32)
    order = jnp.sum(((e_idx[None, :] < e_idx[:, None]) & nonempty[None, :]).astype(jnp.int32), axis=1)
    next_expert = next_nonempty[tile_expert]
    slot = (order[tile_expert] % 2).astype(jnp.int32)
    return pos, tile_expert, tile_flag, tile_rows, next_expert, slot


def _largest_tile(n, cap):
    t = min(n, cap)
    while n % t:
        t //= 2
    return t


def kernel(x, g_mix, w_in, conv_w, a_log, dt_bias, gdn_norm_w, diff_lambda, diff_norm_w, w_branch_a, w_branch_d,
           w_out, g_ffn, w_group, b_group, w_router, b_router, w_exp_gate, w_exp_up, w_exp_down, g_final):
    B, S, D = x.shape
    T = B * S
    depth = g_mix.shape[0]
    Hg = a_log.shape[-1]
    gdn_qk = Hg * GDN_DK
    gdn_v = Hg * GDN_DV
    diff_v = w_branch_d.shape[1]
    Hd = diff_v // DIFF_DV
    diff_qk = Hd * 2 * DIFF_DH
    n_groups = w_group.shape[-1]
    n_experts = w_router.shape[-1]
    per_group = n_experts // n_groups
    assert 4 * Hg <= LANE and n_groups + n_experts <= LANE
    assert S % GDN_CHUNK == 0 and D % LANE == 0

    ab_lo = 2 * gdn_qk + 2 * gdn_v
    ab_hi = ab_lo + 4 * Hg
    gates_lo = ab_hi + 2 * diff_qk + diff_v
    main_cols = 2 * D + ab_lo + 2 * diff_qk + diff_v
    gdn_off = 2 * D
    q_d_off = gdn_off + ab_lo
    k_d_off = q_d_off + diff_qk
    v_d_off = k_d_off + diff_qk

    tm_norm = _largest_tile(T, 512)
    tm_proj = _largest_tile(T, 1024)
    tn_proj = _largest_tile(main_cols, 1024)
    tm_merge = _largest_tile(T, 256)
    tm_exp = _largest_tile(T * TOP_K, 256)
    tm_final = _largest_tile(T, 512)

    slopes = jnp.exp2(-8.0 * (jnp.arange(Hd, dtype=F32) + 1.0) / Hd)
    x2 = x.reshape(T, D)
    for layer in range(depth):
        lam_init = 0.8 - 0.6 * math.exp(-0.3 * layer)
        w_in_l = w_in[layer]
        w_main = _reorder_cast(w_in_l, ((gates_lo, 2 * D), (0, ab_lo), (ab_hi, gates_lo - ab_hi)),
                               _largest_tile(D, 256))
        w_ab = jnp.pad(w_in_l[:, ab_lo:ab_hi], ((0, 0), (0, LANE - 4 * Hg))).astype(BF16)
        alog_pad = jnp.pad(a_log[layer].reshape(1, 2 * Hg), ((0, 0), (0, LANE - 2 * Hg)))
        dtb_pad = jnp.pad(dt_bias[layer].reshape(1, 2 * Hg), ((0, 0), (0, LANE - 2 * Hg)))

        h, gb = _norm_proj(x2, g_mix[layer].reshape(1, D), w_ab, alog_pad, dtb_pad, 2 * Hg, tm_norm)
        p = _matmul(h, w_main, tm_proj, tn_proj, BF16)

        grow = gb[:, :2 * Hg].reshape(B, S // GDN_CHUNK, GDN_CHUNK, 2 * Hg).transpose(0, 3, 1, 2)
        conv_w8 = jnp.pad(conv_w[layer], ((0, 8 - GDN_CONV), (0, 0)))
        y_a = _gdn(p, gdn_off // LANE, conv_w8, gb, grow, gdn_norm_w[layer].reshape(1, GDN_DV), B, S, Hg)
        y_d = _diff_attn(p, slopes, diff_lambda[layer], diff_norm_w[layer].reshape(DIFF_DV, 1), B, S, Hd,
                         q_d_off // LANE, k_d_off // LANE, v_d_off // LANE, lam_init)

        w_r = jnp.pad(jnp.concatenate([w_group[layer], w_router[layer]], axis=1),
                      ((0, 0), (0, LANE - n_groups - n_experts)))
        wr_hi = w_r.astype(BF16)
        wr_lo = (w_r - wr_hi.astype(F32)).astype(BF16)
        b_r = jnp.pad(jnp.concatenate([b_group[layer], b_router[layer]]).reshape(1, -1),
                      ((0, 0), (0, LANE - n_groups - n_experts)))
        x1, h2, route, counts = _merge(y_a, y_d, p, x2,
                                       w_branch_a[layer].astype(BF16), w_branch_d[layer].astype(BF16),
                                       w_out[layer].astype(BF16), g_ffn[layer].reshape(1, D),
                                       wr_hi, wr_lo, b_r, n_groups, per_group, tm_merge)

        ids = route[:, 0:TOP_K].astype(jnp.int32)
        rank = route[:, 2 * TOP_K:3 * TOP_K].astype(jnp.int32)
        counts = counts[0, n_groups:n_groups + n_experts].astype(jnp.int32)
        pos, tile_expert, tile_flag, tile_rows, next_expert, slot = _route_tables(ids, rank, counts, tm_exp)
        xs = _dispatch_rows(h2, pos, tile_expert.shape[0] * tm_exp)
        ys = _experts(tile_expert, tile_flag, tile_rows, next_expert, slot, xs, w_exp_gate[layer],
                      w_exp_up[layer], w_exp_down[layer], tm_exp)
        y0 = ys.at[pos[:, 0]].get(mode="promise_in_bounds")
        y1 = ys.at[pos[:, 1]].get(mode="promise_in_bounds")
        if layer + 1 < depth:
            wts = route[:, TOP_K:2 * TOP_K]
            x2 = x1 + (wts[:, 0:1] * y0.astype(F32) + wts[:, 1:2] * y1.astype(F32))
    out = _final(x1, y0, y1, route, g_final.reshape(1, D), tm_final)
    return out.reshape(B, S, D)
```

```python
import functools
import math

import jax
import jax.numpy as jnp
from jax import lax
from jax.experimental import pallas as pl
from jax.experimental.pallas import tpu as pltpu
from jax.experimental.pallas import tpu_sc as plsc

F32 = jnp.float32
BF16 = jnp.bfloat16
EPS = 1e-6
LANE = 128
GDN_DK = 128
GDN_DV = 128
GDN_CONV = 5
GDN_CHUNK = 64
DIFF_DH = 64
DIFF_DV = 2 * DIFF_DH
TOP_K = 2
LOG2E = 1.4426950408889634
NEG_BIG = -1e30
VMEM_LIMIT_BYTES = 56 * 1024 * 1024

_HI = lax.Precision.HIGHEST


def _cparams(*sem):
    return pltpu.CompilerParams(dimension_semantics=sem, vmem_limit_bytes=VMEM_LIMIT_BYTES)


def _const_spec(shape):
    nd = len(shape)
    return pl.BlockSpec(shape, lambda *_: (0,) * nd, pipeline_mode=pl.Buffered(1))


def _mm(a, b):
    return jnp.dot(a.astype(BF16), b.astype(BF16), preferred_element_type=F32)


def _split3(x):
    hi = x.astype(BF16)
    r = x - hi.astype(F32)
    mid = r.astype(BF16)
    lo = (r - mid.astype(F32)).astype(BF16)
    return hi, mid, lo


def _softplus(x):
    return jnp.maximum(x, 0.0) + jnp.log(1.0 + jnp.exp(-jnp.abs(x)))


def _sigmoid(x):
    return 1.0 / (1.0 + jnp.exp(-x))


def _norm_proj_kernel(x_ref, g_ref, wab_ref, alog_ref, dtb_ref, h_ref, gb_ref, *, n_dir_heads):
    x = x_ref[...]
    h = x * lax.rsqrt(jnp.mean(x * x, axis=-1, keepdims=True) + EPS) * g_ref[...]
    hb = h.astype(BF16)
    h_ref[...] = hb
    ab = lax.dot_general(hb, wab_ref[...], (((1,), (1,)), ((), ())), preferred_element_type=F32)
    g = -jnp.exp(alog_ref[...]) * _softplus(ab + dtb_ref[...])
    beta = _sigmoid(ab)
    tm = x.shape[0]
    row = lax.broadcasted_iota(jnp.int32, (tm, tm), 0)
    col = lax.broadcasted_iota(jnp.int32, (tm, tm), 1)
    same = (row // GDN_CHUNK) == (col // GDN_CHUNK)
    prefix = jnp.where(same & (col <= row), 1.0, 0.0).astype(BF16)
    suffix = jnp.where(same & (col >= row), 1.0, 0.0).astype(BF16)
    pieces = _split3(g)
    cs_f = sum(jnp.dot(prefix, p, preferred_element_type=F32) for p in pieces)
    cs_b = sum(jnp.dot(suffix, p, preferred_element_type=F32) for p in pieces)
    lane = lax.broadcasted_iota(jnp.int32, g.shape, 1)
    gb_ref[...] = jnp.where(lane < n_dir_heads // 2, cs_f, jnp.where(lane < n_dir_heads, cs_b, beta))


def _norm_proj(x2, g_mix, w_ab, alog_pad, dtb_pad, n_dir_heads, tm):
    T, D = x2.shape
    return pl.pallas_call(
        functools.partial(_norm_proj_kernel, n_dir_heads=n_dir_heads),
        grid=(T // tm,),
        in_specs=[pl.BlockSpec((tm, D), lambda i: (i, 0)),
                  _const_spec((1, D)), _const_spec((LANE, D)), _const_spec((1, LANE)), _const_spec((1, LANE))],
        out_specs=[pl.BlockSpec((tm, D), lambda i: (i, 0)),
                   pl.BlockSpec((tm, LANE), lambda i: (i, 0))],
        out_shape=[jax.ShapeDtypeStruct((T, D), BF16),
                   jax.ShapeDtypeStruct((T, LANE), F32)],
        compiler_params=_cparams("parallel"),
        name="norm_proj",
    )(x2, g_mix, w_ab, alog_pad, dtb_pad)


REORDER_ROWS = 512


def _reorder_cast_kernel(w_ref, o_ref, narrow_ref, *, segments, narrow):
    n_start, n_width = narrow
    row = lax.broadcasted_iota(jnp.int32, narrow_ref.shape, 0)
    narrow_ref[...] = jnp.where(row < n_width, w_ref[n_start:n_start + LANE, :], 0.0).astype(narrow_ref.dtype)
    off = 0
    for start, width in segments:
        for c in range(0, width, REORDER_ROWS):
            n = min(REORDER_ROWS, width - c)
            o_ref[off:off + n, :] = w_ref[start + c:start + c + n, :].astype(o_ref.dtype)
            off += n


def _reorder_cast(wT, segments, narrow, tc):
    rows, cols = wT.shape
    out_rows = sum(width for _, width in segments)
    assert narrow[0] % 8 == 0 and narrow[1] <= LANE and narrow[0] + LANE <= rows
    assert all(start % 8 == 0 and width % 16 == 0 for start, width in segments)
    return pl.pallas_call(
        functools.partial(_reorder_cast_kernel, segments=segments, narrow=narrow),
        grid=(cols // tc,),
        in_specs=[pl.BlockSpec((rows, tc), lambda i: (0, i))],
        out_specs=[pl.BlockSpec((out_rows, tc), lambda i: (0, i)), pl.BlockSpec((LANE, tc), lambda i: (0, i))],
        out_shape=[jax.ShapeDtypeStruct((out_rows, cols), BF16), jax.ShapeDtypeStruct((LANE, cols), BF16)],
        compiler_params=_cparams("parallel"),
        name="reorder_cast",
    )(wT)


_NT_DIMS = (((1,), (1,)), ((), ()))


def _matmul_nt_kernel(a_ref, bT_ref, o_ref):
    o_ref[...] = lax.dot_general(a_ref[...], bT_ref[...], _NT_DIMS, preferred_element_type=F32).astype(o_ref.dtype)


def _matmul_nt(a, bT, tm, tn, out_dtype):
    M, K = a.shape
    N = bT.shape[0]
    return pl.pallas_call(
        _matmul_nt_kernel,
        grid=(N // tn, M // tm),
        in_specs=[pl.BlockSpec((tm, K), lambda j, i: (i, 0)),
                  pl.BlockSpec((tn, K), lambda j, i: (j, 0))],
        out_specs=pl.BlockSpec((tm, tn), lambda j, i: (i, j)),
        out_shape=jax.ShapeDtypeStruct((M, N), out_dtype),
        compiler_params=_cparams("parallel", "parallel"),
        name="in_proj",
    )(a, bT)


def _bmm_tn(a, b):
    return jnp.stack([lax.dot_general(a[n], b[n], (((0,), (0,)), ((), ())), preferred_element_type=F32)
                      for n in range(a.shape[0])])


def _gdn_kernel(q_ref, k_ref, v_ref, z_ref, cwq_ref, cwk_ref, cwv_ref, gb_ref, grow_ref, nw_ref,
                y_ref,
                xpad, qn, kn, vn, u_s, w_s, qg_s, at_s, tm_s, tn_s, egl_s, st_s,
                *, n_heads, seq, conv_rows, prep_chunks):
    C = GDN_CHUNK
    n_chunks = seq // C
    head = pl.program_id(1)
    pad = 8
    half = GDN_CONV // 2

    zeros_pad = jnp.zeros((pad, LANE), F32)
    xpad[0:pad, :] = zeros_pad
    xpad[pad + seq:pad + seq + pad, :] = zeros_pad
    for src, cw_ref, dst, mode in ((q_ref, cwq_ref, qn, "q"), (k_ref, cwk_ref, kn, "k"), (v_ref, cwv_ref, vn, "v")):
        xpad[pad:pad + seq, :] = src[...].astype(F32)
        for t in range(seq // conv_rows):
            r0 = t * conv_rows
            acc = jnp.zeros((conv_rows, LANE), F32)
            for j in range(GDN_CONV):
                lo = pad + r0 + j - half
                acc = acc + xpad[lo:lo + conv_rows, :] * cw_ref[j:j + 1, :]
            y = acc * _sigmoid(acc)
            if mode != "v":
                y = y * lax.rsqrt(jnp.sum(y * y, axis=-1, keepdims=True) + EPS)
            if mode == "q":
                y = y * (GDN_DK ** -0.5)
            dst[r0:r0 + conv_rows, :] = y

    nb = prep_chunks
    R = nb * C
    nb2 = 2 * nb
    bi = lax.broadcasted_iota(jnp.int32, (nb2, C, C), 0)
    ii = lax.broadcasted_iota(jnp.int32, (nb2, C, C), 1)
    jj = lax.broadcasted_iota(jnp.int32, (nb2, C, C), 2)
    fwd = bi < nb
    incl = (fwd & (ii >= jj)) | (~fwd & (ii <= jj))
    strict = (fwd & (ii > jj)) | (~fwd & (ii < jj))
    eye = jnp.where(ii == jj, 1.0, 0.0).astype(F32)
    lane = lax.broadcasted_iota(jnp.int32, (R, LANE), 1)
    both = lambda x: jnp.concatenate([x, x], axis=0)

    def prep(t, carry):
        r0 = pl.multiple_of(t * R, R)
        n0 = pl.multiple_of(t * nb, nb)
        q3 = qn[pl.ds(r0, R), :].reshape(nb, C, LANE)
        k3 = kn[pl.ds(r0, R), :].reshape(nb, C, LANE)
        v3 = vn[pl.ds(r0, R), :].reshape(nb, C, LANE)
        gb_blk = gb_ref[pl.ds(r0, R), :]
        kq = jnp.concatenate([k3, q3], axis=1).astype(BF16)
        gram = jnp.einsum("nik,njk->nij", kq, k3.astype(BF16), preferred_element_type=F32)
        kk, qk = both(gram[:, :C, :]), both(gram[:, C:, :])
        gcb, bt, grow, gl = [], [], [], []
        for d in range(2):
            colidx = d * n_heads + head
            gcb_d = jnp.sum(jnp.where(lane == colidx, gb_blk, 0.0), axis=-1, keepdims=True).reshape(nb, C, 1)
            gcb.append(gcb_d)
            bt.append(jnp.sum(jnp.where(lane == 2 * n_heads + colidx, gb_blk, 0.0), axis=-1,
                              keepdims=True).reshape(nb, C, 1))
            grow.append(grow_ref[0, colidx, pl.ds(n0, nb), :].reshape(nb, 1, C))
            gl.append(gcb_d[:, C - 1:C, :] if d == 0 else gcb_d[:, 0:1, :])
        gcb, bt, grow, gl = (jnp.concatenate(x, axis=0) for x in (gcb, bt, grow, gl))
        k2, q2, v2 = both(k3), both(q3), both(v3)

        decay = jnp.exp(jnp.where(incl, gcb - grow, -jnp.inf))
        L = jnp.where(strict, bt * kk * decay, 0.0)
        attn = qk * decay
        ainv = eye - L
        P = L
        for _ in range(int(math.log2(C)) - 1):
            Pb = P.astype(BF16)
            P = jnp.einsum("nij,njk->nik", Pb, Pb, preferred_element_type=F32)
            ainv = ainv + jnp.einsum("nij,njk->nik", ainv.astype(BF16), P.astype(BF16),
                                     preferred_element_type=F32)
        eg = jnp.exp(gcb)
        rhs = jnp.concatenate([v2 * bt, k2 * (bt * eg)], axis=-1).astype(BF16)
        uw = jnp.einsum("nij,njd->nid", ainv.astype(BF16), rhs, preferred_element_type=F32).astype(BF16)
        kg = (k2 * jnp.exp(gl - gcb)).astype(BF16)
        trans = _bmm_tn(kg, uw)
        qg = (q2 * eg).astype(BF16)
        egl = jnp.broadcast_to(jnp.exp(gl).reshape(nb2, 1), (nb2, LANE))
        for d in range(2):
            sl = slice(d * nb, (d + 1) * nb)
            u_s[d, pl.ds(r0, R), :] = uw[sl, :, :LANE].reshape(R, LANE)
            w_s[d, pl.ds(r0, R), :] = uw[sl, :, LANE:].reshape(R, LANE)
            qg_s[d, pl.ds(r0, R), :] = qg[sl].reshape(R, LANE)
            at_s[d, pl.ds(r0, R), :] = attn[sl].reshape(R, C).astype(BF16)
            tn_s[d, pl.ds(n0 * GDN_DK, nb * GDN_DK), :] = trans[sl, :, :LANE].reshape(nb * GDN_DK, LANE).astype(BF16)
            tm_s[d, pl.ds(n0 * GDN_DK, nb * GDN_DK), :] = (-trans[sl, :, LANE:]).reshape(nb * GDN_DK, LANE).astype(BF16)
            egl_s[d, pl.ds(n0, nb), :] = egl[sl]
        return carry

    lax.fori_loop(0, n_chunks // nb, prep, 0)

    def chunk_step(d, n, state):
        r = pl.multiple_of(n * GDN_DK, GDN_DK)
        sb = state.astype(BF16)
        st_s[d, pl.ds(r, GDN_DK), :] = sb
        return (state * egl_s[d, pl.ds(n, 1), :]
                + jnp.dot(tm_s[d, pl.ds(r, GDN_DK), :], sb, preferred_element_type=F32)
                + tn_s[d, pl.ds(r, GDN_DK), :].astype(F32))

    def scan(n, carry):
        sf, sbw = carry
        return chunk_step(0, n, sf), chunk_step(1, n_chunks - 1 - n, sbw)

    zero_state = jnp.zeros((GDN_DK, GDN_DV), F32)
    lax.fori_loop(0, n_chunks, scan, (zero_state, zero_state))

    def emit(t, carry):
        r0 = pl.multiple_of(t * R, R)
        s0 = pl.multiple_of(t * nb * GDN_DK, nb * GDN_DK)
        two = lambda ref, start, rows: jnp.concatenate([ref[0, pl.ds(start, rows), :], ref[1, pl.ds(start, rows), :]])
        st = two(st_s, s0, nb * GDN_DK).reshape(nb2, GDN_DK, GDN_DV)
        wq = jnp.concatenate([two(w_s, r0, R).reshape(nb2, C, LANE), two(qg_s, r0, R).reshape(nb2, C, LANE)], axis=1)
        ws_qs = jnp.einsum("ncd,nde->nce", wq, st, preferred_element_type=F32)
        vnew = two(u_s, r0, R).reshape(nb2, C, LANE).astype(F32) - ws_qs[:, :C, :]
        o = ws_qs[:, C:, :] + jnp.einsum("ncs,nse->nce", two(at_s, r0, R).reshape(nb2, C, C), vnew.astype(BF16),
                                         preferred_element_type=F32)
        o = (o[:nb] + o[nb:]).reshape(R, LANE)
        on = o * lax.rsqrt(jnp.mean(o * o, axis=-1, keepdims=True) + EPS) * nw_ref[...]
        z = z_ref[pl.ds(r0, R), :].astype(F32)
        y_ref[pl.ds(r0, R), :] = (on * (z * _sigmoid(z))).astype(y_ref.dtype)
        return carry

    lax.fori_loop(0, n_chunks // nb, emit, 0)


def _gdn(p, p_off, conv_w8, gb, grow, norm_w, batch, seq, n_heads):
    T = batch * seq
    qk_blocks = n_heads
    n_chunks = seq // GDN_CHUNK
    conv_rows = min(512, seq)
    prep_chunks = min(16, n_chunks)
    blk = lambda off: pl.BlockSpec((seq, LANE), lambda b, h: (b, p_off + off + h))
    cw = lambda off: pl.BlockSpec((8, LANE), lambda b, h: (0, off + h))
    return pl.pallas_call(
        functools.partial(_gdn_kernel, n_heads=n_heads, seq=seq, conv_rows=conv_rows, prep_chunks=prep_chunks),
        grid=(batch, n_heads),
        in_specs=[blk(0), blk(qk_blocks), blk(2 * qk_blocks), blk(3 * qk_blocks),
                  cw(0), cw(qk_blocks), cw(2 * qk_blocks),
                  pl.BlockSpec((seq, LANE), lambda b, h: (b, 0), pipeline_mode=pl.Buffered(1)),
                  pl.BlockSpec((1, 2 * n_heads, n_chunks, GDN_CHUNK), lambda b, h: (b, 0, 0, 0),
                               pipeline_mode=pl.Buffered(1)),
                  _const_spec((1, LANE))],
        out_specs=pl.BlockSpec((seq, LANE), lambda b, h: (b, h)),
        out_shape=jax.ShapeDtypeStruct((T, n_heads * GDN_DV), BF16),
        scratch_shapes=[
            pltpu.VMEM((seq + 16, LANE), F32),
            pltpu.VMEM((seq, LANE), F32),
            pltpu.VMEM((seq, LANE), F32),
            pltpu.VMEM((seq, LANE), F32),
            pltpu.VMEM((2, seq, LANE), BF16),
            pltpu.VMEM((2, seq, LANE), BF16),
            pltpu.VMEM((2, seq, LANE), BF16),
            pltpu.VMEM((2, seq, GDN_CHUNK), BF16),
            pltpu.VMEM((2, n_chunks * GDN_DK, GDN_DV), BF16),
            pltpu.VMEM((2, n_chunks * GDN_DK, GDN_DV), BF16),
            pltpu.VMEM((2, max(n_chunks, 8), LANE), F32),
            pltpu.VMEM((2, n_chunks * GDN_DK, GDN_DV), BF16),
        ],
        compiler_params=_cparams("parallel", "parallel"),
        name="gdn",
    )(p, p, p, p, conv_w8, conv_w8, conv_w8, gb, grow, norm_w)


def _diff_attn_kernel(slopes_ref, q_ref, k_ref, v_ref, lam_ref, nw_ref, o_ref,
                      qT_s, vT_s, klo_s, khi_s, rel_s, s_s, p_s, acc_s,
                      *, seq, tq, tk, lam_init):
    head = pl.program_id(1)
    slope2 = slopes_ref[head] * LOG2E
    scale2 = (DIFF_DH ** -0.5) * LOG2E
    n_kv = seq // tk
    maps = (klo_s, khi_s)

    rows = min(512, seq)
    for t in range(seq // rows):
        r0 = t * rows
        qt = (q_ref[r0:r0 + rows, :].astype(F32) * scale2).T.astype(BF16)
        frow = lax.broadcasted_iota(jnp.int32, (LANE, rows), 0)
        fcol = lax.broadcasted_iota(jnp.int32, (LANE, rows), 1)
        qa, qb, qc = (x.astype(F32) for x in _split3(((r0 + fcol) % tq).astype(F32) * slope2))
        qfeat = jnp.where(frow < 3, 1.0, jnp.where(frow == 3, -qa, jnp.where(frow == 4, -qb,
                                                   jnp.where(frow == 5, -qc, 0.0)))).astype(BF16)
        for sgn, feat in enumerate((qfeat, -qfeat)):
            qT_s[sgn, 0:LANE, r0:r0 + rows] = qt
            qT_s[sgn, LANE:2 * LANE, r0:r0 + rows] = feat
        vT_s[0:LANE, r0:r0 + rows] = v_ref[r0:r0 + rows, :].astype(F32).T.astype(BF16)
        vT_s[LANE:LANE + 16, r0:r0 + rows] = jnp.ones((16, rows), BF16)
        kt = k_ref[r0:r0 + rows, :]
        krow = lax.broadcasted_iota(jnp.int32, (rows, LANE), 0)
        lane = lax.broadcasted_iota(jnp.int32, (rows, LANE), 1)
        ka, kb, kc = (x.astype(F32) for x in _split3(((r0 + krow) % tk).astype(F32) * slope2))
        kfeat = jnp.where(lane == 0, ka, jnp.where(lane == 1, kb, jnp.where(lane == 2, kc,
                                                   jnp.where(lane < 6, 1.0, 0.0)))).astype(BF16)
        klo_s[r0:r0 + rows, 0:LANE] = jnp.where(lane < DIFF_DH, kt, jnp.zeros_like(kt))
        khi_s[r0:r0 + rows, 0:LANE] = jnp.where(lane >= DIFF_DH, kt, jnp.zeros_like(kt))
        klo_s[r0:r0 + rows, LANE:2 * LANE] = kfeat
        khi_s[r0:r0 + rows, LANE:2 * LANE] = kfeat
    kr = lax.broadcasted_iota(jnp.int32, (tk, tq), 0)
    qcol = lax.broadcasted_iota(jnp.int32, (tk, tq), 1)
    rel_s[...] = (qcol - kr).astype(F32) * slope2

    lf = lam_ref[...]
    lam = (jnp.exp(jnp.sum(lf[0:1, :] * lf[1:2, :], axis=-1, keepdims=True))
           - jnp.exp(jnp.sum(lf[2:3, :] * lf[3:4, :], axis=-1, keepdims=True)) + lam_init)

    def q_tile(i, carry):
        c_q = pl.multiple_of(i * tq, tq)
        j_diag = (i * tq) // tk

        def scores(t):
            j = j_diag if t == 0 else lax.rem(j_diag + t, n_kv)
            r_k = pl.multiple_of(j * tk, tk)
            d0 = jnp.asarray(i * tq - j * tk)
            if t == 0:
                bias = jnp.abs(rel_s[...] + d0.astype(F32) * slope2)
                for mp, k_s in enumerate(maps):
                    s_s[t % 2, mp] = jnp.dot(k_s[pl.ds(r_k, tk), 0:LANE], qT_s[0, 0:LANE, pl.ds(c_q, tq)],
                                             preferred_element_type=F32) - bias
                return r_k, 0.0
            sgn = jnp.asarray(j > j_diag).astype(jnp.int32)
            for mp, k_s in enumerate(maps):
                s_s[t % 2, mp] = jnp.dot(k_s[pl.ds(r_k, tk), :], qT_s[sgn, :, pl.ds(c_q, tq)],
                                         preferred_element_type=F32)
            return r_k, jnp.abs(d0).astype(F32) * slope2

        def weighted_values(t, r_k, alphas):
            vT = vT_s[:, pl.ds(r_k, tk)]
            for mp in range(len(maps)):
                upd = jnp.dot(vT, p_s[t % 2, mp], preferred_element_type=F32)
                acc_s[mp] = upd if t == 0 else acc_s[mp] * alphas[mp] + upd

        m = [jnp.full((1, tq), NEG_BIG, F32) for _ in maps]
        nxt = scores(0)
        prev = None
        for t in range(n_kv):
            r_k, const = nxt
            if t + 1 < n_kv:
                nxt = scores(t + 1)
            alphas = []
            for mp in range(len(maps)):
                m_new = jnp.maximum(m[mp], jnp.max(s_s[t % 2, mp], axis=0, keepdims=True) - const)
                alphas.append(jnp.exp2(m[mp] - m_new))
                p_s[t % 2, mp] = jnp.exp2(s_s[t % 2, mp] - (m_new + const)).astype(BF16)
                m[mp] = m_new
            if prev is not None:
                weighted_values(t - 1, *prev)
            prev = (r_k, alphas)
        weighted_values(n_kv - 1, *prev)

        o1, o2 = acc_s[0], acc_s[1]
        od = o1[0:DIFF_DV] / o1[DIFF_DV:DIFF_DV + 1] - lam * (o2[0:DIFF_DV] / o2[DIFF_DV:DIFF_DV + 1])
        yn = od * lax.rsqrt(jnp.mean(od * od, axis=0, keepdims=True) + EPS) * nw_ref[...] * (1.0 - lam_init)
        o_ref[pl.ds(c_q, tq), :] = yn.T.astype(o_ref.dtype)
        return carry

    lax.fori_loop(0, seq // tq, q_tile, 0)


def _diff_attn(p, slopes, diff_lambda, norm_w_col, batch, seq, n_heads, q_off, k_off, v_off, lam_init):
    T = batch * seq
    tq = min(256, seq)
    tk = min(512, seq)
    blk = lambda off: pl.BlockSpec((seq, LANE), lambda b, h: (b, off + h))
    return pl.pallas_call(
        functools.partial(_diff_attn_kernel, seq=seq, tq=tq, tk=tk, lam_init=lam_init),
        grid=(batch, n_heads),
        in_specs=[pl.BlockSpec(memory_space=pltpu.SMEM),
                  blk(q_off), blk(k_off), blk(v_off),
                  _const_spec((4, DIFF_DH)), _const_spec((DIFF_DV, 1))],
        out_specs=pl.BlockSpec((seq, LANE), lambda b, h: (b, h)),
        out_shape=jax.ShapeDtypeStruct((T, n_heads * DIFF_DV), BF16),
        scratch_shapes=[
            pltpu.VMEM((2, 2 * LANE, seq), BF16),
            pltpu.VMEM((DIFF_DV + 16, seq), BF16),
            pltpu.VMEM((seq, 2 * LANE), BF16),
            pltpu.VMEM((seq, 2 * LANE), BF16),
            pltpu.VMEM((tk, tq), F32),
            pltpu.VMEM((2, 2, tk, tq), F32),
            pltpu.VMEM((2, 2, tk, tq), BF16),
            pltpu.VMEM((2, DIFF_DV + 16, tq), F32),
        ],
        compiler_params=_cparams("parallel", "parallel"),
        name="diff_attn",
    )(slopes, p, p, p, diff_lambda, norm_w_col)


def _merge_kernel(ya_ref, yd_ref, ga_ref, gd_ref, x_ref, wa_ref, wd_ref, wo_ref, gffn_ref,
                  wr_hi_ref, wr_lo_ref, br_ref, x1_ref, h2_ref, route_ref, count_ref, *, n_groups, per_group):
    ma = jnp.dot(ya_ref[...], wa_ref[...], preferred_element_type=F32)
    md = jnp.dot(yd_ref[...], wd_ref[...], preferred_element_type=F32)
    merged = _sigmoid(ga_ref[...].astype(F32)) * ma + _sigmoid(gd_ref[...].astype(F32)) * md
    x1 = x_ref[...] + jnp.dot(merged.astype(BF16), wo_ref[...], preferred_element_type=F32)
    x1_ref[...] = x1
    h2 = x1 * lax.rsqrt(jnp.mean(x1 * x1, axis=-1, keepdims=True) + EPS) * gffn_ref[...]
    bits = lax.bitcast_convert_type(h2.astype(BF16).astype(F32), jnp.uint32)
    half = bits.shape[1] // 2
    h2_ref[...] = bits[:, :half] | (bits[:, half:] >> 16)

    h_hi = h2.astype(BF16)
    h_lo = (h2 - h_hi.astype(F32)).astype(BF16)
    logits = (jnp.dot(h_hi, wr_hi_ref[...], preferred_element_type=F32)
              + jnp.dot(h_hi, wr_lo_ref[...], preferred_element_type=F32)
              + jnp.dot(h_lo, wr_hi_ref[...], preferred_element_type=F32)) + br_ref[...]
    lane = lax.broadcasted_iota(jnp.int32, logits.shape, 1)
    lane_f = lane.astype(F32)
    big = float(LANE)

    def first_argmax(vals, vmax):
        return jnp.min(jnp.where(vals == vmax, lane_f, big), axis=-1, keepdims=True)

    gl = jnp.where(lane < n_groups, logits, -jnp.inf)
    gmax = jnp.max(gl, axis=-1, keepdims=True)
    g_idx = first_argmax(gl, gmax)
    g_w = 1.0 / jnp.sum(jnp.exp(gl - gmax), axis=-1, keepdims=True)
    e_lo = n_groups + g_idx * per_group
    in_group = (lane_f >= e_lo) & (lane_f < e_lo + per_group)
    el = jnp.where(in_group, logits, -jnp.inf)
    emax = jnp.max(el, axis=-1, keepdims=True)
    pe = jnp.exp(el - emax)
    pe = pe / jnp.sum(pe, axis=-1, keepdims=True)
    p1 = jnp.max(pe, axis=-1, keepdims=True)
    i1 = first_argmax(jnp.where(in_group, pe, -1.0), p1)
    rest = jnp.where(in_group & (lane_f != i1), pe, -1.0)
    p2 = jnp.max(rest, axis=-1, keepdims=True)
    i2 = first_argmax(rest, p2)
    denom = p1 + p2
    w1 = p1 / denom * g_w
    w2 = p2 / denom * g_w
    @pl.when(pl.program_id(0) == 0)
    def _():
        count_ref[...] = jnp.zeros_like(count_ref)

    hit1 = lane_f == i1
    hit2 = lane_f == i2
    hits = jnp.where(hit1 | hit2, 1.0, 0.0)
    tm = hits.shape[0]
    earlier = (lax.broadcasted_iota(jnp.int32, (tm, tm), 1) < lax.broadcasted_iota(jnp.int32, (tm, tm), 0))
    before = count_ref[0:1, :] + jnp.dot(jnp.where(earlier, 1.0, 0.0).astype(BF16), hits.astype(BF16),
                                         preferred_element_type=F32)
    rank1 = jnp.sum(jnp.where(hit1, before, 0.0), axis=-1, keepdims=True)
    rank2 = jnp.sum(jnp.where(hit2, before, 0.0), axis=-1, keepdims=True)
    count_ref[...] = count_ref[...] + jnp.sum(hits, axis=0, keepdims=True)

    route = jnp.where(lane == 0, i1 - n_groups, 0.0)
    route = jnp.where(lane == 1, i2 - n_groups, route)
    route = jnp.where(lane == 2, w1, route)
    route = jnp.where(lane == 3, w2, route)
    route = jnp.where(lane == 4, rank1, route)
    route = jnp.where(lane == 5, rank2, route)
    route_ref[...] = route


def _merge(y_a, y_d, p, x2, wa, wd, wo, g_ffn, wr_hi, wr_lo, b_r, n_groups, per_group, tm):
    T, D = x2.shape
    va, vd = y_a.shape[1], y_d.shape[1]
    row = lambda w: pl.BlockSpec((tm, w), lambda i: (i, 0))
    return pl.pallas_call(
        functools.partial(_merge_kernel, n_groups=n_groups, per_group=per_group),
        grid=(T // tm,),
        in_specs=[row(va), row(vd),
                  pl.BlockSpec((tm, D), lambda i: (i, 0)),
                  pl.BlockSpec((tm, D), lambda i: (i, 1)),
                  row(D),
                  _const_spec((va, D)), _const_spec((vd, D)), _const_spec((D, D)), _const_spec((1, D)),
                  _const_spec((D, LANE)), _const_spec((D, LANE)), _const_spec((1, LANE))],
        out_specs=[row(D), row(D // 2), row(LANE), pl.BlockSpec((8, LANE), lambda i: (0, 0))],
        out_shape=[jax.ShapeDtypeStruct((T, D), F32),
                   jax.ShapeDtypeStruct((T, D // 2), jnp.uint32),
                   jax.ShapeDtypeStruct((T, LANE), F32),
                   jax.ShapeDtypeStruct((8, LANE), F32)],
        compiler_params=_cparams("arbitrary"),
        name="merge",
    )(y_a, y_d, p, p, x2, wa, wd, wo, g_ffn, wr_hi, wr_lo, b_r)


def _expert_kernel(tile_expert_ref, tile_flag_ref, tile_rows_ref, next_expert_ref, slot_ref,
                   xs_ref, wg_hbm, wu_hbm, wd_hbm, o_ref,
                   wg_f, wu_f, wd_f, wg_s, wu_s, wd_s, sems):
    i = pl.program_id(0)
    flag = tile_flag_ref[i]

    def weight_copies(expert, slot):
        return [pltpu.make_async_copy(hbm.at[expert], buf.at[slot], sems.at[slot, n])
                for n, (hbm, buf) in enumerate(((wg_hbm, wg_f), (wu_hbm, wu_f), (wd_hbm, wd_f)))]

    @pl.when(flag == 2)
    def _():
        slot = slot_ref[i]

        @pl.when(i == 0)
        def _():
            for copy in weight_copies(tile_expert_ref[i], slot):
                copy.start()

        for copy in weight_copies(tile_expert_ref[i], slot):
            copy.wait()
        nxt = next_expert_ref[i]

        @pl.when(nxt >= 0)
        def _():
            for copy in weight_copies(nxt, 1 - slot):
                copy.start()

        wg_s[...] = wg_f[slot].astype(BF16)
        wu_s[...] = wu_f[slot].astype(BF16)
        wd_s[...] = wd_f[slot].astype(BF16)

    @pl.when(flag > 0)
    def _():
        row = lax.broadcasted_iota(jnp.int32, xs_ref.shape, 0)
        packed = jnp.where(row < tile_rows_ref[i], xs_ref[...], jnp.uint32(0))
        left = lax.bitcast_convert_type(packed & jnp.uint32(0xFFFF0000), F32).astype(BF16)
        right = lax.bitcast_convert_type(packed << 16, F32).astype(BF16)
        x = jnp.concatenate([left, right], axis=1)
        g = jnp.dot(x, wg_s[...], preferred_element_type=F32)
        u = jnp.dot(x, wu_s[...], preferred_element_type=F32)
        hid = g * _sigmoid(g) * u
        o_ref[...] = jnp.dot(hid.astype(BF16), wd_s[...], preferred_element_type=F32).astype(o_ref.dtype)

    @pl.when(flag == 0)
    def _():
        o_ref[...] = jnp.zeros_like(o_ref)


def _experts(tile_expert, tile_flag, tile_rows, next_expert, slot, xs, w_gate, w_up, w_down, tm):
    n_rows, half = xs.shape
    E, D, Fd = w_gate.shape
    assert D == 2 * half
    n_tiles = n_rows // tm
    hbm = pl.BlockSpec(memory_space=pl.ANY)
    grid_spec = pltpu.PrefetchScalarGridSpec(
        num_scalar_prefetch=5,
        grid=(n_tiles,),
        in_specs=[pl.BlockSpec((tm, half), lambda i, *_: (i, 0)), hbm, hbm, hbm],
        out_specs=pl.BlockSpec((tm, D), lambda i, *_: (i, 0)),
        scratch_shapes=[pltpu.VMEM((2, D, Fd), F32), pltpu.VMEM((2, D, Fd), F32), pltpu.VMEM((2, Fd, D), F32),
                        pltpu.VMEM((D, Fd), BF16), pltpu.VMEM((D, Fd), BF16), pltpu.VMEM((Fd, D), BF16),
                        pltpu.SemaphoreType.DMA((2, 3))],
    )
    return pl.pallas_call(
        _expert_kernel,
        grid_spec=grid_spec,
        out_shape=jax.ShapeDtypeStruct((n_rows, D), BF16),
        compiler_params=_cparams("arbitrary"),
        name="experts",
    )(tile_expert, tile_flag, tile_rows, next_expert, slot, xs, w_gate, w_up, w_down)


SC_CORES = 2
SC_SUBCORES = 16
SC_ROWS = 32


def _dispatch_rows(table, pos, n_rows):
    T, W = table.shape
    workers = SC_CORES * SC_SUBCORES
    chunks = T // (workers * SC_ROWS)
    assert workers * chunks * SC_ROWS == T
    pos4 = pos.T.reshape(TOP_K, workers, chunks, SC_ROWS).transpose(1, 0, 2, 3)
    mesh = plsc.VectorSubcoreMesh(core_axis_name="c", subcore_axis_name="s")

    @functools.partial(
        pl.kernel, mesh=mesh,
        out_type=jax.ShapeDtypeStruct((n_rows, W), table.dtype),
        scratch_types=[pltpu.VMEM((TOP_K, chunks, SC_ROWS), jnp.int32),
                       pltpu.VMEM((2, SC_ROWS, W), table.dtype),
                       pltpu.SemaphoreType.DMA, pltpu.SemaphoreType.DMA, pltpu.SemaphoreType.DMA],
        name="dispatch_rows",
    )
    def dispatch(table_hbm, pos_hbm, out_hbm, pos_v, rows_v, sem_in, sem_out0, sem_out1):
        wid = lax.axis_index("s") * SC_CORES + lax.axis_index("c")
        base = wid * (chunks * SC_ROWS)
        pltpu.sync_copy(pos_hbm.at[wid], pos_v)

        def load(c):
            return pltpu.async_copy(table_hbm.at[pl.ds(base + c * SC_ROWS, SC_ROWS)], rows_v.at[c % 2], sem_in)

        pending = load(0)
        for c in range(chunks):
            pending.wait()
            if c + 1 < chunks:
                pending = load(c + 1)
            out0 = pltpu.async_copy(rows_v.at[c % 2], out_hbm.at[pos_v.at[0, c]], sem_out0)
            out1 = pltpu.async_copy(rows_v.at[c % 2], out_hbm.at[pos_v.at[1, c]], sem_out1)
            out0.wait()
            out1.wait()

    return dispatch(table, pos4)


def _final_kernel(x1_ref, y0_ref, y1_ref, route_ref, g_ref, o_ref):
    route = route_ref[...]
    lane = lax.broadcasted_iota(jnp.int32, route.shape, 1)
    w0 = jnp.sum(jnp.where(lane == TOP_K, route, 0.0), axis=-1, keepdims=True)
    w1 = jnp.sum(jnp.where(lane == TOP_K + 1, route, 0.0), axis=-1, keepdims=True)
    x = x1_ref[...] + (w0 * y0_ref[...].astype(F32) + w1 * y1_ref[...].astype(F32))
    o_ref[...] = x * lax.rsqrt(jnp.mean(x * x, axis=-1, keepdims=True) + EPS) * g_ref[...]


def _final(x1, y0, y1, route, g_final, tm):
    T, D = x1.shape
    row = pl.BlockSpec((tm, D), lambda i: (i, 0))
    return pl.pallas_call(
        _final_kernel,
        grid=(T // tm,),
        in_specs=[row, row, row, pl.BlockSpec((tm, LANE), lambda i: (i, 0)), _const_spec((1, D))],
        out_specs=row,
        out_shape=jax.ShapeDtypeStruct((T, D), F32),
        compiler_params=_cparams("parallel"),
        name="final",
    )(x1, y0, y1, route, g_final)


def _route_tables(ids, rank, counts, tm):
    T = ids.shape[0]
    A = T * TOP_K
    n_experts = counts.shape[0]
    padded = ((counts + tm - 1) // tm) * tm
    ends = jnp.cumsum(padded)
    starts = ends - padded
    onehot = ids[:, :, None] == jnp.arange(n_experts, dtype=jnp.int32)[None, None, :]
    pos = jnp.sum(jnp.where(onehot, starts[None, None, :], 0), axis=-1) + rank
    n_tiles = A // tm + n_experts
    tile_start = jnp.arange(n_tiles, dtype=jnp.int32) * tm
    tile_expert_raw = jnp.sum((ends[None, :] <= tile_start[:, None]).astype(jnp.int32), axis=1)
    valid = tile_start < ends[-1]
    last_expert = jnp.max(jnp.where(counts > 0, jnp.arange(n_experts, dtype=jnp.int32), 0))
    tile_expert = jnp.where(valid, jnp.minimum(tile_expert_raw, n_experts - 1), last_expert)
    first = jnp.concatenate([jnp.ones((1,), bool), tile_expert[1:] != tile_expert[:-1]])
    tile_flag = jnp.where(valid, jnp.where(first, 2, 1), 0).astype(jnp.int32)
    tile_rows = jnp.clip((starts + counts)[tile_expert] - tile_start, 0, tm)
    tile_rows = jnp.where(valid, tile_rows, 0).astype(jnp.int32)
    e_idx = jnp.arange(n_experts, dtype=jnp.int32)
    nonempty = counts > 0
    later = (e_idx[None, :] > e_idx[:, None]) & nonempty[None, :]
    next_nonempty = jnp.min(jnp.where(later, e_idx[None, :], n_experts), axis=1)
    next_nonempty = jnp.where(next_nonempty < n_experts, next_nonempty, -1).astype(jnp.int32)
    order = jnp.sum(((e_idx[None, :] < e_idx[:, None]) & nonempty[None, :]).astype(jnp.int32), axis=1)
    next_expert = next_nonempty[tile_expert]
    slot = (order[tile_expert] % 2).astype(jnp.int32)
    return pos, tile_expert, tile_flag, tile_rows, next_expert, slot


def _largest_tile(n, cap):
    t = min(n, cap)
    while n % t:
        t //= 2
    return t


def kernel(x, g_mix, w_in, conv_w, a_log, dt_bias, gdn_norm_w, diff_lambda, diff_norm_w, w_branch_a, w_branch_d,
           w_out, g_ffn, w_group, b_group, w_router, b_router, w_exp_gate, w_exp_up, w_exp_down, g_final):
    B, S, D = x.shape
    T = B * S
    depth = g_mix.shape[0]
    Hg = a_log.shape[-1]
    gdn_qk = Hg * GDN_DK
    gdn_v = Hg * GDN_DV
    diff_v = w_branch_d.shape[1]
    Hd = diff_v // DIFF_DV
    diff_qk = Hd * 2 * DIFF_DH
    n_groups = w_group.shape[-1]
    n_experts = w_router.shape[-1]
    per_group = n_experts // n_groups
    assert 4 * Hg <= LANE and n_groups + n_experts <= LANE
    assert S % GDN_CHUNK == 0 and D % LANE == 0

    ab_lo = 2 * gdn_qk + 2 * gdn_v
    ab_hi = ab_lo + 4 * Hg
    gates_lo = ab_hi + 2 * diff_qk + diff_v
    main_cols = 2 * D + ab_lo + 2 * diff_qk + diff_v
    gdn_off = 2 * D
    q_d_off = gdn_off + ab_lo
    k_d_off = q_d_off + diff_qk
    v_d_off = k_d_off + diff_qk

    tm_norm = _largest_tile(T, 512)
    tm_proj = _largest_tile(T, 1024)
    tn_proj = _largest_tile(main_cols, 1024)
    tm_merge = _largest_tile(T, 256)
    tm_exp = _largest_tile(T * TOP_K, 256)
    tm_final = _largest_tile(T, 512)

    slopes = jnp.exp2(-8.0 * (jnp.arange(Hd, dtype=F32) + 1.0) / Hd)
    x2 = x.reshape(T, D)
    for layer in range(depth):
        lam_init = 0.8 - 0.6 * math.exp(-0.3 * layer)
        w_in_l = w_in[layer]
        w_main, w_ab = _reorder_cast(w_in_l.T, ((gates_lo, 2 * D), (0, ab_lo), (ab_hi, gates_lo - ab_hi)),
                                     (ab_lo, 4 * Hg), _largest_tile(D, 256))
        alog_pad = jnp.pad(a_log[layer].reshape(1, 2 * Hg), ((0, 0), (0, LANE - 2 * Hg)))
        dtb_pad = jnp.pad(dt_bias[layer].reshape(1, 2 * Hg), ((0, 0), (0, LANE - 2 * Hg)))

        h, gb = _norm_proj(x2, g_mix[layer].reshape(1, D), w_ab, alog_pad, dtb_pad, 2 * Hg, tm_norm)
        p = _matmul_nt(h, w_main, tm_proj, tn_proj, BF16)

        grow = gb[:, :2 * Hg].reshape(B, S // GDN_CHUNK, GDN_CHUNK, 2 * Hg).transpose(0, 3, 1, 2)
        conv_w8 = jnp.pad(conv_w[layer], ((0, 8 - GDN_CONV), (0, 0)))
        y_a = _gdn(p, gdn_off // LANE, conv_w8, gb, grow, gdn_norm_w[layer].reshape(1, GDN_DV), B, S, Hg)
        y_d = _diff_attn(p, slopes, diff_lambda[layer], diff_norm_w[layer].reshape(DIFF_DV, 1), B, S, Hd,
                         q_d_off // LANE, k_d_off // LANE, v_d_off // LANE, lam_init)

        w_r = jnp.pad(jnp.concatenate([w_group[layer], w_router[layer]], axis=1),
                      ((0, 0), (0, LANE - n_groups - n_experts)))
        wr_hi = w_r.astype(BF16)
        wr_lo = (w_r - wr_hi.astype(F32)).astype(BF16)
        b_r = jnp.pad(jnp.concatenate([b_group[layer], b_router[layer]]).reshape(1, -1),
                      ((0, 0), (0, LANE - n_groups - n_experts)))
        x1, h2, route, counts = _merge(y_a, y_d, p, x2,
                                       w_branch_a[layer].astype(BF16), w_branch_d[layer].astype(BF16),
                                       w_out[layer].astype(BF16), g_ffn[layer].reshape(1, D),
                                       wr_hi, wr_lo, b_r, n_groups, per_group, tm_merge)

        ids = route[:, 0:TOP_K].astype(jnp.int32)
        rank = route[:, 2 * TOP_K:3 * TOP_K].astype(jnp.int32)
        counts = counts[0, n_groups:n_groups + n_experts].astype(jnp.int32)
        pos, tile_expert, tile_flag, tile_rows, next_expert, slot = _route_tables(ids, rank, counts, tm_exp)
        xs = _dispatch_rows(h2, pos, tile_expert.shape[0] * tm_exp)
        ys = _experts(tile_expert, tile_flag, tile_rows, next_expert, slot, xs, w_exp_gate[layer],
                      w_exp_up[layer], w_exp_down[layer], tm_exp)
        y0 = ys.at[pos[:, 0]].get(mode="promise_in_bounds")
        y1 = ys.at[pos[:, 1]].get(mode="promise_in_bounds")
        if layer + 1 < depth:
            wts = route[:, TOP_K:2 * TOP_K]
            x2 = x1 + (wts[:, 0:1] * y0.astype(F32) + wts[:, 1:2] * y1.astype(F32))
    out = _final(x1, y0, y1, route, g_final.reshape(1, D), tm_final)
    return out.reshape(B, S, D)
```

```python
import functools
import math

import jax
import jax.numpy as jnp
from jax import lax
from jax.experimental import pallas as pl
from jax.experimental.pallas import tpu as pltpu
from jax.experimental.pallas import tpu_sc as plsc

F32 = jnp.float32
BF16 = jnp.bfloat16
EPS = 1e-6
LANE = 128
GDN_DK = 128
GDN_DV = 128
GDN_CONV = 5
GDN_CHUNK = 64
DIFF_DH = 64
DIFF_DV = 2 * DIFF_DH
TOP_K = 2
LOG2E = 1.4426950408889634
NEG_BIG = -1e30
VMEM_LIMIT_BYTES = 56 * 1024 * 1024

_HI = lax.Precision.HIGHEST


def _cparams(*sem):
    return pltpu.CompilerParams(dimension_semantics=sem, vmem_limit_bytes=VMEM_LIMIT_BYTES)


def _const_spec(shape):
    nd = len(shape)
    return pl.BlockSpec(shape, lambda *_: (0,) * nd, pipeline_mode=pl.Buffered(1))


def _mm(a, b):
    return jnp.dot(a.astype(BF16), b.astype(BF16), preferred_element_type=F32)


def _split3(x):
    hi = x.astype(BF16)
    r = x - hi.astype(F32)
    mid = r.astype(BF16)
    lo = (r - mid.astype(F32)).astype(BF16)
    return hi, mid, lo


def _pack_pairs(x):
    bits = lax.bitcast_convert_type(x.astype(BF16).astype(F32), jnp.uint32)
    half = bits.shape[1] // 2
    return bits[:, :half] | (bits[:, half:] >> 16)


def _unpack_pairs(packed):
    left = lax.bitcast_convert_type(packed & jnp.uint32(0xFFFF0000), F32).astype(BF16)
    right = lax.bitcast_convert_type(packed << 16, F32).astype(BF16)
    return jnp.concatenate([left, right], axis=1)


def _softplus(x):
    return jnp.maximum(x, 0.0) + jnp.log(1.0 + jnp.exp(-jnp.abs(x)))


def _sigmoid(x):
    return 1.0 / (1.0 + jnp.exp(-x))


def _norm_proj_kernel(x_ref, g_ref, wab_ref, alog_ref, dtb_ref, h_ref, gb_ref, *, n_dir_heads):
    x = x_ref[...]
    h = x * lax.rsqrt(jnp.mean(x * x, axis=-1, keepdims=True) + EPS) * g_ref[...]
    hb = h.astype(BF16)
    h_ref[...] = hb
    ab = lax.dot_general(hb, wab_ref[...], (((1,), (1,)), ((), ())), preferred_element_type=F32)
    g = -jnp.exp(alog_ref[...]) * _softplus(ab + dtb_ref[...])
    beta = _sigmoid(ab)
    tm = x.shape[0]
    row = lax.broadcasted_iota(jnp.int32, (tm, tm), 0)
    col = lax.broadcasted_iota(jnp.int32, (tm, tm), 1)
    same = (row // GDN_CHUNK) == (col // GDN_CHUNK)
    prefix = jnp.where(same & (col <= row), 1.0, 0.0).astype(BF16)
    suffix = jnp.where(same & (col >= row), 1.0, 0.0).astype(BF16)
    pieces = _split3(g)
    cs_f = sum(jnp.dot(prefix, p, preferred_element_type=F32) for p in pieces)
    cs_b = sum(jnp.dot(suffix, p, preferred_element_type=F32) for p in pieces)
    lane = lax.broadcasted_iota(jnp.int32, g.shape, 1)
    gb_ref[...] = jnp.where(lane < n_dir_heads // 2, cs_f, jnp.where(lane < n_dir_heads, cs_b, beta))


def _norm_proj(x2, g_mix, w_ab, alog_pad, dtb_pad, n_dir_heads, tm):
    T, D = x2.shape
    return pl.pallas_call(
        functools.partial(_norm_proj_kernel, n_dir_heads=n_dir_heads),
        grid=(T // tm,),
        in_specs=[pl.BlockSpec((tm, D), lambda i: (i, 0)),
                  _const_spec((1, D)), _const_spec((LANE, D)), _const_spec((1, LANE)), _const_spec((1, LANE))],
        out_specs=[pl.BlockSpec((tm, D), lambda i: (i, 0)),
                   pl.BlockSpec((tm, LANE), lambda i: (i, 0))],
        out_shape=[jax.ShapeDtypeStruct((T, D), BF16),
                   jax.ShapeDtypeStruct((T, LANE), F32)],
        compiler_params=_cparams("parallel"),
        name="norm_proj",
    )(x2, g_mix, w_ab, alog_pad, dtb_pad)


REORDER_ROWS = 512


def _reorder_cast_kernel(w_ref, o_ref, narrow_ref, *, segments, narrow):
    n_start, n_width = narrow
    row = lax.broadcasted_iota(jnp.int32, narrow_ref.shape, 0)
    narrow_ref[...] = jnp.where(row < n_width, w_ref[n_start:n_start + LANE, :], 0.0).astype(narrow_ref.dtype)
    off = 0
    for start, width in segments:
        for c in range(0, width, REORDER_ROWS):
            n = min(REORDER_ROWS, width - c)
            o_ref[off:off + n, :] = w_ref[start + c:start + c + n, :].astype(o_ref.dtype)
            off += n


def _reorder_cast(wT, segments, narrow, tc):
    rows, cols = wT.shape
    out_rows = sum(width for _, width in segments)
    assert narrow[0] % 8 == 0 and narrow[1] <= LANE and narrow[0] + LANE <= rows
    assert all(start % 8 == 0 and width % 16 == 0 for start, width in segments)
    return pl.pallas_call(
        functools.partial(_reorder_cast_kernel, segments=segments, narrow=narrow),
        grid=(cols // tc,),
        in_specs=[pl.BlockSpec((rows, tc), lambda i: (0, i))],
        out_specs=[pl.BlockSpec((out_rows, tc), lambda i: (0, i)), pl.BlockSpec((LANE, tc), lambda i: (0, i))],
        out_shape=[jax.ShapeDtypeStruct((out_rows, cols), BF16), jax.ShapeDtypeStruct((LANE, cols), BF16)],
        compiler_params=_cparams("parallel"),
        name="reorder_cast",
    )(wT)


_NT_DIMS = (((1,), (1,)), ((), ()))


def _matmul_nt_kernel(a_ref, bT_ref, o_ref):
    o_ref[...] = lax.dot_general(a_ref[...], bT_ref[...], _NT_DIMS, preferred_element_type=F32).astype(o_ref.dtype)


def _matmul_nt(a, bT, tm, tn, out_dtype):
    M, K = a.shape
    N = bT.shape[0]
    return pl.pallas_call(
        _matmul_nt_kernel,
        grid=(N // tn, M // tm),
        in_specs=[pl.BlockSpec((tm, K), lambda j, i: (i, 0)),
                  pl.BlockSpec((tn, K), lambda j, i: (j, 0))],
        out_specs=pl.BlockSpec((tm, tn), lambda j, i: (i, j)),
        out_shape=jax.ShapeDtypeStruct((M, N), out_dtype),
        compiler_params=_cparams("parallel", "parallel"),
        name="in_proj",
    )(a, bT)


def _bmm_tn(a, b):
    return jnp.stack([lax.dot_general(a[n], b[n], (((0,), (0,)), ((), ())), preferred_element_type=F32)
                      for n in range(a.shape[0])])


def _gdn_kernel(q_ref, k_ref, v_ref, z_ref, cwq_ref, cwk_ref, cwv_ref, gb_ref, grow_ref, nw_ref,
                y_ref,
                xpad, qn, kn, vn, u_s, w_s, qg_s, at_s, tm_s, tn_s, egl_s, st_s,
                *, n_heads, seq, conv_rows, prep_chunks):
    C = GDN_CHUNK
    n_chunks = seq // C
    head = pl.program_id(1)
    pad = 8
    half = GDN_CONV // 2

    zeros_pad = jnp.zeros((pad, LANE), F32)
    xpad[0:pad, :] = zeros_pad
    xpad[pad + seq:pad + seq + pad, :] = zeros_pad
    for src, cw_ref, dst, mode in ((q_ref, cwq_ref, qn, "q"), (k_ref, cwk_ref, kn, "k"), (v_ref, cwv_ref, vn, "v")):
        xpad[pad:pad + seq, :] = src[...].astype(F32)
        for t in range(seq // conv_rows):
            r0 = t * conv_rows
            acc = jnp.zeros((conv_rows, LANE), F32)
            for j in range(GDN_CONV):
                lo = pad + r0 + j - half
                acc = acc + xpad[lo:lo + conv_rows, :] * cw_ref[j:j + 1, :]
            y = acc * _sigmoid(acc)
            if mode != "v":
                y = y * lax.rsqrt(jnp.sum(y * y, axis=-1, keepdims=True) + EPS)
            if mode == "q":
                y = y * (GDN_DK ** -0.5)
            dst[r0:r0 + conv_rows, :] = y

    nb = prep_chunks
    R = nb * C
    nb2 = 2 * nb
    bi = lax.broadcasted_iota(jnp.int32, (nb2, C, C), 0)
    ii = lax.broadcasted_iota(jnp.int32, (nb2, C, C), 1)
    jj = lax.broadcasted_iota(jnp.int32, (nb2, C, C), 2)
    fwd = bi < nb
    incl = (fwd & (ii >= jj)) | (~fwd & (ii <= jj))
    strict = (fwd & (ii > jj)) | (~fwd & (ii < jj))
    eye = jnp.where(ii == jj, 1.0, 0.0).astype(F32)
    lane = lax.broadcasted_iota(jnp.int32, (R, LANE), 1)
    both = lambda x: jnp.concatenate([x, x], axis=0)

    def prep(t, carry):
        r0 = pl.multiple_of(t * R, R)
        n0 = pl.multiple_of(t * nb, nb)
        q3 = qn[pl.ds(r0, R), :].reshape(nb, C, LANE)
        k3 = kn[pl.ds(r0, R), :].reshape(nb, C, LANE)
        v3 = vn[pl.ds(r0, R), :].reshape(nb, C, LANE)
        gb_blk = gb_ref[pl.ds(r0, R), :]
        kq = jnp.concatenate([k3, q3], axis=1).astype(BF16)
        gram = jnp.einsum("nik,njk->nij", kq, k3.astype(BF16), preferred_element_type=F32)
        kk, qk = both(gram[:, :C, :]), both(gram[:, C:, :])
        gcb, bt, grow, gl = [], [], [], []
        for d in range(2):
            colidx = d * n_heads + head
            gcb_d = jnp.sum(jnp.where(lane == colidx, gb_blk, 0.0), axis=-1, keepdims=True).reshape(nb, C, 1)
            gcb.append(gcb_d)
            bt.append(jnp.sum(jnp.where(lane == 2 * n_heads + colidx, gb_blk, 0.0), axis=-1,
                              keepdims=True).reshape(nb, C, 1))
            grow.append(grow_ref[0, colidx, pl.ds(n0, nb), :].reshape(nb, 1, C))
            gl.append(gcb_d[:, C - 1:C, :] if d == 0 else gcb_d[:, 0:1, :])
        gcb, bt, grow, gl = (jnp.concatenate(x, axis=0) for x in (gcb, bt, grow, gl))
        k2, q2, v2 = both(k3), both(q3), both(v3)

        decay = jnp.exp(jnp.where(incl, gcb - grow, -jnp.inf))
        L = jnp.where(strict, bt * kk * decay, 0.0)
        attn = qk * decay
        ainv = eye - L
        P = L
        for _ in range(int(math.log2(C)) - 1):
            Pb = P.astype(BF16)
            P = jnp.einsum("nij,njk->nik", Pb, Pb, preferred_element_type=F32)
            ainv = ainv + jnp.einsum("nij,njk->nik", ainv.astype(BF16), P.astype(BF16),
                                     preferred_element_type=F32)
        eg = jnp.exp(gcb)
        rhs = jnp.concatenate([v2 * bt, k2 * (bt * eg)], axis=-1).astype(BF16)
        uw = jnp.einsum("nij,njd->nid", ainv.astype(BF16), rhs, preferred_element_type=F32).astype(BF16)
        kg = (k2 * jnp.exp(gl - gcb)).astype(BF16)
        trans = _bmm_tn(kg, uw)
        qg = (q2 * eg).astype(BF16)
        egl = jnp.broadcast_to(jnp.exp(gl).reshape(nb2, 1), (nb2, LANE))
        for d in range(2):
            sl = slice(d * nb, (d + 1) * nb)
            u_s[d, pl.ds(r0, R), :] = uw[sl, :, :LANE].reshape(R, LANE)
            w_s[d, pl.ds(r0, R), :] = uw[sl, :, LANE:].reshape(R, LANE)
            qg_s[d, pl.ds(r0, R), :] = qg[sl].reshape(R, LANE)
            at_s[d, pl.ds(r0, R), :] = attn[sl].reshape(R, C).astype(BF16)
            tn_s[d, pl.ds(n0 * GDN_DK, nb * GDN_DK), :] = trans[sl, :, :LANE].reshape(nb * GDN_DK, LANE).astype(BF16)
            tm_s[d, pl.ds(n0 * GDN_DK, nb * GDN_DK), :] = (-trans[sl, :, LANE:]).reshape(nb * GDN_DK, LANE).astype(BF16)
            egl_s[d, pl.ds(n0, nb), :] = egl[sl]
        return carry

    lax.fori_loop(0, n_chunks // nb, prep, 0)

    def chunk_step(d, n, state):
        r = pl.multiple_of(n * GDN_DK, GDN_DK)
        sb = state.astype(BF16)
        st_s[d, pl.ds(r, GDN_DK), :] = sb
        return (state * egl_s[d, pl.ds(n, 1), :]
                + jnp.dot(tm_s[d, pl.ds(r, GDN_DK), :], sb, preferred_element_type=F32)
                + tn_s[d, pl.ds(r, GDN_DK), :].astype(F32))

    def scan(n, carry):
        sf, sbw = carry
        return chunk_step(0, n, sf), chunk_step(1, n_chunks - 1 - n, sbw)

    zero_state = jnp.zeros((GDN_DK, GDN_DV), F32)
    lax.fori_loop(0, n_chunks, scan, (zero_state, zero_state))

    def emit(t, carry):
        r0 = pl.multiple_of(t * R, R)
        s0 = pl.multiple_of(t * nb * GDN_DK, nb * GDN_DK)
        two = lambda ref, start, rows: jnp.concatenate([ref[0, pl.ds(start, rows), :], ref[1, pl.ds(start, rows), :]])
        st = two(st_s, s0, nb * GDN_DK).reshape(nb2, GDN_DK, GDN_DV)
        wq = jnp.concatenate([two(w_s, r0, R).reshape(nb2, C, LANE), two(qg_s, r0, R).reshape(nb2, C, LANE)], axis=1)
        ws_qs = jnp.einsum("ncd,nde->nce", wq, st, preferred_element_type=F32)
        vnew = two(u_s, r0, R).reshape(nb2, C, LANE).astype(F32) - ws_qs[:, :C, :]
        o = ws_qs[:, C:, :] + jnp.einsum("ncs,nse->nce", two(at_s, r0, R).reshape(nb2, C, C), vnew.astype(BF16),
                                         preferred_element_type=F32)
        o = (o[:nb] + o[nb:]).reshape(R, LANE)
        on = o * lax.rsqrt(jnp.mean(o * o, axis=-1, keepdims=True) + EPS) * nw_ref[...]
        z = z_ref[pl.ds(r0, R), :].astype(F32)
        y_ref[pl.ds(r0, R), :] = (on * (z * _sigmoid(z))).astype(y_ref.dtype)
        return carry

    lax.fori_loop(0, n_chunks // nb, emit, 0)


def _gdn(p, p_off, conv_w8, gb, grow, norm_w, batch, seq, n_heads):
    T = batch * seq
    qk_blocks = n_heads
    n_chunks = seq // GDN_CHUNK
    conv_rows = min(512, seq)
    prep_chunks = min(16, n_chunks)
    blk = lambda off: pl.BlockSpec((seq, LANE), lambda b, h: (b, p_off + off + h))
    cw = lambda off: pl.BlockSpec((8, LANE), lambda b, h: (0, off + h))
    return pl.pallas_call(
        functools.partial(_gdn_kernel, n_heads=n_heads, seq=seq, conv_rows=conv_rows, prep_chunks=prep_chunks),
        grid=(batch, n_heads),
        in_specs=[blk(0), blk(qk_blocks), blk(2 * qk_blocks), blk(3 * qk_blocks),
                  cw(0), cw(qk_blocks), cw(2 * qk_blocks),
                  pl.BlockSpec((seq, LANE), lambda b, h: (b, 0), pipeline_mode=pl.Buffered(1)),
                  pl.BlockSpec((1, 2 * n_heads, n_chunks, GDN_CHUNK), lambda b, h: (b, 0, 0, 0),
                               pipeline_mode=pl.Buffered(1)),
                  _const_spec((1, LANE))],
        out_specs=pl.BlockSpec((seq, LANE), lambda b, h: (b, h)),
        out_shape=jax.ShapeDtypeStruct((T, n_heads * GDN_DV), BF16),
        scratch_shapes=[
            pltpu.VMEM((seq + 16, LANE), F32),
            pltpu.VMEM((seq, LANE), F32),
            pltpu.VMEM((seq, LANE), F32),
            pltpu.VMEM((seq, LANE), F32),
            pltpu.VMEM((2, seq, LANE), BF16),
            pltpu.VMEM((2, seq, LANE), BF16),
            pltpu.VMEM((2, seq, LANE), BF16),
            pltpu.VMEM((2, seq, GDN_CHUNK), BF16),
            pltpu.VMEM((2, n_chunks * GDN_DK, GDN_DV), BF16),
            pltpu.VMEM((2, n_chunks * GDN_DK, GDN_DV), BF16),
            pltpu.VMEM((2, max(n_chunks, 8), LANE), F32),
            pltpu.VMEM((2, n_chunks * GDN_DK, GDN_DV), BF16),
        ],
        compiler_params=_cparams("parallel", "parallel"),
        name="gdn",
    )(p, p, p, p, conv_w8, conv_w8, conv_w8, gb, grow, norm_w)


def _diff_attn_kernel(slopes_ref, q_ref, k_ref, v_ref, lam_ref, nw_ref, o_ref,
                      qT_s, vT_s, klo_s, khi_s, rel_s, s_s, p_s, acc_s,
                      *, seq, tq, tk, lam_init):
    head = pl.program_id(1)
    slope2 = slopes_ref[head] * LOG2E
    scale2 = (DIFF_DH ** -0.5) * LOG2E
    n_kv = seq // tk
    maps = (klo_s, khi_s)

    rows = min(512, seq)
    for t in range(seq // rows):
        r0 = t * rows
        qt = (q_ref[r0:r0 + rows, :].astype(F32) * scale2).T.astype(BF16)
        frow = lax.broadcasted_iota(jnp.int32, (LANE, rows), 0)
        fcol = lax.broadcasted_iota(jnp.int32, (LANE, rows), 1)
        qa, qb, qc = (x.astype(F32) for x in _split3(((r0 + fcol) % tq).astype(F32) * slope2))
        qfeat = jnp.where(frow < 3, 1.0, jnp.where(frow == 3, -qa, jnp.where(frow == 4, -qb,
                                                   jnp.where(frow == 5, -qc, 0.0)))).astype(BF16)
        for sgn, feat in enumerate((qfeat, -qfeat)):
            qT_s[sgn, 0:LANE, r0:r0 + rows] = qt
            qT_s[sgn, LANE:2 * LANE, r0:r0 + rows] = feat
        vT_s[0:LANE, r0:r0 + rows] = v_ref[r0:r0 + rows, :].astype(F32).T.astype(BF16)
        vT_s[LANE:LANE + 16, r0:r0 + rows] = jnp.ones((16, rows), BF16)
        kt = k_ref[r0:r0 + rows, :]
        krow = lax.broadcasted_iota(jnp.int32, (rows, LANE), 0)
        lane = lax.broadcasted_iota(jnp.int32, (rows, LANE), 1)
        ka, kb, kc = (x.astype(F32) for x in _split3(((r0 + krow) % tk).astype(F32) * slope2))
        kfeat = jnp.where(lane == 0, ka, jnp.where(lane == 1, kb, jnp.where(lane == 2, kc,
                                                   jnp.where(lane < 6, 1.0, 0.0)))).astype(BF16)
        klo_s[r0:r0 + rows, 0:LANE] = jnp.where(lane < DIFF_DH, kt, jnp.zeros_like(kt))
        khi_s[r0:r0 + rows, 0:LANE] = jnp.where(lane >= DIFF_DH, kt, jnp.zeros_like(kt))
        klo_s[r0:r0 + rows, LANE:2 * LANE] = kfeat
        khi_s[r0:r0 + rows, LANE:2 * LANE] = kfeat
    kr = lax.broadcasted_iota(jnp.int32, (tk, tq), 0)
    qcol = lax.broadcasted_iota(jnp.int32, (tk, tq), 1)
    rel_s[...] = (qcol - kr).astype(F32) * slope2

    lf = lam_ref[...]
    lam = (jnp.exp(jnp.sum(lf[0:1, :] * lf[1:2, :], axis=-1, keepdims=True))
           - jnp.exp(jnp.sum(lf[2:3, :] * lf[3:4, :], axis=-1, keepdims=True)) + lam_init)

    def q_tile(i, carry):
        c_q = pl.multiple_of(i * tq, tq)
        j_diag = (i * tq) // tk

        def scores(t):
            j = j_diag if t == 0 else lax.rem(j_diag + t, n_kv)
            r_k = pl.multiple_of(j * tk, tk)
            d0 = jnp.asarray(i * tq - j * tk)
            if t == 0:
                bias = jnp.abs(rel_s[...] + d0.astype(F32) * slope2)
                for mp, k_s in enumerate(maps):
                    s_s[t % 2, mp] = jnp.dot(k_s[pl.ds(r_k, tk), 0:LANE], qT_s[0, 0:LANE, pl.ds(c_q, tq)],
                                             preferred_element_type=F32) - bias
                return r_k, 0.0
            sgn = jnp.asarray(j > j_diag).astype(jnp.int32)
            for mp, k_s in enumerate(maps):
                s_s[t % 2, mp] = jnp.dot(k_s[pl.ds(r_k, tk), :], qT_s[sgn, :, pl.ds(c_q, tq)],
                                         preferred_element_type=F32)
            return r_k, jnp.abs(d0).astype(F32) * slope2

        def weighted_values(t, r_k, alphas):
            vT = vT_s[:, pl.ds(r_k, tk)]
            for mp in range(len(maps)):
                upd = jnp.dot(vT, p_s[t % 2, mp], preferred_element_type=F32)
                acc_s[mp] = upd if t == 0 else acc_s[mp] * alphas[mp] + upd

        m = [jnp.full((1, tq), NEG_BIG, F32) for _ in maps]
        nxt = scores(0)
        prev = None
        for t in range(n_kv):
            r_k, const = nxt
            if t + 1 < n_kv:
                nxt = scores(t + 1)
            alphas = []
            for mp in range(len(maps)):
                m_new = jnp.maximum(m[mp], jnp.max(s_s[t % 2, mp], axis=0, keepdims=True) - const)
                alphas.append(jnp.exp2(m[mp] - m_new))
                p_s[t % 2, mp] = jnp.exp2(s_s[t % 2, mp] - (m_new + const)).astype(BF16)
                m[mp] = m_new
            if prev is not None:
                weighted_values(t - 1, *prev)
            prev = (r_k, alphas)
        weighted_values(n_kv - 1, *prev)

        o1, o2 = acc_s[0], acc_s[1]
        od = o1[0:DIFF_DV] / o1[DIFF_DV:DIFF_DV + 1] - lam * (o2[0:DIFF_DV] / o2[DIFF_DV:DIFF_DV + 1])
        yn = od * lax.rsqrt(jnp.mean(od * od, axis=0, keepdims=True) + EPS) * nw_ref[...] * (1.0 - lam_init)
        o_ref[pl.ds(c_q, tq), :] = yn.T.astype(o_ref.dtype)
        return carry

    lax.fori_loop(0, seq // tq, q_tile, 0)


def _diff_attn(p, slopes, diff_lambda, norm_w_col, batch, seq, n_heads, q_off, k_off, v_off, lam_init):
    T = batch * seq
    tq = min(256, seq)
    tk = min(512, seq)
    blk = lambda off: pl.BlockSpec((seq, LANE), lambda b, h: (b, off + h))
    return pl.pallas_call(
        functools.partial(_diff_attn_kernel, seq=seq, tq=tq, tk=tk, lam_init=lam_init),
        grid=(batch, n_heads),
        in_specs=[pl.BlockSpec(memory_space=pltpu.SMEM),
                  blk(q_off), blk(k_off), blk(v_off),
                  _const_spec((4, DIFF_DH)), _const_spec((DIFF_DV, 1))],
        out_specs=pl.BlockSpec((seq, LANE), lambda b, h: (b, h)),
        out_shape=jax.ShapeDtypeStruct((T, n_heads * DIFF_DV), BF16),
        scratch_shapes=[
            pltpu.VMEM((2, 2 * LANE, seq), BF16),
            pltpu.VMEM((DIFF_DV + 16, seq), BF16),
            pltpu.VMEM((seq, 2 * LANE), BF16),
            pltpu.VMEM((seq, 2 * LANE), BF16),
            pltpu.VMEM((tk, tq), F32),
            pltpu.VMEM((2, 2, tk, tq), F32),
            pltpu.VMEM((2, 2, tk, tq), BF16),
            pltpu.VMEM((2, DIFF_DV + 16, tq), F32),
        ],
        compiler_params=_cparams("parallel", "parallel"),
        name="diff_attn",
    )(slopes, p, p, p, diff_lambda, norm_w_col)


def _merge_kernel(ya_ref, yd_ref, ga_ref, gd_ref, x_ref, wa_ref, wd_ref, wo_ref, gffn_ref,
                  wr_hi_ref, wr_lo_ref, br_ref, x1_ref, h2_ref, route_ref, count_ref, *, n_groups, per_group):
    ma = jnp.dot(ya_ref[...], wa_ref[...], preferred_element_type=F32)
    md = jnp.dot(yd_ref[...], wd_ref[...], preferred_element_type=F32)
    merged = _sigmoid(ga_ref[...].astype(F32)) * ma + _sigmoid(gd_ref[...].astype(F32)) * md
    x1 = x_ref[...] + jnp.dot(merged.astype(BF16), wo_ref[...], preferred_element_type=F32)
    x1_ref[...] = x1
    h2 = x1 * lax.rsqrt(jnp.mean(x1 * x1, axis=-1, keepdims=True) + EPS) * gffn_ref[...]
    h2_ref[...] = _pack_pairs(h2)

    h_hi = h2.astype(BF16)
    h_lo = (h2 - h_hi.astype(F32)).astype(BF16)
    logits = (jnp.dot(h_hi, wr_hi_ref[...], preferred_element_type=F32)
              + jnp.dot(h_hi, wr_lo_ref[...], preferred_element_type=F32)
              + jnp.dot(h_lo, wr_hi_ref[...], preferred_element_type=F32)) + br_ref[...]
    lane = lax.broadcasted_iota(jnp.int32, logits.shape, 1)
    lane_f = lane.astype(F32)
    big = float(LANE)

    def first_argmax(vals, vmax):
        return jnp.min(jnp.where(vals == vmax, lane_f, big), axis=-1, keepdims=True)

    gl = jnp.where(lane < n_groups, logits, -jnp.inf)
    gmax = jnp.max(gl, axis=-1, keepdims=True)
    g_idx = first_argmax(gl, gmax)
    g_w = 1.0 / jnp.sum(jnp.exp(gl - gmax), axis=-1, keepdims=True)
    e_lo = n_groups + g_idx * per_group
    in_group = (lane_f >= e_lo) & (lane_f < e_lo + per_group)
    el = jnp.where(in_group, logits, -jnp.inf)
    emax = jnp.max(el, axis=-1, keepdims=True)
    pe = jnp.exp(el - emax)
    pe = pe / jnp.sum(pe, axis=-1, keepdims=True)
    p1 = jnp.max(pe, axis=-1, keepdims=True)
    i1 = first_argmax(jnp.where(in_group, pe, -1.0), p1)
    rest = jnp.where(in_group & (lane_f != i1), pe, -1.0)
    p2 = jnp.max(rest, axis=-1, keepdims=True)
    i2 = first_argmax(rest, p2)
    denom = p1 + p2
    w1 = p1 / denom * g_w
    w2 = p2 / denom * g_w
    @pl.when(pl.program_id(0) == 0)
    def _():
        count_ref[...] = jnp.zeros_like(count_ref)

    hit1 = lane_f == i1
    hit2 = lane_f == i2
    hits = jnp.where(hit1 | hit2, 1.0, 0.0)
    tm = hits.shape[0]
    earlier = (lax.broadcasted_iota(jnp.int32, (tm, tm), 1) < lax.broadcasted_iota(jnp.int32, (tm, tm), 0))
    before = count_ref[0:1, :] + jnp.dot(jnp.where(earlier, 1.0, 0.0).astype(BF16), hits.astype(BF16),
                                         preferred_element_type=F32)
    rank1 = jnp.sum(jnp.where(hit1, before, 0.0), axis=-1, keepdims=True)
    rank2 = jnp.sum(jnp.where(hit2, before, 0.0), axis=-1, keepdims=True)
    count_ref[...] = count_ref[...] + jnp.sum(hits, axis=0, keepdims=True)

    route = jnp.where(lane == 0, i1 - n_groups, 0.0)
    route = jnp.where(lane == 1, i2 - n_groups, route)
    route = jnp.where(lane == 2, w1, route)
    route = jnp.where(lane == 3, w2, route)
    route = jnp.where(lane == 4, rank1, route)
    route = jnp.where(lane == 5, rank2, route)
    route_ref[...] = route


def _merge(y_a, y_d, p, x2, wa, wd, wo, g_ffn, wr_hi, wr_lo, b_r, n_groups, per_group, tm):
    T, D = x2.shape
    va, vd = y_a.shape[1], y_d.shape[1]
    row = lambda w: pl.BlockSpec((tm, w), lambda i: (i, 0))
    return pl.pallas_call(
        functools.partial(_merge_kernel, n_groups=n_groups, per_group=per_group),
        grid=(T // tm,),
        in_specs=[row(va), row(vd),
                  pl.BlockSpec((tm, D), lambda i: (i, 0)),
                  pl.BlockSpec((tm, D), lambda i: (i, 1)),
                  row(D),
                  _const_spec((va, D)), _const_spec((vd, D)), _const_spec((D, D)), _const_spec((1, D)),
                  _const_spec((D, LANE)), _const_spec((D, LANE)), _const_spec((1, LANE))],
        out_specs=[row(D), row(D // 2), row(LANE), pl.BlockSpec((8, LANE), lambda i: (0, 0))],
        out_shape=[jax.ShapeDtypeStruct((T, D), F32),
                   jax.ShapeDtypeStruct((T, D // 2), jnp.uint32),
                   jax.ShapeDtypeStruct((T, LANE), F32),
                   jax.ShapeDtypeStruct((8, LANE), F32)],
        compiler_params=_cparams("arbitrary"),
        name="merge",
    )(y_a, y_d, p, p, x2, wa, wd, wo, g_ffn, wr_hi, wr_lo, b_r)


def _expert_kernel(tile_expert_ref, tile_flag_ref, tile_rows_ref, next_expert_ref, slot_ref,
                   xs_ref, wg_hbm, wu_hbm, wd_hbm, o_ref,
                   wg_f, wu_f, wd_f, wg_s, wu_s, wd_s, sems):
    i = pl.program_id(0)
    flag = tile_flag_ref[i]

    def weight_copies(expert, slot):
        return [pltpu.make_async_copy(hbm.at[expert], buf.at[slot], sems.at[slot, n])
                for n, (hbm, buf) in enumerate(((wg_hbm, wg_f), (wu_hbm, wu_f), (wd_hbm, wd_f)))]

    @pl.when(flag == 2)
    def _():
        slot = slot_ref[i]

        @pl.when(i == 0)
        def _():
            for copy in weight_copies(tile_expert_ref[i], slot):
                copy.start()

        for copy in weight_copies(tile_expert_ref[i], slot):
            copy.wait()
        nxt = next_expert_ref[i]

        @pl.when(nxt >= 0)
        def _():
            for copy in weight_copies(nxt, 1 - slot):
                copy.start()

        wg_s[...] = wg_f[slot].astype(BF16)
        wu_s[...] = wu_f[slot].astype(BF16)
        wd_s[...] = wd_f[slot].astype(BF16)

    @pl.when(flag > 0)
    def _():
        row = lax.broadcasted_iota(jnp.int32, xs_ref.shape, 0)
        x = _unpack_pairs(jnp.where(row < tile_rows_ref[i], xs_ref[...], jnp.uint32(0)))
        g = jnp.dot(x, wg_s[...], preferred_element_type=F32)
        u = jnp.dot(x, wu_s[...], preferred_element_type=F32)
        hid = g * _sigmoid(g) * u
        o_ref[...] = _pack_pairs(jnp.dot(hid.astype(BF16), wd_s[...], preferred_element_type=F32))

    @pl.when(flag == 0)
    def _():
        o_ref[...] = jnp.zeros_like(o_ref)


def _experts(tile_expert, tile_flag, tile_rows, next_expert, slot, xs, w_gate, w_up, w_down, tm):
    n_rows, half = xs.shape
    E, D, Fd = w_gate.shape
    assert D == 2 * half
    n_tiles = n_rows // tm
    hbm = pl.BlockSpec(memory_space=pl.ANY)
    grid_spec = pltpu.PrefetchScalarGridSpec(
        num_scalar_prefetch=5,
        grid=(n_tiles,),
        in_specs=[pl.BlockSpec((tm, half), lambda i, *_: (i, 0)), hbm, hbm, hbm],
        out_specs=pl.BlockSpec((tm, half), lambda i, *_: (i, 0)),
        scratch_shapes=[pltpu.VMEM((2, D, Fd), F32), pltpu.VMEM((2, D, Fd), F32), pltpu.VMEM((2, Fd, D), F32),
                        pltpu.VMEM((D, Fd), BF16), pltpu.VMEM((D, Fd), BF16), pltpu.VMEM((Fd, D), BF16),
                        pltpu.SemaphoreType.DMA((2, 3))],
    )
    return pl.pallas_call(
        _expert_kernel,
        grid_spec=grid_spec,
        out_shape=jax.ShapeDtypeStruct((n_rows, half), jnp.uint32),
        compiler_params=_cparams("arbitrary"),
        name="experts",
    )(tile_expert, tile_flag, tile_rows, next_expert, slot, xs, w_gate, w_up, w_down)


SC_CORES = 2
SC_SUBCORES = 16
SC_ROWS = 32


def _dispatch_rows(table, pos, n_rows):
    T, W = table.shape
    workers = SC_CORES * SC_SUBCORES
    chunks = T // (workers * SC_ROWS)
    assert workers * chunks * SC_ROWS == T
    pos4 = pos.T.reshape(TOP_K, workers, chunks, SC_ROWS).transpose(1, 0, 2, 3)
    mesh = plsc.VectorSubcoreMesh(core_axis_name="c", subcore_axis_name="s")

    @functools.partial(
        pl.kernel, mesh=mesh,
        out_type=jax.ShapeDtypeStruct((n_rows, W), table.dtype),
        scratch_types=[pltpu.VMEM((TOP_K, chunks, SC_ROWS), jnp.int32),
                       pltpu.VMEM((2, SC_ROWS, W), table.dtype),
                       pltpu.SemaphoreType.DMA, pltpu.SemaphoreType.DMA, pltpu.SemaphoreType.DMA],
        name="dispatch_rows",
    )
    def dispatch(table_hbm, pos_hbm, out_hbm, pos_v, rows_v, sem_in, sem_out0, sem_out1):
        wid = lax.axis_index("s") * SC_CORES + lax.axis_index("c")
        base = wid * (chunks * SC_ROWS)
        pltpu.sync_copy(pos_hbm.at[wid], pos_v)

        def load(c):
            return pltpu.async_copy(table_hbm.at[pl.ds(base + c * SC_ROWS, SC_ROWS)], rows_v.at[c % 2], sem_in)

        pending = load(0)
        for c in range(chunks):
            pending.wait()
            if c + 1 < chunks:
                pending = load(c + 1)
            out0 = pltpu.async_copy(rows_v.at[c % 2], out_hbm.at[pos_v.at[0, c]], sem_out0)
            out1 = pltpu.async_copy(rows_v.at[c % 2], out_hbm.at[pos_v.at[1, c]], sem_out1)
            out0.wait()
            out1.wait()

    return dispatch(table, pos4)


def _collect_rows(table, pos):
    T = pos.shape[0]
    W = table.shape[1]
    workers = SC_CORES * SC_SUBCORES
    chunks = T // (workers * SC_ROWS)
    assert workers * chunks * SC_ROWS == T
    pos4 = pos.T.reshape(TOP_K, workers, chunks, SC_ROWS).transpose(1, 0, 2, 3)
    mesh = plsc.VectorSubcoreMesh(core_axis_name="c", subcore_axis_name="s")
    steps = [(c, k) for c in range(chunks) for k in range(TOP_K)]

    @functools.partial(
        pl.kernel, mesh=mesh,
        out_type=jax.ShapeDtypeStruct((TOP_K, T, W), table.dtype),
        scratch_types=[pltpu.VMEM((TOP_K, chunks, SC_ROWS), jnp.int32),
                       pltpu.VMEM((2, SC_ROWS, W), table.dtype),
                       pltpu.SemaphoreType.DMA],
        name="collect_rows",
    )
    def collect(table_hbm, pos_hbm, out_hbm, pos_v, rows_v, sem_in):
        wid = lax.axis_index("s") * SC_CORES + lax.axis_index("c")
        base = wid * (chunks * SC_ROWS)
        pltpu.sync_copy(pos_hbm.at[wid], pos_v)

        def gather(n):
            c, k = steps[n]
            return pltpu.async_copy(table_hbm.at[pos_v.at[k, c]], rows_v.at[n % 2], sem_in)

        pending = gather(0)
        for n, (c, k) in enumerate(steps):
            pending.wait()
            if n + 1 < len(steps):
                pending = gather(n + 1)
            pltpu.sync_copy(rows_v.at[n % 2], out_hbm.at[k, pl.ds(base + c * SC_ROWS, SC_ROWS)])

    return collect(table, pos4)


def _final_kernel(x1_ref, y0_ref, y1_ref, route_ref, g_ref, o_ref):
    route = route_ref[...]
    lane = lax.broadcasted_iota(jnp.int32, route.shape, 1)
    w0 = jnp.sum(jnp.where(lane == TOP_K, route, 0.0), axis=-1, keepdims=True)
    w1 = jnp.sum(jnp.where(lane == TOP_K + 1, route, 0.0), axis=-1, keepdims=True)
    y0 = _unpack_pairs(y0_ref[...]).astype(F32)
    y1 = _unpack_pairs(y1_ref[...]).astype(F32)
    x = x1_ref[...] + (w0 * y0 + w1 * y1)
    o_ref[...] = x * lax.rsqrt(jnp.mean(x * x, axis=-1, keepdims=True) + EPS) * g_ref[...]


def _final(x1, ys_tok, route, g_final, tm):
    T, D = x1.shape
    row = pl.BlockSpec((tm, D), lambda i: (i, 0))
    y_spec = lambda k: pl.BlockSpec((None, tm, D // 2), lambda i: (k, i, 0))
    return pl.pallas_call(
        _final_kernel,
        grid=(T // tm,),
        in_specs=[row, y_spec(0), y_spec(1), pl.BlockSpec((tm, LANE), lambda i: (i, 0)), _const_spec((1, D))],
        out_specs=row,
        out_shape=jax.ShapeDtypeStruct((T, D), F32),
        compiler_params=_cparams("parallel"),
        name="final",
    )(x1, ys_tok, ys_tok, route, g_final)


def _route_tables(ids, rank, counts, tm):
    T = ids.shape[0]
    A = T * TOP_K
    n_experts = counts.shape[0]
    padded = ((counts + tm - 1) // tm) * tm
    ends = jnp.cumsum(padded)
    starts = ends - padded
    onehot = ids[:, :, None] == jnp.arange(n_experts, dtype=jnp.int32)[None, None, :]
    pos = jnp.sum(jnp.where(onehot, starts[None, None, :], 0), axis=-1) + rank
    n_tiles = A // tm + n_experts
    tile_start = jnp.arange(n_tiles, dtype=jnp.int32) * tm
    tile_expert_raw = jnp.sum((ends[None, :] <= tile_start[:, None]).astype(jnp.int32), axis=1)
    valid = tile_start < ends[-1]
    last_expert = jnp.max(jnp.where(counts > 0, jnp.arange(n_experts, dtype=jnp.int32), 0))
    tile_expert = jnp.where(valid, jnp.minimum(tile_expert_raw, n_experts - 1), last_expert)
    first = jnp.concatenate([jnp.ones((1,), bool), tile_expert[1:] != tile_expert[:-1]])
    tile_flag = jnp.where(valid, jnp.where(first, 2, 1), 0).astype(jnp.int32)
    tile_rows = jnp.clip((starts + counts)[tile_expert] - tile_start, 0, tm)
    tile_rows = jnp.where(valid, tile_rows, 0).astype(jnp.int32)
    e_idx = jnp.arange(n_experts, dtype=jnp.int32)
    nonempty = counts > 0
    later = (e_idx[None, :] > e_idx[:, None]) & nonempty[None, :]
    next_nonempty = jnp.min(jnp.where(later, e_idx[None, :], n_experts), axis=1)
    next_nonempty = jnp.where(next_nonempty < n_experts, next_nonempty, -1).astype(jnp.int32)
    order = jnp.sum(((e_idx[None, :] < e_idx[:, None]) & nonempty[None, :]).astype(jnp.int32), axis=1)
    next_expert = next_nonempty[tile_expert]
    slot = (order[tile_expert] % 2).astype(jnp.int32)
    return pos, tile_expert, tile_flag, tile_rows, next_expert, slot


def _largest_tile(n, cap):
    t = min(n, cap)
    while n % t:
        t //= 2
    return t


def kernel(x, g_mix, w_in, conv_w, a_log, dt_bias, gdn_norm_w, diff_lambda, diff_norm_w, w_branch_a, w_branch_d,
           w_out, g_ffn, w_group, b_group, w_router, b_router, w_exp_gate, w_exp_up, w_exp_down, g_final):
    B, S, D = x.shape
    T = B * S
    depth = g_mix.shape[0]
    Hg = a_log.shape[-1]
    gdn_qk = Hg * GDN_DK
    gdn_v = Hg * GDN_DV
    diff_v = w_branch_d.shape[1]
    Hd = diff_v // DIFF_DV
    diff_qk = Hd * 2 * DIFF_DH
    n_groups = w_group.shape[-1]
    n_experts = w_router.shape[-1]
    per_group = n_experts // n_groups
    assert 4 * Hg <= LANE and n_groups + n_experts <= LANE
    assert S % GDN_CHUNK == 0 and D % LANE == 0

    ab_lo = 2 * gdn_qk + 2 * gdn_v
    ab_hi = ab_lo + 4 * Hg
    gates_lo = ab_hi + 2 * diff_qk + diff_v
    main_cols = 2 * D + ab_lo + 2 * diff_qk + diff_v
    gdn_off = 2 * D
    q_d_off = gdn_off + ab_lo
    k_d_off = q_d_off + diff_qk
    v_d_off = k_d_off + diff_qk

    tm_norm = _largest_tile(T, 512)
    tm_proj = _largest_tile(T, 1024)
    tn_proj = _largest_tile(main_cols, 1024)
    tm_merge = _largest_tile(T, 256)
    tm_exp = _largest_tile(T * TOP_K, 256)
    tm_final = _largest_tile(T, 512)

    slopes = jnp.exp2(-8.0 * (jnp.arange(Hd, dtype=F32) + 1.0) / Hd)
    x2 = x.reshape(T, D)
    for layer in range(depth):
        lam_init = 0.8 - 0.6 * math.exp(-0.3 * layer)
        w_in_l = w_in[layer]
        w_main, w_ab = _reorder_cast(w_in_l.T, ((gates_lo, 2 * D), (0, ab_lo), (ab_hi, gates_lo - ab_hi)),
                                     (ab_lo, 4 * Hg), _largest_tile(D, 256))
        alog_pad = jnp.pad(a_log[layer].reshape(1, 2 * Hg), ((0, 0), (0, LANE - 2 * Hg)))
        dtb_pad = jnp.pad(dt_bias[layer].reshape(1, 2 * Hg), ((0, 0), (0, LANE - 2 * Hg)))

        h, gb = _norm_proj(x2, g_mix[layer].reshape(1, D), w_ab, alog_pad, dtb_pad, 2 * Hg, tm_norm)
        p = _matmul_nt(h, w_main, tm_proj, tn_proj, BF16)

        grow = gb[:, :2 * Hg].reshape(B, S // GDN_CHUNK, GDN_CHUNK, 2 * Hg).transpose(0, 3, 1, 2)
        conv_w8 = jnp.pad(conv_w[layer], ((0, 8 - GDN_CONV), (0, 0)))
        y_a = _gdn(p, gdn_off // LANE, conv_w8, gb, grow, gdn_norm_w[layer].reshape(1, GDN_DV), B, S, Hg)
        y_d = _diff_attn(p, slopes, diff_lambda[layer], diff_norm_w[layer].reshape(DIFF_DV, 1), B, S, Hd,
                         q_d_off // LANE, k_d_off // LANE, v_d_off // LANE, lam_init)

        w_r = jnp.pad(jnp.concatenate([w_group[layer], w_router[layer]], axis=1),
                      ((0, 0), (0, LANE - n_groups - n_experts)))
        wr_hi = w_r.astype(BF16)
        wr_lo = (w_r - wr_hi.astype(F32)).astype(BF16)
        b_r = jnp.pad(jnp.concatenate([b_group[layer], b_router[layer]]).reshape(1, -1),
                      ((0, 0), (0, LANE - n_groups - n_experts)))
        x1, h2, route, counts = _merge(y_a, y_d, p, x2,
                                       w_branch_a[layer].astype(BF16), w_branch_d[layer].astype(BF16),
                                       w_out[layer].astype(BF16), g_ffn[layer].reshape(1, D),
                                       wr_hi, wr_lo, b_r, n_groups, per_group, tm_merge)

        ids = route[:, 0:TOP_K].astype(jnp.int32)
        rank = route[:, 2 * TOP_K:3 * TOP_K].astype(jnp.int32)
        counts = counts[0, n_groups:n_groups + n_experts].astype(jnp.int32)
        pos, tile_expert, tile_flag, tile_rows, next_expert, slot = _route_tables(ids, rank, counts, tm_exp)
        xs = _dispatch_rows(h2, pos, tile_expert.shape[0] * tm_exp)
        ys = _experts(tile_expert, tile_flag, tile_rows, next_expert, slot, xs, w_exp_gate[layer],
                      w_exp_up[layer], w_exp_down[layer], tm_exp)
        ys_tok = _collect_rows(ys, pos)
        if layer + 1 < depth:
            wts = route[:, TOP_K:2 * TOP_K]
            x2 = x1 + (wts[:, 0:1] * _unpack_pairs(ys_tok[0]).astype(F32)
                       + wts[:, 1:2] * _unpack_pairs(ys_tok[1]).astype(F32))
    out = _final(x1, ys_tok, route, g_final.reshape(1, D), tm_final)
    return out.reshape(B, S, D)
```

```python
import functools
import math

import jax
import jax.numpy as jnp
from jax import lax
from jax.experimental import pallas as pl
from jax.experimental.pallas import tpu as pltpu
from jax.experimental.pallas import tpu_sc as plsc

F32 = jnp.float32
BF16 = jnp.bfloat16
EPS = 1e-6
LANE = 128
GDN_DK = 128
GDN_DV = 128
GDN_CONV = 5
GDN_CHUNK = 64
DIFF_DH = 64
DIFF_DV = 2 * DIFF_DH
TOP_K = 2
LOG2E = 1.4426950408889634
NEG_BIG = -1e30
VMEM_LIMIT_BYTES = 56 * 1024 * 1024

_HI = lax.Precision.HIGHEST


def _cparams(*sem):
    return pltpu.CompilerParams(dimension_semantics=sem, vmem_limit_bytes=VMEM_LIMIT_BYTES)


def _const_spec(shape):
    nd = len(shape)
    return pl.BlockSpec(shape, lambda *_: (0,) * nd, pipeline_mode=pl.Buffered(1))


def _mm(a, b):
    return jnp.dot(a.astype(BF16), b.astype(BF16), preferred_element_type=F32)


def _split3(x):
    hi = x.astype(BF16)
    r = x - hi.astype(F32)
    mid = r.astype(BF16)
    lo = (r - mid.astype(F32)).astype(BF16)
    return hi, mid, lo


def _pack_pairs(x):
    bits = lax.bitcast_convert_type(x.astype(BF16).astype(F32), jnp.uint32)
    half = bits.shape[1] // 2
    return bits[:, :half] | (bits[:, half:] >> 16)


def _unpack_pairs(packed):
    left = lax.bitcast_convert_type(packed & jnp.uint32(0xFFFF0000), F32).astype(BF16)
    right = lax.bitcast_convert_type(packed << 16, F32).astype(BF16)
    return jnp.concatenate([left, right], axis=1)


def _softplus(x):
    return jnp.maximum(x, 0.0) + jnp.log(1.0 + jnp.exp(-jnp.abs(x)))


def _sigmoid(x):
    return 1.0 / (1.0 + jnp.exp(-x))


def _norm_proj_kernel(x_ref, g_ref, wab_ref, alog_ref, dtb_ref, h_ref, gb_ref, *, n_dir_heads):
    x = x_ref[...]
    h = x * lax.rsqrt(jnp.mean(x * x, axis=-1, keepdims=True) + EPS) * g_ref[...]
    hb = h.astype(BF16)
    h_ref[...] = hb
    ab = lax.dot_general(hb, wab_ref[...], (((1,), (1,)), ((), ())), preferred_element_type=F32)
    g = -jnp.exp(alog_ref[...]) * _softplus(ab + dtb_ref[...])
    beta = _sigmoid(ab)
    tm = x.shape[0]
    row = lax.broadcasted_iota(jnp.int32, (tm, tm), 0)
    col = lax.broadcasted_iota(jnp.int32, (tm, tm), 1)
    same = (row // GDN_CHUNK) == (col // GDN_CHUNK)
    prefix = jnp.where(same & (col <= row), 1.0, 0.0).astype(BF16)
    suffix = jnp.where(same & (col >= row), 1.0, 0.0).astype(BF16)
    pieces = _split3(g)
    cs_f = sum(jnp.dot(prefix, p, preferred_element_type=F32) for p in pieces)
    cs_b = sum(jnp.dot(suffix, p, preferred_element_type=F32) for p in pieces)
    lane = lax.broadcasted_iota(jnp.int32, g.shape, 1)
    gb_ref[...] = jnp.where(lane < n_dir_heads // 2, cs_f, jnp.where(lane < n_dir_heads, cs_b, beta))


def _norm_proj(x2, g_mix, w_ab, alog_pad, dtb_pad, n_dir_heads, tm):
    T, D = x2.shape
    return pl.pallas_call(
        functools.partial(_norm_proj_kernel, n_dir_heads=n_dir_heads),
        grid=(T // tm,),
        in_specs=[pl.BlockSpec((tm, D), lambda i: (i, 0)),
                  _const_spec((1, D)), _const_spec((LANE, D)), _const_spec((1, LANE)), _const_spec((1, LANE))],
        out_specs=[pl.BlockSpec((tm, D), lambda i: (i, 0)),
                   pl.BlockSpec((tm, LANE), lambda i: (i, 0))],
        out_shape=[jax.ShapeDtypeStruct((T, D), BF16),
                   jax.ShapeDtypeStruct((T, LANE), F32)],
        compiler_params=_cparams("parallel"),
        name="norm_proj",
    )(x2, g_mix, w_ab, alog_pad, dtb_pad)


REORDER_ROWS = 512


def _reorder_cast_kernel(w_ref, o_ref, narrow_ref, *, segments, narrow):
    n_start, n_width = narrow
    row = lax.broadcasted_iota(jnp.int32, narrow_ref.shape, 0)
    narrow_ref[...] = jnp.where(row < n_width, w_ref[n_start:n_start + LANE, :], 0.0).astype(narrow_ref.dtype)
    off = 0
    for start, width in segments:
        for c in range(0, width, REORDER_ROWS):
            n = min(REORDER_ROWS, width - c)
            o_ref[off:off + n, :] = w_ref[start + c:start + c + n, :].astype(o_ref.dtype)
            off += n


def _reorder_cast(wT, segments, narrow, tc):
    rows, cols = wT.shape
    out_rows = sum(width for _, width in segments)
    assert narrow[0] % 8 == 0 and narrow[1] <= LANE and narrow[0] + LANE <= rows
    assert all(start % 8 == 0 and width % 16 == 0 for start, width in segments)
    return pl.pallas_call(
        functools.partial(_reorder_cast_kernel, segments=segments, narrow=narrow),
        grid=(cols // tc,),
        in_specs=[pl.BlockSpec((rows, tc), lambda i: (0, i))],
        out_specs=[pl.BlockSpec((out_rows, tc), lambda i: (0, i)), pl.BlockSpec((LANE, tc), lambda i: (0, i))],
        out_shape=[jax.ShapeDtypeStruct((out_rows, cols), BF16), jax.ShapeDtypeStruct((LANE, cols), BF16)],
        compiler_params=_cparams("parallel"),
        name="reorder_cast",
    )(wT)


_NT_DIMS = (((1,), (1,)), ((), ()))


def _matmul_nt_kernel(a_ref, bT_ref, o_ref):
    o_ref[...] = lax.dot_general(a_ref[...], bT_ref[...], _NT_DIMS, preferred_element_type=F32).astype(o_ref.dtype)


def _matmul_nt(a, bT, tm, tn, out_dtype):
    M, K = a.shape
    N = bT.shape[0]
    return pl.pallas_call(
        _matmul_nt_kernel,
        grid=(N // tn, M // tm),
        in_specs=[pl.BlockSpec((tm, K), lambda j, i: (i, 0)),
                  pl.BlockSpec((tn, K), lambda j, i: (j, 0))],
        out_specs=pl.BlockSpec((tm, tn), lambda j, i: (i, j)),
        out_shape=jax.ShapeDtypeStruct((M, N), out_dtype),
        compiler_params=_cparams("parallel", "parallel"),
        name="in_proj",
    )(a, bT)


def _bmm_tn(a, b):
    return jnp.stack([lax.dot_general(a[n], b[n], (((0,), (0,)), ((), ())), preferred_element_type=F32)
                      for n in range(a.shape[0])])


def _gdn_kernel(q_ref, k_ref, v_ref, z_ref, cwq_ref, cwk_ref, cwv_ref, gb_ref, grow_ref, nw_ref,
                y_ref,
                xpad, qn, kn, vn, u_s, w_s, qg_s, at_s, tm_s, tn_s, egl_s, st_s,
                *, n_heads, seq, conv_rows, prep_chunks):
    C = GDN_CHUNK
    n_chunks = seq // C
    head = pl.program_id(1)
    pad = 8
    half = GDN_CONV // 2

    zeros_pad = jnp.zeros((pad, LANE), F32)
    xpad[0:pad, :] = zeros_pad
    xpad[pad + seq:pad + seq + pad, :] = zeros_pad
    for src, cw_ref, dst, mode in ((q_ref, cwq_ref, qn, "q"), (k_ref, cwk_ref, kn, "k"), (v_ref, cwv_ref, vn, "v")):
        xpad[pad:pad + seq, :] = src[...].astype(F32)
        for t in range(seq // conv_rows):
            r0 = t * conv_rows
            acc = jnp.zeros((conv_rows, LANE), F32)
            for j in range(GDN_CONV):
                lo = pad + r0 + j - half
                acc = acc + xpad[lo:lo + conv_rows, :] * cw_ref[j:j + 1, :]
            y = acc * _sigmoid(acc)
            if mode != "v":
                y = y * lax.rsqrt(jnp.sum(y * y, axis=-1, keepdims=True) + EPS)
            if mode == "q":
                y = y * (GDN_DK ** -0.5)
            dst[r0:r0 + conv_rows, :] = y

    nb = prep_chunks
    R = nb * C
    nb2 = 2 * nb
    bi = lax.broadcasted_iota(jnp.int32, (nb2, C, C), 0)
    ii = lax.broadcasted_iota(jnp.int32, (nb2, C, C), 1)
    jj = lax.broadcasted_iota(jnp.int32, (nb2, C, C), 2)
    fwd = bi < nb
    incl = (fwd & (ii >= jj)) | (~fwd & (ii <= jj))
    strict = (fwd & (ii > jj)) | (~fwd & (ii < jj))
    eye = jnp.where(ii == jj, 1.0, 0.0).astype(F32)
    lane = lax.broadcasted_iota(jnp.int32, (R, LANE), 1)
    both = lambda x: jnp.concatenate([x, x], axis=0)

    def prep(t, carry):
        r0 = pl.multiple_of(t * R, R)
        n0 = pl.multiple_of(t * nb, nb)
        q3 = qn[pl.ds(r0, R), :].reshape(nb, C, LANE)
        k3 = kn[pl.ds(r0, R), :].reshape(nb, C, LANE)
        v3 = vn[pl.ds(r0, R), :].reshape(nb, C, LANE)
        gb_blk = gb_ref[pl.ds(r0, R), :]
        kq = jnp.concatenate([k3, q3], axis=1).astype(BF16)
        gram = jnp.einsum("nik,njk->nij", kq, k3.astype(BF16), preferred_element_type=F32)
        kk, qk = both(gram[:, :C, :]), both(gram[:, C:, :])
        gcb, bt, grow, gl = [], [], [], []
        for d in range(2):
            colidx = d * n_heads + head
            gcb_d = jnp.sum(jnp.where(lane == colidx, gb_blk, 0.0), axis=-1, keepdims=True).reshape(nb, C, 1)
            gcb.append(gcb_d)
            bt.append(jnp.sum(jnp.where(lane == 2 * n_heads + colidx, gb_blk, 0.0), axis=-1,
                              keepdims=True).reshape(nb, C, 1))
            grow.append(grow_ref[0, colidx, pl.ds(n0, nb), :].reshape(nb, 1, C))
            gl.append(gcb_d[:, C - 1:C, :] if d == 0 else gcb_d[:, 0:1, :])
        gcb, bt, grow, gl = (jnp.concatenate(x, axis=0) for x in (gcb, bt, grow, gl))
        k2, q2, v2 = both(k3), both(q3), both(v3)

        decay = jnp.exp(jnp.where(incl, gcb - grow, -jnp.inf))
        L = jnp.where(strict, bt * kk * decay, 0.0)
        attn = qk * decay
        ainv = eye - L
        P = L
        for _ in range(int(math.log2(C)) - 1):
            Pb = P.astype(BF16)
            P = jnp.einsum("nij,njk->nik", Pb, Pb, preferred_element_type=F32)
            ainv = ainv + jnp.einsum("nij,njk->nik", ainv.astype(BF16), P.astype(BF16),
                                     preferred_element_type=F32)
        eg = jnp.exp(gcb)
        rhs = jnp.concatenate([v2 * bt, k2 * (bt * eg)], axis=-1).astype(BF16)
        uw = jnp.einsum("nij,njd->nid", ainv.astype(BF16), rhs, preferred_element_type=F32).astype(BF16)
        kg = (k2 * jnp.exp(gl - gcb)).astype(BF16)
        trans = _bmm_tn(kg, uw)
        qg = (q2 * eg).astype(BF16)
        egl = jnp.broadcast_to(jnp.exp(gl).reshape(nb2, 1), (nb2, LANE))
        for d in range(2):
            sl = slice(d * nb, (d + 1) * nb)
            u_s[d, pl.ds(r0, R), :] = uw[sl, :, :LANE].reshape(R, LANE)
            w_s[d, pl.ds(r0, R), :] = uw[sl, :, LANE:].reshape(R, LANE)
            qg_s[d, pl.ds(r0, R), :] = qg[sl].reshape(R, LANE)
            at_s[d, pl.ds(r0, R), :] = attn[sl].reshape(R, C).astype(BF16)
            tn_s[d, pl.ds(n0 * GDN_DK, nb * GDN_DK), :] = trans[sl, :, :LANE].reshape(nb * GDN_DK, LANE).astype(BF16)
            tm_s[d, pl.ds(n0 * GDN_DK, nb * GDN_DK), :] = (-trans[sl, :, LANE:]).reshape(nb * GDN_DK, LANE).astype(BF16)
            egl_s[d, pl.ds(n0, nb), :] = egl[sl]
        return carry

    lax.fori_loop(0, n_chunks // nb, prep, 0)

    def chunk_step(d, n, state):
        r = pl.multiple_of(n * GDN_DK, GDN_DK)
        sb = state.astype(BF16)
        st_s[d, pl.ds(r, GDN_DK), :] = sb
        return (state * egl_s[d, pl.ds(n, 1), :]
                + jnp.dot(tm_s[d, pl.ds(r, GDN_DK), :], sb, preferred_element_type=F32)
                + tn_s[d, pl.ds(r, GDN_DK), :].astype(F32))

    def scan(n, carry):
        sf, sbw = carry
        return chunk_step(0, n, sf), chunk_step(1, n_chunks - 1 - n, sbw)

    zero_state = jnp.zeros((GDN_DK, GDN_DV), F32)
    lax.fori_loop(0, n_chunks, scan, (zero_state, zero_state))

    def emit(t, carry):
        r0 = pl.multiple_of(t * R, R)
        s0 = pl.multiple_of(t * nb * GDN_DK, nb * GDN_DK)
        two = lambda ref, start, rows: jnp.concatenate([ref[0, pl.ds(start, rows), :], ref[1, pl.ds(start, rows), :]])
        st = two(st_s, s0, nb * GDN_DK).reshape(nb2, GDN_DK, GDN_DV)
        wq = jnp.concatenate([two(w_s, r0, R).reshape(nb2, C, LANE), two(qg_s, r0, R).reshape(nb2, C, LANE)], axis=1)
        ws_qs = jnp.einsum("ncd,nde->nce", wq, st, preferred_element_type=F32)
        vnew = two(u_s, r0, R).reshape(nb2, C, LANE).astype(F32) - ws_qs[:, :C, :]
        o = ws_qs[:, C:, :] + jnp.einsum("ncs,nse->nce", two(at_s, r0, R).reshape(nb2, C, C), vnew.astype(BF16),
                                         preferred_element_type=F32)
        o = (o[:nb] + o[nb:]).reshape(R, LANE)
        on = o * lax.rsqrt(jnp.mean(o * o, axis=-1, keepdims=True) + EPS) * nw_ref[...]
        z = z_ref[pl.ds(r0, R), :].astype(F32)
        y_ref[pl.ds(r0, R), :] = (on * (z * _sigmoid(z))).astype(y_ref.dtype)
        return carry

    lax.fori_loop(0, n_chunks // nb, emit, 0)


def _gdn(p, p_off, conv_w8, gb, grow, norm_w, batch, seq, n_heads):
    T = batch * seq
    qk_blocks = n_heads
    n_chunks = seq // GDN_CHUNK
    conv_rows = min(512, seq)
    prep_chunks = min(16, n_chunks)
    blk = lambda off: pl.BlockSpec((seq, LANE), lambda b, h: (b, p_off + off + h))
    cw = lambda off: pl.BlockSpec((8, LANE), lambda b, h: (0, off + h))
    return pl.pallas_call(
        functools.partial(_gdn_kernel, n_heads=n_heads, seq=seq, conv_rows=conv_rows, prep_chunks=prep_chunks),
        grid=(batch, n_heads),
        in_specs=[blk(0), blk(qk_blocks), blk(2 * qk_blocks), blk(3 * qk_blocks),
                  cw(0), cw(qk_blocks), cw(2 * qk_blocks),
                  pl.BlockSpec((seq, LANE), lambda b, h: (b, 0), pipeline_mode=pl.Buffered(1)),
                  pl.BlockSpec((1, 2 * n_heads, n_chunks, GDN_CHUNK), lambda b, h: (b, 0, 0, 0),
                               pipeline_mode=pl.Buffered(1)),
                  _const_spec((1, LANE))],
        out_specs=pl.BlockSpec((seq, LANE), lambda b, h: (b, h)),
        out_shape=jax.ShapeDtypeStruct((T, n_heads * GDN_DV), BF16),
        scratch_shapes=[
            pltpu.VMEM((seq + 16, LANE), F32),
            pltpu.VMEM((seq, LANE), F32),
            pltpu.VMEM((seq, LANE), F32),
            pltpu.VMEM((seq, LANE), F32),
            pltpu.VMEM((2, seq, LANE), BF16),
            pltpu.VMEM((2, seq, LANE), BF16),
            pltpu.VMEM((2, seq, LANE), BF16),
            pltpu.VMEM((2, seq, GDN_CHUNK), BF16),
            pltpu.VMEM((2, n_chunks * GDN_DK, GDN_DV), BF16),
            pltpu.VMEM((2, n_chunks * GDN_DK, GDN_DV), BF16),
            pltpu.VMEM((2, max(n_chunks, 8), LANE), F32),
            pltpu.VMEM((2, n_chunks * GDN_DK, GDN_DV), BF16),
        ],
        compiler_params=_cparams("parallel", "parallel"),
        name="gdn",
    )(p, p, p, p, conv_w8, conv_w8, conv_w8, gb, grow, norm_w)


def _diff_attn_kernel(slopes_ref, q_ref, k_ref, v_ref, lam_ref, nw_ref, o_ref,
                      qT_s, vT_s, klo_s, khi_s, rel_s, s_s, p_s, acc_s,
                      *, seq, tq, tk, lam_init):
    head = pl.program_id(1)
    slope2 = slopes_ref[head] * LOG2E
    scale2 = (DIFF_DH ** -0.5) * LOG2E
    n_kv = seq // tk
    maps = (klo_s, khi_s)

    rows = min(512, seq)
    for t in range(seq // rows):
        r0 = t * rows
        qt = (q_ref[r0:r0 + rows, :].astype(F32) * scale2).T.astype(BF16)
        frow = lax.broadcasted_iota(jnp.int32, (LANE, rows), 0)
        fcol = lax.broadcasted_iota(jnp.int32, (LANE, rows), 1)
        qa, qb, qc = (x.astype(F32) for x in _split3(((r0 + fcol) % tq).astype(F32) * slope2))
        qfeat = jnp.where(frow < 3, 1.0, jnp.where(frow == 3, -qa, jnp.where(frow == 4, -qb,
                                                   jnp.where(frow == 5, -qc, 0.0)))).astype(BF16)
        for sgn, feat in enumerate((qfeat, -qfeat)):
            qT_s[sgn, 0:LANE, r0:r0 + rows] = qt
            qT_s[sgn, LANE:2 * LANE, r0:r0 + rows] = feat
        vT_s[0:LANE, r0:r0 + rows] = v_ref[r0:r0 + rows, :].astype(F32).T.astype(BF16)
        vT_s[LANE:LANE + 16, r0:r0 + rows] = jnp.ones((16, rows), BF16)
        kt = k_ref[r0:r0 + rows, :]
        krow = lax.broadcasted_iota(jnp.int32, (rows, LANE), 0)
        lane = lax.broadcasted_iota(jnp.int32, (rows, LANE), 1)
        ka, kb, kc = (x.astype(F32) for x in _split3(((r0 + krow) % tk).astype(F32) * slope2))
        kfeat = jnp.where(lane == 0, ka, jnp.where(lane == 1, kb, jnp.where(lane == 2, kc,
                                                   jnp.where(lane < 6, 1.0, 0.0)))).astype(BF16)
        klo_s[r0:r0 + rows, 0:LANE] = jnp.where(lane < DIFF_DH, kt, jnp.zeros_like(kt))
        khi_s[r0:r0 + rows, 0:LANE] = jnp.where(lane >= DIFF_DH, kt, jnp.zeros_like(kt))
        klo_s[r0:r0 + rows, LANE:2 * LANE] = kfeat
        khi_s[r0:r0 + rows, LANE:2 * LANE] = kfeat
    kr = lax.broadcasted_iota(jnp.int32, (tk, tq), 0)
    qcol = lax.broadcasted_iota(jnp.int32, (tk, tq), 1)
    rel_s[...] = (qcol - kr).astype(F32) * slope2

    lf = lam_ref[...]
    lam = (jnp.exp(jnp.sum(lf[0:1, :] * lf[1:2, :], axis=-1, keepdims=True))
           - jnp.exp(jnp.sum(lf[2:3, :] * lf[3:4, :], axis=-1, keepdims=True)) + lam_init)

    def q_tile(i, carry):
        c_q = pl.multiple_of(i * tq, tq)
        j_diag = (i * tq) // tk

        def scores(t):
            j = j_diag if t == 0 else lax.rem(j_diag + t, n_kv)
            r_k = pl.multiple_of(j * tk, tk)
            d0 = jnp.asarray(i * tq - j * tk)
            if t == 0:
                bias = jnp.abs(rel_s[...] + d0.astype(F32) * slope2)
                for mp, k_s in enumerate(maps):
                    s_s[t % 2, mp] = jnp.dot(k_s[pl.ds(r_k, tk), 0:LANE], qT_s[0, 0:LANE, pl.ds(c_q, tq)],
                                             preferred_element_type=F32) - bias
                return r_k, 0.0
            sgn = jnp.asarray(j > j_diag).astype(jnp.int32)
            for mp, k_s in enumerate(maps):
                s_s[t % 2, mp] = jnp.dot(k_s[pl.ds(r_k, tk), :], qT_s[sgn, :, pl.ds(c_q, tq)],
                                         preferred_element_type=F32)
            return r_k, jnp.abs(d0).astype(F32) * slope2

        def weighted_values(t, r_k, alphas):
            vT = vT_s[:, pl.ds(r_k, tk)]
            for mp in range(len(maps)):
                upd = jnp.dot(vT, p_s[t % 2, mp], preferred_element_type=F32)
                acc_s[mp] = upd if t == 0 else acc_s[mp] * alphas[mp] + upd

        m = [jnp.full((1, tq), NEG_BIG, F32) for _ in maps]
        nxt = scores(0)
        prev = None
        for t in range(n_kv):
            r_k, const = nxt
            if t + 1 < n_kv:
                nxt = scores(t + 1)
            alphas = []
            for mp in range(len(maps)):
                m_new = jnp.maximum(m[mp], jnp.max(s_s[t % 2, mp], axis=0, keepdims=True) - const)
                alphas.append(jnp.exp2(m[mp] - m_new))
                p_s[t % 2, mp] = jnp.exp2(s_s[t % 2, mp] - (m_new + const)).astype(BF16)
                m[mp] = m_new
            if prev is not None:
                weighted_values(t - 1, *prev)
            prev = (r_k, alphas)
        weighted_values(n_kv - 1, *prev)

        o1, o2 = acc_s[0], acc_s[1]
        od = o1[0:DIFF_DV] / o1[DIFF_DV:DIFF_DV + 1] - lam * (o2[0:DIFF_DV] / o2[DIFF_DV:DIFF_DV + 1])
        yn = od * lax.rsqrt(jnp.mean(od * od, axis=0, keepdims=True) + EPS) * nw_ref[...] * (1.0 - lam_init)
        o_ref[pl.ds(c_q, tq), :] = yn.T.astype(o_ref.dtype)
        return carry

    lax.fori_loop(0, seq // tq, q_tile, 0)


def _diff_attn(p, slopes, diff_lambda, norm_w_col, batch, seq, n_heads, q_off, k_off, v_off, lam_init):
    T = batch * seq
    tq = min(512, seq)
    tk = min(512, seq)
    blk = lambda off: pl.BlockSpec((seq, LANE), lambda b, h: (b, off + h))
    return pl.pallas_call(
        functools.partial(_diff_attn_kernel, seq=seq, tq=tq, tk=tk, lam_init=lam_init),
        grid=(batch, n_heads),
        in_specs=[pl.BlockSpec(memory_space=pltpu.SMEM),
                  blk(q_off), blk(k_off), blk(v_off),
                  _const_spec((4, DIFF_DH)), _const_spec((DIFF_DV, 1))],
        out_specs=pl.BlockSpec((seq, LANE), lambda b, h: (b, h)),
        out_shape=jax.ShapeDtypeStruct((T, n_heads * DIFF_DV), BF16),
        scratch_shapes=[
            pltpu.VMEM((2, 2 * LANE, seq), BF16),
            pltpu.VMEM((DIFF_DV + 16, seq), BF16),
            pltpu.VMEM((seq, 2 * LANE), BF16),
            pltpu.VMEM((seq, 2 * LANE), BF16),
            pltpu.VMEM((tk, tq), F32),
            pltpu.VMEM((2, 2, tk, tq), F32),
            pltpu.VMEM((2, 2, tk, tq), BF16),
            pltpu.VMEM((2, DIFF_DV + 16, tq), F32),
        ],
        compiler_params=_cparams("parallel", "parallel"),
        name="diff_attn",
    )(slopes, p, p, p, diff_lambda, norm_w_col)


def _merge_kernel(ya_ref, yd_ref, ga_ref, gd_ref, x_ref, wa_ref, wd_ref, wo_ref, gffn_ref,
                  wr_hi_ref, wr_lo_ref, br_ref, x1_ref, h2_ref, route_ref, count_ref, *, n_groups, per_group):
    ma = jnp.dot(ya_ref[...], wa_ref[...], preferred_element_type=F32)
    md = jnp.dot(yd_ref[...], wd_ref[...], preferred_element_type=F32)
    merged = _sigmoid(ga_ref[...].astype(F32)) * ma + _sigmoid(gd_ref[...].astype(F32)) * md
    x1 = x_ref[...] + jnp.dot(merged.astype(BF16), wo_ref[...], preferred_element_type=F32)
    x1_ref[...] = x1
    h2 = x1 * lax.rsqrt(jnp.mean(x1 * x1, axis=-1, keepdims=True) + EPS) * gffn_ref[...]
    h2_ref[...] = _pack_pairs(h2)

    h_hi = h2.astype(BF16)
    h_lo = (h2 - h_hi.astype(F32)).astype(BF16)
    logits = (jnp.dot(h_hi, wr_hi_ref[...], preferred_element_type=F32)
              + jnp.dot(h_hi, wr_lo_ref[...], preferred_element_type=F32)
              + jnp.dot(h_lo, wr_hi_ref[...], preferred_element_type=F32)) + br_ref[...]
    lane = lax.broadcasted_iota(jnp.int32, logits.shape, 1)
    lane_f = lane.astype(F32)
    big = float(LANE)

    def first_argmax(vals, vmax):
        return jnp.min(jnp.where(vals == vmax, lane_f, big), axis=-1, keepdims=True)

    gl = jnp.where(lane < n_groups, logits, -jnp.inf)
    gmax = jnp.max(gl, axis=-1, keepdims=True)
    g_idx = first_argmax(gl, gmax)
    g_w = 1.0 / jnp.sum(jnp.exp(gl - gmax), axis=-1, keepdims=True)
    e_lo = n_groups + g_idx * per_group
    in_group = (lane_f >= e_lo) & (lane_f < e_lo + per_group)
    el = jnp.where(in_group, logits, -jnp.inf)
    emax = jnp.max(el, axis=-1, keepdims=True)
    pe = jnp.exp(el - emax)
    pe = pe / jnp.sum(pe, axis=-1, keepdims=True)
    p1 = jnp.max(pe, axis=-1, keepdims=True)
    i1 = first_argmax(jnp.where(in_group, pe, -1.0), p1)
    rest = jnp.where(in_group & (lane_f != i1), pe, -1.0)
    p2 = jnp.max(rest, axis=-1, keepdims=True)
    i2 = first_argmax(rest, p2)
    denom = p1 + p2
    w1 = p1 / denom * g_w
    w2 = p2 / denom * g_w
    @pl.when(pl.program_id(0) == 0)
    def _():
        count_ref[...] = jnp.zeros_like(count_ref)

    hit1 = lane_f == i1
    hit2 = lane_f == i2
    hits = jnp.where(hit1 | hit2, 1.0, 0.0)
    tm = hits.shape[0]
    earlier = (lax.broadcasted_iota(jnp.int32, (tm, tm), 1) < lax.broadcasted_iota(jnp.int32, (tm, tm), 0))
    before = count_ref[0:1, :] + jnp.dot(jnp.where(earlier, 1.0, 0.0).astype(BF16), hits.astype(BF16),
                                         preferred_element_type=F32)
    rank1 = jnp.sum(jnp.where(hit1, before, 0.0), axis=-1, keepdims=True)
    rank2 = jnp.sum(jnp.where(hit2, before, 0.0), axis=-1, keepdims=True)
    count_ref[...] = count_ref[...] + jnp.sum(hits, axis=0, keepdims=True)

    route = jnp.where(lane == 0, i1 - n_groups, 0.0)
    route = jnp.where(lane == 1, i2 - n_groups, route)
    route = jnp.where(lane == 2, w1, route)
    route = jnp.where(lane == 3, w2, route)
    route = jnp.where(lane == 4, rank1, route)
    route = jnp.where(lane == 5, rank2, route)
    route_ref[...] = route


def _merge(y_a, y_d, p, x2, wa, wd, wo, g_ffn, wr_hi, wr_lo, b_r, n_groups, per_group, tm):
    T, D = x2.shape
    va, vd = y_a.shape[1], y_d.shape[1]
    row = lambda w: pl.BlockSpec((tm, w), lambda i: (i, 0))
    return pl.pallas_call(
        functools.partial(_merge_kernel, n_groups=n_groups, per_group=per_group),
        grid=(T // tm,),
        in_specs=[row(va), row(vd),
                  pl.BlockSpec((tm, D), lambda i: (i, 0)),
                  pl.BlockSpec((tm, D), lambda i: (i, 1)),
                  row(D),
                  _const_spec((va, D)), _const_spec((vd, D)), _const_spec((D, D)), _const_spec((1, D)),
                  _const_spec((D, LANE)), _const_spec((D, LANE)), _const_spec((1, LANE))],
        out_specs=[row(D), row(D // 2), row(LANE), pl.BlockSpec((8, LANE), lambda i: (0, 0))],
        out_shape=[jax.ShapeDtypeStruct((T, D), F32),
                   jax.ShapeDtypeStruct((T, D // 2), jnp.uint32),
                   jax.ShapeDtypeStruct((T, LANE), F32),
                   jax.ShapeDtypeStruct((8, LANE), F32)],
        compiler_params=_cparams("arbitrary"),
        name="merge",
    )(y_a, y_d, p, p, x2, wa, wd, wo, g_ffn, wr_hi, wr_lo, b_r)


def _expert_kernel(tile_expert_ref, tile_flag_ref, tile_rows_ref, next_expert_ref, slot_ref,
                   xs_ref, wg_hbm, wu_hbm, wd_hbm, o_ref,
                   wg_f, wu_f, wd_f, wg_s, wu_s, wd_s, sems):
    i = pl.program_id(0)
    flag = tile_flag_ref[i]

    def weight_copies(expert, slot):
        return [pltpu.make_async_copy(hbm.at[expert], buf.at[slot], sems.at[slot, n])
                for n, (hbm, buf) in enumerate(((wg_hbm, wg_f), (wu_hbm, wu_f), (wd_hbm, wd_f)))]

    @pl.when(flag == 2)
    def _():
        slot = slot_ref[i]

        @pl.when(i == 0)
        def _():
            for copy in weight_copies(tile_expert_ref[i], slot):
                copy.start()

        for copy in weight_copies(tile_expert_ref[i], slot):
            copy.wait()
        nxt = next_expert_ref[i]

        @pl.when(nxt >= 0)
        def _():
            for copy in weight_copies(nxt, 1 - slot):
                copy.start()

        wg_s[...] = wg_f[slot].astype(BF16)
        wu_s[...] = wu_f[slot].astype(BF16)
        wd_s[...] = wd_f[slot].astype(BF16)

    @pl.when(flag > 0)
    def _():
        row = lax.broadcasted_iota(jnp.int32, xs_ref.shape, 0)
        x = _unpack_pairs(jnp.where(row < tile_rows_ref[i], xs_ref[...], jnp.uint32(0)))
        g = jnp.dot(x, wg_s[...], preferred_element_type=F32)
        u = jnp.dot(x, wu_s[...], preferred_element_type=F32)
        hid = g * _sigmoid(g) * u
        o_ref[...] = _pack_pairs(jnp.dot(hid.astype(BF16), wd_s[...], preferred_element_type=F32))

    @pl.when(flag == 0)
    def _():
        o_ref[...] = jnp.zeros_like(o_ref)


def _experts(tile_expert, tile_flag, tile_rows, next_expert, slot, xs, w_gate, w_up, w_down, tm):
    n_rows, half = xs.shape
    E, D, Fd = w_gate.shape
    assert D == 2 * half
    n_tiles = n_rows // tm
    hbm = pl.BlockSpec(memory_space=pl.ANY)
    grid_spec = pltpu.PrefetchScalarGridSpec(
        num_scalar_prefetch=5,
        grid=(n_tiles,),
        in_specs=[pl.BlockSpec((tm, half), lambda i, *_: (i, 0)), hbm, hbm, hbm],
        out_specs=pl.BlockSpec((tm, half), lambda i, *_: (i, 0)),
        scratch_shapes=[pltpu.VMEM((2, D, Fd), F32), pltpu.VMEM((2, D, Fd), F32), pltpu.VMEM((2, Fd, D), F32),
                        pltpu.VMEM((D, Fd), BF16), pltpu.VMEM((D, Fd), BF16), pltpu.VMEM((Fd, D), BF16),
                        pltpu.SemaphoreType.DMA((2, 3))],
    )
    return pl.pallas_call(
        _expert_kernel,
        grid_spec=grid_spec,
        out_shape=jax.ShapeDtypeStruct((n_rows, half), jnp.uint32),
        compiler_params=_cparams("arbitrary"),
        name="experts",
    )(tile_expert, tile_flag, tile_rows, next_expert, slot, xs, w_gate, w_up, w_down)


SC_CORES = 2
SC_SUBCORES = 16
SC_ROWS = 32
SC_COLLECT_BUFFERS = 3


def _dispatch_rows(table, pos, n_rows):
    T, W = table.shape
    workers = SC_CORES * SC_SUBCORES
    chunks = T // (workers * SC_ROWS)
    assert workers * chunks * SC_ROWS == T
    pos4 = pos.T.reshape(TOP_K, workers, chunks, SC_ROWS).transpose(1, 0, 2, 3)
    mesh = plsc.VectorSubcoreMesh(core_axis_name="c", subcore_axis_name="s")

    @functools.partial(
        pl.kernel, mesh=mesh,
        out_type=jax.ShapeDtypeStruct((n_rows, W), table.dtype),
        scratch_types=[pltpu.VMEM((TOP_K, chunks, SC_ROWS), jnp.int32),
                       pltpu.VMEM((2, SC_ROWS, W), table.dtype),
                       pltpu.SemaphoreType.DMA, pltpu.SemaphoreType.DMA, pltpu.SemaphoreType.DMA],
        name="dispatch_rows",
    )
    def dispatch(table_hbm, pos_hbm, out_hbm, pos_v, rows_v, sem_in, sem_out0, sem_out1):
        wid = lax.axis_index("s") * SC_CORES + lax.axis_index("c")
        base = wid * (chunks * SC_ROWS)
        pltpu.sync_copy(pos_hbm.at[wid], pos_v)

        def load(c):
            return pltpu.async_copy(table_hbm.at[pl.ds(base + c * SC_ROWS, SC_ROWS)], rows_v.at[c % 2], sem_in)

        pending = load(0)
        for c in range(chunks):
            pending.wait()
            if c + 1 < chunks:
                pending = load(c + 1)
            out0 = pltpu.async_copy(rows_v.at[c % 2], out_hbm.at[pos_v.at[0, c]], sem_out0)
            out1 = pltpu.async_copy(rows_v.at[c % 2], out_hbm.at[pos_v.at[1, c]], sem_out1)
            out0.wait()
            out1.wait()

    return dispatch(table, pos4)


def _collect_rows(table, pos):
    T = pos.shape[0]
    W = table.shape[1]
    workers = SC_CORES * SC_SUBCORES
    chunks = T // (workers * SC_ROWS)
    assert workers * chunks * SC_ROWS == T
    pos4 = pos.T.reshape(TOP_K, workers, chunks, SC_ROWS).transpose(1, 0, 2, 3)
    mesh = plsc.VectorSubcoreMesh(core_axis_name="c", subcore_axis_name="s")
    steps = [(c, k) for c in range(chunks) for k in range(TOP_K)]

    @functools.partial(
        pl.kernel, mesh=mesh,
        out_type=jax.ShapeDtypeStruct((TOP_K, T, W), table.dtype),
        scratch_types=[pltpu.VMEM((TOP_K, chunks, SC_ROWS), jnp.int32),
                       pltpu.VMEM((SC_COLLECT_BUFFERS, SC_ROWS, W), table.dtype)]
                      + [pltpu.SemaphoreType.DMA] * SC_COLLECT_BUFFERS,
        name="collect_rows",
    )
    def collect(table_hbm, pos_hbm, out_hbm, pos_v, rows_v, *sems):
        wid = lax.axis_index("s") * SC_CORES + lax.axis_index("c")
        base = wid * (chunks * SC_ROWS)
        pltpu.sync_copy(pos_hbm.at[wid], pos_v)

        def gather(n):
            c, k = steps[n]
            buf = n % SC_COLLECT_BUFFERS
            return pltpu.async_copy(table_hbm.at[pos_v.at[k, c]], rows_v.at[buf], sems[buf])

        in_flight = [gather(n) for n in range(min(SC_COLLECT_BUFFERS - 1, len(steps)))]
        for n, (c, k) in enumerate(steps):
            in_flight.pop(0).wait()
            ahead = n + SC_COLLECT_BUFFERS - 1
            if ahead < len(steps):
                in_flight.append(gather(ahead))
            pltpu.sync_copy(rows_v.at[n % SC_COLLECT_BUFFERS],
                            out_hbm.at[k, pl.ds(base + c * SC_ROWS, SC_ROWS)])

    return collect(table, pos4)


def _final_kernel(x1_ref, y0_ref, y1_ref, route_ref, g_ref, o_ref):
    route = route_ref[...]
    lane = lax.broadcasted_iota(jnp.int32, route.shape, 1)
    w0 = jnp.sum(jnp.where(lane == TOP_K, route, 0.0), axis=-1, keepdims=True)
    w1 = jnp.sum(jnp.where(lane == TOP_K + 1, route, 0.0), axis=-1, keepdims=True)
    y0 = _unpack_pairs(y0_ref[...]).astype(F32)
    y1 = _unpack_pairs(y1_ref[...]).astype(F32)
    x = x1_ref[...] + (w0 * y0 + w1 * y1)
    o_ref[...] = x * lax.rsqrt(jnp.mean(x * x, axis=-1, keepdims=True) + EPS) * g_ref[...]


def _final(x1, ys_tok, route, g_final, tm):
    T, D = x1.shape
    row = pl.BlockSpec((tm, D), lambda i: (i, 0))
    y_spec = lambda k: pl.BlockSpec((None, tm, D // 2), lambda i: (k, i, 0))
    return pl.pallas_call(
        _final_kernel,
        grid=(T // tm,),
        in_specs=[row, y_spec(0), y_spec(1), pl.BlockSpec((tm, LANE), lambda i: (i, 0)), _const_spec((1, D))],
        out_specs=row,
        out_shape=jax.ShapeDtypeStruct((T, D), F32),
        compiler_params=_cparams("parallel"),
        name="final",
    )(x1, ys_tok, ys_tok, route, g_final)


def _route_tables(ids, rank, counts, tm):
    T = ids.shape[0]
    A = T * TOP_K
    n_experts = counts.shape[0]
    padded = ((counts + tm - 1) // tm) * tm
    ends = jnp.cumsum(padded)
    starts = ends - padded
    onehot = ids[:, :, None] == jnp.arange(n_experts, dtype=jnp.int32)[None, None, :]
    pos = jnp.sum(jnp.where(onehot, starts[None, None, :], 0), axis=-1) + rank
    n_tiles = A // tm + n_experts
    tile_start = jnp.arange(n_tiles, dtype=jnp.int32) * tm
    tile_expert_raw = jnp.sum((ends[None, :] <= tile_start[:, None]).astype(jnp.int32), axis=1)
    valid = tile_start < ends[-1]
    last_expert = jnp.max(jnp.where(counts > 0, jnp.arange(n_experts, dtype=jnp.int32), 0))
    tile_expert = jnp.where(valid, jnp.minimum(tile_expert_raw, n_experts - 1), last_expert)
    first = jnp.concatenate([jnp.ones((1,), bool), tile_expert[1:] != tile_expert[:-1]])
    tile_flag = jnp.where(valid, jnp.where(first, 2, 1), 0).astype(jnp.int32)
    tile_rows = jnp.clip((starts + counts)[tile_expert] - tile_start, 0, tm)
    tile_rows = jnp.where(valid, tile_rows, 0).astype(jnp.int32)
    e_idx = jnp.arange(n_experts, dtype=jnp.int32)
    nonempty = counts > 0
    later = (e_idx[None, :] > e_idx[:, None]) & nonempty[None, :]
    next_nonempty = jnp.min(jnp.where(later, e_idx[None, :], n_experts), axis=1)
    next_nonempty = jnp.where(next_nonempty < n_experts, next_nonempty, -1).astype(jnp.int32)
    order = jnp.sum(((e_idx[None, :] < e_idx[:, None]) & nonempty[None, :]).astype(jnp.int32), axis=1)
    next_expert = next_nonempty[tile_expert]
    slot = (order[tile_expert] % 2).astype(jnp.int32)
    return pos, tile_expert, tile_flag, tile_rows, next_expert, slot


def _largest_tile(n, cap):
    t = min(n, cap)
    while n % t:
        t //= 2
    return t


def kernel(x, g_mix, w_in, conv_w, a_log, dt_bias, gdn_norm_w, diff_lambda, diff_norm_w, w_branch_a, w_branch_d,
           w_out, g_ffn, w_group, b_group, w_router, b_router, w_exp_gate, w_exp_up, w_exp_down, g_final):
    B, S, D = x.shape
    T = B * S
    depth = g_mix.shape[0]
    Hg = a_log.shape[-1]
    gdn_qk = Hg * GDN_DK
    gdn_v = Hg * GDN_DV
    diff_v = w_branch_d.shape[1]
    Hd = diff_v // DIFF_DV
    diff_qk = Hd * 2 * DIFF_DH
    n_groups = w_group.shape[-1]
    n_experts = w_router.shape[-1]
    per_group = n_experts // n_groups
    assert 4 * Hg <= LANE and n_groups + n_experts <= LANE
    assert S % GDN_CHUNK == 0 and D % LANE == 0

    ab_lo = 2 * gdn_qk + 2 * gdn_v
    ab_hi = ab_lo + 4 * Hg
    gates_lo = ab_hi + 2 * diff_qk + diff_v
    main_cols = 2 * D + ab_lo + 2 * diff_qk + diff_v
    gdn_off = 2 * D
    q_d_off = gdn_off + ab_lo
    k_d_off = q_d_off + diff_qk
    v_d_off = k_d_off + diff_qk

    tm_norm = _largest_tile(T, 512)
    tm_proj = _largest_tile(T, 1024)
    tn_proj = _largest_tile(main_cols, 1024)
    tm_merge = _largest_tile(T, 256)
    tm_exp = _largest_tile(T * TOP_K, 256)
    tm_final = _largest_tile(T, 512)

    slopes = jnp.exp2(-8.0 * (jnp.arange(Hd, dtype=F32) + 1.0) / Hd)
    x2 = x.reshape(T, D)
    for layer in range(depth):
        lam_init = 0.8 - 0.6 * math.exp(-0.3 * layer)
        w_in_l = w_in[layer]
        w_main, w_ab = _reorder_cast(w_in_l.T, ((gates_lo, 2 * D), (0, ab_lo), (ab_hi, gates_lo - ab_hi)),
                                     (ab_lo, 4 * Hg), _largest_tile(D, 256))
        alog_pad = jnp.pad(a_log[layer].reshape(1, 2 * Hg), ((0, 0), (0, LANE - 2 * Hg)))
        dtb_pad = jnp.pad(dt_bias[layer].reshape(1, 2 * Hg), ((0, 0), (0, LANE - 2 * Hg)))

        h, gb = _norm_proj(x2, g_mix[layer].reshape(1, D), w_ab, alog_pad, dtb_pad, 2 * Hg, tm_norm)
        p = _matmul_nt(h, w_main, tm_proj, tn_proj, BF16)

        grow = gb[:, :2 * Hg].reshape(B, S // GDN_CHUNK, GDN_CHUNK, 2 * Hg).transpose(0, 3, 1, 2)
        conv_w8 = jnp.pad(conv_w[layer], ((0, 8 - GDN_CONV), (0, 0)))
        y_a = _gdn(p, gdn_off // LANE, conv_w8, gb, grow, gdn_norm_w[layer].reshape(1, GDN_DV), B, S, Hg)
        y_d = _diff_attn(p, slopes, diff_lambda[layer], diff_norm_w[layer].reshape(DIFF_DV, 1), B, S, Hd,
                         q_d_off // LANE, k_d_off // LANE, v_d_off // LANE, lam_init)

        w_r = jnp.pad(jnp.concatenate([w_group[layer], w_router[layer]], axis=1),
                      ((0, 0), (0, LANE - n_groups - n_experts)))
        wr_hi = w_r.astype(BF16)
        wr_lo = (w_r - wr_hi.astype(F32)).astype(BF16)
        b_r = jnp.pad(jnp.concatenate([b_group[layer], b_router[layer]]).reshape(1, -1),
                      ((0, 0), (0, LANE - n_groups - n_experts)))
        x1, h2, route, counts = _merge(y_a, y_d, p, x2,
                                       w_branch_a[layer].astype(BF16), w_branch_d[layer].astype(BF16),
                                       w_out[layer].astype(BF16), g_ffn[layer].reshape(1, D),
                                       wr_hi, wr_lo, b_r, n_groups, per_group, tm_merge)

        ids = route[:, 0:TOP_K].astype(jnp.int32)
        rank = route[:, 2 * TOP_K:3 * TOP_K].astype(jnp.int32)
        counts = counts[0, n_groups:n_groups + n_experts].astype(jnp.int32)
        pos, tile_expert, tile_flag, tile_rows, next_expert, slot = _route_tables(ids, rank, counts, tm_exp)
        xs = _dispatch_rows(h2, pos, tile_expert.shape[0] * tm_exp)
        ys = _experts(tile_expert, tile_flag, tile_rows, next_expert, slot, xs, w_exp_gate[layer],
                      w_exp_up[layer], w_exp_down[layer], tm_exp)
        ys_tok = _collect_rows(ys, pos)
        if layer + 1 < depth:
            wts = route[:, TOP_K:2 * TOP_K]
            x2 = x1 + (wts[:, 0:1] * _unpack_pairs(ys_tok[0]).astype(F32)
                       + wts[:, 1:2] * _unpack_pairs(ys_tok[1]).astype(F32))
    out = _final(x1, ys_tok, route, g_final.reshape(1, D), tm_final)
    return out.reshape(B, S, D)
```

```python
import functools
import math

import jax
import jax.numpy as jnp
from jax import lax
from jax.experimental import pallas as pl
from jax.experimental.pallas import tpu as pltpu
from jax.experimental.pallas import tpu_sc as plsc

F32 = jnp.float32
BF16 = jnp.bfloat16
EPS = 1e-6
LANE = 128
GDN_DK = 128
GDN_DV = 128
GDN_CONV = 5
GDN_CHUNK = 64
DIFF_DH = 64
DIFF_DV = 2 * DIFF_DH
TOP_K = 2
LOG2E = 1.4426950408889634
NEG_BIG = -1e30
VMEM_LIMIT_BYTES = 56 * 1024 * 1024

_HI = lax.Precision.HIGHEST


def _cparams(*sem):
    return pltpu.CompilerParams(dimension_semantics=sem, vmem_limit_bytes=VMEM_LIMIT_BYTES)


def _const_spec(shape):
    nd = len(shape)
    return pl.BlockSpec(shape, lambda *_: (0,) * nd, pipeline_mode=pl.Buffered(1))


def _mm(a, b):
    return jnp.dot(a.astype(BF16), b.astype(BF16), preferred_element_type=F32)


def _split3(x):
    hi = x.astype(BF16)
    r = x - hi.astype(F32)
    mid = r.astype(BF16)
    lo = (r - mid.astype(F32)).astype(BF16)
    return hi, mid, lo


def _pack_pairs(x):
    bits = lax.bitcast_convert_type(x.astype(BF16).astype(F32), jnp.uint32)
    half = bits.shape[1] // 2
    return bits[:, :half] | (bits[:, half:] >> 16)


def _unpack_pairs(packed):
    left = lax.bitcast_convert_type(packed & jnp.uint32(0xFFFF0000), F32).astype(BF16)
    right = lax.bitcast_convert_type(packed << 16, F32).astype(BF16)
    return jnp.concatenate([left, right], axis=1)


def _softplus(x):
    return jnp.maximum(x, 0.0) + jnp.log(1.0 + jnp.exp(-jnp.abs(x)))


def _sigmoid(x):
    return 1.0 / (1.0 + jnp.exp(-x))


def _norm_proj_kernel(x_ref, g_ref, wab_ref, alog_ref, dtb_ref, h_ref, gb_ref, *, n_dir_heads):
    x = x_ref[...]
    h = x * lax.rsqrt(jnp.mean(x * x, axis=-1, keepdims=True) + EPS) * g_ref[...]
    hb = h.astype(BF16)
    h_ref[...] = hb
    ab = lax.dot_general(hb, wab_ref[...], (((1,), (1,)), ((), ())), preferred_element_type=F32)
    g = -jnp.exp(alog_ref[...]) * _softplus(ab + dtb_ref[...])
    beta = _sigmoid(ab)
    tm = x.shape[0]
    row = lax.broadcasted_iota(jnp.int32, (tm, tm), 0)
    col = lax.broadcasted_iota(jnp.int32, (tm, tm), 1)
    same = (row // GDN_CHUNK) == (col // GDN_CHUNK)
    prefix = jnp.where(same & (col <= row), 1.0, 0.0).astype(BF16)
    suffix = jnp.where(same & (col >= row), 1.0, 0.0).astype(BF16)
    pieces = _split3(g)
    cs_f = sum(jnp.dot(prefix, p, preferred_element_type=F32) for p in pieces)
    cs_b = sum(jnp.dot(suffix, p, preferred_element_type=F32) for p in pieces)
    lane = lax.broadcasted_iota(jnp.int32, g.shape, 1)
    gb_ref[...] = jnp.where(lane < n_dir_heads // 2, cs_f, jnp.where(lane < n_dir_heads, cs_b, beta))


def _norm_proj(x2, g_mix, w_ab, alog_pad, dtb_pad, n_dir_heads, tm):
    T, D = x2.shape
    return pl.pallas_call(
        functools.partial(_norm_proj_kernel, n_dir_heads=n_dir_heads),
        grid=(T // tm,),
        in_specs=[pl.BlockSpec((tm, D), lambda i: (i, 0)),
                  _const_spec((1, D)), _const_spec((LANE, D)), _const_spec((1, LANE)), _const_spec((1, LANE))],
        out_specs=[pl.BlockSpec((tm, D), lambda i: (i, 0)),
                   pl.BlockSpec((tm, LANE), lambda i: (i, 0))],
        out_shape=[jax.ShapeDtypeStruct((T, D), BF16),
                   jax.ShapeDtypeStruct((T, LANE), F32)],
        compiler_params=_cparams("parallel"),
        name="norm_proj",
    )(x2, g_mix, w_ab, alog_pad, dtb_pad)


REORDER_ROWS = 512


def _reorder_cast_kernel(w_ref, o_ref, narrow_ref, *, segments, narrow):
    n_start, n_width = narrow
    row = lax.broadcasted_iota(jnp.int32, narrow_ref.shape, 0)
    narrow_ref[...] = jnp.where(row < n_width, w_ref[n_start:n_start + LANE, :], 0.0).astype(narrow_ref.dtype)
    off = 0
    for start, width in segments:
        for c in range(0, width, REORDER_ROWS):
            n = min(REORDER_ROWS, width - c)
            o_ref[off:off + n, :] = w_ref[start + c:start + c + n, :].astype(o_ref.dtype)
            off += n


def _reorder_cast(wT, segments, narrow, tc):
    rows, cols = wT.shape
    out_rows = sum(width for _, width in segments)
    assert narrow[0] % 8 == 0 and narrow[1] <= LANE and narrow[0] + LANE <= rows
    assert all(start % 8 == 0 and width % 16 == 0 for start, width in segments)
    return pl.pallas_call(
        functools.partial(_reorder_cast_kernel, segments=segments, narrow=narrow),
        grid=(cols // tc,),
        in_specs=[pl.BlockSpec((rows, tc), lambda i: (0, i))],
        out_specs=[pl.BlockSpec((out_rows, tc), lambda i: (0, i)), pl.BlockSpec((LANE, tc), lambda i: (0, i))],
        out_shape=[jax.ShapeDtypeStruct((out_rows, cols), BF16), jax.ShapeDtypeStruct((LANE, cols), BF16)],
        compiler_params=_cparams("parallel"),
        name="reorder_cast",
    )(wT)


_NT_DIMS = (((1,), (1,)), ((), ()))


def _matmul_nt_kernel(a_ref, bT_ref, o_ref):
    o_ref[...] = lax.dot_general(a_ref[...], bT_ref[...], _NT_DIMS, preferred_element_type=F32).astype(o_ref.dtype)


def _matmul_nt(a, bT, tm, tn, out_dtype):
    M, K = a.shape
    N = bT.shape[0]
    return pl.pallas_call(
        _matmul_nt_kernel,
        grid=(N // tn, M // tm),
        in_specs=[pl.BlockSpec((tm, K), lambda j, i: (i, 0)),
                  pl.BlockSpec((tn, K), lambda j, i: (j, 0))],
        out_specs=pl.BlockSpec((tm, tn), lambda j, i: (i, j)),
        out_shape=jax.ShapeDtypeStruct((M, N), out_dtype),
        compiler_params=_cparams("parallel", "parallel"),
        name="in_proj",
    )(a, bT)


def _bmm_tn(a, b):
    return jnp.stack([lax.dot_general(a[n], b[n], (((0,), (0,)), ((), ())), preferred_element_type=F32)
                      for n in range(a.shape[0])])


def _gdn_kernel(q_ref, k_ref, v_ref, z_ref, cwq_ref, cwk_ref, cwv_ref, gb_ref, grow_ref, nw_ref,
                y_ref,
                xpad, qn, kn, vn, u_s, w_s, qg_s, at_s, tm_s, tn_s, egl_s, st_s,
                *, n_heads, seq, conv_rows, prep_chunks):
    C = GDN_CHUNK
    n_chunks = seq // C
    head = pl.program_id(1)
    pad = 8
    half = GDN_CONV // 2

    zeros_pad = jnp.zeros((pad, LANE), F32)
    xpad[0:pad, :] = zeros_pad
    xpad[pad + seq:pad + seq + pad, :] = zeros_pad
    for src, cw_ref, dst, mode in ((q_ref, cwq_ref, qn, "q"), (k_ref, cwk_ref, kn, "k"), (v_ref, cwv_ref, vn, "v")):
        xpad[pad:pad + seq, :] = src[...].astype(F32)
        for t in range(seq // conv_rows):
            r0 = t * conv_rows
            acc = jnp.zeros((conv_rows, LANE), F32)
            for j in range(GDN_CONV):
                lo = pad + r0 + j - half
                acc = acc + xpad[lo:lo + conv_rows, :] * cw_ref[j:j + 1, :]
            y = acc * _sigmoid(acc)
            if mode != "v":
                y = y * lax.rsqrt(jnp.sum(y * y, axis=-1, keepdims=True) + EPS)
            if mode == "q":
                y = y * (GDN_DK ** -0.5)
            dst[r0:r0 + conv_rows, :] = y

    nb = prep_chunks
    R = nb * C
    nb2 = 2 * nb
    bi = lax.broadcasted_iota(jnp.int32, (nb2, C, C), 0)
    ii = lax.broadcasted_iota(jnp.int32, (nb2, C, C), 1)
    jj = lax.broadcasted_iota(jnp.int32, (nb2, C, C), 2)
    fwd = bi < nb
    incl = (fwd & (ii >= jj)) | (~fwd & (ii <= jj))
    strict = (fwd & (ii > jj)) | (~fwd & (ii < jj))
    eye = jnp.where(ii == jj, 1.0, 0.0).astype(F32)
    lane = lax.broadcasted_iota(jnp.int32, (R, LANE), 1)
    both = lambda x: jnp.concatenate([x, x], axis=0)

    def prep(t, carry):
        r0 = pl.multiple_of(t * R, R)
        n0 = pl.multiple_of(t * nb, nb)
        q3 = qn[pl.ds(r0, R), :].reshape(nb, C, LANE)
        k3 = kn[pl.ds(r0, R), :].reshape(nb, C, LANE)
        v3 = vn[pl.ds(r0, R), :].reshape(nb, C, LANE)
        gb_blk = gb_ref[pl.ds(r0, R), :]
        kq = jnp.concatenate([k3, q3], axis=1).astype(BF16)
        gram = jnp.einsum("nik,njk->nij", kq, k3.astype(BF16), preferred_element_type=F32)
        kk, qk = both(gram[:, :C, :]), both(gram[:, C:, :])
        gcb, bt, grow, gl = [], [], [], []
        for d in range(2):
            colidx = d * n_heads + head
            gcb_d = jnp.sum(jnp.where(lane == colidx, gb_blk, 0.0), axis=-1, keepdims=True).reshape(nb, C, 1)
            gcb.append(gcb_d)
            bt.append(jnp.sum(jnp.where(lane == 2 * n_heads + colidx, gb_blk, 0.0), axis=-1,
                              keepdims=True).reshape(nb, C, 1))
            grow.append(grow_ref[0, colidx, pl.ds(n0, nb), :].reshape(nb, 1, C))
            gl.append(gcb_d[:, C - 1:C, :] if d == 0 else gcb_d[:, 0:1, :])
        gcb, bt, grow, gl = (jnp.concatenate(x, axis=0) for x in (gcb, bt, grow, gl))
        k2, q2, v2 = both(k3), both(q3), both(v3)

        decay = jnp.exp(jnp.where(incl, gcb - grow, -jnp.inf))
        L = jnp.where(strict, bt * kk * decay, 0.0)
        attn = qk * decay
        ainv = eye - L
        P = L
        for _ in range(int(math.log2(C)) - 1):
            Pb = P.astype(BF16)
            P = jnp.einsum("nij,njk->nik", Pb, Pb, preferred_element_type=F32)
            ainv = ainv + jnp.einsum("nij,njk->nik", ainv.astype(BF16), P.astype(BF16),
                                     preferred_element_type=F32)
        eg = jnp.exp(gcb)
        rhs = jnp.concatenate([v2 * bt, k2 * (bt * eg)], axis=-1).astype(BF16)
        uw = jnp.einsum("nij,njd->nid", ainv.astype(BF16), rhs, preferred_element_type=F32).astype(BF16)
        kg = (k2 * jnp.exp(gl - gcb)).astype(BF16)
        trans = _bmm_tn(kg, uw)
        qg = (q2 * eg).astype(BF16)
        egl = jnp.broadcast_to(jnp.exp(gl).reshape(nb2, 1), (nb2, LANE))
        for d in range(2):
            sl = slice(d * nb, (d + 1) * nb)
            u_s[d, pl.ds(r0, R), :] = uw[sl, :, :LANE].reshape(R, LANE)
            w_s[d, pl.ds(r0, R), :] = uw[sl, :, LANE:].reshape(R, LANE)
            qg_s[d, pl.ds(r0, R), :] = qg[sl].reshape(R, LANE)
            at_s[d, pl.ds(r0, R), :] = attn[sl].reshape(R, C).astype(BF16)
            tn_s[d, pl.ds(n0 * GDN_DK, nb * GDN_DK), :] = trans[sl, :, :LANE].reshape(nb * GDN_DK, LANE).astype(BF16)
            tm_s[d, pl.ds(n0 * GDN_DK, nb * GDN_DK), :] = (-trans[sl, :, LANE:]).reshape(nb * GDN_DK, LANE).astype(BF16)
            egl_s[d, pl.ds(n0, nb), :] = egl[sl]
        return carry

    lax.fori_loop(0, n_chunks // nb, prep, 0)

    def chunk_step(d, n, state):
        r = pl.multiple_of(n * GDN_DK, GDN_DK)
        sb = state.astype(BF16)
        st_s[d, pl.ds(r, GDN_DK), :] = sb
        return (state * egl_s[d, pl.ds(n, 1), :]
                + jnp.dot(tm_s[d, pl.ds(r, GDN_DK), :], sb, preferred_element_type=F32)
                + tn_s[d, pl.ds(r, GDN_DK), :].astype(F32))

    def scan(n, carry):
        sf, sbw = carry
        return chunk_step(0, n, sf), chunk_step(1, n_chunks - 1 - n, sbw)

    zero_state = jnp.zeros((GDN_DK, GDN_DV), F32)
    lax.fori_loop(0, n_chunks, scan, (zero_state, zero_state))

    def emit(t, carry):
        r0 = pl.multiple_of(t * R, R)
        s0 = pl.multiple_of(t * nb * GDN_DK, nb * GDN_DK)
        two = lambda ref, start, rows: jnp.concatenate([ref[0, pl.ds(start, rows), :], ref[1, pl.ds(start, rows), :]])
        st = two(st_s, s0, nb * GDN_DK).reshape(nb2, GDN_DK, GDN_DV)
        wq = jnp.concatenate([two(w_s, r0, R).reshape(nb2, C, LANE), two(qg_s, r0, R).reshape(nb2, C, LANE)], axis=1)
        ws_qs = jnp.einsum("ncd,nde->nce", wq, st, preferred_element_type=F32)
        vnew = two(u_s, r0, R).reshape(nb2, C, LANE).astype(F32) - ws_qs[:, :C, :]
        o = ws_qs[:, C:, :] + jnp.einsum("ncs,nse->nce", two(at_s, r0, R).reshape(nb2, C, C), vnew.astype(BF16),
                                         preferred_element_type=F32)
        o = (o[:nb] + o[nb:]).reshape(R, LANE)
        on = o * lax.rsqrt(jnp.mean(o * o, axis=-1, keepdims=True) + EPS) * nw_ref[...]
        z = z_ref[pl.ds(r0, R), :].astype(F32)
        y_ref[pl.ds(r0, R), :] = (on * (z * _sigmoid(z))).astype(y_ref.dtype)
        return carry

    lax.fori_loop(0, n_chunks // nb, emit, 0)


def _gdn(p, p_off, conv_w8, gb, grow, norm_w, batch, seq, n_heads):
    T = batch * seq
    qk_blocks = n_heads
    n_chunks = seq // GDN_CHUNK
    conv_rows = min(512, seq)
    prep_chunks = min(16, n_chunks)
    blk = lambda off: pl.BlockSpec((seq, LANE), lambda b, h: (b, p_off + off + h))
    cw = lambda off: pl.BlockSpec((8, LANE), lambda b, h: (0, off + h))
    return pl.pallas_call(
        functools.partial(_gdn_kernel, n_heads=n_heads, seq=seq, conv_rows=conv_rows, prep_chunks=prep_chunks),
        grid=(batch, n_heads),
        in_specs=[blk(0), blk(qk_blocks), blk(2 * qk_blocks), blk(3 * qk_blocks),
                  cw(0), cw(qk_blocks), cw(2 * qk_blocks),
                  pl.BlockSpec((seq, LANE), lambda b, h: (b, 0), pipeline_mode=pl.Buffered(1)),
                  pl.BlockSpec((1, 2 * n_heads, n_chunks, GDN_CHUNK), lambda b, h: (b, 0, 0, 0),
                               pipeline_mode=pl.Buffered(1)),
                  _const_spec((1, LANE))],
        out_specs=pl.BlockSpec((seq, LANE), lambda b, h: (b, h)),
        out_shape=jax.ShapeDtypeStruct((T, n_heads * GDN_DV), BF16),
        scratch_shapes=[
            pltpu.VMEM((seq + 16, LANE), F32),
            pltpu.VMEM((seq, LANE), F32),
            pltpu.VMEM((seq, LANE), F32),
            pltpu.VMEM((seq, LANE), F32),
            pltpu.VMEM((2, seq, LANE), BF16),
            pltpu.VMEM((2, seq, LANE), BF16),
            pltpu.VMEM((2, seq, LANE), BF16),
            pltpu.VMEM((2, seq, GDN_CHUNK), BF16),
            pltpu.VMEM((2, n_chunks * GDN_DK, GDN_DV), BF16),
            pltpu.VMEM((2, n_chunks * GDN_DK, GDN_DV), BF16),
            pltpu.VMEM((2, max(n_chunks, 8), LANE), F32),
            pltpu.VMEM((2, n_chunks * GDN_DK, GDN_DV), BF16),
        ],
        compiler_params=_cparams("parallel", "parallel"),
        name="gdn",
    )(p, p, p, p, conv_w8, conv_w8, conv_w8, gb, grow, norm_w)


def _diff_attn_kernel(slopes_ref, q_ref, k_ref, v_ref, lam_ref, nw_ref, o_ref,
                      qT_s, vT_s, klo_s, khi_s, rel_s, s_s, p_s, acc_s,
                      *, seq, tq, tk, lam_init):
    head = pl.program_id(1)
    slope2 = slopes_ref[head] * LOG2E
    scale2 = (DIFF_DH ** -0.5) * LOG2E
    n_kv = seq // tk
    maps = (klo_s, khi_s)

    rows = min(512, seq)
    for t in range(seq // rows):
        r0 = t * rows
        qt = (q_ref[r0:r0 + rows, :].astype(F32) * scale2).T.astype(BF16)
        frow = lax.broadcasted_iota(jnp.int32, (LANE, rows), 0)
        fcol = lax.broadcasted_iota(jnp.int32, (LANE, rows), 1)
        qa, qb, qc = (x.astype(F32) for x in _split3(((r0 + fcol) % tq).astype(F32) * slope2))
        qfeat = jnp.where(frow < 3, 1.0, jnp.where(frow == 3, -qa, jnp.where(frow == 4, -qb,
                                                   jnp.where(frow == 5, -qc, 0.0)))).astype(BF16)
        for sgn, feat in enumerate((qfeat, -qfeat)):
            qT_s[sgn, 0:LANE, r0:r0 + rows] = qt
            qT_s[sgn, LANE:2 * LANE, r0:r0 + rows] = feat
        vT_s[0:LANE, r0:r0 + rows] = v_ref[r0:r0 + rows, :].astype(F32).T.astype(BF16)
        vT_s[LANE:LANE + 16, r0:r0 + rows] = jnp.ones((16, rows), BF16)
        kt = k_ref[r0:r0 + rows, :]
        krow = lax.broadcasted_iota(jnp.int32, (rows, LANE), 0)
        lane = lax.broadcasted_iota(jnp.int32, (rows, LANE), 1)
        ka, kb, kc = (x.astype(F32) for x in _split3(((r0 + krow) % tk).astype(F32) * slope2))
        kfeat = jnp.where(lane == 0, ka, jnp.where(lane == 1, kb, jnp.where(lane == 2, kc,
                                                   jnp.where(lane < 6, 1.0, 0.0)))).astype(BF16)
        klo_s[r0:r0 + rows, 0:LANE] = jnp.where(lane < DIFF_DH, kt, jnp.zeros_like(kt))
        khi_s[r0:r0 + rows, 0:LANE] = jnp.where(lane >= DIFF_DH, kt, jnp.zeros_like(kt))
        klo_s[r0:r0 + rows, LANE:2 * LANE] = kfeat
        khi_s[r0:r0 + rows, LANE:2 * LANE] = kfeat
    kr = lax.broadcasted_iota(jnp.int32, (tk, tq), 0)
    qcol = lax.broadcasted_iota(jnp.int32, (tk, tq), 1)
    rel_s[...] = (qcol - kr).astype(F32) * slope2

    lf = lam_ref[...]
    lam = (jnp.exp(jnp.sum(lf[0:1, :] * lf[1:2, :], axis=-1, keepdims=True))
           - jnp.exp(jnp.sum(lf[2:3, :] * lf[3:4, :], axis=-1, keepdims=True)) + lam_init)

    def q_tile(i, carry):
        c_q = pl.multiple_of(i * tq, tq)
        j_diag = (i * tq) // tk

        def scores(t):
            j = j_diag if t == 0 else lax.rem(j_diag + t, n_kv)
            r_k = pl.multiple_of(j * tk, tk)
            d0 = jnp.asarray(i * tq - j * tk)
            if t == 0:
                bias = jnp.abs(rel_s[...] + d0.astype(F32) * slope2)
                for mp, k_s in enumerate(maps):
                    s_s[t % 2, mp] = jnp.dot(k_s[pl.ds(r_k, tk), 0:LANE], qT_s[0, 0:LANE, pl.ds(c_q, tq)],
                                             preferred_element_type=F32) - bias
                return r_k, 0.0
            sgn = jnp.asarray(j > j_diag).astype(jnp.int32)
            for mp, k_s in enumerate(maps):
                s_s[t % 2, mp] = jnp.dot(k_s[pl.ds(r_k, tk), :], qT_s[sgn, :, pl.ds(c_q, tq)],
                                         preferred_element_type=F32)
            return r_k, jnp.abs(d0).astype(F32) * slope2

        def weighted_values(t, r_k, alphas):
            vT = vT_s[:, pl.ds(r_k, tk)]
            for mp in range(len(maps)):
                upd = jnp.dot(vT, p_s[t % 2, mp], preferred_element_type=F32)
                acc_s[mp] = upd if t == 0 else acc_s[mp] * alphas[mp] + upd

        m = [jnp.full((1, tq), NEG_BIG, F32) for _ in maps]
        nxt = scores(0)
        prev = None
        for t in range(n_kv):
            r_k, const = nxt
            if t + 1 < n_kv:
                nxt = scores(t + 1)
            alphas = []
            for mp in range(len(maps)):
                m_new = jnp.maximum(m[mp], jnp.max(s_s[t % 2, mp], axis=0, keepdims=True) - const)
                alphas.append(jnp.exp2(m[mp] - m_new))
                p_s[t % 2, mp] = jnp.exp2(s_s[t % 2, mp] - (m_new + const)).astype(BF16)
                m[mp] = m_new
            if prev is not None:
                weighted_values(t - 1, *prev)
            prev = (r_k, alphas)
        weighted_values(n_kv - 1, *prev)

        o1, o2 = acc_s[0], acc_s[1]
        od = o1[0:DIFF_DV] / o1[DIFF_DV:DIFF_DV + 1] - lam * (o2[0:DIFF_DV] / o2[DIFF_DV:DIFF_DV + 1])
        yn = od * lax.rsqrt(jnp.mean(od * od, axis=0, keepdims=True) + EPS) * nw_ref[...] * (1.0 - lam_init)
        o_ref[pl.ds(c_q, tq), :] = yn.T.astype(o_ref.dtype)
        return carry

    lax.fori_loop(0, seq // tq, q_tile, 0)


def _diff_attn(p, slopes, diff_lambda, norm_w_col, batch, seq, n_heads, q_off, k_off, v_off, lam_init):
    T = batch * seq
    tq = min(512, seq)
    tk = min(512, seq)
    blk = lambda off: pl.BlockSpec((seq, LANE), lambda b, h: (b, off + h))
    return pl.pallas_call(
        functools.partial(_diff_attn_kernel, seq=seq, tq=tq, tk=tk, lam_init=lam_init),
        grid=(batch, n_heads),
        in_specs=[pl.BlockSpec(memory_space=pltpu.SMEM),
                  blk(q_off), blk(k_off), blk(v_off),
                  _const_spec((4, DIFF_DH)), _const_spec((DIFF_DV, 1))],
        out_specs=pl.BlockSpec((seq, LANE), lambda b, h: (b, h)),
        out_shape=jax.ShapeDtypeStruct((T, n_heads * DIFF_DV), BF16),
        scratch_shapes=[
            pltpu.VMEM((2, 2 * LANE, seq), BF16),
            pltpu.VMEM((DIFF_DV + 16, seq), BF16),
            pltpu.VMEM((seq, 2 * LANE), BF16),
            pltpu.VMEM((seq, 2 * LANE), BF16),
            pltpu.VMEM((tk, tq), F32),
            pltpu.VMEM((2, 2, tk, tq), F32),
            pltpu.VMEM((2, 2, tk, tq), BF16),
            pltpu.VMEM((2, DIFF_DV + 16, tq), F32),
        ],
        compiler_params=_cparams("parallel", "parallel"),
        name="diff_attn",
    )(slopes, p, p, p, diff_lambda, norm_w_col)


def _merge_kernel(ya_ref, yd_ref, ga_ref, gd_ref, x_ref, wa_ref, wd_ref, wo_ref, gffn_ref,
                  wr_hi_ref, wr_lo_ref, br_ref, x1_ref, h2_ref, route_ref, count_ref, *, n_groups, per_group):
    ma = jnp.dot(ya_ref[...], wa_ref[...], preferred_element_type=F32)
    md = jnp.dot(yd_ref[...], wd_ref[...], preferred_element_type=F32)
    merged = _sigmoid(ga_ref[...].astype(F32)) * ma + _sigmoid(gd_ref[...].astype(F32)) * md
    x1 = x_ref[...] + jnp.dot(merged.astype(BF16), wo_ref[...], preferred_element_type=F32)
    x1_ref[...] = x1
    h2 = x1 * lax.rsqrt(jnp.mean(x1 * x1, axis=-1, keepdims=True) + EPS) * gffn_ref[...]
    h2_ref[...] = _pack_pairs(h2)

    h_hi = h2.astype(BF16)
    h_lo = (h2 - h_hi.astype(F32)).astype(BF16)
    logits = (jnp.dot(h_hi, wr_hi_ref[...], preferred_element_type=F32)
              + jnp.dot(h_hi, wr_lo_ref[...], preferred_element_type=F32)
              + jnp.dot(h_lo, wr_hi_ref[...], preferred_element_type=F32)) + br_ref[...]
    lane = lax.broadcasted_iota(jnp.int32, logits.shape, 1)
    lane_f = lane.astype(F32)
    big = float(LANE)

    def first_argmax(vals, vmax):
        return jnp.min(jnp.where(vals == vmax, lane_f, big), axis=-1, keepdims=True)

    gl = jnp.where(lane < n_groups, logits, -jnp.inf)
    gmax = jnp.max(gl, axis=-1, keepdims=True)
    g_idx = first_argmax(gl, gmax)
    g_w = 1.0 / jnp.sum(jnp.exp(gl - gmax), axis=-1, keepdims=True)
    e_lo = n_groups + g_idx * per_group
    in_group = (lane_f >= e_lo) & (lane_f < e_lo + per_group)
    el = jnp.where(in_group, logits, -jnp.inf)
    emax = jnp.max(el, axis=-1, keepdims=True)
    pe = jnp.exp(el - emax)
    pe = pe / jnp.sum(pe, axis=-1, keepdims=True)
    p1 = jnp.max(pe, axis=-1, keepdims=True)
    i1 = first_argmax(jnp.where(in_group, pe, -1.0), p1)
    rest = jnp.where(in_group & (lane_f != i1), pe, -1.0)
    p2 = jnp.max(rest, axis=-1, keepdims=True)
    i2 = first_argmax(rest, p2)
    denom = p1 + p2
    w1 = p1 / denom * g_w
    w2 = p2 / denom * g_w
    @pl.when(pl.program_id(0) == 0)
    def _():
        count_ref[...] = jnp.zeros_like(count_ref)

    hit1 = lane_f == i1
    hit2 = lane_f == i2
    hits = jnp.where(hit1 | hit2, 1.0, 0.0)
    tm = hits.shape[0]
    earlier = (lax.broadcasted_iota(jnp.int32, (tm, tm), 1) < lax.broadcasted_iota(jnp.int32, (tm, tm), 0))
    before = count_ref[0:1, :] + jnp.dot(jnp.where(earlier, 1.0, 0.0).astype(BF16), hits.astype(BF16),
                                         preferred_element_type=F32)
    rank1 = jnp.sum(jnp.where(hit1, before, 0.0), axis=-1, keepdims=True)
    rank2 = jnp.sum(jnp.where(hit2, before, 0.0), axis=-1, keepdims=True)
    count_ref[...] = count_ref[...] + jnp.sum(hits, axis=0, keepdims=True)

    route = jnp.where(lane == 0, i1 - n_groups, 0.0)
    route = jnp.where(lane == 1, i2 - n_groups, route)
    route = jnp.where(lane == 2, w1, route)
    route = jnp.where(lane == 3, w2, route)
    route = jnp.where(lane == 4, rank1, route)
    route = jnp.where(lane == 5, rank2, route)
    route_ref[...] = route


def _merge(y_a, y_d, p, x2, wa, wd, wo, g_ffn, wr_hi, wr_lo, b_r, n_groups, per_group, tm):
    T, D = x2.shape
    va, vd = y_a.shape[1], y_d.shape[1]
    row = lambda w: pl.BlockSpec((tm, w), lambda i: (i, 0))
    return pl.pallas_call(
        functools.partial(_merge_kernel, n_groups=n_groups, per_group=per_group),
        grid=(T // tm,),
        in_specs=[row(va), row(vd),
                  pl.BlockSpec((tm, D), lambda i: (i, 0)),
                  pl.BlockSpec((tm, D), lambda i: (i, 1)),
                  row(D),
                  _const_spec((va, D)), _const_spec((vd, D)), _const_spec((D, D)), _const_spec((1, D)),
                  _const_spec((D, LANE)), _const_spec((D, LANE)), _const_spec((1, LANE))],
        out_specs=[row(D), row(D // 2), row(LANE), pl.BlockSpec((8, LANE), lambda i: (0, 0))],
        out_shape=[jax.ShapeDtypeStruct((T, D), F32),
                   jax.ShapeDtypeStruct((T, D // 2), jnp.uint32),
                   jax.ShapeDtypeStruct((T, LANE), F32),
                   jax.ShapeDtypeStruct((8, LANE), F32)],
        compiler_params=_cparams("arbitrary"),
        name="merge",
    )(y_a, y_d, p, p, x2, wa, wd, wo, g_ffn, wr_hi, wr_lo, b_r)


def _expert_kernel(tile_expert_ref, tile_flag_ref, tile_rows_ref, next_expert_ref, slot_ref,
                   xs_ref, wg_hbm, wu_hbm, wd_hbm, o_ref,
                   wg_f, wu_f, wd_f, wg_s, wu_s, wd_s, sems):
    i = pl.program_id(0)
    flag = tile_flag_ref[i]

    def weight_copies(expert, slot):
        return [pltpu.make_async_copy(hbm.at[expert], buf.at[slot], sems.at[slot, n])
                for n, (hbm, buf) in enumerate(((wg_hbm, wg_f), (wu_hbm, wu_f), (wd_hbm, wd_f)))]

    @pl.when(flag == 2)
    def _():
        slot = slot_ref[i]

        @pl.when(i == 0)
        def _():
            for copy in weight_copies(tile_expert_ref[i], slot):
                copy.start()

        for copy in weight_copies(tile_expert_ref[i], slot):
            copy.wait()
        nxt = next_expert_ref[i]

        @pl.when(nxt >= 0)
        def _():
            for copy in weight_copies(nxt, 1 - slot):
                copy.start()

        wg_s[...] = wg_f[slot].astype(BF16)
        wu_s[...] = wu_f[slot].astype(BF16)
        wd_s[...] = wd_f[slot].astype(BF16)

    @pl.when(flag > 0)
    def _():
        row = lax.broadcasted_iota(jnp.int32, xs_ref.shape, 0)
        x = _unpack_pairs(jnp.where(row < tile_rows_ref[i], xs_ref[...], jnp.uint32(0)))
        g = jnp.dot(x, wg_s[...], preferred_element_type=F32)
        u = jnp.dot(x, wu_s[...], preferred_element_type=F32)
        hid = g * _sigmoid(g) * u
        o_ref[...] = _pack_pairs(jnp.dot(hid.astype(BF16), wd_s[...], preferred_element_type=F32))

    @pl.when(flag == 0)
    def _():
        o_ref[...] = jnp.zeros_like(o_ref)


def _experts(tile_expert, tile_flag, tile_rows, next_expert, slot, xs, w_gate, w_up, w_down, tm):
    n_rows, half = xs.shape
    E, D, Fd = w_gate.shape
    assert D == 2 * half
    n_tiles = n_rows // tm
    hbm = pl.BlockSpec(memory_space=pl.ANY)
    grid_spec = pltpu.PrefetchScalarGridSpec(
        num_scalar_prefetch=5,
        grid=(n_tiles,),
        in_specs=[pl.BlockSpec((tm, half), lambda i, *_: (i, 0)), hbm, hbm, hbm],
        out_specs=pl.BlockSpec((tm, half), lambda i, *_: (i, 0)),
        scratch_shapes=[pltpu.VMEM((2, D, Fd), F32), pltpu.VMEM((2, D, Fd), F32), pltpu.VMEM((2, Fd, D), F32),
                        pltpu.VMEM((D, Fd), BF16), pltpu.VMEM((D, Fd), BF16), pltpu.VMEM((Fd, D), BF16),
                        pltpu.SemaphoreType.DMA((2, 3))],
    )
    return pl.pallas_call(
        _expert_kernel,
        grid_spec=grid_spec,
        out_shape=jax.ShapeDtypeStruct((n_rows, half), jnp.uint32),
        compiler_params=_cparams("arbitrary"),
        name="experts",
    )(tile_expert, tile_flag, tile_rows, next_expert, slot, xs, w_gate, w_up, w_down)


SC_CORES = 2
SC_SUBCORES = 16
SC_ROWS = 32
SC_COLLECT_BUFFERS = 3


def _dispatch_rows(table, pos, n_rows):
    T, W = table.shape
    workers = SC_CORES * SC_SUBCORES
    chunks = T // (workers * SC_ROWS)
    assert workers * chunks * SC_ROWS == T
    pos4 = pos.T.reshape(TOP_K, workers, chunks, SC_ROWS).transpose(1, 0, 2, 3)
    mesh = plsc.VectorSubcoreMesh(core_axis_name="c", subcore_axis_name="s")

    @functools.partial(
        pl.kernel, mesh=mesh,
        out_type=jax.ShapeDtypeStruct((n_rows, W), table.dtype),
        scratch_types=[pltpu.VMEM((TOP_K, chunks, SC_ROWS), jnp.int32),
                       pltpu.VMEM((2, SC_ROWS, W), table.dtype),
                       pltpu.SemaphoreType.DMA, pltpu.SemaphoreType.DMA, pltpu.SemaphoreType.DMA],
        name="dispatch_rows",
    )
    def dispatch(table_hbm, pos_hbm, out_hbm, pos_v, rows_v, sem_in, sem_out0, sem_out1):
        wid = lax.axis_index("s") * SC_CORES + lax.axis_index("c")
        base = wid * (chunks * SC_ROWS)
        pltpu.sync_copy(pos_hbm.at[wid], pos_v)

        def load(c):
            return pltpu.async_copy(table_hbm.at[pl.ds(base + c * SC_ROWS, SC_ROWS)], rows_v.at[c % 2], sem_in)

        pending = load(0)
        for c in range(chunks):
            pending.wait()
            if c + 1 < chunks:
                pending = load(c + 1)
            out0 = pltpu.async_copy(rows_v.at[c % 2], out_hbm.at[pos_v.at[0, c]], sem_out0)
            out1 = pltpu.async_copy(rows_v.at[c % 2], out_hbm.at[pos_v.at[1, c]], sem_out1)
            out0.wait()
            out1.wait()

    return dispatch(table, pos4)


def _collect_rows(table, pos):
    T = pos.shape[0]
    W = table.shape[1]
    workers = SC_CORES * SC_SUBCORES
    chunks = T // (workers * SC_ROWS)
    assert workers * chunks * SC_ROWS == T
    pos4 = pos.T.reshape(TOP_K, workers, chunks, SC_ROWS).transpose(1, 0, 2, 3)
    mesh = plsc.VectorSubcoreMesh(core_axis_name="c", subcore_axis_name="s")
    steps = [(c, k) for c in range(chunks) for k in range(TOP_K)]

    @functools.partial(
        pl.kernel, mesh=mesh,
        out_type=jax.ShapeDtypeStruct((TOP_K, T, W), table.dtype),
        scratch_types=[pltpu.VMEM((TOP_K, chunks, SC_ROWS), jnp.int32),
                       pltpu.VMEM((SC_COLLECT_BUFFERS, SC_ROWS, W), table.dtype)]
                      + [pltpu.SemaphoreType.DMA] * SC_COLLECT_BUFFERS,
        name="collect_rows",
    )
    def collect(table_hbm, pos_hbm, out_hbm, pos_v, rows_v, *sems):
        wid = lax.axis_index("s") * SC_CORES + lax.axis_index("c")
        base = wid * (chunks * SC_ROWS)
        pltpu.sync_copy(pos_hbm.at[wid], pos_v)

        def gather(n):
            c, k = steps[n]
            buf = n % SC_COLLECT_BUFFERS
            return pltpu.async_copy(table_hbm.at[pos_v.at[k, c]], rows_v.at[buf], sems[buf])

        in_flight = [gather(n) for n in range(min(SC_COLLECT_BUFFERS - 1, len(steps)))]
        for n, (c, k) in enumerate(steps):
            in_flight.pop(0).wait()
            ahead = n + SC_COLLECT_BUFFERS - 1
            if ahead < len(steps):
                in_flight.append(gather(ahead))
            pltpu.sync_copy(rows_v.at[n % SC_COLLECT_BUFFERS],
                            out_hbm.at[k, pl.ds(base + c * SC_ROWS, SC_ROWS)])

    return collect(table, pos4)


def _final_kernel(x1_ref, y0_ref, y1_ref, route_ref, g_ref, o_ref):
    route = route_ref[...]
    lane = lax.broadcasted_iota(jnp.int32, route.shape, 1)
    w0 = jnp.sum(jnp.where(lane == TOP_K, route, 0.0), axis=-1, keepdims=True)
    w1 = jnp.sum(jnp.where(lane == TOP_K + 1, route, 0.0), axis=-1, keepdims=True)
    y0 = _unpack_pairs(y0_ref[...]).astype(F32)
    y1 = _unpack_pairs(y1_ref[...]).astype(F32)
    x = x1_ref[...] + (w0 * y0 + w1 * y1)
    o_ref[...] = x * lax.rsqrt(jnp.mean(x * x, axis=-1, keepdims=True) + EPS) * g_ref[...]


def _final(x1, ys_tok, route, g_final, tm):
    T, D = x1.shape
    row = pl.BlockSpec((tm, D), lambda i: (i, 0))
    y_spec = lambda k: pl.BlockSpec((None, tm, D // 2), lambda i: (k, i, 0))
    return pl.pallas_call(
        _final_kernel,
        grid=(T // tm,),
        in_specs=[row, y_spec(0), y_spec(1), pl.BlockSpec((tm, LANE), lambda i: (i, 0)), _const_spec((1, D))],
        out_specs=row,
        out_shape=jax.ShapeDtypeStruct((T, D), F32),
        compiler_params=_cparams("parallel"),
        name="final",
    )(x1, ys_tok, ys_tok, route, g_final)


def _route_tables(ids, rank, counts, tm):
    T = ids.shape[0]
    A = T * TOP_K
    n_experts = counts.shape[0]
    padded = ((counts + tm - 1) // tm) * tm
    ends = jnp.cumsum(padded)
    starts = ends - padded
    onehot = ids[:, :, None] == jnp.arange(n_experts, dtype=jnp.int32)[None, None, :]
    pos = jnp.sum(jnp.where(onehot, starts[None, None, :], 0), axis=-1) + rank
    n_tiles = A // tm + n_experts
    tile_start = jnp.arange(n_tiles, dtype=jnp.int32) * tm
    tile_expert_raw = jnp.sum((ends[None, :] <= tile_start[:, None]).astype(jnp.int32), axis=1)
    valid = tile_start < ends[-1]
    last_expert = jnp.max(jnp.where(counts > 0, jnp.arange(n_experts, dtype=jnp.int32), 0))
    tile_expert = jnp.where(valid, jnp.minimum(tile_expert_raw, n_experts - 1), last_expert)
    first = jnp.concatenate([jnp.ones((1,), bool), tile_expert[1:] != tile_expert[:-1]])
    tile_flag = jnp.where(valid, jnp.where(first, 2, 1), 0).astype(jnp.int32)
    tile_rows = jnp.clip((starts + counts)[tile_expert] - tile_start, 0, tm)
    tile_rows = jnp.where(valid, tile_rows, 0).astype(jnp.int32)
    e_idx = jnp.arange(n_experts, dtype=jnp.int32)
    nonempty = counts > 0
    later = (e_idx[None, :] > e_idx[:, None]) & nonempty[None, :]
    next_nonempty = jnp.min(jnp.where(later, e_idx[None, :], n_experts), axis=1)
    next_nonempty = jnp.where(next_nonempty < n_experts, next_nonempty, -1).astype(jnp.int32)
    order = jnp.sum(((e_idx[None, :] < e_idx[:, None]) & nonempty[None, :]).astype(jnp.int32), axis=1)
    next_expert = next_nonempty[tile_expert]
    slot = (order[tile_expert] % 2).astype(jnp.int32)
    return pos, tile_expert, tile_flag, tile_rows, next_expert, slot


def _largest_tile(n, cap):
    t = min(n, cap)
    while n % t:
        t //= 2
    return t


def kernel(x, g_mix, w_in, conv_w, a_log, dt_bias, gdn_norm_w, diff_lambda, diff_norm_w, w_branch_a, w_branch_d,
           w_out, g_ffn, w_group, b_group, w_router, b_router, w_exp_gate, w_exp_up, w_exp_down, g_final):
    B, S, D = x.shape
    T = B * S
    depth = g_mix.shape[0]
    Hg = a_log.shape[-1]
    gdn_qk = Hg * GDN_DK
    gdn_v = Hg * GDN_DV
    diff_v = w_branch_d.shape[1]
    Hd = diff_v // DIFF_DV
    diff_qk = Hd * 2 * DIFF_DH
    n_groups = w_group.shape[-1]
    n_experts = w_router.shape[-1]
    per_group = n_experts // n_groups
    assert 4 * Hg <= LANE and n_groups + n_experts <= LANE
    assert S % GDN_CHUNK == 0 and D % LANE == 0

    ab_lo = 2 * gdn_qk + 2 * gdn_v
    ab_hi = ab_lo + 4 * Hg
    gates_lo = ab_hi + 2 * diff_qk + diff_v
    main_cols = 2 * D + ab_lo + 2 * diff_qk + diff_v
    gdn_off = 2 * D
    q_d_off = gdn_off + ab_lo
    k_d_off = q_d_off + diff_qk
    v_d_off = k_d_off + diff_qk

    tm_norm = _largest_tile(T, 512)
    tm_proj = _largest_tile(T, 2048)
    tn_proj = _largest_tile(main_cols, 1024)
    tm_merge = _largest_tile(T, 512)
    tm_exp = _largest_tile(T * TOP_K, 256)
    tm_final = _largest_tile(T, 512)

    slopes = jnp.exp2(-8.0 * (jnp.arange(Hd, dtype=F32) + 1.0) / Hd)
    x2 = x.reshape(T, D)
    for layer in range(depth):
        lam_init = 0.8 - 0.6 * math.exp(-0.3 * layer)
        w_in_l = w_in[layer]
        w_main, w_ab = _reorder_cast(w_in_l.T, ((gates_lo, 2 * D), (0, ab_lo), (ab_hi, gates_lo - ab_hi)),
                                     (ab_lo, 4 * Hg), _largest_tile(D, 256))
        alog_pad = jnp.pad(a_log[layer].reshape(1, 2 * Hg), ((0, 0), (0, LANE - 2 * Hg)))
        dtb_pad = jnp.pad(dt_bias[layer].reshape(1, 2 * Hg), ((0, 0), (0, LANE - 2 * Hg)))

        h, gb = _norm_proj(x2, g_mix[layer].reshape(1, D), w_ab, alog_pad, dtb_pad, 2 * Hg, tm_norm)
        p = _matmul_nt(h, w_main, tm_proj, tn_proj, BF16)

        grow = gb[:, :2 * Hg].reshape(B, S // GDN_CHUNK, GDN_CHUNK, 2 * Hg).transpose(0, 3, 1, 2)
        conv_w8 = jnp.pad(conv_w[layer], ((0, 8 - GDN_CONV), (0, 0)))
        y_a = _gdn(p, gdn_off // LANE, conv_w8, gb, grow, gdn_norm_w[layer].reshape(1, GDN_DV), B, S, Hg)
        y_d = _diff_attn(p, slopes, diff_lambda[layer], diff_norm_w[layer].reshape(DIFF_DV, 1), B, S, Hd,
                         q_d_off // LANE, k_d_off // LANE, v_d_off // LANE, lam_init)

        w_r = jnp.pad(jnp.concatenate([w_group[layer], w_router[layer]], axis=1),
                      ((0, 0), (0, LANE - n_groups - n_experts)))
        wr_hi = w_r.astype(BF16)
        wr_lo = (w_r - wr_hi.astype(F32)).astype(BF16)
        b_r = jnp.pad(jnp.concatenate([b_group[layer], b_router[layer]]).reshape(1, -1),
                      ((0, 0), (0, LANE - n_groups - n_experts)))
        x1, h2, route, counts = _merge(y_a, y_d, p, x2,
                                       w_branch_a[layer].astype(BF16), w_branch_d[layer].astype(BF16),
                                       w_out[layer].astype(BF16), g_ffn[layer].reshape(1, D),
                                       wr_hi, wr_lo, b_r, n_groups, per_group, tm_merge)

        ids = route[:, 0:TOP_K].astype(jnp.int32)
        rank = route[:, 2 * TOP_K:3 * TOP_K].astype(jnp.int32)
        counts = counts[0, n_groups:n_groups + n_experts].astype(jnp.int32)
        pos, tile_expert, tile_flag, tile_rows, next_expert, slot = _route_tables(ids, rank, counts, tm_exp)
        xs = _dispatch_rows(h2, pos, tile_expert.shape[0] * tm_exp)
        ys = _experts(tile_expert, tile_flag, tile_rows, next_expert, slot, xs, w_exp_gate[layer],
                      w_exp_up[layer], w_exp_down[layer], tm_exp)
        ys_tok = _collect_rows(ys, pos)
        if layer + 1 < depth:
            wts = route[:, TOP_K:2 * TOP_K]
            x2 = x1 + (wts[:, 0:1] * _unpack_pairs(ys_tok[0]).astype(F32)
                       + wts[:, 1:2] * _unpack_pairs(ys_tok[1]).astype(F32))
    out = _final(x1, ys_tok, route, g_final.reshape(1, D), tm_final)
    return out.reshape(B, S, D)
```

```python
import functools
import math

import jax
import jax.numpy as jnp
from jax import lax
from jax.experimental import pallas as pl
from jax.experimental.pallas import tpu as pltpu
from jax.experimental.pallas import tpu_sc as plsc

F32 = jnp.float32
BF16 = jnp.bfloat16
EPS = 1e-6
LANE = 128
GDN_DK = 128
GDN_DV = 128
GDN_CONV = 5
GDN_CHUNK = 64
DIFF_DH = 64
DIFF_DV = 2 * DIFF_DH
TOP_K = 2
LOG2E = 1.4426950408889634
NEG_BIG = -1e30
VMEM_LIMIT_BYTES = 56 * 1024 * 1024

def _cparams(*sem):
    return pltpu.CompilerParams(dimension_semantics=sem, vmem_limit_bytes=VMEM_LIMIT_BYTES)


def _const_spec(shape):
    nd = len(shape)
    return pl.BlockSpec(shape, lambda *_: (0,) * nd, pipeline_mode=pl.Buffered(1))


def _split3(x):
    hi = x.astype(BF16)
    r = x - hi.astype(F32)
    mid = r.astype(BF16)
    lo = (r - mid.astype(F32)).astype(BF16)
    return hi, mid, lo


def _pack_pairs(x):
    bits = lax.bitcast_convert_type(x.astype(BF16).astype(F32), jnp.uint32)
    half = bits.shape[1] // 2
    return bits[:, :half] | (bits[:, half:] >> 16)


def _unpack_pairs(packed):
    left = lax.bitcast_convert_type(packed & jnp.uint32(0xFFFF0000), F32).astype(BF16)
    right = lax.bitcast_convert_type(packed << 16, F32).astype(BF16)
    return jnp.concatenate([left, right], axis=1)


def _softplus(x):
    return jnp.maximum(x, 0.0) + jnp.log(1.0 + jnp.exp(-jnp.abs(x)))


def _sigmoid(x):
    return 1.0 / (1.0 + jnp.exp(-x))


def _norm_proj_kernel(x_ref, g_ref, wab_ref, alog_ref, dtb_ref, h_ref, gb_ref, *, n_dir_heads):
    x = x_ref[...]
    h = x * lax.rsqrt(jnp.mean(x * x, axis=-1, keepdims=True) + EPS) * g_ref[...]
    hb = h.astype(BF16)
    h_ref[...] = hb
    ab = lax.dot_general(hb, wab_ref[...], (((1,), (1,)), ((), ())), preferred_element_type=F32)
    g = -jnp.exp(alog_ref[...]) * _softplus(ab + dtb_ref[...])
    beta = _sigmoid(ab)
    tm = x.shape[0]
    row = lax.broadcasted_iota(jnp.int32, (tm, tm), 0)
    col = lax.broadcasted_iota(jnp.int32, (tm, tm), 1)
    same = (row // GDN_CHUNK) == (col // GDN_CHUNK)
    prefix = jnp.where(same & (col <= row), 1.0, 0.0).astype(BF16)
    suffix = jnp.where(same & (col >= row), 1.0, 0.0).astype(BF16)
    pieces = _split3(g)
    cs_f = sum(jnp.dot(prefix, p, preferred_element_type=F32) for p in pieces)
    cs_b = sum(jnp.dot(suffix, p, preferred_element_type=F32) for p in pieces)
    lane = lax.broadcasted_iota(jnp.int32, g.shape, 1)
    gb_ref[...] = jnp.where(lane < n_dir_heads // 2, cs_f, jnp.where(lane < n_dir_heads, cs_b, beta))


def _norm_proj(x2, g_mix, w_ab, alog_pad, dtb_pad, n_dir_heads, tm):
    T, D = x2.shape
    return pl.pallas_call(
        functools.partial(_norm_proj_kernel, n_dir_heads=n_dir_heads),
        grid=(T // tm,),
        in_specs=[pl.BlockSpec((tm, D), lambda i: (i, 0)),
                  _const_spec((1, D)), _const_spec((LANE, D)), _const_spec((1, LANE)), _const_spec((1, LANE))],
        out_specs=[pl.BlockSpec((tm, D), lambda i: (i, 0)),
                   pl.BlockSpec((tm, LANE), lambda i: (i, 0))],
        out_shape=[jax.ShapeDtypeStruct((T, D), BF16),
                   jax.ShapeDtypeStruct((T, LANE), F32)],
        compiler_params=_cparams("parallel"),
        name="norm_proj",
    )(x2, g_mix, w_ab, alog_pad, dtb_pad)


REORDER_ROWS = 512


def _reorder_cast_kernel(w_ref, o_ref, narrow_ref, *, segments, narrow):
    n_start, n_width = narrow
    row = lax.broadcasted_iota(jnp.int32, narrow_ref.shape, 0)
    narrow_ref[...] = jnp.where(row < n_width, w_ref[n_start:n_start + LANE, :], 0.0).astype(narrow_ref.dtype)
    off = 0
    for start, width in segments:
        for c in range(0, width, REORDER_ROWS):
            n = min(REORDER_ROWS, width - c)
            o_ref[off:off + n, :] = w_ref[start + c:start + c + n, :].astype(o_ref.dtype)
            off += n


def _reorder_cast(wT, segments, narrow, tc):
    rows, cols = wT.shape
    out_rows = sum(width for _, width in segments)
    assert narrow[0] % 8 == 0 and narrow[1] <= LANE and narrow[0] + LANE <= rows
    assert all(start % 8 == 0 and width % 16 == 0 for start, width in segments)
    return pl.pallas_call(
        functools.partial(_reorder_cast_kernel, segments=segments, narrow=narrow),
        grid=(cols // tc,),
        in_specs=[pl.BlockSpec((rows, tc), lambda i: (0, i))],
        out_specs=[pl.BlockSpec((out_rows, tc), lambda i: (0, i)), pl.BlockSpec((LANE, tc), lambda i: (0, i))],
        out_shape=[jax.ShapeDtypeStruct((out_rows, cols), BF16), jax.ShapeDtypeStruct((LANE, cols), BF16)],
        compiler_params=_cparams("parallel"),
        name="reorder_cast",
    )(wT)


_NT_DIMS = (((1,), (1,)), ((), ()))


def _matmul_nt_kernel(a_ref, bT_ref, o_ref):
    o_ref[...] = lax.dot_general(a_ref[...], bT_ref[...], _NT_DIMS, preferred_element_type=F32).astype(o_ref.dtype)


def _matmul_nt(a, bT, tm, tn, out_dtype):
    M, K = a.shape
    N = bT.shape[0]
    return pl.pallas_call(
        _matmul_nt_kernel,
        grid=(N // tn, M // tm),
        in_specs=[pl.BlockSpec((tm, K), lambda j, i: (i, 0)),
                  pl.BlockSpec((tn, K), lambda j, i: (j, 0))],
        out_specs=pl.BlockSpec((tm, tn), lambda j, i: (i, j)),
        out_shape=jax.ShapeDtypeStruct((M, N), out_dtype),
        compiler_params=_cparams("parallel", "parallel"),
        name="in_proj",
    )(a, bT)


def _bmm_tn(a, b):
    return jnp.stack([lax.dot_general(a[n], b[n], (((0,), (0,)), ((), ())), preferred_element_type=F32)
                      for n in range(a.shape[0])])


def _gdn_kernel(q_ref, k_ref, v_ref, z_ref, cwq_ref, cwk_ref, cwv_ref, gb_ref, grow_ref, nw_ref,
                y_ref,
                xpad, qn, kn, vn, u_s, w_s, qg_s, at_s, tm_s, tn_s, egl_s, st_s,
                *, n_heads, seq, conv_rows, prep_chunks):
    C = GDN_CHUNK
    n_chunks = seq // C
    head = pl.program_id(1)
    pad = 8
    half = GDN_CONV // 2

    zeros_pad = jnp.zeros((pad, LANE), F32)
    xpad[0:pad, :] = zeros_pad
    xpad[pad + seq:pad + seq + pad, :] = zeros_pad
    for src, cw_ref, dst, mode in ((q_ref, cwq_ref, qn, "q"), (k_ref, cwk_ref, kn, "k"), (v_ref, cwv_ref, vn, "v")):
        xpad[pad:pad + seq, :] = src[...].astype(F32)
        for t in range(seq // conv_rows):
            r0 = t * conv_rows
            acc = jnp.zeros((conv_rows, LANE), F32)
            for j in range(GDN_CONV):
                lo = pad + r0 + j - half
                acc = acc + xpad[lo:lo + conv_rows, :] * cw_ref[j:j + 1, :]
            y = acc * _sigmoid(acc)
            if mode != "v":
                y = y * lax.rsqrt(jnp.sum(y * y, axis=-1, keepdims=True) + EPS)
            if mode == "q":
                y = y * (GDN_DK ** -0.5)
            dst[r0:r0 + conv_rows, :] = y

    nb = prep_chunks
    R = nb * C
    nb2 = 2 * nb
    bi = lax.broadcasted_iota(jnp.int32, (nb2, C, C), 0)
    ii = lax.broadcasted_iota(jnp.int32, (nb2, C, C), 1)
    jj = lax.broadcasted_iota(jnp.int32, (nb2, C, C), 2)
    fwd = bi < nb
    incl = (fwd & (ii >= jj)) | (~fwd & (ii <= jj))
    strict = (fwd & (ii > jj)) | (~fwd & (ii < jj))
    eye = jnp.where(ii == jj, 1.0, 0.0).astype(F32)
    lane = lax.broadcasted_iota(jnp.int32, (R, LANE), 1)
    both = lambda x: jnp.concatenate([x, x], axis=0)

    def prep(t, carry):
        r0 = pl.multiple_of(t * R, R)
        n0 = pl.multiple_of(t * nb, nb)
        q3 = qn[pl.ds(r0, R), :].reshape(nb, C, LANE)
        k3 = kn[pl.ds(r0, R), :].reshape(nb, C, LANE)
        v3 = vn[pl.ds(r0, R), :].reshape(nb, C, LANE)
        gb_blk = gb_ref[pl.ds(r0, R), :]
        kq = jnp.concatenate([k3, q3], axis=1).astype(BF16)
        gram = jnp.einsum("nik,njk->nij", kq, k3.astype(BF16), preferred_element_type=F32)
        kk, qk = both(gram[:, :C, :]), both(gram[:, C:, :])
        gcb, bt, grow, gl = [], [], [], []
        for d in range(2):
            colidx = d * n_heads + head
            gcb_d = jnp.sum(jnp.where(lane == colidx, gb_blk, 0.0), axis=-1, keepdims=True).reshape(nb, C, 1)
            gcb.append(gcb_d)
            bt.append(jnp.sum(jnp.where(lane == 2 * n_heads + colidx, gb_blk, 0.0), axis=-1,
                              keepdims=True).reshape(nb, C, 1))
            grow.append(grow_ref[0, colidx, pl.ds(n0, nb), :].reshape(nb, 1, C))
            gl.append(gcb_d[:, C - 1:C, :] if d == 0 else gcb_d[:, 0:1, :])
        gcb, bt, grow, gl = (jnp.concatenate(x, axis=0) for x in (gcb, bt, grow, gl))
        k2, q2, v2 = both(k3), both(q3), both(v3)

        decay = jnp.exp(jnp.where(incl, gcb - grow, -jnp.inf))
        L = jnp.where(strict, bt * kk * decay, 0.0)
        attn = qk * decay
        ainv = eye - L
        P = L
        for _ in range(int(math.log2(C)) - 1):
            Pb = P.astype(BF16)
            P = jnp.einsum("nij,njk->nik", Pb, Pb, preferred_element_type=F32)
            ainv = ainv + jnp.einsum("nij,njk->nik", ainv.astype(BF16), P.astype(BF16),
                                     preferred_element_type=F32)
        eg = jnp.exp(gcb)
        rhs = jnp.concatenate([v2 * bt, k2 * (bt * eg)], axis=-1).astype(BF16)
        uw = jnp.einsum("nij,njd->nid", ainv.astype(BF16), rhs, preferred_element_type=F32).astype(BF16)
        kg = (k2 * jnp.exp(gl - gcb)).astype(BF16)
        trans = _bmm_tn(kg, uw)
        qg = (q2 * eg).astype(BF16)
        egl = jnp.broadcast_to(jnp.exp(gl).reshape(nb2, 1), (nb2, LANE))
        for d in range(2):
            sl = slice(d * nb, (d + 1) * nb)
            u_s[d, pl.ds(r0, R), :] = uw[sl, :, :LANE].reshape(R, LANE)
            w_s[d, pl.ds(r0, R), :] = uw[sl, :, LANE:].reshape(R, LANE)
            qg_s[d, pl.ds(r0, R), :] = qg[sl].reshape(R, LANE)
            at_s[d, pl.ds(r0, R), :] = attn[sl].reshape(R, C).astype(BF16)
            tn_s[d, pl.ds(n0 * GDN_DK, nb * GDN_DK), :] = trans[sl, :, :LANE].reshape(nb * GDN_DK, LANE).astype(BF16)
            tm_s[d, pl.ds(n0 * GDN_DK, nb * GDN_DK), :] = (-trans[sl, :, LANE:]).reshape(nb * GDN_DK, LANE).astype(BF16)
            egl_s[d, pl.ds(n0, nb), :] = egl[sl]
        return carry

    lax.fori_loop(0, n_chunks // nb, prep, 0)

    def chunk_step(d, n, state):
        r = pl.multiple_of(n * GDN_DK, GDN_DK)
        sb = state.astype(BF16)
        st_s[d, pl.ds(r, GDN_DK), :] = sb
        return (state * egl_s[d, pl.ds(n, 1), :]
                + jnp.dot(tm_s[d, pl.ds(r, GDN_DK), :], sb, preferred_element_type=F32)
                + tn_s[d, pl.ds(r, GDN_DK), :].astype(F32))

    def scan(n, carry):
        sf, sbw = carry
        return chunk_step(0, n, sf), chunk_step(1, n_chunks - 1 - n, sbw)

    zero_state = jnp.zeros((GDN_DK, GDN_DV), F32)
    lax.fori_loop(0, n_chunks, scan, (zero_state, zero_state))

    def emit(t, carry):
        r0 = pl.multiple_of(t * R, R)
        s0 = pl.multiple_of(t * nb * GDN_DK, nb * GDN_DK)
        two = lambda ref, start, rows: jnp.concatenate([ref[0, pl.ds(start, rows), :], ref[1, pl.ds(start, rows), :]])
        st = two(st_s, s0, nb * GDN_DK).reshape(nb2, GDN_DK, GDN_DV)
        wq = jnp.concatenate([two(w_s, r0, R).reshape(nb2, C, LANE), two(qg_s, r0, R).reshape(nb2, C, LANE)], axis=1)
        ws_qs = jnp.einsum("ncd,nde->nce", wq, st, preferred_element_type=F32)
        vnew = two(u_s, r0, R).reshape(nb2, C, LANE).astype(F32) - ws_qs[:, :C, :]
        o = ws_qs[:, C:, :] + jnp.einsum("ncs,nse->nce", two(at_s, r0, R).reshape(nb2, C, C), vnew.astype(BF16),
                                         preferred_element_type=F32)
        o = (o[:nb] + o[nb:]).reshape(R, LANE)
        on = o * lax.rsqrt(jnp.mean(o * o, axis=-1, keepdims=True) + EPS) * nw_ref[...]
        z = z_ref[pl.ds(r0, R), :].astype(F32)
        y_ref[pl.ds(r0, R), :] = (on * (z * _sigmoid(z))).astype(y_ref.dtype)
        return carry

    lax.fori_loop(0, n_chunks // nb, emit, 0)


def _gdn(p, p_off, conv_w8, gb, grow, norm_w, batch, seq, n_heads):
    T = batch * seq
    qk_blocks = n_heads
    n_chunks = seq // GDN_CHUNK
    conv_rows = min(512, seq)
    prep_chunks = min(16, n_chunks)
    blk = lambda off: pl.BlockSpec((seq, LANE), lambda b, h: (b, p_off + off + h))
    cw = lambda off: pl.BlockSpec((8, LANE), lambda b, h: (0, off + h))
    return pl.pallas_call(
        functools.partial(_gdn_kernel, n_heads=n_heads, seq=seq, conv_rows=conv_rows, prep_chunks=prep_chunks),
        grid=(batch, n_heads),
        in_specs=[blk(0), blk(qk_blocks), blk(2 * qk_blocks), blk(3 * qk_blocks),
                  cw(0), cw(qk_blocks), cw(2 * qk_blocks),
                  pl.BlockSpec((seq, LANE), lambda b, h: (b, 0), pipeline_mode=pl.Buffered(1)),
                  pl.BlockSpec((1, 2 * n_heads, n_chunks, GDN_CHUNK), lambda b, h: (b, 0, 0, 0),
                               pipeline_mode=pl.Buffered(1)),
                  _const_spec((1, LANE))],
        out_specs=pl.BlockSpec((seq, LANE), lambda b, h: (b, h)),
        out_shape=jax.ShapeDtypeStruct((T, n_heads * GDN_DV), BF16),
        scratch_shapes=[
            pltpu.VMEM((seq + 16, LANE), F32),
            pltpu.VMEM((seq, LANE), F32),
            pltpu.VMEM((seq, LANE), F32),
            pltpu.VMEM((seq, LANE), F32),
            pltpu.VMEM((2, seq, LANE), BF16),
            pltpu.VMEM((2, seq, LANE), BF16),
            pltpu.VMEM((2, seq, LANE), BF16),
            pltpu.VMEM((2, seq, GDN_CHUNK), BF16),
            pltpu.VMEM((2, n_chunks * GDN_DK, GDN_DV), BF16),
            pltpu.VMEM((2, n_chunks * GDN_DK, GDN_DV), BF16),
            pltpu.VMEM((2, max(n_chunks, 8), LANE), F32),
            pltpu.VMEM((2, n_chunks * GDN_DK, GDN_DV), BF16),
        ],
        compiler_params=_cparams("parallel", "parallel"),
        name="gdn",
    )(p, p, p, p, conv_w8, conv_w8, conv_w8, gb, grow, norm_w)


def _diff_attn_kernel(slopes_ref, q_ref, k_ref, v_ref, lam_ref, nw_ref, o_ref,
                      qT_s, vT_s, klo_s, khi_s, rel_s, s_s, p_s, acc_s,
                      *, seq, tq, tk, lam_init):
    head = pl.program_id(1)
    slope2 = slopes_ref[head] * LOG2E
    scale2 = (DIFF_DH ** -0.5) * LOG2E
    n_kv = seq // tk
    maps = (klo_s, khi_s)

    rows = min(512, seq)
    for t in range(seq // rows):
        r0 = t * rows
        qt = (q_ref[r0:r0 + rows, :].astype(F32) * scale2).T.astype(BF16)
        frow = lax.broadcasted_iota(jnp.int32, (LANE, rows), 0)
        fcol = lax.broadcasted_iota(jnp.int32, (LANE, rows), 1)
        qa, qb, qc = (x.astype(F32) for x in _split3(((r0 + fcol) % tq).astype(F32) * slope2))
        qfeat = jnp.where(frow < 3, 1.0, jnp.where(frow == 3, -qa, jnp.where(frow == 4, -qb,
                                                   jnp.where(frow == 5, -qc, 0.0)))).astype(BF16)
        for sgn, feat in enumerate((qfeat, -qfeat)):
            qT_s[sgn, 0:LANE, r0:r0 + rows] = qt
            qT_s[sgn, LANE:2 * LANE, r0:r0 + rows] = feat
        vT_s[0:LANE, r0:r0 + rows] = v_ref[r0:r0 + rows, :].astype(F32).T.astype(BF16)
        vT_s[LANE:LANE + 16, r0:r0 + rows] = jnp.ones((16, rows), BF16)
        kt = k_ref[r0:r0 + rows, :]
        krow = lax.broadcasted_iota(jnp.int32, (rows, LANE), 0)
        lane = lax.broadcasted_iota(jnp.int32, (rows, LANE), 1)
        ka, kb, kc = (x.astype(F32) for x in _split3(((r0 + krow) % tk).astype(F32) * slope2))
        kfeat = jnp.where(lane == 0, ka, jnp.where(lane == 1, kb, jnp.where(lane == 2, kc,
                                                   jnp.where(lane < 6, 1.0, 0.0)))).astype(BF16)
        klo_s[r0:r0 + rows, 0:LANE] = jnp.where(lane < DIFF_DH, kt, jnp.zeros_like(kt))
        khi_s[r0:r0 + rows, 0:LANE] = jnp.where(lane >= DIFF_DH, kt, jnp.zeros_like(kt))
        klo_s[r0:r0 + rows, LANE:2 * LANE] = kfeat
        khi_s[r0:r0 + rows, LANE:2 * LANE] = kfeat
    kr = lax.broadcasted_iota(jnp.int32, (tk, tq), 0)
    qcol = lax.broadcasted_iota(jnp.int32, (tk, tq), 1)
    rel_s[...] = (qcol - kr).astype(F32) * slope2

    lf = lam_ref[...]
    lam = (jnp.exp(jnp.sum(lf[0:1, :] * lf[1:2, :], axis=-1, keepdims=True))
           - jnp.exp(jnp.sum(lf[2:3, :] * lf[3:4, :], axis=-1, keepdims=True)) + lam_init)

    def q_tile(i, carry):
        c_q = pl.multiple_of(i * tq, tq)
        j_diag = (i * tq) // tk

        def scores(t):
            j = j_diag if t == 0 else lax.rem(j_diag + t, n_kv)
            r_k = pl.multiple_of(j * tk, tk)
            d0 = jnp.asarray(i * tq - j * tk)
            if t == 0:
                bias = jnp.abs(rel_s[...] + d0.astype(F32) * slope2)
                for mp, k_s in enumerate(maps):
                    s_s[t % 2, mp] = jnp.dot(k_s[pl.ds(r_k, tk), 0:LANE], qT_s[0, 0:LANE, pl.ds(c_q, tq)],
                                             preferred_element_type=F32) - bias
                return r_k, 0.0
            sgn = jnp.asarray(j > j_diag).astype(jnp.int32)
            for mp, k_s in enumerate(maps):
                s_s[t % 2, mp] = jnp.dot(k_s[pl.ds(r_k, tk), :], qT_s[sgn, :, pl.ds(c_q, tq)],
                                         preferred_element_type=F32)
            return r_k, jnp.abs(d0).astype(F32) * slope2

        def weighted_values(t, r_k, alphas):
            vT = vT_s[:, pl.ds(r_k, tk)]
            for mp in range(len(maps)):
                upd = jnp.dot(vT, p_s[t % 2, mp], preferred_element_type=F32)
                acc_s[mp] = upd if t == 0 else acc_s[mp] * alphas[mp] + upd

        m = [jnp.full((1, tq), NEG_BIG, F32) for _ in maps]
        nxt = scores(0)
        prev = None
        for t in range(n_kv):
            r_k, const = nxt
            if t + 1 < n_kv:
                nxt = scores(t + 1)
            alphas = []
            for mp in range(len(maps)):
                m_new = jnp.maximum(m[mp], jnp.max(s_s[t % 2, mp], axis=0, keepdims=True) - const)
                alphas.append(jnp.exp2(m[mp] - m_new))
                p_s[t % 2, mp] = jnp.exp2(s_s[t % 2, mp] - (m_new + const)).astype(BF16)
                m[mp] = m_new
            if prev is not None:
                weighted_values(t - 1, *prev)
            prev = (r_k, alphas)
        weighted_values(n_kv - 1, *prev)

        o1, o2 = acc_s[0], acc_s[1]
        od = o1[0:DIFF_DV] / o1[DIFF_DV:DIFF_DV + 1] - lam * (o2[0:DIFF_DV] / o2[DIFF_DV:DIFF_DV + 1])
        yn = od * lax.rsqrt(jnp.mean(od * od, axis=0, keepdims=True) + EPS) * nw_ref[...] * (1.0 - lam_init)
        o_ref[pl.ds(c_q, tq), :] = yn.T.astype(o_ref.dtype)
        return carry

    lax.fori_loop(0, seq // tq, q_tile, 0)


def _diff_attn(p, slopes, diff_lambda, norm_w_col, batch, seq, n_heads, q_off, k_off, v_off, lam_init):
    T = batch * seq
    tq = min(512, seq)
    tk = min(512, seq)
    blk = lambda off: pl.BlockSpec((seq, LANE), lambda b, h: (b, off + h))
    return pl.pallas_call(
        functools.partial(_diff_attn_kernel, seq=seq, tq=tq, tk=tk, lam_init=lam_init),
        grid=(batch, n_heads),
        in_specs=[pl.BlockSpec(memory_space=pltpu.SMEM),
                  blk(q_off), blk(k_off), blk(v_off),
                  _const_spec((4, DIFF_DH)), _const_spec((DIFF_DV, 1))],
        out_specs=pl.BlockSpec((seq, LANE), lambda b, h: (b, h)),
        out_shape=jax.ShapeDtypeStruct((T, n_heads * DIFF_DV), BF16),
        scratch_shapes=[
            pltpu.VMEM((2, 2 * LANE, seq), BF16),
            pltpu.VMEM((DIFF_DV + 16, seq), BF16),
            pltpu.VMEM((seq, 2 * LANE), BF16),
            pltpu.VMEM((seq, 2 * LANE), BF16),
            pltpu.VMEM((tk, tq), F32),
            pltpu.VMEM((2, 2, tk, tq), F32),
            pltpu.VMEM((2, 2, tk, tq), BF16),
            pltpu.VMEM((2, DIFF_DV + 16, tq), F32),
        ],
        compiler_params=_cparams("parallel", "parallel"),
        name="diff_attn",
    )(slopes, p, p, p, diff_lambda, norm_w_col)


def _merge_kernel(ya_ref, yd_ref, ga_ref, gd_ref, x_ref, wa_ref, wd_ref, wo_ref, gffn_ref,
                  wr_hi_ref, wr_lo_ref, br_ref, x1_ref, h2_ref, route_ref, count_ref, *, n_groups, per_group):
    ma = jnp.dot(ya_ref[...], wa_ref[...], preferred_element_type=F32)
    md = jnp.dot(yd_ref[...], wd_ref[...], preferred_element_type=F32)
    merged = _sigmoid(ga_ref[...].astype(F32)) * ma + _sigmoid(gd_ref[...].astype(F32)) * md
    x1 = x_ref[...] + jnp.dot(merged.astype(BF16), wo_ref[...], preferred_element_type=F32)
    x1_ref[...] = x1
    h2 = x1 * lax.rsqrt(jnp.mean(x1 * x1, axis=-1, keepdims=True) + EPS) * gffn_ref[...]
    h2_ref[...] = _pack_pairs(h2)

    h_hi = h2.astype(BF16)
    h_lo = (h2 - h_hi.astype(F32)).astype(BF16)
    logits = (jnp.dot(h_hi, wr_hi_ref[...], preferred_element_type=F32)
              + jnp.dot(h_hi, wr_lo_ref[...], preferred_element_type=F32)
              + jnp.dot(h_lo, wr_hi_ref[...], preferred_element_type=F32)) + br_ref[...]
    lane = lax.broadcasted_iota(jnp.int32, logits.shape, 1)
    lane_f = lane.astype(F32)
    big = float(LANE)

    def first_argmax(vals, vmax):
        return jnp.min(jnp.where(vals == vmax, lane_f, big), axis=-1, keepdims=True)

    gl = jnp.where(lane < n_groups, logits, -jnp.inf)
    gmax = jnp.max(gl, axis=-1, keepdims=True)
    g_idx = first_argmax(gl, gmax)
    g_w = 1.0 / jnp.sum(jnp.exp(gl - gmax), axis=-1, keepdims=True)
    e_lo = n_groups + g_idx * per_group
    in_group = (lane_f >= e_lo) & (lane_f < e_lo + per_group)
    el = jnp.where(in_group, logits, -jnp.inf)
    emax = jnp.max(el, axis=-1, keepdims=True)
    pe = jnp.exp(el - emax)
    pe = pe / jnp.sum(pe, axis=-1, keepdims=True)
    p1 = jnp.max(pe, axis=-1, keepdims=True)
    i1 = first_argmax(jnp.where(in_group, pe, -1.0), p1)
    rest = jnp.where(in_group & (lane_f != i1), pe, -1.0)
    p2 = jnp.max(rest, axis=-1, keepdims=True)
    i2 = first_argmax(rest, p2)
    denom = p1 + p2
    w1 = p1 / denom * g_w
    w2 = p2 / denom * g_w
    @pl.when(pl.program_id(0) == 0)
    def _():
        count_ref[...] = jnp.zeros_like(count_ref)

    hit1 = lane_f == i1
    hit2 = lane_f == i2
    hits = jnp.where(hit1 | hit2, 1.0, 0.0)
    tm = hits.shape[0]
    earlier = (lax.broadcasted_iota(jnp.int32, (tm, tm), 1) < lax.broadcasted_iota(jnp.int32, (tm, tm), 0))
    before = count_ref[0:1, :] + jnp.dot(jnp.where(earlier, 1.0, 0.0).astype(BF16), hits.astype(BF16),
                                         preferred_element_type=F32)
    rank1 = jnp.sum(jnp.where(hit1, before, 0.0), axis=-1, keepdims=True)
    rank2 = jnp.sum(jnp.where(hit2, before, 0.0), axis=-1, keepdims=True)
    count_ref[...] = count_ref[...] + jnp.sum(hits, axis=0, keepdims=True)

    route = jnp.where(lane == 0, i1 - n_groups, 0.0)
    route = jnp.where(lane == 1, i2 - n_groups, route)
    route = jnp.where(lane == 2, w1, route)
    route = jnp.where(lane == 3, w2, route)
    route = jnp.where(lane == 4, rank1, route)
    route = jnp.where(lane == 5, rank2, route)
    route_ref[...] = route


def _merge(y_a, y_d, p, x2, wa, wd, wo, g_ffn, wr_hi, wr_lo, b_r, n_groups, per_group, tm):
    T, D = x2.shape
    va, vd = y_a.shape[1], y_d.shape[1]
    row = lambda w: pl.BlockSpec((tm, w), lambda i: (i, 0))
    return pl.pallas_call(
        functools.partial(_merge_kernel, n_groups=n_groups, per_group=per_group),
        grid=(T // tm,),
        in_specs=[row(va), row(vd),
                  pl.BlockSpec((tm, D), lambda i: (i, 0)),
                  pl.BlockSpec((tm, D), lambda i: (i, 1)),
                  row(D),
                  _const_spec((va, D)), _const_spec((vd, D)), _const_spec((D, D)), _const_spec((1, D)),
                  _const_spec((D, LANE)), _const_spec((D, LANE)), _const_spec((1, LANE))],
        out_specs=[row(D), row(D // 2), row(LANE), pl.BlockSpec((8, LANE), lambda i: (0, 0))],
        out_shape=[jax.ShapeDtypeStruct((T, D), F32),
                   jax.ShapeDtypeStruct((T, D // 2), jnp.uint32),
                   jax.ShapeDtypeStruct((T, LANE), F32),
                   jax.ShapeDtypeStruct((8, LANE), F32)],
        compiler_params=_cparams("arbitrary"),
        name="merge",
    )(y_a, y_d, p, p, x2, wa, wd, wo, g_ffn, wr_hi, wr_lo, b_r)


def _expert_kernel(tile_expert_ref, tile_flag_ref, tile_rows_ref, next_expert_ref, next2_expert_ref, slot_ref,
                   xs_ref, wg_hbm, wu_hbm, wd_hbm, o_ref,
                   wg_f, wu_f, wd_f, wg_s, wu_s, wd_s, sems):
    i = pl.program_id(0)
    flag = tile_flag_ref[i]

    def weight_copies(expert, slot):
        return [pltpu.make_async_copy(hbm.at[expert], buf.at[slot], sems.at[slot, n])
                for n, (hbm, buf) in enumerate(((wg_hbm, wg_f), (wu_hbm, wu_f), (wd_hbm, wd_f)))]

    def start_weights(expert, slot):
        @pl.when(expert >= 0)
        def _():
            for copy in weight_copies(expert, slot):
                copy.start()

    @pl.when(flag == 2)
    def _():
        slot = slot_ref[i]

        @pl.when(i == 0)
        def _():
            start_weights(tile_expert_ref[i], slot)
            start_weights(next_expert_ref[i], 1 - slot)

        for copy in weight_copies(tile_expert_ref[i], slot):
            copy.wait()
        wg_s[...] = wg_f[slot].astype(BF16)
        wu_s[...] = wu_f[slot].astype(BF16)
        wd_s[...] = wd_f[slot].astype(BF16)
        start_weights(next2_expert_ref[i], slot)

    @pl.when(flag > 0)
    def _():
        row = lax.broadcasted_iota(jnp.int32, xs_ref.shape, 0)
        x = _unpack_pairs(jnp.where(row < tile_rows_ref[i], xs_ref[...], jnp.uint32(0)))
        g = jnp.dot(x, wg_s[...], preferred_element_type=F32)
        u = jnp.dot(x, wu_s[...], preferred_element_type=F32)
        hid = g * _sigmoid(g) * u
        o_ref[...] = _pack_pairs(jnp.dot(hid.astype(BF16), wd_s[...], preferred_element_type=F32))

    @pl.when(flag == 0)
    def _():
        o_ref[...] = jnp.zeros_like(o_ref)


def _experts(tile_plan, xs, w_gate, w_up, w_down, tm):
    n_rows, half = xs.shape
    E, D, Fd = w_gate.shape
    assert D == 2 * half
    n_tiles = n_rows // tm
    hbm = pl.BlockSpec(memory_space=pl.ANY)
    grid_spec = pltpu.PrefetchScalarGridSpec(
        num_scalar_prefetch=len(tile_plan),
        grid=(n_tiles,),
        in_specs=[pl.BlockSpec((tm, half), lambda i, *_: (i, 0)), hbm, hbm, hbm],
        out_specs=pl.BlockSpec((tm, half), lambda i, *_: (i, 0)),
        scratch_shapes=[pltpu.VMEM((2, D, Fd), F32), pltpu.VMEM((2, D, Fd), F32), pltpu.VMEM((2, Fd, D), F32),
                        pltpu.VMEM((D, Fd), BF16), pltpu.VMEM((D, Fd), BF16), pltpu.VMEM((Fd, D), BF16),
                        pltpu.SemaphoreType.DMA((2, 3))],
    )
    return pl.pallas_call(
        _expert_kernel,
        grid_spec=grid_spec,
        out_shape=jax.ShapeDtypeStruct((n_rows, half), jnp.uint32),
        compiler_params=_cparams("arbitrary"),
        name="experts",
    )(*tile_plan, xs, w_gate, w_up, w_down)


SC_CORES = 2
SC_SUBCORES = 16
SC_ROWS = 32
SC_COLLECT_BUFFERS = 3


def _dispatch_rows(table, pos, n_rows):
    T, W = table.shape
    workers = SC_CORES * SC_SUBCORES
    chunks = T // (workers * SC_ROWS)
    assert workers * chunks * SC_ROWS == T
    pos4 = pos.T.reshape(TOP_K, workers, chunks, SC_ROWS).transpose(1, 0, 2, 3)
    mesh = plsc.VectorSubcoreMesh(core_axis_name="c", subcore_axis_name="s")

    @functools.partial(
        pl.kernel, mesh=mesh,
        out_type=jax.ShapeDtypeStruct((n_rows, W), table.dtype),
        scratch_types=[pltpu.VMEM((TOP_K, chunks, SC_ROWS), jnp.int32),
                       pltpu.VMEM((2, SC_ROWS, W), table.dtype),
                       pltpu.SemaphoreType.DMA, pltpu.SemaphoreType.DMA, pltpu.SemaphoreType.DMA],
        name="dispatch_rows",
    )
    def dispatch(table_hbm, pos_hbm, out_hbm, pos_v, rows_v, sem_in, sem_out0, sem_out1):
        wid = lax.axis_index("s") * SC_CORES + lax.axis_index("c")
        base = wid * (chunks * SC_ROWS)
        pltpu.sync_copy(pos_hbm.at[wid], pos_v)

        def load(c):
            return pltpu.async_copy(table_hbm.at[pl.ds(base + c * SC_ROWS, SC_ROWS)], rows_v.at[c % 2], sem_in)

        pending = load(0)
        for c in range(chunks):
            pending.wait()
            if c + 1 < chunks:
                pending = load(c + 1)
            out0 = pltpu.async_copy(rows_v.at[c % 2], out_hbm.at[pos_v.at[0, c]], sem_out0)
            out1 = pltpu.async_copy(rows_v.at[c % 2], out_hbm.at[pos_v.at[1, c]], sem_out1)
            out0.wait()
            out1.wait()

    return dispatch(table, pos4)


def _collect_rows(table, pos):
    T = pos.shape[0]
    W = table.shape[1]
    workers = SC_CORES * SC_SUBCORES
    chunks = T // (workers * SC_ROWS)
    assert workers * chunks * SC_ROWS == T
    pos4 = pos.T.reshape(TOP_K, workers, chunks, SC_ROWS).transpose(1, 0, 2, 3)
    mesh = plsc.VectorSubcoreMesh(core_axis_name="c", subcore_axis_name="s")
    steps = [(c, k) for c in range(chunks) for k in range(TOP_K)]

    @functools.partial(
        pl.kernel, mesh=mesh,
        out_type=jax.ShapeDtypeStruct((TOP_K, T, W), table.dtype),
        scratch_types=[pltpu.VMEM((TOP_K, chunks, SC_ROWS), jnp.int32),
                       pltpu.VMEM((SC_COLLECT_BUFFERS, SC_ROWS, W), table.dtype)]
                      + [pltpu.SemaphoreType.DMA] * SC_COLLECT_BUFFERS,
        name="collect_rows",
    )
    def collect(table_hbm, pos_hbm, out_hbm, pos_v, rows_v, *sems):
        wid = lax.axis_index("s") * SC_CORES + lax.axis_index("c")
        base = wid * (chunks * SC_ROWS)
        pltpu.sync_copy(pos_hbm.at[wid], pos_v)

        def gather(n):
            c, k = steps[n]
            buf = n % SC_COLLECT_BUFFERS
            return pltpu.async_copy(table_hbm.at[pos_v.at[k, c]], rows_v.at[buf], sems[buf])

        in_flight = [gather(n) for n in range(min(SC_COLLECT_BUFFERS - 1, len(steps)))]
        for n, (c, k) in enumerate(steps):
            in_flight.pop(0).wait()
            ahead = n + SC_COLLECT_BUFFERS - 1
            if ahead < len(steps):
                in_flight.append(gather(ahead))
            pltpu.sync_copy(rows_v.at[n % SC_COLLECT_BUFFERS],
                            out_hbm.at[k, pl.ds(base + c * SC_ROWS, SC_ROWS)])

    return collect(table, pos4)


def _final_kernel(x1_ref, y0_ref, y1_ref, route_ref, g_ref, o_ref):
    route = route_ref[...]
    lane = lax.broadcasted_iota(jnp.int32, route.shape, 1)
    w0 = jnp.sum(jnp.where(lane == TOP_K, route, 0.0), axis=-1, keepdims=True)
    w1 = jnp.sum(jnp.where(lane == TOP_K + 1, route, 0.0), axis=-1, keepdims=True)
    y0 = _unpack_pairs(y0_ref[...]).astype(F32)
    y1 = _unpack_pairs(y1_ref[...]).astype(F32)
    x = x1_ref[...] + (w0 * y0 + w1 * y1)
    o_ref[...] = x * lax.rsqrt(jnp.mean(x * x, axis=-1, keepdims=True) + EPS) * g_ref[...]


def _final(x1, ys_tok, route, g_final, tm):
    T, D = x1.shape
    row = pl.BlockSpec((tm, D), lambda i: (i, 0))
    y_spec = lambda k: pl.BlockSpec((None, tm, D // 2), lambda i: (k, i, 0))
    return pl.pallas_call(
        _final_kernel,
        grid=(T // tm,),
        in_specs=[row, y_spec(0), y_spec(1), pl.BlockSpec((tm, LANE), lambda i: (i, 0)), _const_spec((1, D))],
        out_specs=row,
        out_shape=jax.ShapeDtypeStruct((T, D), F32),
        compiler_params=_cparams("parallel"),
        name="final",
    )(x1, ys_tok, ys_tok, route, g_final)


def _route_tables(ids, rank, counts, tm):
    T = ids.shape[0]
    A = T * TOP_K
    n_experts = counts.shape[0]
    padded = ((counts + tm - 1) // tm) * tm
    ends = jnp.cumsum(padded)
    starts = ends - padded
    onehot = ids[:, :, None] == jnp.arange(n_experts, dtype=jnp.int32)[None, None, :]
    pos = jnp.sum(jnp.where(onehot, starts[None, None, :], 0), axis=-1) + rank
    n_tiles = A // tm + n_experts
    tile_start = jnp.arange(n_tiles, dtype=jnp.int32) * tm
    tile_expert_raw = jnp.sum((ends[None, :] <= tile_start[:, None]).astype(jnp.int32), axis=1)
    valid = tile_start < ends[-1]
    last_expert = jnp.max(jnp.where(counts > 0, jnp.arange(n_experts, dtype=jnp.int32), 0))
    tile_expert = jnp.where(valid, jnp.minimum(tile_expert_raw, n_experts - 1), last_expert)
    first = jnp.concatenate([jnp.ones((1,), bool), tile_expert[1:] != tile_expert[:-1]])
    tile_flag = jnp.where(valid, jnp.where(first, 2, 1), 0).astype(jnp.int32)
    tile_rows = jnp.clip((starts + counts)[tile_expert] - tile_start, 0, tm)
    tile_rows = jnp.where(valid, tile_rows, 0).astype(jnp.int32)
    e_idx = jnp.arange(n_experts, dtype=jnp.int32)
    nonempty = counts > 0
    later = (e_idx[None, :] > e_idx[:, None]) & nonempty[None, :]
    next_nonempty = jnp.min(jnp.where(later, e_idx[None, :], n_experts), axis=1)
    next_nonempty = jnp.where(next_nonempty < n_experts, next_nonempty, -1).astype(jnp.int32)
    order = jnp.sum(((e_idx[None, :] < e_idx[:, None]) & nonempty[None, :]).astype(jnp.int32), axis=1)
    next_expert = next_nonempty[tile_expert]
    next2_expert = jnp.where(next_expert >= 0, next_nonempty[jnp.maximum(next_expert, 0)], -1)
    slot = (order[tile_expert] % 2).astype(jnp.int32)
    return pos, tile_expert, tile_flag, tile_rows, next_expert, next2_expert, slot


def _largest_tile(n, cap):
    t = min(n, cap)
    while n % t:
        t //= 2
    return t


def kernel(x, g_mix, w_in, conv_w, a_log, dt_bias, gdn_norm_w, diff_lambda, diff_norm_w, w_branch_a, w_branch_d,
           w_out, g_ffn, w_group, b_group, w_router, b_router, w_exp_gate, w_exp_up, w_exp_down, g_final):
    B, S, D = x.shape
    T = B * S
    depth = g_mix.shape[0]
    Hg = a_log.shape[-1]
    gdn_qk = Hg * GDN_DK
    gdn_v = Hg * GDN_DV
    diff_v = w_branch_d.shape[1]
    Hd = diff_v // DIFF_DV
    diff_qk = Hd * 2 * DIFF_DH
    n_groups = w_group.shape[-1]
    n_experts = w_router.shape[-1]
    per_group = n_experts // n_groups
    assert 4 * Hg <= LANE and n_groups + n_experts <= LANE
    assert S % GDN_CHUNK == 0 and D % LANE == 0

    ab_lo = 2 * gdn_qk + 2 * gdn_v
    ab_hi = ab_lo + 4 * Hg
    gates_lo = ab_hi + 2 * diff_qk + diff_v
    main_cols = 2 * D + ab_lo + 2 * diff_qk + diff_v
    gdn_off = 2 * D
    q_d_off = gdn_off + ab_lo
    k_d_off = q_d_off + diff_qk
    v_d_off = k_d_off + diff_qk

    tm_norm = _largest_tile(T, 512)
    tm_proj = _largest_tile(T, 2048)
    tn_proj = _largest_tile(main_cols, 1024)
    tm_merge = _largest_tile(T, 512)
    tm_exp = _largest_tile(T * TOP_K, 256)
    tm_final = _largest_tile(T, 512)

    slopes = jnp.exp2(-8.0 * (jnp.arange(Hd, dtype=F32) + 1.0) / Hd)
    x2 = x.reshape(T, D)
    for layer in range(depth):
        lam_init = 0.8 - 0.6 * math.exp(-0.3 * layer)
        w_in_l = w_in[layer]
        w_main, w_ab = _reorder_cast(w_in_l.T, ((gates_lo, 2 * D), (0, ab_lo), (ab_hi, gates_lo - ab_hi)),
                                     (ab_lo, 4 * Hg), _largest_tile(D, 256))
        alog_pad = jnp.pad(a_log[layer].reshape(1, 2 * Hg), ((0, 0), (0, LANE - 2 * Hg)))
        dtb_pad = jnp.pad(dt_bias[layer].reshape(1, 2 * Hg), ((0, 0), (0, LANE - 2 * Hg)))

        h, gb = _norm_proj(x2, g_mix[layer].reshape(1, D), w_ab, alog_pad, dtb_pad, 2 * Hg, tm_norm)
        p = _matmul_nt(h, w_main, tm_proj, tn_proj, BF16)

        grow = gb[:, :2 * Hg].reshape(B, S // GDN_CHUNK, GDN_CHUNK, 2 * Hg).transpose(0, 3, 1, 2)
        conv_w8 = jnp.pad(conv_w[layer], ((0, 8 - GDN_CONV), (0, 0)))
        y_a = _gdn(p, gdn_off // LANE, conv_w8, gb, grow, gdn_norm_w[layer].reshape(1, GDN_DV), B, S, Hg)
        y_d = _diff_attn(p, slopes, diff_lambda[layer], diff_norm_w[layer].reshape(DIFF_DV, 1), B, S, Hd,
                         q_d_off // LANE, k_d_off // LANE, v_d_off // LANE, lam_init)

        w_r = jnp.pad(jnp.concatenate([w_group[layer], w_router[layer]], axis=1),
                      ((0, 0), (0, LANE - n_groups - n_experts)))
        wr_hi = w_r.astype(BF16)
        wr_lo = (w_r - wr_hi.astype(F32)).astype(BF16)
        b_r = jnp.pad(jnp.concatenate([b_group[layer], b_router[layer]]).reshape(1, -1),
                      ((0, 0), (0, LANE - n_groups - n_experts)))
        x1, h2, route, counts = _merge(y_a, y_d, p, x2,
                                       w_branch_a[layer].astype(BF16), w_branch_d[layer].astype(BF16),
                                       w_out[layer].astype(BF16), g_ffn[layer].reshape(1, D),
                                       wr_hi, wr_lo, b_r, n_groups, per_group, tm_merge)

        ids = route[:, 0:TOP_K].astype(jnp.int32)
        rank = route[:, 2 * TOP_K:3 * TOP_K].astype(jnp.int32)
        counts = counts[0, n_groups:n_groups + n_experts].astype(jnp.int32)
        pos, *tile_plan = _route_tables(ids, rank, counts, tm_exp)
        xs = _dispatch_rows(h2, pos, tile_plan[0].shape[0] * tm_exp)
        ys = _experts(tile_plan, xs, w_exp_gate[layer], w_exp_up[layer], w_exp_down[layer], tm_exp)
        ys_tok = _collect_rows(ys, pos)
        if layer + 1 < depth:
            wts = route[:, TOP_K:2 * TOP_K]
            x2 = x1 + (wts[:, 0:1] * _unpack_pairs(ys_tok[0]).astype(F32)
                       + wts[:, 1:2] * _unpack_pairs(ys_tok[1]).astype(F32))
    out = _final(x1, ys_tok, route, g_final.reshape(1, D), tm_final)
    return out.reshape(B, S, D)
```

```python
import functools
import math

import jax
import jax.numpy as jnp
from jax import lax
from jax.experimental import pallas as pl
from jax.experimental.pallas import tpu as pltpu
from jax.experimental.pallas import tpu_sc as plsc

F32 = jnp.float32
BF16 = jnp.bfloat16
EPS = 1e-6
LANE = 128
SUBLANE = 8
PACKED_ROWS = 16
GDN_DK = 128
GDN_DV = 128
GDN_CONV = 5
GDN_CHUNK = 64
DIFF_DH = 64
DIFF_DV = 2 * DIFF_DH
TOP_K = 2
LOG2E = 1.4426950408889634
NEG_BIG = -1e30
VMEM_LIMIT_BYTES = 56 * 1024 * 1024

def _cparams(*sem):
    return pltpu.CompilerParams(dimension_semantics=sem, vmem_limit_bytes=VMEM_LIMIT_BYTES)


def _const_spec(shape):
    nd = len(shape)
    return pl.BlockSpec(shape, lambda *_: (0,) * nd, pipeline_mode=pl.Buffered(1))


def _split3(x):
    hi = x.astype(BF16)
    r = x - hi.astype(F32)
    mid = r.astype(BF16)
    lo = (r - mid.astype(F32)).astype(BF16)
    return hi, mid, lo


def _pack_pairs(x):
    bits = lax.bitcast_convert_type(x.astype(BF16).astype(F32), jnp.uint32)
    half = bits.shape[1] // 2
    return bits[:, :half] | (bits[:, half:] >> 16)


def _unpack_pairs(packed):
    left = lax.bitcast_convert_type(packed & jnp.uint32(0xFFFF0000), F32).astype(BF16)
    right = lax.bitcast_convert_type(packed << 16, F32).astype(BF16)
    return jnp.concatenate([left, right], axis=1)


def _softplus(x):
    return jnp.maximum(x, 0.0) + jnp.log(1.0 + jnp.exp(-jnp.abs(x)))


def _sigmoid(x):
    return 1.0 / (1.0 + jnp.exp(-x))


def _norm_proj_kernel(x_ref, g_ref, wab_ref, alog_ref, dtb_ref, h_ref, gb_ref, *, n_dir_heads):
    x = x_ref[...]
    h = x * lax.rsqrt(jnp.mean(x * x, axis=-1, keepdims=True) + EPS) * g_ref[...]
    hb = h.astype(BF16)
    h_ref[...] = hb
    ab = lax.dot_general(hb, wab_ref[...], (((1,), (1,)), ((), ())), preferred_element_type=F32)
    g = -jnp.exp(alog_ref[...]) * _softplus(ab + dtb_ref[...])
    beta = _sigmoid(ab)
    tm = x.shape[0]
    row = lax.broadcasted_iota(jnp.int32, (tm, tm), 0)
    col = lax.broadcasted_iota(jnp.int32, (tm, tm), 1)
    same = (row // GDN_CHUNK) == (col // GDN_CHUNK)
    prefix = jnp.where(same & (col <= row), 1.0, 0.0).astype(BF16)
    suffix = jnp.where(same & (col >= row), 1.0, 0.0).astype(BF16)
    pieces = _split3(g)
    cs_f = sum(jnp.dot(prefix, p, preferred_element_type=F32) for p in pieces)
    cs_b = sum(jnp.dot(suffix, p, preferred_element_type=F32) for p in pieces)
    lane = lax.broadcasted_iota(jnp.int32, g.shape, 1)
    gb_ref[...] = jnp.where(lane < n_dir_heads // 2, cs_f, jnp.where(lane < n_dir_heads, cs_b, beta))


def _norm_proj(x2, g_mix, w_ab, alog_pad, dtb_pad, n_dir_heads, tm):
    T, D = x2.shape
    return pl.pallas_call(
        functools.partial(_norm_proj_kernel, n_dir_heads=n_dir_heads),
        grid=(T // tm,),
        in_specs=[pl.BlockSpec((tm, D), lambda i: (i, 0)),
                  _const_spec((1, D)), _const_spec((LANE, D)), _const_spec((1, LANE)), _const_spec((1, LANE))],
        out_specs=[pl.BlockSpec((tm, D), lambda i: (i, 0)),
                   pl.BlockSpec((tm, LANE), lambda i: (i, 0))],
        out_shape=[jax.ShapeDtypeStruct((T, D), BF16),
                   jax.ShapeDtypeStruct((T, LANE), F32)],
        compiler_params=_cparams("parallel"),
        name="norm_proj",
    )(x2, g_mix, w_ab, alog_pad, dtb_pad)


REORDER_ROWS = 512


def _reorder_cast_kernel(w_ref, o_ref, narrow_ref, *, segments, narrow):
    n_start, n_width = narrow
    row = lax.broadcasted_iota(jnp.int32, narrow_ref.shape, 0)
    narrow_ref[...] = jnp.where(row < n_width, w_ref[n_start:n_start + LANE, :], 0.0).astype(narrow_ref.dtype)
    off = 0
    for start, width in segments:
        for c in range(0, width, REORDER_ROWS):
            n = min(REORDER_ROWS, width - c)
            o_ref[off:off + n, :] = w_ref[start + c:start + c + n, :].astype(o_ref.dtype)
            off += n


def _reorder_cast(wT, segments, narrow, tc):
    rows, cols = wT.shape
    out_rows = sum(width for _, width in segments)
    assert narrow[0] % SUBLANE == 0 and narrow[1] <= LANE and narrow[0] + LANE <= rows
    assert all(start % SUBLANE == 0 and width % PACKED_ROWS == 0 for start, width in segments)
    return pl.pallas_call(
        functools.partial(_reorder_cast_kernel, segments=segments, narrow=narrow),
        grid=(cols // tc,),
        in_specs=[pl.BlockSpec((rows, tc), lambda i: (0, i))],
        out_specs=[pl.BlockSpec((out_rows, tc), lambda i: (0, i)), pl.BlockSpec((LANE, tc), lambda i: (0, i))],
        out_shape=[jax.ShapeDtypeStruct((out_rows, cols), BF16), jax.ShapeDtypeStruct((LANE, cols), BF16)],
        compiler_params=_cparams("parallel"),
        name="reorder_cast",
    )(wT)


_NT_DIMS = (((1,), (1,)), ((), ()))


def _matmul_nt_kernel(a_ref, bT_ref, o_ref):
    o_ref[...] = lax.dot_general(a_ref[...], bT_ref[...], _NT_DIMS, preferred_element_type=F32).astype(o_ref.dtype)


def _matmul_nt(a, bT, tm, tn, out_dtype):
    M, K = a.shape
    N = bT.shape[0]
    return pl.pallas_call(
        _matmul_nt_kernel,
        grid=(N // tn, M // tm),
        in_specs=[pl.BlockSpec((tm, K), lambda j, i: (i, 0)),
                  pl.BlockSpec((tn, K), lambda j, i: (j, 0))],
        out_specs=pl.BlockSpec((tm, tn), lambda j, i: (i, j)),
        out_shape=jax.ShapeDtypeStruct((M, N), out_dtype),
        compiler_params=_cparams("parallel", "parallel"),
        name="in_proj",
    )(a, bT)


def _bmm_tn(a, b):
    return jnp.stack([lax.dot_general(a[n], b[n], (((0,), (0,)), ((), ())), preferred_element_type=F32)
                      for n in range(a.shape[0])])


def _gdn_kernel(q_ref, k_ref, v_ref, z_ref, cwq_ref, cwk_ref, cwv_ref, gb_ref, grow_ref, nw_ref,
                y_ref,
                xpad, qn, kn, vn, u_s, w_s, qg_s, at_s, tm_s, tn_s, egl_s, st_s,
                *, n_heads, seq, conv_rows, prep_chunks):
    C = GDN_CHUNK
    n_chunks = seq // C
    head = pl.program_id(1)
    pad = SUBLANE
    half = GDN_CONV // 2

    zeros_pad = jnp.zeros((pad, LANE), F32)
    xpad[0:pad, :] = zeros_pad
    xpad[pad + seq:pad + seq + pad, :] = zeros_pad
    for src, cw_ref, dst, mode in ((q_ref, cwq_ref, qn, "q"), (k_ref, cwk_ref, kn, "k"), (v_ref, cwv_ref, vn, "v")):
        xpad[pad:pad + seq, :] = src[...].astype(F32)
        for t in range(seq // conv_rows):
            r0 = t * conv_rows
            acc = jnp.zeros((conv_rows, LANE), F32)
            for j in range(GDN_CONV):
                lo = pad + r0 + j - half
                acc = acc + xpad[lo:lo + conv_rows, :] * cw_ref[j:j + 1, :]
            y = acc * _sigmoid(acc)
            if mode != "v":
                y = y * lax.rsqrt(jnp.sum(y * y, axis=-1, keepdims=True) + EPS)
            if mode == "q":
                y = y * (GDN_DK ** -0.5)
            dst[r0:r0 + conv_rows, :] = y

    nb = prep_chunks
    R = nb * C
    nb2 = 2 * nb
    bi = lax.broadcasted_iota(jnp.int32, (nb2, C, C), 0)
    ii = lax.broadcasted_iota(jnp.int32, (nb2, C, C), 1)
    jj = lax.broadcasted_iota(jnp.int32, (nb2, C, C), 2)
    fwd = bi < nb
    incl = (fwd & (ii >= jj)) | (~fwd & (ii <= jj))
    strict = (fwd & (ii > jj)) | (~fwd & (ii < jj))
    eye = jnp.where(ii == jj, 1.0, 0.0).astype(F32)
    lane = lax.broadcasted_iota(jnp.int32, (R, LANE), 1)
    both = lambda x: jnp.concatenate([x, x], axis=0)

    def prep(t, carry):
        r0 = pl.multiple_of(t * R, R)
        n0 = pl.multiple_of(t * nb, nb)
        q3 = qn[pl.ds(r0, R), :].reshape(nb, C, LANE)
        k3 = kn[pl.ds(r0, R), :].reshape(nb, C, LANE)
        v3 = vn[pl.ds(r0, R), :].reshape(nb, C, LANE)
        gb_blk = gb_ref[pl.ds(r0, R), :]
        kq = jnp.concatenate([k3, q3], axis=1).astype(BF16)
        gram = jnp.einsum("nik,njk->nij", kq, k3.astype(BF16), preferred_element_type=F32)
        kk, qk = both(gram[:, :C, :]), both(gram[:, C:, :])
        gcb, bt, grow, gl = [], [], [], []
        for d in range(2):
            colidx = d * n_heads + head
            gcb_d = jnp.sum(jnp.where(lane == colidx, gb_blk, 0.0), axis=-1, keepdims=True).reshape(nb, C, 1)
            gcb.append(gcb_d)
            bt.append(jnp.sum(jnp.where(lane == 2 * n_heads + colidx, gb_blk, 0.0), axis=-1,
                              keepdims=True).reshape(nb, C, 1))
            grow.append(grow_ref[0, colidx, pl.ds(n0, nb), :].reshape(nb, 1, C))
            gl.append(gcb_d[:, C - 1:C, :] if d == 0 else gcb_d[:, 0:1, :])
        gcb, bt, grow, gl = (jnp.concatenate(x, axis=0) for x in (gcb, bt, grow, gl))
        k2, q2, v2 = both(k3), both(q3), both(v3)

        decay = jnp.exp(jnp.where(incl, gcb - grow, -jnp.inf))
        L = jnp.where(strict, bt * kk * decay, 0.0)
        attn = qk * decay
        ainv = eye - L
        P = L
        for _ in range(int(math.log2(C)) - 1):
            Pb = P.astype(BF16)
            P = jnp.einsum("nij,njk->nik", Pb, Pb, preferred_element_type=F32)
            ainv = ainv + jnp.einsum("nij,njk->nik", ainv.astype(BF16), P.astype(BF16),
                                     preferred_element_type=F32)
        eg = jnp.exp(gcb)
        rhs = jnp.concatenate([v2 * bt, k2 * (bt * eg)], axis=-1).astype(BF16)
        uw = jnp.einsum("nij,njd->nid", ainv.astype(BF16), rhs, preferred_element_type=F32).astype(BF16)
        kg = (k2 * jnp.exp(gl - gcb)).astype(BF16)
        trans = _bmm_tn(kg, uw)
        qg = (q2 * eg).astype(BF16)
        egl = jnp.broadcast_to(jnp.exp(gl).reshape(nb2, 1), (nb2, LANE))
        for d in range(2):
            sl = slice(d * nb, (d + 1) * nb)
            u_s[d, pl.ds(r0, R), :] = uw[sl, :, :LANE].reshape(R, LANE)
            w_s[d, pl.ds(r0, R), :] = uw[sl, :, LANE:].reshape(R, LANE)
            qg_s[d, pl.ds(r0, R), :] = qg[sl].reshape(R, LANE)
            at_s[d, pl.ds(r0, R), :] = attn[sl].reshape(R, C).astype(BF16)
            tn_s[d, pl.ds(n0 * GDN_DK, nb * GDN_DK), :] = trans[sl, :, :LANE].reshape(nb * GDN_DK, LANE).astype(BF16)
            tm_s[d, pl.ds(n0 * GDN_DK, nb * GDN_DK), :] = (-trans[sl, :, LANE:]).reshape(nb * GDN_DK, LANE).astype(BF16)
            egl_s[d, pl.ds(n0, nb), :] = egl[sl]
        return carry

    lax.fori_loop(0, n_chunks // nb, prep, 0)

    def chunk_step(d, n, state):
        r = pl.multiple_of(n * GDN_DK, GDN_DK)
        sb = state.astype(BF16)
        st_s[d, pl.ds(r, GDN_DK), :] = sb
        return (state * egl_s[d, pl.ds(n, 1), :]
                + jnp.dot(tm_s[d, pl.ds(r, GDN_DK), :], sb, preferred_element_type=F32)
                + tn_s[d, pl.ds(r, GDN_DK), :].astype(F32))

    def scan(n, carry):
        sf, sbw = carry
        return chunk_step(0, n, sf), chunk_step(1, n_chunks - 1 - n, sbw)

    zero_state = jnp.zeros((GDN_DK, GDN_DV), F32)
    lax.fori_loop(0, n_chunks, scan, (zero_state, zero_state))

    def emit(t, carry):
        r0 = pl.multiple_of(t * R, R)
        s0 = pl.multiple_of(t * nb * GDN_DK, nb * GDN_DK)
        two = lambda ref, start, rows: jnp.concatenate([ref[0, pl.ds(start, rows), :], ref[1, pl.ds(start, rows), :]])
        st = two(st_s, s0, nb * GDN_DK).reshape(nb2, GDN_DK, GDN_DV)
        wq = jnp.concatenate([two(w_s, r0, R).reshape(nb2, C, LANE), two(qg_s, r0, R).reshape(nb2, C, LANE)], axis=1)
        ws_qs = jnp.einsum("ncd,nde->nce", wq, st, preferred_element_type=F32)
        vnew = two(u_s, r0, R).reshape(nb2, C, LANE).astype(F32) - ws_qs[:, :C, :]
        o = ws_qs[:, C:, :] + jnp.einsum("ncs,nse->nce", two(at_s, r0, R).reshape(nb2, C, C), vnew.astype(BF16),
                                         preferred_element_type=F32)
        o = (o[:nb] + o[nb:]).reshape(R, LANE)
        on = o * lax.rsqrt(jnp.mean(o * o, axis=-1, keepdims=True) + EPS) * nw_ref[...]
        z = z_ref[pl.ds(r0, R), :].astype(F32)
        y_ref[pl.ds(r0, R), :] = (on * (z * _sigmoid(z))).astype(y_ref.dtype)
        return carry

    lax.fori_loop(0, n_chunks // nb, emit, 0)


def _gdn(p, p_off, conv_w8, gb, grow, norm_w, batch, seq, n_heads):
    T = batch * seq
    qk_blocks = n_heads
    n_chunks = seq // GDN_CHUNK
    conv_rows = min(512, seq)
    prep_chunks = min(16, n_chunks)
    blk = lambda off: pl.BlockSpec((seq, LANE), lambda b, h: (b, p_off + off + h))
    cw = lambda off: pl.BlockSpec((SUBLANE, LANE), lambda b, h: (0, off + h))
    return pl.pallas_call(
        functools.partial(_gdn_kernel, n_heads=n_heads, seq=seq, conv_rows=conv_rows, prep_chunks=prep_chunks),
        grid=(batch, n_heads),
        in_specs=[blk(0), blk(qk_blocks), blk(2 * qk_blocks), blk(3 * qk_blocks),
                  cw(0), cw(qk_blocks), cw(2 * qk_blocks),
                  pl.BlockSpec((seq, LANE), lambda b, h: (b, 0), pipeline_mode=pl.Buffered(1)),
                  pl.BlockSpec((1, 2 * n_heads, n_chunks, GDN_CHUNK), lambda b, h: (b, 0, 0, 0),
                               pipeline_mode=pl.Buffered(1)),
                  _const_spec((1, LANE))],
        out_specs=pl.BlockSpec((seq, LANE), lambda b, h: (b, h)),
        out_shape=jax.ShapeDtypeStruct((T, n_heads * GDN_DV), BF16),
        scratch_shapes=[
            pltpu.VMEM((seq + 2 * SUBLANE, LANE), F32),
            pltpu.VMEM((seq, LANE), F32),
            pltpu.VMEM((seq, LANE), F32),
            pltpu.VMEM((seq, LANE), F32),
            pltpu.VMEM((2, seq, LANE), BF16),
            pltpu.VMEM((2, seq, LANE), BF16),
            pltpu.VMEM((2, seq, LANE), BF16),
            pltpu.VMEM((2, seq, GDN_CHUNK), BF16),
            pltpu.VMEM((2, n_chunks * GDN_DK, GDN_DV), BF16),
            pltpu.VMEM((2, n_chunks * GDN_DK, GDN_DV), BF16),
            pltpu.VMEM((2, max(n_chunks, SUBLANE), LANE), F32),
            pltpu.VMEM((2, n_chunks * GDN_DK, GDN_DV), BF16),
        ],
        compiler_params=_cparams("parallel", "parallel"),
        name="gdn",
    )(p, p, p, p, conv_w8, conv_w8, conv_w8, gb, grow, norm_w)


def _diff_attn_kernel(slopes_ref, q_ref, k_ref, v_ref, lam_ref, nw_ref, o_ref,
                      qT_s, vT_s, klo_s, khi_s, rel_s, s_s, p_s, acc_s,
                      *, seq, tq, tk, lam_init):
    head = pl.program_id(1)
    slope2 = slopes_ref[head] * LOG2E
    scale2 = (DIFF_DH ** -0.5) * LOG2E
    n_kv = seq // tk
    maps = (klo_s, khi_s)

    rows = min(512, seq)
    for t in range(seq // rows):
        r0 = t * rows
        qt = (q_ref[r0:r0 + rows, :].astype(F32) * scale2).T.astype(BF16)
        frow = lax.broadcasted_iota(jnp.int32, (LANE, rows), 0)
        fcol = lax.broadcasted_iota(jnp.int32, (LANE, rows), 1)
        qa, qb, qc = (x.astype(F32) for x in _split3(((r0 + fcol) % tq).astype(F32) * slope2))
        qfeat = jnp.where(frow < 3, 1.0, jnp.where(frow == 3, -qa, jnp.where(frow == 4, -qb,
                                                   jnp.where(frow == 5, -qc, 0.0)))).astype(BF16)
        for sgn, feat in enumerate((qfeat, -qfeat)):
            qT_s[sgn, 0:LANE, r0:r0 + rows] = qt
            qT_s[sgn, LANE:2 * LANE, r0:r0 + rows] = feat
        vT_s[0:LANE, r0:r0 + rows] = v_ref[r0:r0 + rows, :].astype(F32).T.astype(BF16)
        vT_s[DIFF_DV:DIFF_DV + PACKED_ROWS, r0:r0 + rows] = jnp.ones((PACKED_ROWS, rows), BF16)
        kt = k_ref[r0:r0 + rows, :]
        krow = lax.broadcasted_iota(jnp.int32, (rows, LANE), 0)
        lane = lax.broadcasted_iota(jnp.int32, (rows, LANE), 1)
        ka, kb, kc = (x.astype(F32) for x in _split3(((r0 + krow) % tk).astype(F32) * slope2))
        kfeat = jnp.where(lane == 0, ka, jnp.where(lane == 1, kb, jnp.where(lane == 2, kc,
                                                   jnp.where(lane < 6, 1.0, 0.0)))).astype(BF16)
        klo_s[r0:r0 + rows, 0:LANE] = jnp.where(lane < DIFF_DH, kt, jnp.zeros_like(kt))
        khi_s[r0:r0 + rows, 0:LANE] = jnp.where(lane >= DIFF_DH, kt, jnp.zeros_like(kt))
        klo_s[r0:r0 + rows, LANE:2 * LANE] = kfeat
        khi_s[r0:r0 + rows, LANE:2 * LANE] = kfeat
    kr = lax.broadcasted_iota(jnp.int32, (tk, tq), 0)
    qcol = lax.broadcasted_iota(jnp.int32, (tk, tq), 1)
    rel_s[...] = (qcol - kr).astype(F32) * slope2

    lf = lam_ref[...]
    lam = (jnp.exp(jnp.sum(lf[0:1, :] * lf[1:2, :], axis=-1, keepdims=True))
           - jnp.exp(jnp.sum(lf[2:3, :] * lf[3:4, :], axis=-1, keepdims=True)) + lam_init)

    def q_tile(i, carry):
        c_q = pl.multiple_of(i * tq, tq)
        j_diag = (i * tq) // tk

        def scores(t):
            j = j_diag if t == 0 else lax.rem(j_diag + t, n_kv)
            r_k = pl.multiple_of(j * tk, tk)
            d0 = jnp.asarray(i * tq - j * tk)
            if t == 0:
                bias = jnp.abs(rel_s[...] + d0.astype(F32) * slope2)
                for mp, k_s in enumerate(maps):
                    s_s[t % 2, mp] = jnp.dot(k_s[pl.ds(r_k, tk), 0:LANE], qT_s[0, 0:LANE, pl.ds(c_q, tq)],
                                             preferred_element_type=F32) - bias
                return r_k, 0.0
            sgn = jnp.asarray(j > j_diag).astype(jnp.int32)
            for mp, k_s in enumerate(maps):
                s_s[t % 2, mp] = jnp.dot(k_s[pl.ds(r_k, tk), :], qT_s[sgn, :, pl.ds(c_q, tq)],
                                         preferred_element_type=F32)
            return r_k, jnp.abs(d0).astype(F32) * slope2

        def weighted_values(t, r_k, alphas):
            vT = vT_s[:, pl.ds(r_k, tk)]
            for mp in range(len(maps)):
                upd = jnp.dot(vT, p_s[t % 2, mp], preferred_element_type=F32)
                acc_s[mp] = upd if t == 0 else acc_s[mp] * alphas[mp] + upd

        m = [jnp.full((1, tq), NEG_BIG, F32) for _ in maps]
        nxt = scores(0)
        prev = None
        for t in range(n_kv):
            r_k, const = nxt
            if t + 1 < n_kv:
                nxt = scores(t + 1)
            alphas = []
            for mp in range(len(maps)):
                m_new = jnp.maximum(m[mp], jnp.max(s_s[t % 2, mp], axis=0, keepdims=True) - const)
                alphas.append(jnp.exp2(m[mp] - m_new))
                p_s[t % 2, mp] = jnp.exp2(s_s[t % 2, mp] - (m_new + const)).astype(BF16)
                m[mp] = m_new
            if prev is not None:
                weighted_values(t - 1, *prev)
            prev = (r_k, alphas)
        weighted_values(n_kv - 1, *prev)

        o1, o2 = acc_s[0], acc_s[1]
        od = o1[0:DIFF_DV] / o1[DIFF_DV:DIFF_DV + 1] - lam * (o2[0:DIFF_DV] / o2[DIFF_DV:DIFF_DV + 1])
        yn = od * lax.rsqrt(jnp.mean(od * od, axis=0, keepdims=True) + EPS) * nw_ref[...] * (1.0 - lam_init)
        o_ref[pl.ds(c_q, tq), :] = yn.T.astype(o_ref.dtype)
        return carry

    lax.fori_loop(0, seq // tq, q_tile, 0)


def _diff_attn(p, slopes, diff_lambda, norm_w_col, batch, seq, n_heads, q_off, k_off, v_off, lam_init):
    T = batch * seq
    tq = min(512, seq)
    tk = min(512, seq)
    blk = lambda off: pl.BlockSpec((seq, LANE), lambda b, h: (b, off + h))
    return pl.pallas_call(
        functools.partial(_diff_attn_kernel, seq=seq, tq=tq, tk=tk, lam_init=lam_init),
        grid=(batch, n_heads),
        in_specs=[pl.BlockSpec(memory_space=pltpu.SMEM),
                  blk(q_off), blk(k_off), blk(v_off),
                  _const_spec((4, DIFF_DH)), _const_spec((DIFF_DV, 1))],
        out_specs=pl.BlockSpec((seq, LANE), lambda b, h: (b, h)),
        out_shape=jax.ShapeDtypeStruct((T, n_heads * DIFF_DV), BF16),
        scratch_shapes=[
            pltpu.VMEM((2, 2 * LANE, seq), BF16),
            pltpu.VMEM((DIFF_DV + PACKED_ROWS, seq), BF16),
            pltpu.VMEM((seq, 2 * LANE), BF16),
            pltpu.VMEM((seq, 2 * LANE), BF16),
            pltpu.VMEM((tk, tq), F32),
            pltpu.VMEM((2, 2, tk, tq), F32),
            pltpu.VMEM((2, 2, tk, tq), BF16),
            pltpu.VMEM((2, DIFF_DV + PACKED_ROWS, tq), F32),
        ],
        compiler_params=_cparams("parallel", "parallel"),
        name="diff_attn",
    )(slopes, p, p, p, diff_lambda, norm_w_col)


def _merge_kernel(ya_ref, yd_ref, ga_ref, gd_ref, x_ref, wa_ref, wd_ref, wo_ref, gffn_ref,
                  wr_hi_ref, wr_lo_ref, br_ref, x1_ref, h2_ref, route_ref, count_ref, *, n_groups, per_group):
    ma = jnp.dot(ya_ref[...], wa_ref[...], preferred_element_type=F32)
    md = jnp.dot(yd_ref[...], wd_ref[...], preferred_element_type=F32)
    merged = _sigmoid(ga_ref[...].astype(F32)) * ma + _sigmoid(gd_ref[...].astype(F32)) * md
    x1 = x_ref[...] + jnp.dot(merged.astype(BF16), wo_ref[...], preferred_element_type=F32)
    x1_ref[...] = x1
    h2 = x1 * lax.rsqrt(jnp.mean(x1 * x1, axis=-1, keepdims=True) + EPS) * gffn_ref[...]
    h2_ref[...] = _pack_pairs(h2)

    h_hi = h2.astype(BF16)
    h_lo = (h2 - h_hi.astype(F32)).astype(BF16)
    logits = (jnp.dot(h_hi, wr_hi_ref[...], preferred_element_type=F32)
              + jnp.dot(h_hi, wr_lo_ref[...], preferred_element_type=F32)
              + jnp.dot(h_lo, wr_hi_ref[...], preferred_element_type=F32)) + br_ref[...]
    lane = lax.broadcasted_iota(jnp.int32, logits.shape, 1)
    lane_f = lane.astype(F32)
    big = float(LANE)

    def first_argmax(vals, vmax):
        return jnp.min(jnp.where(vals == vmax, lane_f, big), axis=-1, keepdims=True)

    gl = jnp.where(lane < n_groups, logits, -jnp.inf)
    gmax = jnp.max(gl, axis=-1, keepdims=True)
    g_idx = first_argmax(gl, gmax)
    g_w = 1.0 / jnp.sum(jnp.exp(gl - gmax), axis=-1, keepdims=True)
    e_lo = n_groups + g_idx * per_group
    in_group = (lane_f >= e_lo) & (lane_f < e_lo + per_group)
    el = jnp.where(in_group, logits, -jnp.inf)
    emax = jnp.max(el, axis=-1, keepdims=True)
    pe = jnp.exp(el - emax)
    pe = pe / jnp.sum(pe, axis=-1, keepdims=True)
    p1 = jnp.max(pe, axis=-1, keepdims=True)
    i1 = first_argmax(jnp.where(in_group, pe, -1.0), p1)
    rest = jnp.where(in_group & (lane_f != i1), pe, -1.0)
    p2 = jnp.max(rest, axis=-1, keepdims=True)
    i2 = first_argmax(rest, p2)
    denom = p1 + p2
    w1 = p1 / denom * g_w
    w2 = p2 / denom * g_w
    @pl.when(pl.program_id(0) == 0)
    def _():
        count_ref[...] = jnp.zeros_like(count_ref)

    hit1 = lane_f == i1
    hit2 = lane_f == i2
    hits = jnp.where(hit1 | hit2, 1.0, 0.0)
    tm = hits.shape[0]
    earlier = (lax.broadcasted_iota(jnp.int32, (tm, tm), 1) < lax.broadcasted_iota(jnp.int32, (tm, tm), 0))
    before = count_ref[0:1, :] + jnp.dot(jnp.where(earlier, 1.0, 0.0).astype(BF16), hits.astype(BF16),
                                         preferred_element_type=F32)
    rank1 = jnp.sum(jnp.where(hit1, before, 0.0), axis=-1, keepdims=True)
    rank2 = jnp.sum(jnp.where(hit2, before, 0.0), axis=-1, keepdims=True)
    count_ref[...] = count_ref[...] + jnp.sum(hits, axis=0, keepdims=True)

    route = jnp.where(lane == 0, i1 - n_groups, 0.0)
    route = jnp.where(lane == 1, i2 - n_groups, route)
    route = jnp.where(lane == 2, w1, route)
    route = jnp.where(lane == 3, w2, route)
    route = jnp.where(lane == 4, rank1, route)
    route = jnp.where(lane == 5, rank2, route)
    route_ref[...] = route


def _merge(y_a, y_d, p, x2, wa, wd, wo, g_ffn, wr_hi, wr_lo, b_r, n_groups, per_group, tm):
    T, D = x2.shape
    va, vd = y_a.shape[1], y_d.shape[1]
    row = lambda w: pl.BlockSpec((tm, w), lambda i: (i, 0))
    return pl.pallas_call(
        functools.partial(_merge_kernel, n_groups=n_groups, per_group=per_group),
        grid=(T // tm,),
        in_specs=[row(va), row(vd),
                  pl.BlockSpec((tm, D), lambda i: (i, 0)),
                  pl.BlockSpec((tm, D), lambda i: (i, 1)),
                  row(D),
                  _const_spec((va, D)), _const_spec((vd, D)), _const_spec((D, D)), _const_spec((1, D)),
                  _const_spec((D, LANE)), _const_spec((D, LANE)), _const_spec((1, LANE))],
        out_specs=[row(D), row(D // 2), row(LANE), pl.BlockSpec((SUBLANE, LANE), lambda i: (0, 0))],
        out_shape=[jax.ShapeDtypeStruct((T, D), F32),
                   jax.ShapeDtypeStruct((T, D // 2), jnp.uint32),
                   jax.ShapeDtypeStruct((T, LANE), F32),
                   jax.ShapeDtypeStruct((SUBLANE, LANE), F32)],
        compiler_params=_cparams("arbitrary"),
        name="merge",
    )(y_a, y_d, p, p, x2, wa, wd, wo, g_ffn, wr_hi, wr_lo, b_r)


def _expert_kernel(tile_expert_ref, tile_flag_ref, tile_rows_ref, next_expert_ref, slot_ref,
                   xs_ref, wg_hbm, wu_hbm, wd_hbm, o_ref,
                   wg_f, wu_f, wd_f, wg_s, wu_s, wd_s, sems):
    i = pl.program_id(0)
    flag = tile_flag_ref[i]

    def weight_copies(expert, slot):
        return [pltpu.make_async_copy(hbm.at[expert], buf.at[slot], sems.at[slot, n])
                for n, (hbm, buf) in enumerate(((wg_hbm, wg_f), (wu_hbm, wu_f), (wd_hbm, wd_f)))]

    @pl.when(flag == 2)
    def _():
        slot = slot_ref[i]

        @pl.when(i == 0)
        def _():
            for copy in weight_copies(tile_expert_ref[i], slot):
                copy.start()

        for copy in weight_copies(tile_expert_ref[i], slot):
            copy.wait()
        nxt = next_expert_ref[i]

        @pl.when(nxt >= 0)
        def _():
            for copy in weight_copies(nxt, 1 - slot):
                copy.start()

        wg_s[...] = wg_f[slot].astype(BF16)
        wu_s[...] = wu_f[slot].astype(BF16)
        wd_s[...] = wd_f[slot].astype(BF16)

    @pl.when(flag > 0)
    def _():
        row = lax.broadcasted_iota(jnp.int32, xs_ref.shape, 0)
        x = _unpack_pairs(jnp.where(row < tile_rows_ref[i], xs_ref[...], jnp.uint32(0)))
        g = jnp.dot(x, wg_s[...], preferred_element_type=F32)
        u = jnp.dot(x, wu_s[...], preferred_element_type=F32)
        hid = g * _sigmoid(g) * u
        o_ref[...] = _pack_pairs(jnp.dot(hid.astype(BF16), wd_s[...], preferred_element_type=F32))

    @pl.when(flag == 0)
    def _():
        o_ref[...] = jnp.zeros_like(o_ref)


def _experts(tile_plan, xs, w_gate, w_up, w_down, tm):
    n_rows, half = xs.shape
    E, D, Fd = w_gate.shape
    assert D == 2 * half
    n_tiles = n_rows // tm
    hbm = pl.BlockSpec(memory_space=pl.ANY)
    grid_spec = pltpu.PrefetchScalarGridSpec(
        num_scalar_prefetch=len(tile_plan),
        grid=(n_tiles,),
        in_specs=[pl.BlockSpec((tm, half), lambda i, *_: (i, 0)), hbm, hbm, hbm],
        out_specs=pl.BlockSpec((tm, half), lambda i, *_: (i, 0)),
        scratch_shapes=[pltpu.VMEM((2, D, Fd), F32), pltpu.VMEM((2, D, Fd), F32), pltpu.VMEM((2, Fd, D), F32),
                        pltpu.VMEM((D, Fd), BF16), pltpu.VMEM((D, Fd), BF16), pltpu.VMEM((Fd, D), BF16),
                        pltpu.SemaphoreType.DMA((2, 3))],
    )
    return pl.pallas_call(
        _expert_kernel,
        grid_spec=grid_spec,
        out_shape=jax.ShapeDtypeStruct((n_rows, half), jnp.uint32),
        compiler_params=_cparams("arbitrary"),
        name="experts",
    )(*tile_plan, xs, w_gate, w_up, w_down)


SC_CORES = 2
SC_SUBCORES = 16
SC_ROWS = 32
SC_COLLECT_BUFFERS = 3


def _dispatch_rows(table, pos, n_rows):
    T, W = table.shape
    workers = SC_CORES * SC_SUBCORES
    chunks = T // (workers * SC_ROWS)
    assert workers * chunks * SC_ROWS == T
    pos4 = pos.T.reshape(TOP_K, workers, chunks, SC_ROWS).transpose(1, 0, 2, 3)
    mesh = plsc.VectorSubcoreMesh(core_axis_name="c", subcore_axis_name="s")

    @functools.partial(
        pl.kernel, mesh=mesh,
        out_type=jax.ShapeDtypeStruct((n_rows, W), table.dtype),
        scratch_types=[pltpu.VMEM((TOP_K, chunks, SC_ROWS), jnp.int32),
                       pltpu.VMEM((2, SC_ROWS, W), table.dtype),
                       pltpu.SemaphoreType.DMA, pltpu.SemaphoreType.DMA, pltpu.SemaphoreType.DMA],
        name="dispatch_rows",
    )
    def dispatch(table_hbm, pos_hbm, out_hbm, pos_v, rows_v, sem_in, sem_out0, sem_out1):
        wid = lax.axis_index("s") * SC_CORES + lax.axis_index("c")
        base = wid * (chunks * SC_ROWS)
        pltpu.sync_copy(pos_hbm.at[wid], pos_v)

        def load(c):
            return pltpu.async_copy(table_hbm.at[pl.ds(base + c * SC_ROWS, SC_ROWS)], rows_v.at[c % 2], sem_in)

        pending = load(0)
        for c in range(chunks):
            pending.wait()
            if c + 1 < chunks:
                pending = load(c + 1)
            out0 = pltpu.async_copy(rows_v.at[c % 2], out_hbm.at[pos_v.at[0, c]], sem_out0)
            out1 = pltpu.async_copy(rows_v.at[c % 2], out_hbm.at[pos_v.at[1, c]], sem_out1)
            out0.wait()
            out1.wait()

    return dispatch(table, pos4)


def _collect_rows(table, pos):
    T = pos.shape[0]
    W = table.shape[1]
    workers = SC_CORES * SC_SUBCORES
    chunks = T // (workers * SC_ROWS)
    assert workers * chunks * SC_ROWS == T
    pos4 = pos.T.reshape(TOP_K, workers, chunks, SC_ROWS).transpose(1, 0, 2, 3)
    mesh = plsc.VectorSubcoreMesh(core_axis_name="c", subcore_axis_name="s")
    steps = [(c, k) for c in range(chunks) for k in range(TOP_K)]

    @functools.partial(
        pl.kernel, mesh=mesh,
        out_type=jax.ShapeDtypeStruct((TOP_K, T, W), table.dtype),
        scratch_types=[pltpu.VMEM((TOP_K, chunks, SC_ROWS), jnp.int32),
                       pltpu.VMEM((SC_COLLECT_BUFFERS, SC_ROWS, W), table.dtype)]
                      + [pltpu.SemaphoreType.DMA] * SC_COLLECT_BUFFERS,
        name="collect_rows",
    )
    def collect(table_hbm, pos_hbm, out_hbm, pos_v, rows_v, *sems):
        wid = lax.axis_index("s") * SC_CORES + lax.axis_index("c")
        base = wid * (chunks * SC_ROWS)
        pltpu.sync_copy(pos_hbm.at[wid], pos_v)

        def gather(n):
            c, k = steps[n]
            buf = n % SC_COLLECT_BUFFERS
            return pltpu.async_copy(table_hbm.at[pos_v.at[k, c]], rows_v.at[buf], sems[buf])

        in_flight = [gather(n) for n in range(min(SC_COLLECT_BUFFERS - 1, len(steps)))]
        for n, (c, k) in enumerate(steps):
            in_flight.pop(0).wait()
            ahead = n + SC_COLLECT_BUFFERS - 1
            if ahead < len(steps):
                in_flight.append(gather(ahead))
            pltpu.sync_copy(rows_v.at[n % SC_COLLECT_BUFFERS],
                            out_hbm.at[k, pl.ds(base + c * SC_ROWS, SC_ROWS)])

    return collect(table, pos4)


def _final_kernel(x1_ref, y0_ref, y1_ref, route_ref, g_ref, o_ref):
    route = route_ref[...]
    lane = lax.broadcasted_iota(jnp.int32, route.shape, 1)
    w0 = jnp.sum(jnp.where(lane == TOP_K, route, 0.0), axis=-1, keepdims=True)
    w1 = jnp.sum(jnp.where(lane == TOP_K + 1, route, 0.0), axis=-1, keepdims=True)
    y0 = _unpack_pairs(y0_ref[...]).astype(F32)
    y1 = _unpack_pairs(y1_ref[...]).astype(F32)
    x = x1_ref[...] + (w0 * y0 + w1 * y1)
    o_ref[...] = x * lax.rsqrt(jnp.mean(x * x, axis=-1, keepdims=True) + EPS) * g_ref[...]


def _final(x1, ys_tok, route, g_final, tm):
    T, D = x1.shape
    row = pl.BlockSpec((tm, D), lambda i: (i, 0))
    y_spec = lambda k: pl.BlockSpec((None, tm, D // 2), lambda i: (k, i, 0))
    return pl.pallas_call(
        _final_kernel,
        grid=(T // tm,),
        in_specs=[row, y_spec(0), y_spec(1), pl.BlockSpec((tm, LANE), lambda i: (i, 0)), _const_spec((1, D))],
        out_specs=row,
        out_shape=jax.ShapeDtypeStruct((T, D), F32),
        compiler_params=_cparams("parallel"),
        name="final",
    )(x1, ys_tok, ys_tok, route, g_final)


def _route_tables(ids, rank, counts, tm):
    T = ids.shape[0]
    A = T * TOP_K
    n_experts = counts.shape[0]
    padded = ((counts + tm - 1) // tm) * tm
    ends = jnp.cumsum(padded)
    starts = ends - padded
    onehot = ids[:, :, None] == jnp.arange(n_experts, dtype=jnp.int32)[None, None, :]
    pos = jnp.sum(jnp.where(onehot, starts[None, None, :], 0), axis=-1) + rank
    n_tiles = A // tm + n_experts
    tile_start = jnp.arange(n_tiles, dtype=jnp.int32) * tm
    tile_expert_raw = jnp.sum((ends[None, :] <= tile_start[:, None]).astype(jnp.int32), axis=1)
    valid = tile_start < ends[-1]
    last_expert = jnp.max(jnp.where(counts > 0, jnp.arange(n_experts, dtype=jnp.int32), 0))
    tile_expert = jnp.where(valid, jnp.minimum(tile_expert_raw, n_experts - 1), last_expert)
    first = jnp.concatenate([jnp.ones((1,), bool), tile_expert[1:] != tile_expert[:-1]])
    tile_flag = jnp.where(valid, jnp.where(first, 2, 1), 0).astype(jnp.int32)
    tile_rows = jnp.clip((starts + counts)[tile_expert] - tile_start, 0, tm)
    tile_rows = jnp.where(valid, tile_rows, 0).astype(jnp.int32)
    e_idx = jnp.arange(n_experts, dtype=jnp.int32)
    nonempty = counts > 0
    later = (e_idx[None, :] > e_idx[:, None]) & nonempty[None, :]
    next_nonempty = jnp.min(jnp.where(later, e_idx[None, :], n_experts), axis=1)
    next_nonempty = jnp.where(next_nonempty < n_experts, next_nonempty, -1).astype(jnp.int32)
    order = jnp.sum(((e_idx[None, :] < e_idx[:, None]) & nonempty[None, :]).astype(jnp.int32), axis=1)
    next_expert = next_nonempty[tile_expert]
    slot = (order[tile_expert] % 2).astype(jnp.int32)
    return pos, tile_expert, tile_flag, tile_rows, next_expert, slot


def _largest_tile(n, cap):
    t = min(n, cap)
    while n % t:
        t //= 2
    return t


def kernel(x, g_mix, w_in, conv_w, a_log, dt_bias, gdn_norm_w, diff_lambda, diff_norm_w, w_branch_a, w_branch_d,
           w_out, g_ffn, w_group, b_group, w_router, b_router, w_exp_gate, w_exp_up, w_exp_down, g_final):
    B, S, D = x.shape
    T = B * S
    depth = g_mix.shape[0]
    Hg = a_log.shape[-1]
    gdn_qk = Hg * GDN_DK
    gdn_v = Hg * GDN_DV
    diff_v = w_branch_d.shape[1]
    Hd = diff_v // DIFF_DV
    diff_qk = Hd * 2 * DIFF_DH
    n_groups = w_group.shape[-1]
    n_experts = w_router.shape[-1]
    per_group = n_experts // n_groups
    assert 4 * Hg <= LANE and n_groups + n_experts <= LANE
    assert S % GDN_CHUNK == 0 and D % LANE == 0

    ab_lo = 2 * gdn_qk + 2 * gdn_v
    ab_hi = ab_lo + 4 * Hg
    gates_lo = ab_hi + 2 * diff_qk + diff_v
    main_cols = 2 * D + ab_lo + 2 * diff_qk + diff_v
    gdn_off = 2 * D
    q_d_off = gdn_off + ab_lo
    k_d_off = q_d_off + diff_qk
    v_d_off = k_d_off + diff_qk

    tm_norm = _largest_tile(T, 512)
    tm_proj = _largest_tile(T, 2048)
    tn_proj = _largest_tile(main_cols, 1024)
    tm_merge = _largest_tile(T, 512)
    tm_exp = _largest_tile(T * TOP_K, 256)
    tm_final = _largest_tile(T, 512)

    slopes = jnp.exp2(-8.0 * (jnp.arange(Hd, dtype=F32) + 1.0) / Hd)
    x2 = x.reshape(T, D)
    for layer in range(depth):
        lam_init = 0.8 - 0.6 * math.exp(-0.3 * layer)
        w_in_l = w_in[layer]
        w_main, w_ab = _reorder_cast(w_in_l.T, ((gates_lo, 2 * D), (0, ab_lo), (ab_hi, gates_lo - ab_hi)),
                                     (ab_lo, 4 * Hg), _largest_tile(D, 256))
        alog_pad = jnp.pad(a_log[layer].reshape(1, 2 * Hg), ((0, 0), (0, LANE - 2 * Hg)))
        dtb_pad = jnp.pad(dt_bias[layer].reshape(1, 2 * Hg), ((0, 0), (0, LANE - 2 * Hg)))

        h, gb = _norm_proj(x2, g_mix[layer].reshape(1, D), w_ab, alog_pad, dtb_pad, 2 * Hg, tm_norm)
        p = _matmul_nt(h, w_main, tm_proj, tn_proj, BF16)

        grow = gb[:, :2 * Hg].reshape(B, S // GDN_CHUNK, GDN_CHUNK, 2 * Hg).transpose(0, 3, 1, 2)
        conv_w8 = jnp.pad(conv_w[layer], ((0, SUBLANE - GDN_CONV), (0, 0)))
        y_a = _gdn(p, gdn_off // LANE, conv_w8, gb, grow, gdn_norm_w[layer].reshape(1, GDN_DV), B, S, Hg)
        y_d = _diff_attn(p, slopes, diff_lambda[layer], diff_norm_w[layer].reshape(DIFF_DV, 1), B, S, Hd,
                         q_d_off // LANE, k_d_off // LANE, v_d_off // LANE, lam_init)

        w_r = jnp.pad(jnp.concatenate([w_group[layer], w_router[layer]], axis=1),
                      ((0, 0), (0, LANE - n_groups - n_experts)))
        wr_hi = w_r.astype(BF16)
        wr_lo = (w_r - wr_hi.astype(F32)).astype(BF16)
        b_r = jnp.pad(jnp.concatenate([b_group[layer], b_router[layer]]).reshape(1, -1),
                      ((0, 0), (0, LANE - n_groups - n_experts)))
        x1, h2, route, counts = _merge(y_a, y_d, p, x2,
                                       w_branch_a[layer].astype(BF16), w_branch_d[layer].astype(BF16),
                                       w_out[layer].astype(BF16), g_ffn[layer].reshape(1, D),
                                       wr_hi, wr_lo, b_r, n_groups, per_group, tm_merge)

        ids = route[:, 0:TOP_K].astype(jnp.int32)
        rank = route[:, 2 * TOP_K:3 * TOP_K].astype(jnp.int32)
        counts = counts[0, n_groups:n_groups + n_experts].astype(jnp.int32)
        pos, *tile_plan = _route_tables(ids, rank, counts, tm_exp)
        xs = _dispatch_rows(h2, pos, tile_plan[0].shape[0] * tm_exp)
        ys = _experts(tile_plan, xs, w_exp_gate[layer], w_exp_up[layer], w_exp_down[layer], tm_exp)
        ys_tok = _collect_rows(ys, pos)
        if layer + 1 < depth:
            wts = route[:, TOP_K:2 * TOP_K]
            x2 = x1 + (wts[:, 0:1] * _unpack_pairs(ys_tok[0]).astype(F32)
                       + wts[:, 1:2] * _unpack_pairs(ys_tok[1]).astype(F32))
    out = _final(x1, ys_tok, route, g_final.reshape(1, D), tm_final)
    return out.reshape(B, S, D)
```

```python
import functools
import math

import jax
import jax.numpy as jnp
from jax import lax
from jax.experimental import pallas as pl
from jax.experimental.pallas import tpu as pltpu
from jax.experimental.pallas import tpu_sc as plsc

F32 = jnp.float32
BF16 = jnp.bfloat16
EPS = 1e-6
LANE = 128
SUBLANE = 8
PACKED_ROWS = 16
GDN_DK = 128
GDN_DV = 128
GDN_CONV = 5
GDN_CHUNK = 64
DIFF_DH = 64
DIFF_DV = 2 * DIFF_DH
TOP_K = 2
LOG2E = 1.4426950408889634
NEG_BIG = -1e30
VMEM_LIMIT_BYTES = 56 * 1024 * 1024

def _cparams(*sem):
    return pltpu.CompilerParams(dimension_semantics=sem, vmem_limit_bytes=VMEM_LIMIT_BYTES)


def _const_spec(shape):
    nd = len(shape)
    return pl.BlockSpec(shape, lambda *_: (0,) * nd, pipeline_mode=pl.Buffered(1))


def _split3(x):
    hi = x.astype(BF16)
    r = x - hi.astype(F32)
    mid = r.astype(BF16)
    lo = (r - mid.astype(F32)).astype(BF16)
    return hi, mid, lo


def _pack_pairs(x):
    bits = lax.bitcast_convert_type(x.astype(BF16).astype(F32), jnp.uint32)
    half = bits.shape[1] // 2
    return bits[:, :half] | (bits[:, half:] >> 16)


def _unpack_pairs(packed):
    left = lax.bitcast_convert_type(packed & jnp.uint32(0xFFFF0000), F32).astype(BF16)
    right = lax.bitcast_convert_type(packed << 16, F32).astype(BF16)
    return jnp.concatenate([left, right], axis=1)


def _softplus(x):
    return jnp.maximum(x, 0.0) + jnp.log(1.0 + jnp.exp(-jnp.abs(x)))


def _sigmoid(x):
    return 1.0 / (1.0 + jnp.exp(-x))


def _norm_proj_kernel(x_ref, g_ref, wab_ref, alog_ref, dtb_ref, h_ref, gb_ref, *, n_dir_heads):
    x = x_ref[...]
    h = x * lax.rsqrt(jnp.mean(x * x, axis=-1, keepdims=True) + EPS) * g_ref[...]
    hb = h.astype(BF16)
    h_ref[...] = hb
    ab = lax.dot_general(hb, wab_ref[...], (((1,), (1,)), ((), ())), preferred_element_type=F32)
    g = -jnp.exp(alog_ref[...]) * _softplus(ab + dtb_ref[...])
    beta = _sigmoid(ab)
    tm = x.shape[0]
    row = lax.broadcasted_iota(jnp.int32, (tm, tm), 0)
    col = lax.broadcasted_iota(jnp.int32, (tm, tm), 1)
    same = (row // GDN_CHUNK) == (col // GDN_CHUNK)
    prefix = jnp.where(same & (col <= row), 1.0, 0.0).astype(BF16)
    suffix = jnp.where(same & (col >= row), 1.0, 0.0).astype(BF16)
    pieces = _split3(g)
    cs_f = sum(jnp.dot(prefix, p, preferred_element_type=F32) for p in pieces)
    cs_b = sum(jnp.dot(suffix, p, preferred_element_type=F32) for p in pieces)
    lane = lax.broadcasted_iota(jnp.int32, g.shape, 1)
    gb_ref[...] = jnp.where(lane < n_dir_heads // 2, cs_f, jnp.where(lane < n_dir_heads, cs_b, beta))


def _norm_proj(x2, g_mix, w_ab, alog_pad, dtb_pad, n_dir_heads, tm):
    T, D = x2.shape
    return pl.pallas_call(
        functools.partial(_norm_proj_kernel, n_dir_heads=n_dir_heads),
        grid=(T // tm,),
        in_specs=[pl.BlockSpec((tm, D), lambda i: (i, 0)),
                  _const_spec((1, D)), _const_spec((LANE, D)), _const_spec((1, LANE)), _const_spec((1, LANE))],
        out_specs=[pl.BlockSpec((tm, D), lambda i: (i, 0)),
                   pl.BlockSpec((tm, LANE), lambda i: (i, 0))],
        out_shape=[jax.ShapeDtypeStruct((T, D), BF16),
                   jax.ShapeDtypeStruct((T, LANE), F32)],
        compiler_params=_cparams("parallel"),
        name="norm_proj",
    )(x2, g_mix, w_ab, alog_pad, dtb_pad)


REORDER_ROWS = 512


def _reorder_cast_kernel(w_ref, o_ref, narrow_ref, *, segments, narrow):
    n_start, n_width = narrow
    row = lax.broadcasted_iota(jnp.int32, narrow_ref.shape, 0)
    narrow_ref[...] = jnp.where(row < n_width, w_ref[n_start:n_start + LANE, :], 0.0).astype(narrow_ref.dtype)
    off = 0
    for start, width in segments:
        for c in range(0, width, REORDER_ROWS):
            n = min(REORDER_ROWS, width - c)
            o_ref[off:off + n, :] = w_ref[start + c:start + c + n, :].astype(o_ref.dtype)
            off += n


def _reorder_cast(wT, segments, narrow, tc):
    rows, cols = wT.shape
    out_rows = sum(width for _, width in segments)
    assert narrow[0] % SUBLANE == 0 and narrow[1] <= LANE and narrow[0] + LANE <= rows
    assert all(start % SUBLANE == 0 and width % PACKED_ROWS == 0 for start, width in segments)
    return pl.pallas_call(
        functools.partial(_reorder_cast_kernel, segments=segments, narrow=narrow),
        grid=(cols // tc,),
        in_specs=[pl.BlockSpec((rows, tc), lambda i: (0, i))],
        out_specs=[pl.BlockSpec((out_rows, tc), lambda i: (0, i)), pl.BlockSpec((LANE, tc), lambda i: (0, i))],
        out_shape=[jax.ShapeDtypeStruct((out_rows, cols), BF16), jax.ShapeDtypeStruct((LANE, cols), BF16)],
        compiler_params=_cparams("parallel"),
        name="reorder_cast",
    )(wT)


_NT_DIMS = (((1,), (1,)), ((), ()))


def _matmul_nt_kernel(a_ref, bT_ref, o_ref):
    o_ref[...] = lax.dot_general(a_ref[...], bT_ref[...], _NT_DIMS, preferred_element_type=F32).astype(o_ref.dtype)


def _matmul_nt(a, bT, tm, tn, out_dtype):
    M, K = a.shape
    N = bT.shape[0]
    return pl.pallas_call(
        _matmul_nt_kernel,
        grid=(N // tn, M // tm),
        in_specs=[pl.BlockSpec((tm, K), lambda j, i: (i, 0)),
                  pl.BlockSpec((tn, K), lambda j, i: (j, 0))],
        out_specs=pl.BlockSpec((tm, tn), lambda j, i: (i, j)),
        out_shape=jax.ShapeDtypeStruct((M, N), out_dtype),
        compiler_params=_cparams("parallel", "parallel"),
        name="in_proj",
    )(a, bT)


def _bmm_tn(a, b):
    return jnp.stack([lax.dot_general(a[n], b[n], (((0,), (0,)), ((), ())), preferred_element_type=F32)
                      for n in range(a.shape[0])])


def _gdn_kernel(q_ref, k_ref, v_ref, z_ref, cwq_ref, cwk_ref, cwv_ref, gb_ref, grow_ref, nw_ref,
                y_ref,
                xpad, qn, kn, vn, u_s, w_s, qg_s, at_s, tm_s, tn_s, egl_s, st_s,
                *, n_heads, seq, conv_rows, prep_chunks):
    C = GDN_CHUNK
    n_chunks = seq // C
    head = pl.program_id(1)
    pad = SUBLANE
    half = GDN_CONV // 2

    zeros_pad = jnp.zeros((pad, LANE), F32)
    xpad[0:pad, :] = zeros_pad
    xpad[pad + seq:pad + seq + pad, :] = zeros_pad
    for src, cw_ref, dst, mode in ((q_ref, cwq_ref, qn, "q"), (k_ref, cwk_ref, kn, "k"), (v_ref, cwv_ref, vn, "v")):
        xpad[pad:pad + seq, :] = src[...].astype(F32)
        for t in range(seq // conv_rows):
            r0 = t * conv_rows
            acc = jnp.zeros((conv_rows, LANE), F32)
            for j in range(GDN_CONV):
                lo = pad + r0 + j - half
                acc = acc + xpad[lo:lo + conv_rows, :] * cw_ref[j:j + 1, :]
            y = acc * _sigmoid(acc)
            if mode != "v":
                y = y * lax.rsqrt(jnp.sum(y * y, axis=-1, keepdims=True) + EPS)
            if mode == "q":
                y = y * (GDN_DK ** -0.5)
            dst[r0:r0 + conv_rows, :] = y

    nb = prep_chunks
    R = nb * C
    nb2 = 2 * nb
    bi = lax.broadcasted_iota(jnp.int32, (nb2, C, C), 0)
    ii = lax.broadcasted_iota(jnp.int32, (nb2, C, C), 1)
    jj = lax.broadcasted_iota(jnp.int32, (nb2, C, C), 2)
    fwd = bi < nb
    incl = (fwd & (ii >= jj)) | (~fwd & (ii <= jj))
    strict = (fwd & (ii > jj)) | (~fwd & (ii < jj))
    eye = jnp.where(ii == jj, 1.0, 0.0).astype(F32)
    lane = lax.broadcasted_iota(jnp.int32, (R, LANE), 1)
    both = lambda x: jnp.concatenate([x, x], axis=0)

    def prep(t, carry):
        r0 = pl.multiple_of(t * R, R)
        n0 = pl.multiple_of(t * nb, nb)
        q3 = qn[pl.ds(r0, R), :].reshape(nb, C, LANE)
        k3 = kn[pl.ds(r0, R), :].reshape(nb, C, LANE)
        v3 = vn[pl.ds(r0, R), :].reshape(nb, C, LANE)
        gb_blk = gb_ref[pl.ds(r0, R), :]
        kq = jnp.concatenate([k3, q3], axis=1).astype(BF16)
        gram = jnp.einsum("nik,njk->nij", kq, k3.astype(BF16), preferred_element_type=F32)
        kk, qk = both(gram[:, :C, :]), both(gram[:, C:, :])
        gcb, bt, grow, gl = [], [], [], []
        for d in range(2):
            colidx = d * n_heads + head
            gcb_d = jnp.sum(jnp.where(lane == colidx, gb_blk, 0.0), axis=-1, keepdims=True).reshape(nb, C, 1)
            gcb.append(gcb_d)
            bt.append(jnp.sum(jnp.where(lane == 2 * n_heads + colidx, gb_blk, 0.0), axis=-1,
                              keepdims=True).reshape(nb, C, 1))
            grow.append(grow_ref[0, colidx, pl.ds(n0, nb), :].reshape(nb, 1, C))
            gl.append(gcb_d[:, C - 1:C, :] if d == 0 else gcb_d[:, 0:1, :])
        gcb, bt, grow, gl = (jnp.concatenate(x, axis=0) for x in (gcb, bt, grow, gl))
        k2, q2, v2 = both(k3), both(q3), both(v3)

        decay = jnp.exp(jnp.where(incl, gcb - grow, -jnp.inf))
        L = jnp.where(strict, bt * kk * decay, 0.0)
        attn = qk * decay
        ainv = eye - L
        P = L
        for _ in range(int(math.log2(C)) - 1):
            Pb = P.astype(BF16)
            P = jnp.einsum("nij,njk->nik", Pb, Pb, preferred_element_type=F32)
            ainv = ainv + jnp.einsum("nij,njk->nik", ainv.astype(BF16), P.astype(BF16),
                                     preferred_element_type=F32)
        eg = jnp.exp(gcb)
        rhs = jnp.concatenate([v2 * bt, k2 * (bt * eg)], axis=-1).astype(BF16)
        uw = jnp.einsum("nij,njd->nid", ainv.astype(BF16), rhs, preferred_element_type=F32).astype(BF16)
        kg = (k2 * jnp.exp(gl - gcb)).astype(BF16)
        trans = _bmm_tn(kg, uw)
        qg = (q2 * eg).astype(BF16)
        egl = jnp.broadcast_to(jnp.exp(gl).reshape(nb2, 1), (nb2, LANE))
        for d in range(2):
            sl = slice(d * nb, (d + 1) * nb)
            u_s[d, pl.ds(r0, R), :] = uw[sl, :, :LANE].reshape(R, LANE)
            w_s[d, pl.ds(r0, R), :] = uw[sl, :, LANE:].reshape(R, LANE)
            qg_s[d, pl.ds(r0, R), :] = qg[sl].reshape(R, LANE)
            at_s[d, pl.ds(r0, R), :] = attn[sl].reshape(R, C).astype(BF16)
            tn_s[d, pl.ds(n0 * GDN_DK, nb * GDN_DK), :] = trans[sl, :, :LANE].reshape(nb * GDN_DK, LANE).astype(BF16)
            tm_s[d, pl.ds(n0 * GDN_DK, nb * GDN_DK), :] = (-trans[sl, :, LANE:]).reshape(nb * GDN_DK, LANE).astype(BF16)
            egl_s[d, pl.ds(n0, nb), :] = egl[sl]
        return carry

    lax.fori_loop(0, n_chunks // nb, prep, 0)

    def chunk_step(d, n, state):
        r = pl.multiple_of(n * GDN_DK, GDN_DK)
        sb = state.astype(BF16)
        st_s[d, pl.ds(r, GDN_DK), :] = sb
        return (state * egl_s[d, pl.ds(n, 1), :]
                + jnp.dot(tm_s[d, pl.ds(r, GDN_DK), :], sb, preferred_element_type=F32)
                + tn_s[d, pl.ds(r, GDN_DK), :].astype(F32))

    def scan(n, carry):
        sf, sbw = carry
        return chunk_step(0, n, sf), chunk_step(1, n_chunks - 1 - n, sbw)

    zero_state = jnp.zeros((GDN_DK, GDN_DV), F32)
    lax.fori_loop(0, n_chunks, scan, (zero_state, zero_state))

    def emit(t, carry):
        r0 = pl.multiple_of(t * R, R)
        s0 = pl.multiple_of(t * nb * GDN_DK, nb * GDN_DK)
        two = lambda ref, start, rows: jnp.concatenate([ref[0, pl.ds(start, rows), :], ref[1, pl.ds(start, rows), :]])
        st = two(st_s, s0, nb * GDN_DK).reshape(nb2, GDN_DK, GDN_DV)
        wq = jnp.concatenate([two(w_s, r0, R).reshape(nb2, C, LANE), two(qg_s, r0, R).reshape(nb2, C, LANE)], axis=1)
        ws_qs = jnp.einsum("ncd,nde->nce", wq, st, preferred_element_type=F32)
        vnew = two(u_s, r0, R).reshape(nb2, C, LANE).astype(F32) - ws_qs[:, :C, :]
        o = ws_qs[:, C:, :] + jnp.einsum("ncs,nse->nce", two(at_s, r0, R).reshape(nb2, C, C), vnew.astype(BF16),
                                         preferred_element_type=F32)
        o = (o[:nb] + o[nb:]).reshape(R, LANE)
        on = o * lax.rsqrt(jnp.mean(o * o, axis=-1, keepdims=True) + EPS) * nw_ref[...]
        z = z_ref[pl.ds(r0, R), :].astype(F32)
        y_ref[pl.ds(r0, R), :] = (on * (z * _sigmoid(z))).astype(y_ref.dtype)
        return carry

    lax.fori_loop(0, n_chunks // nb, emit, 0)


def _gdn(p, p_off, conv_w8, gb, grow, norm_w, batch, seq, n_heads):
    T = batch * seq
    qk_blocks = n_heads
    n_chunks = seq // GDN_CHUNK
    conv_rows = min(512, seq)
    prep_chunks = min(16, n_chunks)
    blk = lambda off: pl.BlockSpec((seq, LANE), lambda b, h: (b, p_off + off + h))
    cw = lambda off: pl.BlockSpec((SUBLANE, LANE), lambda b, h: (0, off + h))
    return pl.pallas_call(
        functools.partial(_gdn_kernel, n_heads=n_heads, seq=seq, conv_rows=conv_rows, prep_chunks=prep_chunks),
        grid=(batch, n_heads),
        in_specs=[blk(0), blk(qk_blocks), blk(2 * qk_blocks), blk(3 * qk_blocks),
                  cw(0), cw(qk_blocks), cw(2 * qk_blocks),
                  pl.BlockSpec((seq, LANE), lambda b, h: (b, 0), pipeline_mode=pl.Buffered(1)),
                  pl.BlockSpec((1, 2 * n_heads, n_chunks, GDN_CHUNK), lambda b, h: (b, 0, 0, 0),
                               pipeline_mode=pl.Buffered(1)),
                  _const_spec((1, LANE))],
        out_specs=pl.BlockSpec((seq, LANE), lambda b, h: (b, h)),
        out_shape=jax.ShapeDtypeStruct((T, n_heads * GDN_DV), BF16),
        scratch_shapes=[
            pltpu.VMEM((seq + 2 * SUBLANE, LANE), F32),
            pltpu.VMEM((seq, LANE), F32),
            pltpu.VMEM((seq, LANE), F32),
            pltpu.VMEM((seq, LANE), F32),
            pltpu.VMEM((2, seq, LANE), BF16),
            pltpu.VMEM((2, seq, LANE), BF16),
            pltpu.VMEM((2, seq, LANE), BF16),
            pltpu.VMEM((2, seq, GDN_CHUNK), BF16),
            pltpu.VMEM((2, n_chunks * GDN_DK, GDN_DV), BF16),
            pltpu.VMEM((2, n_chunks * GDN_DK, GDN_DV), BF16),
            pltpu.VMEM((2, max(n_chunks, SUBLANE), LANE), F32),
            pltpu.VMEM((2, n_chunks * GDN_DK, GDN_DV), BF16),
        ],
        compiler_params=_cparams("parallel", "parallel"),
        name="gdn",
    )(p, p, p, p, conv_w8, conv_w8, conv_w8, gb, grow, norm_w)


def _diff_attn_kernel(slopes_ref, q_ref, k_ref, v_ref, lam_ref, nw_ref, o_ref,
                      qT_s, vT_s, klo_s, khi_s, rel_s, s_s, p_s, acc_s,
                      *, seq, tq, tk, lam_init):
    head = pl.program_id(1)
    slope2 = slopes_ref[head] * LOG2E
    scale2 = (DIFF_DH ** -0.5) * LOG2E
    n_kv = seq // tk
    maps = (klo_s, khi_s)

    rows = max(tq, tk)
    assert seq % rows == 0 and rows % tq == 0 and rows % tk == 0
    frow = lax.broadcasted_iota(jnp.int32, (LANE, rows), 0)
    fcol = lax.broadcasted_iota(jnp.int32, (LANE, rows), 1)
    qa, qb, qc = (x.astype(F32) for x in _split3((fcol % tq).astype(F32) * slope2))
    qfeat = jnp.where(frow < 3, 1.0, jnp.where(frow == 3, -qa, jnp.where(frow == 4, -qb,
                                               jnp.where(frow == 5, -qc, 0.0)))).astype(BF16)
    krow = lax.broadcasted_iota(jnp.int32, (rows, LANE), 0)
    lane = lax.broadcasted_iota(jnp.int32, (rows, LANE), 1)
    ka, kb, kc = (x.astype(F32) for x in _split3((krow % tk).astype(F32) * slope2))
    kfeat = jnp.where(lane == 0, ka, jnp.where(lane == 1, kb, jnp.where(lane == 2, kc,
                                               jnp.where(lane < 6, 1.0, 0.0)))).astype(BF16)
    for t in range(seq // rows):
        r0 = t * rows
        qt = (q_ref[r0:r0 + rows, :].astype(F32) * scale2).T.astype(BF16)
        for sgn, feat in enumerate((qfeat, -qfeat)):
            qT_s[sgn, 0:LANE, r0:r0 + rows] = qt
            qT_s[sgn, LANE:2 * LANE, r0:r0 + rows] = feat
        vT_s[0:LANE, r0:r0 + rows] = v_ref[r0:r0 + rows, :].astype(F32).T.astype(BF16)
        vT_s[DIFF_DV:DIFF_DV + PACKED_ROWS, r0:r0 + rows] = jnp.ones((PACKED_ROWS, rows), BF16)
        kt = k_ref[r0:r0 + rows, :]
        klo_s[r0:r0 + rows, 0:LANE] = jnp.where(lane < DIFF_DH, kt, jnp.zeros_like(kt))
        khi_s[r0:r0 + rows, 0:LANE] = jnp.where(lane >= DIFF_DH, kt, jnp.zeros_like(kt))
        klo_s[r0:r0 + rows, LANE:2 * LANE] = kfeat
        khi_s[r0:r0 + rows, LANE:2 * LANE] = kfeat
    kr = lax.broadcasted_iota(jnp.int32, (tk, tq), 0)
    qcol = lax.broadcasted_iota(jnp.int32, (tk, tq), 1)
    rel_s[...] = (qcol - kr).astype(F32) * slope2

    lf = lam_ref[...]
    lam = (jnp.exp(jnp.sum(lf[0:1, :] * lf[1:2, :], axis=-1, keepdims=True))
           - jnp.exp(jnp.sum(lf[2:3, :] * lf[3:4, :], axis=-1, keepdims=True)) + lam_init)

    def q_tile(i, carry):
        c_q = pl.multiple_of(i * tq, tq)
        j_diag = (i * tq) // tk

        def scores(t):
            j = j_diag if t == 0 else lax.rem(j_diag + t, n_kv)
            r_k = pl.multiple_of(j * tk, tk)
            d0 = jnp.asarray(i * tq - j * tk)
            if t == 0:
                bias = jnp.abs(rel_s[...] + d0.astype(F32) * slope2)
                for mp, k_s in enumerate(maps):
                    s_s[t % 2, mp] = jnp.dot(k_s[pl.ds(r_k, tk), 0:LANE], qT_s[0, 0:LANE, pl.ds(c_q, tq)],
                                             preferred_element_type=F32) - bias
                return r_k, 0.0
            sgn = jnp.asarray(j > j_diag).astype(jnp.int32)
            for mp, k_s in enumerate(maps):
                s_s[t % 2, mp] = jnp.dot(k_s[pl.ds(r_k, tk), :], qT_s[sgn, :, pl.ds(c_q, tq)],
                                         preferred_element_type=F32)
            return r_k, jnp.abs(d0).astype(F32) * slope2

        def weighted_values(t, r_k, alphas):
            vT = vT_s[:, pl.ds(r_k, tk)]
            for mp in range(len(maps)):
                upd = jnp.dot(vT, p_s[t % 2, mp], preferred_element_type=F32)
                acc_s[mp] = upd if t == 0 else acc_s[mp] * alphas[mp] + upd

        m = [jnp.full((1, tq), NEG_BIG, F32) for _ in maps]
        nxt = scores(0)
        prev = None
        for t in range(n_kv):
            r_k, const = nxt
            if t + 1 < n_kv:
                nxt = scores(t + 1)
            alphas = []
            for mp in range(len(maps)):
                m_new = jnp.maximum(m[mp], jnp.max(s_s[t % 2, mp], axis=0, keepdims=True) - const)
                alphas.append(jnp.exp2(m[mp] - m_new))
                p_s[t % 2, mp] = jnp.exp2(s_s[t % 2, mp] - (m_new + const)).astype(BF16)
                m[mp] = m_new
            if prev is not None:
                weighted_values(t - 1, *prev)
            prev = (r_k, alphas)
        weighted_values(n_kv - 1, *prev)

        o1, o2 = acc_s[0], acc_s[1]
        od = o1[0:DIFF_DV] / o1[DIFF_DV:DIFF_DV + 1] - lam * (o2[0:DIFF_DV] / o2[DIFF_DV:DIFF_DV + 1])
        yn = od * lax.rsqrt(jnp.mean(od * od, axis=0, keepdims=True) + EPS) * nw_ref[...] * (1.0 - lam_init)
        o_ref[pl.ds(c_q, tq), :] = yn.T.astype(o_ref.dtype)
        return carry

    lax.fori_loop(0, seq // tq, q_tile, 0)


def _diff_attn(p, slopes, diff_lambda, norm_w_col, batch, seq, n_heads, q_off, k_off, v_off, lam_init):
    T = batch * seq
    tq = min(512, seq)
    tk = min(512, seq)
    blk = lambda off: pl.BlockSpec((seq, LANE), lambda b, h: (b, off + h))
    return pl.pallas_call(
        functools.partial(_diff_attn_kernel, seq=seq, tq=tq, tk=tk, lam_init=lam_init),
        grid=(batch, n_heads),
        in_specs=[pl.BlockSpec(memory_space=pltpu.SMEM),
                  blk(q_off), blk(k_off), blk(v_off),
                  _const_spec((4, DIFF_DH)), _const_spec((DIFF_DV, 1))],
        out_specs=pl.BlockSpec((seq, LANE), lambda b, h: (b, h)),
        out_shape=jax.ShapeDtypeStruct((T, n_heads * DIFF_DV), BF16),
        scratch_shapes=[
            pltpu.VMEM((2, 2 * LANE, seq), BF16),
            pltpu.VMEM((DIFF_DV + PACKED_ROWS, seq), BF16),
            pltpu.VMEM((seq, 2 * LANE), BF16),
            pltpu.VMEM((seq, 2 * LANE), BF16),
            pltpu.VMEM((tk, tq), F32),
            pltpu.VMEM((2, 2, tk, tq), F32),
            pltpu.VMEM((2, 2, tk, tq), BF16),
            pltpu.VMEM((2, DIFF_DV + PACKED_ROWS, tq), F32),
        ],
        compiler_params=_cparams("parallel", "parallel"),
        name="diff_attn",
    )(slopes, p, p, p, diff_lambda, norm_w_col)


def _merge_kernel(ya_ref, yd_ref, ga_ref, gd_ref, x_ref, wa_ref, wd_ref, wo_ref, gffn_ref,
                  wr_hi_ref, wr_lo_ref, br_ref, x1_ref, h2_ref, route_ref, count_ref, *, n_groups, per_group):
    ma = jnp.dot(ya_ref[...], wa_ref[...], preferred_element_type=F32)
    md = jnp.dot(yd_ref[...], wd_ref[...], preferred_element_type=F32)
    merged = _sigmoid(ga_ref[...].astype(F32)) * ma + _sigmoid(gd_ref[...].astype(F32)) * md
    x1 = x_ref[...] + jnp.dot(merged.astype(BF16), wo_ref[...], preferred_element_type=F32)
    x1_ref[...] = x1
    h2 = x1 * lax.rsqrt(jnp.mean(x1 * x1, axis=-1, keepdims=True) + EPS) * gffn_ref[...]
    h2_ref[...] = _pack_pairs(h2)

    h_hi = h2.astype(BF16)
    h_lo = (h2 - h_hi.astype(F32)).astype(BF16)
    logits = (jnp.dot(h_hi, wr_hi_ref[...], preferred_element_type=F32)
              + jnp.dot(h_hi, wr_lo_ref[...], preferred_element_type=F32)
              + jnp.dot(h_lo, wr_hi_ref[...], preferred_element_type=F32)) + br_ref[...]
    lane = lax.broadcasted_iota(jnp.int32, logits.shape, 1)
    lane_f = lane.astype(F32)
    big = float(LANE)

    def first_argmax(vals, vmax):
        return jnp.min(jnp.where(vals == vmax, lane_f, big), axis=-1, keepdims=True)

    gl = jnp.where(lane < n_groups, logits, -jnp.inf)
    gmax = jnp.max(gl, axis=-1, keepdims=True)
    g_idx = first_argmax(gl, gmax)
    g_w = 1.0 / jnp.sum(jnp.exp(gl - gmax), axis=-1, keepdims=True)
    e_lo = n_groups + g_idx * per_group
    in_group = (lane_f >= e_lo) & (lane_f < e_lo + per_group)
    el = jnp.where(in_group, logits, -jnp.inf)
    emax = jnp.max(el, axis=-1, keepdims=True)
    pe = jnp.exp(el - emax)
    pe = pe / jnp.sum(pe, axis=-1, keepdims=True)
    p1 = jnp.max(pe, axis=-1, keepdims=True)
    i1 = first_argmax(jnp.where(in_group, pe, -1.0), p1)
    rest = jnp.where(in_group & (lane_f != i1), pe, -1.0)
    p2 = jnp.max(rest, axis=-1, keepdims=True)
    i2 = first_argmax(rest, p2)
    denom = p1 + p2
    w1 = p1 / denom * g_w
    w2 = p2 / denom * g_w
    @pl.when(pl.program_id(0) == 0)
    def _():
        count_ref[...] = jnp.zeros_like(count_ref)

    hit1 = lane_f == i1
    hit2 = lane_f == i2
    hits = jnp.where(hit1 | hit2, 1.0, 0.0)
    tm = hits.shape[0]
    earlier = (lax.broadcasted_iota(jnp.int32, (tm, tm), 1) < lax.broadcasted_iota(jnp.int32, (tm, tm), 0))
    before = count_ref[0:1, :] + jnp.dot(jnp.where(earlier, 1.0, 0.0).astype(BF16), hits.astype(BF16),
                                         preferred_element_type=F32)
    rank1 = jnp.sum(jnp.where(hit1, before, 0.0), axis=-1, keepdims=True)
    rank2 = jnp.sum(jnp.where(hit2, before, 0.0), axis=-1, keepdims=True)
    count_ref[...] = count_ref[...] + jnp.sum(hits, axis=0, keepdims=True)

    route = jnp.where(lane == 0, i1 - n_groups, 0.0)
    route = jnp.where(lane == 1, i2 - n_groups, route)
    route = jnp.where(lane == 2, w1, route)
    route = jnp.where(lane == 3, w2, route)
    route = jnp.where(lane == 4, rank1, route)
    route = jnp.where(lane == 5, rank2, route)
    route_ref[...] = route


def _merge(y_a, y_d, p, x2, wa, wd, wo, g_ffn, wr_hi, wr_lo, b_r, n_groups, per_group, tm):
    T, D = x2.shape
    va, vd = y_a.shape[1], y_d.shape[1]
    row = lambda w: pl.BlockSpec((tm, w), lambda i: (i, 0))
    return pl.pallas_call(
        functools.partial(_merge_kernel, n_groups=n_groups, per_group=per_group),
        grid=(T // tm,),
        in_specs=[row(va), row(vd),
                  pl.BlockSpec((tm, D), lambda i: (i, 0)),
                  pl.BlockSpec((tm, D), lambda i: (i, 1)),
                  row(D),
                  _const_spec((va, D)), _const_spec((vd, D)), _const_spec((D, D)), _const_spec((1, D)),
                  _const_spec((D, LANE)), _const_spec((D, LANE)), _const_spec((1, LANE))],
        out_specs=[row(D), row(D // 2), row(LANE), pl.BlockSpec((SUBLANE, LANE), lambda i: (0, 0))],
        out_shape=[jax.ShapeDtypeStruct((T, D), F32),
                   jax.ShapeDtypeStruct((T, D // 2), jnp.uint32),
                   jax.ShapeDtypeStruct((T, LANE), F32),
                   jax.ShapeDtypeStruct((SUBLANE, LANE), F32)],
        compiler_params=_cparams("arbitrary"),
        name="merge",
    )(y_a, y_d, p, p, x2, wa, wd, wo, g_ffn, wr_hi, wr_lo, b_r)


def _expert_kernel(tile_expert_ref, tile_flag_ref, tile_rows_ref, next_expert_ref, slot_ref,
                   xs_ref, wg_hbm, wu_hbm, wd_hbm, o_ref,
                   wg_f, wu_f, wd_f, wg_s, wu_s, wd_s, sems):
    i = pl.program_id(0)
    flag = tile_flag_ref[i]

    def weight_copies(expert, slot):
        return [pltpu.make_async_copy(hbm.at[expert], buf.at[slot], sems.at[slot, n])
                for n, (hbm, buf) in enumerate(((wg_hbm, wg_f), (wu_hbm, wu_f), (wd_hbm, wd_f)))]

    @pl.when(flag == 2)
    def _():
        slot = slot_ref[i]

        @pl.when(i == 0)
        def _():
            for copy in weight_copies(tile_expert_ref[i], slot):
                copy.start()

        for copy in weight_copies(tile_expert_ref[i], slot):
            copy.wait()
        nxt = next_expert_ref[i]

        @pl.when(nxt >= 0)
        def _():
            for copy in weight_copies(nxt, 1 - slot):
                copy.start()

        wg_s[...] = wg_f[slot].astype(BF16)
        wu_s[...] = wu_f[slot].astype(BF16)
        wd_s[...] = wd_f[slot].astype(BF16)

    @pl.when(flag > 0)
    def _():
        row = lax.broadcasted_iota(jnp.int32, xs_ref.shape, 0)
        x = _unpack_pairs(jnp.where(row < tile_rows_ref[i], xs_ref[...], jnp.uint32(0)))
        g = jnp.dot(x, wg_s[...], preferred_element_type=F32)
        u = jnp.dot(x, wu_s[...], preferred_element_type=F32)
        hid = g * _sigmoid(g) * u
        o_ref[...] = _pack_pairs(jnp.dot(hid.astype(BF16), wd_s[...], preferred_element_type=F32))

    @pl.when(flag == 0)
    def _():
        o_ref[...] = jnp.zeros_like(o_ref)


def _experts(tile_plan, xs, w_gate, w_up, w_down, tm):
    n_rows, half = xs.shape
    E, D, Fd = w_gate.shape
    assert D == 2 * half
    n_tiles = n_rows // tm
    hbm = pl.BlockSpec(memory_space=pl.ANY)
    grid_spec = pltpu.PrefetchScalarGridSpec(
        num_scalar_prefetch=len(tile_plan),
        grid=(n_tiles,),
        in_specs=[pl.BlockSpec((tm, half), lambda i, *_: (i, 0)), hbm, hbm, hbm],
        out_specs=pl.BlockSpec((tm, half), lambda i, *_: (i, 0)),
        scratch_shapes=[pltpu.VMEM((2, D, Fd), F32), pltpu.VMEM((2, D, Fd), F32), pltpu.VMEM((2, Fd, D), F32),
                        pltpu.VMEM((D, Fd), BF16), pltpu.VMEM((D, Fd), BF16), pltpu.VMEM((Fd, D), BF16),
                        pltpu.SemaphoreType.DMA((2, 3))],
    )
    return pl.pallas_call(
        _expert_kernel,
        grid_spec=grid_spec,
        out_shape=jax.ShapeDtypeStruct((n_rows, half), jnp.uint32),
        compiler_params=_cparams("arbitrary"),
        name="experts",
    )(*tile_plan, xs, w_gate, w_up, w_down)


SC_CORES = 2
SC_SUBCORES = 16
SC_ROWS = 32
SC_COLLECT_BUFFERS = 3


def _dispatch_rows(table, pos, n_rows):
    T, W = table.shape
    workers = SC_CORES * SC_SUBCORES
    chunks = T // (workers * SC_ROWS)
    assert workers * chunks * SC_ROWS == T
    pos4 = pos.T.reshape(TOP_K, workers, chunks, SC_ROWS).transpose(1, 0, 2, 3)
    mesh = plsc.VectorSubcoreMesh(core_axis_name="c", subcore_axis_name="s")

    @functools.partial(
        pl.kernel, mesh=mesh,
        out_type=jax.ShapeDtypeStruct((n_rows, W), table.dtype),
        scratch_types=[pltpu.VMEM((TOP_K, chunks, SC_ROWS), jnp.int32),
                       pltpu.VMEM((2, SC_ROWS, W), table.dtype),
                       pltpu.SemaphoreType.DMA, pltpu.SemaphoreType.DMA, pltpu.SemaphoreType.DMA],
        name="dispatch_rows",
    )
    def dispatch(table_hbm, pos_hbm, out_hbm, pos_v, rows_v, sem_in, sem_out0, sem_out1):
        wid = lax.axis_index("s") * SC_CORES + lax.axis_index("c")
        base = wid * (chunks * SC_ROWS)
        pltpu.sync_copy(pos_hbm.at[wid], pos_v)

        def load(c):
            return pltpu.async_copy(table_hbm.at[pl.ds(base + c * SC_ROWS, SC_ROWS)], rows_v.at[c % 2], sem_in)

        pending = load(0)
        for c in range(chunks):
            pending.wait()
            if c + 1 < chunks:
                pending = load(c + 1)
            out0 = pltpu.async_copy(rows_v.at[c % 2], out_hbm.at[pos_v.at[0, c]], sem_out0)
            out1 = pltpu.async_copy(rows_v.at[c % 2], out_hbm.at[pos_v.at[1, c]], sem_out1)
            out0.wait()
            out1.wait()

    return dispatch(table, pos4)


def _collect_rows(table, pos):
    T = pos.shape[0]
    W = table.shape[1]
    workers = SC_CORES * SC_SUBCORES
    chunks = T // (workers * SC_ROWS)
    assert workers * chunks * SC_ROWS == T
    pos4 = pos.T.reshape(TOP_K, workers, chunks, SC_ROWS).transpose(1, 0, 2, 3)
    mesh = plsc.VectorSubcoreMesh(core_axis_name="c", subcore_axis_name="s")
    steps = [(c, k) for c in range(chunks) for k in range(TOP_K)]

    @functools.partial(
        pl.kernel, mesh=mesh,
        out_type=jax.ShapeDtypeStruct((TOP_K, T, W), table.dtype),
        scratch_types=[pltpu.VMEM((TOP_K, chunks, SC_ROWS), jnp.int32),
                       pltpu.VMEM((SC_COLLECT_BUFFERS, SC_ROWS, W), table.dtype)]
                      + [pltpu.SemaphoreType.DMA] * SC_COLLECT_BUFFERS,
        name="collect_rows",
    )
    def collect(table_hbm, pos_hbm, out_hbm, pos_v, rows_v, *sems):
        wid = lax.axis_index("s") * SC_CORES + lax.axis_index("c")
        base = wid * (chunks * SC_ROWS)
        pltpu.sync_copy(pos_hbm.at[wid], pos_v)

        def gather(n):
            c, k = steps[n]
            buf = n % SC_COLLECT_BUFFERS
            return pltpu.async_copy(table_hbm.at[pos_v.at[k, c]], rows_v.at[buf], sems[buf])

        in_flight = [gather(n) for n in range(min(SC_COLLECT_BUFFERS - 1, len(steps)))]
        for n, (c, k) in enumerate(steps):
            in_flight.pop(0).wait()
            ahead = n + SC_COLLECT_BUFFERS - 1
            if ahead < len(steps):
                in_flight.append(gather(ahead))
            pltpu.sync_copy(rows_v.at[n % SC_COLLECT_BUFFERS],
                            out_hbm.at[k, pl.ds(base + c * SC_ROWS, SC_ROWS)])

    return collect(table, pos4)


def _final_kernel(x1_ref, y0_ref, y1_ref, route_ref, g_ref, o_ref):
    route = route_ref[...]
    lane = lax.broadcasted_iota(jnp.int32, route.shape, 1)
    w0 = jnp.sum(jnp.where(lane == TOP_K, route, 0.0), axis=-1, keepdims=True)
    w1 = jnp.sum(jnp.where(lane == TOP_K + 1, route, 0.0), axis=-1, keepdims=True)
    y0 = _unpack_pairs(y0_ref[...]).astype(F32)
    y1 = _unpack_pairs(y1_ref[...]).astype(F32)
    x = x1_ref[...] + (w0 * y0 + w1 * y1)
    o_ref[...] = x * lax.rsqrt(jnp.mean(x * x, axis=-1, keepdims=True) + EPS) * g_ref[...]


def _final(x1, ys_tok, route, g_final, tm):
    T, D = x1.shape
    row = pl.BlockSpec((tm, D), lambda i: (i, 0))
    y_spec = lambda k: pl.BlockSpec((None, tm, D // 2), lambda i: (k, i, 0))
    return pl.pallas_call(
        _final_kernel,
        grid=(T // tm,),
        in_specs=[row, y_spec(0), y_spec(1), pl.BlockSpec((tm, LANE), lambda i: (i, 0)), _const_spec((1, D))],
        out_specs=row,
        out_shape=jax.ShapeDtypeStruct((T, D), F32),
        compiler_params=_cparams("parallel"),
        name="final",
    )(x1, ys_tok, ys_tok, route, g_final)


def _route_tables(ids, rank, counts, tm):
    T = ids.shape[0]
    A = T * TOP_K
    n_experts = counts.shape[0]
    padded = ((counts + tm - 1) // tm) * tm
    ends = jnp.cumsum(padded)
    starts = ends - padded
    onehot = ids[:, :, None] == jnp.arange(n_experts, dtype=jnp.int32)[None, None, :]
    pos = jnp.sum(jnp.where(onehot, starts[None, None, :], 0), axis=-1) + rank
    n_tiles = A // tm + n_experts
    tile_start = jnp.arange(n_tiles, dtype=jnp.int32) * tm
    tile_expert_raw = jnp.sum((ends[None, :] <= tile_start[:, None]).astype(jnp.int32), axis=1)
    valid = tile_start < ends[-1]
    last_expert = jnp.max(jnp.where(counts > 0, jnp.arange(n_experts, dtype=jnp.int32), 0))
    tile_expert = jnp.where(valid, jnp.minimum(tile_expert_raw, n_experts - 1), last_expert)
    first = jnp.concatenate([jnp.ones((1,), bool), tile_expert[1:] != tile_expert[:-1]])
    tile_flag = jnp.where(valid, jnp.where(first, 2, 1), 0).astype(jnp.int32)
    tile_rows = jnp.clip((starts + counts)[tile_expert] - tile_start, 0, tm)
    tile_rows = jnp.where(valid, tile_rows, 0).astype(jnp.int32)
    e_idx = jnp.arange(n_experts, dtype=jnp.int32)
    nonempty = counts > 0
    later = (e_idx[None, :] > e_idx[:, None]) & nonempty[None, :]
    next_nonempty = jnp.min(jnp.where(later, e_idx[None, :], n_experts), axis=1)
    next_nonempty = jnp.where(next_nonempty < n_experts, next_nonempty, -1).astype(jnp.int32)
    order = jnp.sum(((e_idx[None, :] < e_idx[:, None]) & nonempty[None, :]).astype(jnp.int32), axis=1)
    next_expert = next_nonempty[tile_expert]
    slot = (order[tile_expert] % 2).astype(jnp.int32)
    return pos, tile_expert, tile_flag, tile_rows, next_expert, slot


def _largest_tile(n, cap):
    t = min(n, cap)
    while n % t:
        t //= 2
    return t


def kernel(x, g_mix, w_in, conv_w, a_log, dt_bias, gdn_norm_w, diff_lambda, diff_norm_w, w_branch_a, w_branch_d,
           w_out, g_ffn, w_group, b_group, w_router, b_router, w_exp_gate, w_exp_up, w_exp_down, g_final):
    B, S, D = x.shape
    T = B * S
    depth = g_mix.shape[0]
    Hg = a_log.shape[-1]
    gdn_qk = Hg * GDN_DK
    gdn_v = Hg * GDN_DV
    diff_v = w_branch_d.shape[1]
    Hd = diff_v // DIFF_DV
    diff_qk = Hd * 2 * DIFF_DH
    n_groups = w_group.shape[-1]
    n_experts = w_router.shape[-1]
    per_group = n_experts // n_groups
    assert 4 * Hg <= LANE and n_groups + n_experts <= LANE
    assert S % GDN_CHUNK == 0 and D % LANE == 0

    ab_lo = 2 * gdn_qk + 2 * gdn_v
    ab_hi = ab_lo + 4 * Hg
    gates_lo = ab_hi + 2 * diff_qk + diff_v
    main_cols = 2 * D + ab_lo + 2 * diff_qk + diff_v
    gdn_off = 2 * D
    q_d_off = gdn_off + ab_lo
    k_d_off = q_d_off + diff_qk
    v_d_off = k_d_off + diff_qk

    tm_norm = _largest_tile(T, 512)
    tm_proj = _largest_tile(T, 2048)
    tn_proj = _largest_tile(main_cols, 1024)
    tm_merge = _largest_tile(T, 512)
    tm_exp = _largest_tile(T * TOP_K, 256)
    tm_final = _largest_tile(T, 512)

    slopes = jnp.exp2(-8.0 * (jnp.arange(Hd, dtype=F32) + 1.0) / Hd)
    x2 = x.reshape(T, D)
    for layer in range(depth):
        lam_init = 0.8 - 0.6 * math.exp(-0.3 * layer)
        w_in_l = w_in[layer]
        w_main, w_ab = _reorder_cast(w_in_l.T, ((gates_lo, 2 * D), (0, ab_lo), (ab_hi, gates_lo - ab_hi)),
                                     (ab_lo, 4 * Hg), _largest_tile(D, 256))
        alog_pad = jnp.pad(a_log[layer].reshape(1, 2 * Hg), ((0, 0), (0, LANE - 2 * Hg)))
        dtb_pad = jnp.pad(dt_bias[layer].reshape(1, 2 * Hg), ((0, 0), (0, LANE - 2 * Hg)))

        h, gb = _norm_proj(x2, g_mix[layer].reshape(1, D), w_ab, alog_pad, dtb_pad, 2 * Hg, tm_norm)
        p = _matmul_nt(h, w_main, tm_proj, tn_proj, BF16)

        grow = gb[:, :2 * Hg].reshape(B, S // GDN_CHUNK, GDN_CHUNK, 2 * Hg).transpose(0, 3, 1, 2)
        conv_w8 = jnp.pad(conv_w[layer], ((0, SUBLANE - GDN_CONV), (0, 0)))
        y_a = _gdn(p, gdn_off // LANE, conv_w8, gb, grow, gdn_norm_w[layer].reshape(1, GDN_DV), B, S, Hg)
        y_d = _diff_attn(p, slopes, diff_lambda[layer], diff_norm_w[layer].reshape(DIFF_DV, 1), B, S, Hd,
                         q_d_off // LANE, k_d_off // LANE, v_d_off // LANE, lam_init)

        w_r = jnp.pad(jnp.concatenate([w_group[layer], w_router[layer]], axis=1),
                      ((0, 0), (0, LANE - n_groups - n_experts)))
        wr_hi = w_r.astype(BF16)
        wr_lo = (w_r - wr_hi.astype(F32)).astype(BF16)
        b_r = jnp.pad(jnp.concatenate([b_group[layer], b_router[layer]]).reshape(1, -1),
                      ((0, 0), (0, LANE - n_groups - n_experts)))
        x1, h2, route, counts = _merge(y_a, y_d, p, x2,
                                       w_branch_a[layer].astype(BF16), w_branch_d[layer].astype(BF16),
                                       w_out[layer].astype(BF16), g_ffn[layer].reshape(1, D),
                                       wr_hi, wr_lo, b_r, n_groups, per_group, tm_merge)

        ids = route[:, 0:TOP_K].astype(jnp.int32)
        rank = route[:, 2 * TOP_K:3 * TOP_K].astype(jnp.int32)
        counts = counts[0, n_groups:n_groups + n_experts].astype(jnp.int32)
        pos, *tile_plan = _route_tables(ids, rank, counts, tm_exp)
        xs = _dispatch_rows(h2, pos, tile_plan[0].shape[0] * tm_exp)
        ys = _experts(tile_plan, xs, w_exp_gate[layer], w_exp_up[layer], w_exp_down[layer], tm_exp)
        ys_tok = _collect_rows(ys, pos)
        if layer + 1 < depth:
            wts = route[:, TOP_K:2 * TOP_K]
            x2 = x1 + (wts[:, 0:1] * _unpack_pairs(ys_tok[0]).astype(F32)
                       + wts[:, 1:2] * _unpack_pairs(ys_tok[1]).astype(F32))
    out = _final(x1, ys_tok, route, g_final.reshape(1, D), tm_final)
    return out.reshape(B, S, D)
```

```python
import functools
import math

import jax
import jax.numpy as jnp
from jax import lax
from jax.experimental import pallas as pl
from jax.experimental.pallas import tpu as pltpu
from jax.experimental.pallas import tpu_sc as plsc

F32 = jnp.float32
BF16 = jnp.bfloat16
EPS = 1e-6
LANE = 128
SUBLANE = 8
PACKED_ROWS = 16
GDN_DK = 128
GDN_DV = 128
GDN_CONV = 5
GDN_CHUNK = 64
DIFF_DH = 64
DIFF_DV = 2 * DIFF_DH
TOP_K = 2
LOG2E = 1.4426950408889634
NEG_BIG = -1e30
VMEM_LIMIT_BYTES = 56 * 1024 * 1024

def _cparams(*sem):
    return pltpu.CompilerParams(dimension_semantics=sem, vmem_limit_bytes=VMEM_LIMIT_BYTES)


def _const_spec(shape):
    nd = len(shape)
    return pl.BlockSpec(shape, lambda *_: (0,) * nd, pipeline_mode=pl.Buffered(1))


def _split3(x):
    hi = x.astype(BF16)
    r = x - hi.astype(F32)
    mid = r.astype(BF16)
    lo = (r - mid.astype(F32)).astype(BF16)
    return hi, mid, lo


def _pack_pairs(x):
    bits = lax.bitcast_convert_type(x.astype(BF16).astype(F32), jnp.uint32)
    half = bits.shape[1] // 2
    return bits[:, :half] | (bits[:, half:] >> 16)


def _unpack_pairs(packed):
    left = lax.bitcast_convert_type(packed & jnp.uint32(0xFFFF0000), F32).astype(BF16)
    right = lax.bitcast_convert_type(packed << 16, F32).astype(BF16)
    return jnp.concatenate([left, right], axis=1)


def _softplus(x):
    return jnp.maximum(x, 0.0) + jnp.log(1.0 + jnp.exp(-jnp.abs(x)))


def _sigmoid(x):
    return 1.0 / (1.0 + jnp.exp(-x))


def _norm_proj_kernel(x_ref, g_ref, wab_ref, alog_ref, dtb_ref, h_ref, gb_ref, *, n_dir_heads):
    x = x_ref[...]
    h = x * lax.rsqrt(jnp.mean(x * x, axis=-1, keepdims=True) + EPS) * g_ref[...]
    hb = h.astype(BF16)
    h_ref[...] = hb
    ab = lax.dot_general(hb, wab_ref[...], (((1,), (1,)), ((), ())), preferred_element_type=F32)
    g = -jnp.exp(alog_ref[...]) * _softplus(ab + dtb_ref[...])
    beta = _sigmoid(ab)
    tm = x.shape[0]
    row = lax.broadcasted_iota(jnp.int32, (tm, tm), 0)
    col = lax.broadcasted_iota(jnp.int32, (tm, tm), 1)
    same = (row // GDN_CHUNK) == (col // GDN_CHUNK)
    prefix = jnp.where(same & (col <= row), 1.0, 0.0).astype(BF16)
    suffix = jnp.where(same & (col >= row), 1.0, 0.0).astype(BF16)
    pieces = _split3(g)
    cs_f = sum(jnp.dot(prefix, p, preferred_element_type=F32) for p in pieces)
    cs_b = sum(jnp.dot(suffix, p, preferred_element_type=F32) for p in pieces)
    lane = lax.broadcasted_iota(jnp.int32, g.shape, 1)
    gb_ref[...] = jnp.where(lane < n_dir_heads // 2, cs_f, jnp.where(lane < n_dir_heads, cs_b, beta))


def _norm_proj(x2, g_mix, w_ab, alog_pad, dtb_pad, n_dir_heads, tm):
    T, D = x2.shape
    return pl.pallas_call(
        functools.partial(_norm_proj_kernel, n_dir_heads=n_dir_heads),
        grid=(T // tm,),
        in_specs=[pl.BlockSpec((tm, D), lambda i: (i, 0)),
                  _const_spec((1, D)), _const_spec((LANE, D)), _const_spec((1, LANE)), _const_spec((1, LANE))],
        out_specs=[pl.BlockSpec((tm, D), lambda i: (i, 0)),
                   pl.BlockSpec((tm, LANE), lambda i: (i, 0))],
        out_shape=[jax.ShapeDtypeStruct((T, D), BF16),
                   jax.ShapeDtypeStruct((T, LANE), F32)],
        compiler_params=_cparams("parallel"),
        name="norm_proj",
    )(x2, g_mix, w_ab, alog_pad, dtb_pad)


REORDER_ROWS = 512


def _reorder_cast_kernel(w_ref, o_ref, narrow_ref, *, segments, narrow):
    n_start, n_width = narrow
    row = lax.broadcasted_iota(jnp.int32, narrow_ref.shape, 0)
    narrow_ref[...] = jnp.where(row < n_width, w_ref[n_start:n_start + LANE, :], 0.0).astype(narrow_ref.dtype)
    off = 0
    for start, width in segments:
        for c in range(0, width, REORDER_ROWS):
            n = min(REORDER_ROWS, width - c)
            o_ref[off:off + n, :] = w_ref[start + c:start + c + n, :].astype(o_ref.dtype)
            off += n


def _reorder_cast(wT, segments, narrow, tc):
    rows, cols = wT.shape
    out_rows = sum(width for _, width in segments)
    assert narrow[0] % SUBLANE == 0 and narrow[1] <= LANE and narrow[0] + LANE <= rows
    assert all(start % SUBLANE == 0 and width % PACKED_ROWS == 0 for start, width in segments)
    return pl.pallas_call(
        functools.partial(_reorder_cast_kernel, segments=segments, narrow=narrow),
        grid=(cols // tc,),
        in_specs=[pl.BlockSpec((rows, tc), lambda i: (0, i))],
        out_specs=[pl.BlockSpec((out_rows, tc), lambda i: (0, i)), pl.BlockSpec((LANE, tc), lambda i: (0, i))],
        out_shape=[jax.ShapeDtypeStruct((out_rows, cols), BF16), jax.ShapeDtypeStruct((LANE, cols), BF16)],
        compiler_params=_cparams("parallel"),
        name="reorder_cast",
    )(wT)


_NT_DIMS = (((1,), (1,)), ((), ()))


def _matmul_nt_kernel(a_ref, bT_ref, o_ref):
    o_ref[...] = lax.dot_general(a_ref[...], bT_ref[...], _NT_DIMS, preferred_element_type=F32).astype(o_ref.dtype)


def _matmul_nt(a, bT, tm, tn, out_dtype):
    M, K = a.shape
    N = bT.shape[0]
    return pl.pallas_call(
        _matmul_nt_kernel,
        grid=(N // tn, M // tm),
        in_specs=[pl.BlockSpec((tm, K), lambda j, i: (i, 0)),
                  pl.BlockSpec((tn, K), lambda j, i: (j, 0))],
        out_specs=pl.BlockSpec((tm, tn), lambda j, i: (i, j)),
        out_shape=jax.ShapeDtypeStruct((M, N), out_dtype),
        compiler_params=_cparams("parallel", "parallel"),
        name="in_proj",
    )(a, bT)


def _bmm_tn(a, b):
    return jnp.stack([lax.dot_general(a[n], b[n], (((0,), (0,)), ((), ())), preferred_element_type=F32)
                      for n in range(a.shape[0])])


def _gdn_kernel(q_ref, k_ref, v_ref, z_ref, cwq_ref, cwk_ref, cwv_ref, gb_ref, grow_ref, nw_ref,
                y_ref,
                xpad, qn, kn, vn, u_s, w_s, qg_s, at_s, tm_s, tn_s, egl_s, st_s,
                *, n_heads, seq, conv_rows, prep_chunks):
    C = GDN_CHUNK
    n_chunks = seq // C
    head = pl.program_id(1)
    pad = SUBLANE
    half = GDN_CONV // 2

    zeros_pad = jnp.zeros((pad, LANE), F32)
    xpad[0:pad, :] = zeros_pad
    xpad[pad + seq:pad + seq + pad, :] = zeros_pad
    for src, cw_ref, dst, mode in ((q_ref, cwq_ref, qn, "q"), (k_ref, cwk_ref, kn, "k"), (v_ref, cwv_ref, vn, "v")):
        xpad[pad:pad + seq, :] = src[...].astype(F32)
        for t in range(seq // conv_rows):
            r0 = t * conv_rows
            acc = jnp.zeros((conv_rows, LANE), F32)
            for j in range(GDN_CONV):
                lo = pad + r0 + j - half
                acc = acc + xpad[lo:lo + conv_rows, :] * cw_ref[j:j + 1, :]
            y = acc * _sigmoid(acc)
            if mode != "v":
                y = y * lax.rsqrt(jnp.sum(y * y, axis=-1, keepdims=True) + EPS)
            if mode == "q":
                y = y * (GDN_DK ** -0.5)
            dst[r0:r0 + conv_rows, :] = y

    nb = prep_chunks
    R = nb * C
    nb2 = 2 * nb
    bi = lax.broadcasted_iota(jnp.int32, (nb2, C, C), 0)
    ii = lax.broadcasted_iota(jnp.int32, (nb2, C, C), 1)
    jj = lax.broadcasted_iota(jnp.int32, (nb2, C, C), 2)
    fwd = bi < nb
    incl = (fwd & (ii >= jj)) | (~fwd & (ii <= jj))
    strict = (fwd & (ii > jj)) | (~fwd & (ii < jj))
    eye = jnp.where(ii == jj, 1.0, 0.0).astype(F32)
    lane = lax.broadcasted_iota(jnp.int32, (R, LANE), 1)
    both = lambda x: jnp.concatenate([x, x], axis=0)

    def prep(t, carry):
        r0 = pl.multiple_of(t * R, R)
        n0 = pl.multiple_of(t * nb, nb)
        q3 = qn[pl.ds(r0, R), :].reshape(nb, C, LANE)
        k3 = kn[pl.ds(r0, R), :].reshape(nb, C, LANE)
        v3 = vn[pl.ds(r0, R), :].reshape(nb, C, LANE)
        gb_blk = gb_ref[pl.ds(r0, R), :]
        kq = jnp.concatenate([k3, q3], axis=1).astype(BF16)
        gram = jnp.einsum("nik,njk->nij", kq, k3.astype(BF16), preferred_element_type=F32)
        kk, qk = both(gram[:, :C, :]), both(gram[:, C:, :])
        gcb, bt, grow, gl = [], [], [], []
        for d in range(2):
            colidx = d * n_heads + head
            gcb_d = jnp.sum(jnp.where(lane == colidx, gb_blk, 0.0), axis=-1, keepdims=True).reshape(nb, C, 1)
            gcb.append(gcb_d)
            bt.append(jnp.sum(jnp.where(lane == 2 * n_heads + colidx, gb_blk, 0.0), axis=-1,
                              keepdims=True).reshape(nb, C, 1))
            grow.append(grow_ref[0, colidx, pl.ds(n0, nb), :].reshape(nb, 1, C))
            gl.append(gcb_d[:, C - 1:C, :] if d == 0 else gcb_d[:, 0:1, :])
        gcb, bt, grow, gl = (jnp.concatenate(x, axis=0) for x in (gcb, bt, grow, gl))
        k2, q2, v2 = both(k3), both(q3), both(v3)

        decay = jnp.exp(jnp.where(incl, gcb - grow, -jnp.inf))
        L = jnp.where(strict, bt * kk * decay, 0.0)
        attn = qk * decay
        ainv = eye - L
        P = L
        for _ in range(int(math.log2(C)) - 1):
            Pb = P.astype(BF16)
            P = jnp.einsum("nij,njk->nik", Pb, Pb, preferred_element_type=F32)
            ainv = ainv + jnp.einsum("nij,njk->nik", ainv.astype(BF16), P.astype(BF16),
                                     preferred_element_type=F32)
        eg = jnp.exp(gcb)
        rhs = jnp.concatenate([v2 * bt, k2 * (bt * eg)], axis=-1).astype(BF16)
        uw = jnp.einsum("nij,njd->nid", ainv.astype(BF16), rhs, preferred_element_type=F32).astype(BF16)
        kg = (k2 * jnp.exp(gl - gcb)).astype(BF16)
        trans = _bmm_tn(kg, uw)
        qg = (q2 * eg).astype(BF16)
        egl = jnp.broadcast_to(jnp.exp(gl).reshape(nb2, 1), (nb2, LANE))
        for d in range(2):
            sl = slice(d * nb, (d + 1) * nb)
            u_s[d, pl.ds(r0, R), :] = uw[sl, :, :LANE].reshape(R, LANE)
            w_s[d, pl.ds(r0, R), :] = uw[sl, :, LANE:].reshape(R, LANE)
            qg_s[d, pl.ds(r0, R), :] = qg[sl].reshape(R, LANE)
            at_s[d, pl.ds(r0, R), :] = attn[sl].reshape(R, C).astype(BF16)
            tn_s[d, pl.ds(n0 * GDN_DK, nb * GDN_DK), :] = trans[sl, :, :LANE].reshape(nb * GDN_DK, LANE).astype(BF16)
            tm_s[d, pl.ds(n0 * GDN_DK, nb * GDN_DK), :] = (-trans[sl, :, LANE:]).reshape(nb * GDN_DK, LANE).astype(BF16)
            egl_s[d, pl.ds(n0, nb), :] = egl[sl]
        return carry

    lax.fori_loop(0, n_chunks // nb, prep, 0)

    def chunk_step(d, n, state):
        r = pl.multiple_of(n * GDN_DK, GDN_DK)
        sb = state.astype(BF16)
        st_s[d, pl.ds(r, GDN_DK), :] = sb
        return (state * egl_s[d, pl.ds(n, 1), :]
                + jnp.dot(tm_s[d, pl.ds(r, GDN_DK), :], sb, preferred_element_type=F32)
                + tn_s[d, pl.ds(r, GDN_DK), :].astype(F32))

    def scan(n, carry):
        sf, sbw = carry
        return chunk_step(0, n, sf), chunk_step(1, n_chunks - 1 - n, sbw)

    zero_state = jnp.zeros((GDN_DK, GDN_DV), F32)
    lax.fori_loop(0, n_chunks, scan, (zero_state, zero_state))

    def emit(t, carry):
        r0 = pl.multiple_of(t * R, R)
        s0 = pl.multiple_of(t * nb * GDN_DK, nb * GDN_DK)
        two = lambda ref, start, rows: jnp.concatenate([ref[0, pl.ds(start, rows), :], ref[1, pl.ds(start, rows), :]])
        st = two(st_s, s0, nb * GDN_DK).reshape(nb2, GDN_DK, GDN_DV)
        wq = jnp.concatenate([two(w_s, r0, R).reshape(nb2, C, LANE), two(qg_s, r0, R).reshape(nb2, C, LANE)], axis=1)
        ws_qs = jnp.einsum("ncd,nde->nce", wq, st, preferred_element_type=F32)
        vnew = two(u_s, r0, R).reshape(nb2, C, LANE).astype(F32) - ws_qs[:, :C, :]
        o = ws_qs[:, C:, :] + jnp.einsum("ncs,nse->nce", two(at_s, r0, R).reshape(nb2, C, C), vnew.astype(BF16),
                                         preferred_element_type=F32)
        o = (o[:nb] + o[nb:]).reshape(R, LANE)
        on = o * lax.rsqrt(jnp.mean(o * o, axis=-1, keepdims=True) + EPS) * nw_ref[...]
        z = z_ref[pl.ds(r0, R), :].astype(F32)
        y_ref[pl.ds(r0, R), :] = (on * (z * _sigmoid(z))).astype(y_ref.dtype)
        return carry

    lax.fori_loop(0, n_chunks // nb, emit, 0)


def _gdn(p, p_off, conv_w8, gb, grow, norm_w, batch, seq, n_heads):
    T = batch * seq
    qk_blocks = n_heads
    n_chunks = seq // GDN_CHUNK
    conv_rows = min(512, seq)
    prep_chunks = min(16, n_chunks)
    blk = lambda off: pl.BlockSpec((seq, LANE), lambda b, h: (b, p_off + off + h))
    cw = lambda off: pl.BlockSpec((SUBLANE, LANE), lambda b, h: (0, off + h))
    return pl.pallas_call(
        functools.partial(_gdn_kernel, n_heads=n_heads, seq=seq, conv_rows=conv_rows, prep_chunks=prep_chunks),
        grid=(batch, n_heads),
        in_specs=[blk(0), blk(qk_blocks), blk(2 * qk_blocks), blk(3 * qk_blocks),
                  cw(0), cw(qk_blocks), cw(2 * qk_blocks),
                  pl.BlockSpec((seq, LANE), lambda b, h: (b, 0), pipeline_mode=pl.Buffered(1)),
                  pl.BlockSpec((1, 2 * n_heads, n_chunks, GDN_CHUNK), lambda b, h: (b, 0, 0, 0),
                               pipeline_mode=pl.Buffered(1)),
                  _const_spec((1, LANE))],
        out_specs=pl.BlockSpec((seq, LANE), lambda b, h: (b, h)),
        out_shape=jax.ShapeDtypeStruct((T, n_heads * GDN_DV), BF16),
        scratch_shapes=[
            pltpu.VMEM((seq + 2 * SUBLANE, LANE), F32),
            pltpu.VMEM((seq, LANE), F32),
            pltpu.VMEM((seq, LANE), F32),
            pltpu.VMEM((seq, LANE), F32),
            pltpu.VMEM((2, seq, LANE), BF16),
            pltpu.VMEM((2, seq, LANE), BF16),
            pltpu.VMEM((2, seq, LANE), BF16),
            pltpu.VMEM((2, seq, GDN_CHUNK), BF16),
            pltpu.VMEM((2, n_chunks * GDN_DK, GDN_DV), BF16),
            pltpu.VMEM((2, n_chunks * GDN_DK, GDN_DV), BF16),
            pltpu.VMEM((2, max(n_chunks, SUBLANE), LANE), F32),
            pltpu.VMEM((2, n_chunks * GDN_DK, GDN_DV), BF16),
        ],
        compiler_params=_cparams("parallel", "parallel"),
        name="gdn",
    )(p, p, p, p, conv_w8, conv_w8, conv_w8, gb, grow, norm_w)


def _diff_attn_kernel(slopes_ref, q_ref, k_ref, v_ref, lam_ref, nw_ref, o_ref,
                      qT_s, vT_s, klo_s, khi_s, rel_s, s_s, p_s, acc_s,
                      *, seq, tq, tk, lam_init):
    head = pl.program_id(1)
    slope2 = slopes_ref[head] * LOG2E
    scale2 = (DIFF_DH ** -0.5) * LOG2E
    n_kv = seq // tk
    maps = (klo_s, khi_s)

    rows = max(tq, tk)
    assert seq % rows == 0 and rows % tq == 0 and rows % tk == 0
    frow = lax.broadcasted_iota(jnp.int32, (LANE, rows), 0)
    fcol = lax.broadcasted_iota(jnp.int32, (LANE, rows), 1)
    qa, qb, qc = (x.astype(F32) for x in _split3((fcol % tq).astype(F32) * slope2))
    qfeat = jnp.where(frow < 3, 1.0, jnp.where(frow == 3, -qa, jnp.where(frow == 4, -qb,
                                               jnp.where(frow == 5, -qc, 0.0)))).astype(BF16)
    krow = lax.broadcasted_iota(jnp.int32, (rows, LANE), 0)
    lane = lax.broadcasted_iota(jnp.int32, (rows, LANE), 1)
    ka, kb, kc = (x.astype(F32) for x in _split3((krow % tk).astype(F32) * slope2))
    kfeat = jnp.where(lane == 0, ka, jnp.where(lane == 1, kb, jnp.where(lane == 2, kc,
                                               jnp.where(lane < 6, 1.0, 0.0)))).astype(BF16)
    for t in range(seq // rows):
        r0 = t * rows
        qt = (q_ref[r0:r0 + rows, :].astype(F32) * scale2).T.astype(BF16)
        for sgn, feat in enumerate((qfeat, -qfeat)):
            qT_s[sgn, 0:LANE, r0:r0 + rows] = qt
            qT_s[sgn, LANE:2 * LANE, r0:r0 + rows] = feat
        vT_s[0:LANE, r0:r0 + rows] = v_ref[r0:r0 + rows, :].astype(F32).T.astype(BF16)
        vT_s[DIFF_DV:DIFF_DV + PACKED_ROWS, r0:r0 + rows] = jnp.ones((PACKED_ROWS, rows), BF16)
        kt = k_ref[r0:r0 + rows, :]
        klo_s[r0:r0 + rows, 0:LANE] = jnp.where(lane < DIFF_DH, kt, jnp.zeros_like(kt))
        khi_s[r0:r0 + rows, 0:LANE] = jnp.where(lane >= DIFF_DH, kt, jnp.zeros_like(kt))
        klo_s[r0:r0 + rows, LANE:2 * LANE] = kfeat
        khi_s[r0:r0 + rows, LANE:2 * LANE] = kfeat
    kr = lax.broadcasted_iota(jnp.int32, (tk, tq), 0)
    qcol = lax.broadcasted_iota(jnp.int32, (tk, tq), 1)
    rel_s[...] = (qcol - kr).astype(F32) * slope2

    lf = lam_ref[...]
    lam = (jnp.exp(jnp.sum(lf[0:1, :] * lf[1:2, :], axis=-1, keepdims=True))
           - jnp.exp(jnp.sum(lf[2:3, :] * lf[3:4, :], axis=-1, keepdims=True)) + lam_init)

    def q_tile(i, carry):
        c_q = pl.multiple_of(i * tq, tq)
        j_diag = (i * tq) // tk

        def scores(t):
            j = j_diag if t == 0 else lax.rem(j_diag + t, n_kv)
            r_k = pl.multiple_of(j * tk, tk)
            d0 = jnp.asarray(i * tq - j * tk)
            if t == 0:
                bias = jnp.abs(rel_s[...] + d0.astype(F32) * slope2)
                for mp, k_s in enumerate(maps):
                    s_s[t % 2, mp] = jnp.dot(k_s[pl.ds(r_k, tk), 0:LANE], qT_s[0, 0:LANE, pl.ds(c_q, tq)],
                                             preferred_element_type=F32) - bias
                return r_k, 0.0
            sgn = jnp.asarray(j > j_diag).astype(jnp.int32)
            for mp, k_s in enumerate(maps):
                s_s[t % 2, mp] = jnp.dot(k_s[pl.ds(r_k, tk), :], qT_s[sgn, :, pl.ds(c_q, tq)],
                                         preferred_element_type=F32)
            return r_k, jnp.abs(d0).astype(F32) * slope2

        def weighted_values(t, r_k, alphas):
            vT = vT_s[:, pl.ds(r_k, tk)]
            for mp in range(len(maps)):
                upd = jnp.dot(vT, p_s[t % 2, mp], preferred_element_type=F32)
                acc_s[mp] = upd if t == 0 else acc_s[mp] * alphas[mp] + upd

        m = [jnp.full((1, tq), NEG_BIG, F32) for _ in maps]
        nxt = scores(0)
        prev = None
        for t in range(n_kv):
            r_k, const = nxt
            if t + 1 < n_kv:
                nxt = scores(t + 1)
            alphas = []
            for mp in range(len(maps)):
                m_new = jnp.maximum(m[mp], jnp.max(s_s[t % 2, mp], axis=0, keepdims=True) - const)
                alphas.append(jnp.exp2(m[mp] - m_new))
                p_s[t % 2, mp] = jnp.exp2(s_s[t % 2, mp] - (m_new + const)).astype(BF16)
                m[mp] = m_new
            if prev is not None:
                weighted_values(t - 1, *prev)
            prev = (r_k, alphas)
        weighted_values(n_kv - 1, *prev)

        o1, o2 = acc_s[0], acc_s[1]
        od = o1[0:DIFF_DV] / o1[DIFF_DV:DIFF_DV + 1] - lam * (o2[0:DIFF_DV] / o2[DIFF_DV:DIFF_DV + 1])
        yn = od * lax.rsqrt(jnp.mean(od * od, axis=0, keepdims=True) + EPS) * nw_ref[...] * (1.0 - lam_init)
        o_ref[pl.ds(c_q, tq), :] = yn.T.astype(o_ref.dtype)
        return carry

    lax.fori_loop(0, seq // tq, q_tile, 0)


def _diff_attn(p, slopes, diff_lambda, norm_w_col, batch, seq, n_heads, q_off, k_off, v_off, lam_init):
    T = batch * seq
    tq = min(512, seq)
    tk = min(512, seq)
    blk = lambda off: pl.BlockSpec((seq, LANE), lambda b, h: (b, off + h))
    return pl.pallas_call(
        functools.partial(_diff_attn_kernel, seq=seq, tq=tq, tk=tk, lam_init=lam_init),
        grid=(batch, n_heads),
        in_specs=[pl.BlockSpec(memory_space=pltpu.SMEM),
                  blk(q_off), blk(k_off), blk(v_off),
                  _const_spec((4, DIFF_DH)), _const_spec((DIFF_DV, 1))],
        out_specs=pl.BlockSpec((seq, LANE), lambda b, h: (b, h)),
        out_shape=jax.ShapeDtypeStruct((T, n_heads * DIFF_DV), BF16),
        scratch_shapes=[
            pltpu.VMEM((2, 2 * LANE, seq), BF16),
            pltpu.VMEM((DIFF_DV + PACKED_ROWS, seq), BF16),
            pltpu.VMEM((seq, 2 * LANE), BF16),
            pltpu.VMEM((seq, 2 * LANE), BF16),
            pltpu.VMEM((tk, tq), F32),
            pltpu.VMEM((2, 2, tk, tq), F32),
            pltpu.VMEM((2, 2, tk, tq), BF16),
            pltpu.VMEM((2, DIFF_DV + PACKED_ROWS, tq), F32),
        ],
        compiler_params=_cparams("parallel", "parallel"),
        name="diff_attn",
    )(slopes, p, p, p, diff_lambda, norm_w_col)


def _merge_kernel(ya_ref, yd_ref, ga_ref, gd_ref, x_ref, wa_ref, wd_ref, wo_ref, gffn_ref,
                  wr_hi_ref, wr_lo_ref, br_ref, x1_ref, h2_ref, route_ref, count_ref, *, n_groups, per_group):
    ma = jnp.dot(ya_ref[...], wa_ref[...], preferred_element_type=F32)
    md = jnp.dot(yd_ref[...], wd_ref[...], preferred_element_type=F32)
    merged = _sigmoid(ga_ref[...].astype(F32)) * ma + _sigmoid(gd_ref[...].astype(F32)) * md
    x1 = x_ref[...] + jnp.dot(merged.astype(BF16), wo_ref[...], preferred_element_type=F32)
    x1_ref[...] = x1
    h2 = x1 * lax.rsqrt(jnp.mean(x1 * x1, axis=-1, keepdims=True) + EPS) * gffn_ref[...]
    h2_ref[...] = _pack_pairs(h2)

    h_hi = h2.astype(BF16)
    h_lo = (h2 - h_hi.astype(F32)).astype(BF16)
    logits = (jnp.dot(h_hi, wr_hi_ref[...], preferred_element_type=F32)
              + jnp.dot(h_hi, wr_lo_ref[...], preferred_element_type=F32)
              + jnp.dot(h_lo, wr_hi_ref[...], preferred_element_type=F32)) + br_ref[...]
    lane = lax.broadcasted_iota(jnp.int32, logits.shape, 1)
    lane_f = lane.astype(F32)
    big = float(LANE)

    def first_argmax(vals, vmax):
        return jnp.min(jnp.where(vals == vmax, lane_f, big), axis=-1, keepdims=True)

    gl = jnp.where(lane < n_groups, logits, -jnp.inf)
    gmax = jnp.max(gl, axis=-1, keepdims=True)
    g_idx = first_argmax(gl, gmax)
    g_w = 1.0 / jnp.sum(jnp.exp(gl - gmax), axis=-1, keepdims=True)
    e_lo = n_groups + g_idx * per_group
    in_group = (lane_f >= e_lo) & (lane_f < e_lo + per_group)
    el = jnp.where(in_group, logits, -jnp.inf)
    emax = jnp.max(el, axis=-1, keepdims=True)
    pe = jnp.exp(el - emax)
    pe = pe / jnp.sum(pe, axis=-1, keepdims=True)
    p1 = jnp.max(pe, axis=-1, keepdims=True)
    i1 = first_argmax(jnp.where(in_group, pe, -1.0), p1)
    rest = jnp.where(in_group & (lane_f != i1), pe, -1.0)
    p2 = jnp.max(rest, axis=-1, keepdims=True)
    i2 = first_argmax(rest, p2)
    denom = p1 + p2
    w1 = p1 / denom * g_w
    w2 = p2 / denom * g_w
    @pl.when(pl.program_id(0) == 0)
    def _():
        count_ref[...] = jnp.zeros_like(count_ref)

    hit1 = lane_f == i1
    hit2 = lane_f == i2
    hits = jnp.where(hit1 | hit2, 1.0, 0.0)
    tm = hits.shape[0]
    earlier = (lax.broadcasted_iota(jnp.int32, (tm, tm), 1) < lax.broadcasted_iota(jnp.int32, (tm, tm), 0))
    before = count_ref[0:1, :] + jnp.dot(jnp.where(earlier, 1.0, 0.0).astype(BF16), hits.astype(BF16),
                                         preferred_element_type=F32)
    rank1 = jnp.sum(jnp.where(hit1, before, 0.0), axis=-1, keepdims=True)
    rank2 = jnp.sum(jnp.where(hit2, before, 0.0), axis=-1, keepdims=True)
    count_ref[...] = count_ref[...] + jnp.sum(hits, axis=0, keepdims=True)

    route = jnp.where(lane == 0, i1 - n_groups, 0.0)
    route = jnp.where(lane == 1, i2 - n_groups, route)
    route = jnp.where(lane == 2, w1, route)
    route = jnp.where(lane == 3, w2, route)
    route = jnp.where(lane == 4, rank1, route)
    route = jnp.where(lane == 5, rank2, route)
    route_ref[...] = route


def _merge(y_a, y_d, p, x2, wa, wd, wo, g_ffn, wr_hi, wr_lo, b_r, n_groups, per_group, tm):
    T, D = x2.shape
    va, vd = y_a.shape[1], y_d.shape[1]
    row = lambda w: pl.BlockSpec((tm, w), lambda i: (i, 0))
    return pl.pallas_call(
        functools.partial(_merge_kernel, n_groups=n_groups, per_group=per_group),
        grid=(T // tm,),
        in_specs=[row(va), row(vd),
                  pl.BlockSpec((tm, D), lambda i: (i, 0)),
                  pl.BlockSpec((tm, D), lambda i: (i, 1)),
                  row(D),
                  _const_spec((va, D)), _const_spec((vd, D)), _const_spec((D, D)), _const_spec((1, D)),
                  _const_spec((D, LANE)), _const_spec((D, LANE)), _const_spec((1, LANE))],
        out_specs=[row(D), row(D // 2), row(LANE), pl.BlockSpec((SUBLANE, LANE), lambda i: (0, 0))],
        out_shape=[jax.ShapeDtypeStruct((T, D), F32),
                   jax.ShapeDtypeStruct((T, D // 2), jnp.uint32),
                   jax.ShapeDtypeStruct((T, LANE), F32),
                   jax.ShapeDtypeStruct((SUBLANE, LANE), F32)],
        compiler_params=_cparams("arbitrary"),
        name="merge",
    )(y_a, y_d, p, p, x2, wa, wd, wo, g_ffn, wr_hi, wr_lo, b_r)


def _expert_kernel(tile_expert_ref, tile_flag_ref, tile_rows_ref, next_expert_ref, slot_ref,
                   xs_ref, wg_hbm, wu_hbm, wd_hbm, o_ref,
                   wg_f, wu_f, wd_f, wg_s, wu_s, wd_s, sems):
    i = pl.program_id(0)
    flag = tile_flag_ref[i]

    def weight_copies(expert, slot):
        return [pltpu.make_async_copy(hbm.at[expert], buf.at[slot], sems.at[slot, n])
                for n, (hbm, buf) in enumerate(((wg_hbm, wg_f), (wu_hbm, wu_f), (wd_hbm, wd_f)))]

    @pl.when(flag == 2)
    def _():
        slot = slot_ref[i]

        @pl.when(i == 0)
        def _():
            for copy in weight_copies(tile_expert_ref[i], slot):
                copy.start()

        for copy in weight_copies(tile_expert_ref[i], slot):
            copy.wait()
        nxt = next_expert_ref[i]

        @pl.when(nxt >= 0)
        def _():
            for copy in weight_copies(nxt, 1 - slot):
                copy.start()

        wg_s[...] = wg_f[slot].astype(BF16)
        wu_s[...] = wu_f[slot].astype(BF16)
        wd_s[...] = wd_f[slot].astype(BF16)

    @pl.when(flag > 0)
    def _():
        row = lax.broadcasted_iota(jnp.int32, xs_ref.shape, 0)
        x = _unpack_pairs(jnp.where(row < tile_rows_ref[i], xs_ref[...], jnp.uint32(0)))
        g = jnp.dot(x, wg_s[...], preferred_element_type=F32)
        u = jnp.dot(x, wu_s[...], preferred_element_type=F32)
        hid = g * _sigmoid(g) * u
        o_ref[...] = _pack_pairs(jnp.dot(hid.astype(BF16), wd_s[...], preferred_element_type=F32))

    @pl.when(flag == 0)
    def _():
        o_ref[...] = jnp.zeros_like(o_ref)


def _experts(tile_plan, xs, w_gate, w_up, w_down, tm):
    n_rows, half = xs.shape
    E, D, Fd = w_gate.shape
    assert D == 2 * half
    n_tiles = n_rows // tm
    hbm = pl.BlockSpec(memory_space=pl.ANY)
    grid_spec = pltpu.PrefetchScalarGridSpec(
        num_scalar_prefetch=len(tile_plan),
        grid=(n_tiles,),
        in_specs=[pl.BlockSpec((tm, half), lambda i, *_: (i, 0)), hbm, hbm, hbm],
        out_specs=pl.BlockSpec((tm, half), lambda i, *_: (i, 0)),
        scratch_shapes=[pltpu.VMEM((2, D, Fd), F32), pltpu.VMEM((2, D, Fd), F32), pltpu.VMEM((2, Fd, D), F32),
                        pltpu.VMEM((D, Fd), BF16), pltpu.VMEM((D, Fd), BF16), pltpu.VMEM((Fd, D), BF16),
                        pltpu.SemaphoreType.DMA((2, 3))],
    )
    return pl.pallas_call(
        _expert_kernel,
        grid_spec=grid_spec,
        out_shape=jax.ShapeDtypeStruct((n_rows, half), jnp.uint32),
        compiler_params=_cparams("arbitrary"),
        name="experts",
    )(*tile_plan, xs, w_gate, w_up, w_down)


SC_CORES = 2
SC_SUBCORES = 16
SC_ROWS = 32
SC_COLLECT_BUFFERS = 3


def _dispatch_rows(table, pos, n_rows):
    T, W = table.shape
    workers = SC_CORES * SC_SUBCORES
    chunks = T // (workers * SC_ROWS)
    assert workers * chunks * SC_ROWS == T
    pos4 = pos.T.reshape(TOP_K, workers, chunks, SC_ROWS).transpose(1, 0, 2, 3)
    mesh = plsc.VectorSubcoreMesh(core_axis_name="c", subcore_axis_name="s")

    @functools.partial(
        pl.kernel, mesh=mesh,
        out_type=jax.ShapeDtypeStruct((n_rows, W), table.dtype),
        scratch_types=[pltpu.VMEM((TOP_K, chunks, SC_ROWS), jnp.int32),
                       pltpu.VMEM((2, SC_ROWS, W), table.dtype),
                       pltpu.SemaphoreType.DMA, pltpu.SemaphoreType.DMA, pltpu.SemaphoreType.DMA],
        name="dispatch_rows",
    )
    def dispatch(table_hbm, pos_hbm, out_hbm, pos_v, rows_v, sem_in, sem_out0, sem_out1):
        wid = lax.axis_index("s") * SC_CORES + lax.axis_index("c")
        base = wid * (chunks * SC_ROWS)
        pltpu.sync_copy(pos_hbm.at[wid], pos_v)

        def load(c):
            return pltpu.async_copy(table_hbm.at[pl.ds(base + c * SC_ROWS, SC_ROWS)], rows_v.at[c % 2], sem_in)

        pending = load(0)
        for c in range(chunks):
            pending.wait()
            if c + 1 < chunks:
                pending = load(c + 1)
            out0 = pltpu.async_copy(rows_v.at[c % 2], out_hbm.at[pos_v.at[0, c]], sem_out0)
            out1 = pltpu.async_copy(rows_v.at[c % 2], out_hbm.at[pos_v.at[1, c]], sem_out1)
            out0.wait()
            out1.wait()

    return dispatch(table, pos4)


def _collect_rows(table, pos):
    T = pos.shape[0]
    W = table.shape[1]
    workers = SC_CORES * SC_SUBCORES
    chunks = T // (workers * SC_ROWS)
    assert workers * chunks * SC_ROWS == T
    pos4 = pos.T.reshape(TOP_K, workers, chunks, SC_ROWS).transpose(1, 0, 2, 3)
    mesh = plsc.VectorSubcoreMesh(core_axis_name="c", subcore_axis_name="s")
    steps = [(c, k) for c in range(chunks) for k in range(TOP_K)]

    @functools.partial(
        pl.kernel, mesh=mesh,
        out_type=jax.ShapeDtypeStruct((TOP_K, T, W), table.dtype),
        scratch_types=[pltpu.VMEM((TOP_K, chunks, SC_ROWS), jnp.int32),
                       pltpu.VMEM((SC_COLLECT_BUFFERS, SC_ROWS, W), table.dtype)]
                      + [pltpu.SemaphoreType.DMA] * SC_COLLECT_BUFFERS,
        name="collect_rows",
    )
    def collect(table_hbm, pos_hbm, out_hbm, pos_v, rows_v, *sems):
        wid = lax.axis_index("s") * SC_CORES + lax.axis_index("c")
        base = wid * (chunks * SC_ROWS)
        pltpu.sync_copy(pos_hbm.at[wid], pos_v)

        def gather(n):
            c, k = steps[n]
            buf = n % SC_COLLECT_BUFFERS
            return pltpu.async_copy(table_hbm.at[pos_v.at[k, c]], rows_v.at[buf], sems[buf])

        in_flight = [gather(n) for n in range(min(SC_COLLECT_BUFFERS - 1, len(steps)))]
        for n, (c, k) in enumerate(steps):
            in_flight.pop(0).wait()
            ahead = n + SC_COLLECT_BUFFERS - 1
            if ahead < len(steps):
                in_flight.append(gather(ahead))
            pltpu.sync_copy(rows_v.at[n % SC_COLLECT_BUFFERS],
                            out_hbm.at[k, pl.ds(base + c * SC_ROWS, SC_ROWS)])

    return collect(table, pos4)


def _final_kernel(x1_ref, y0_ref, y1_ref, route_ref, g_ref, o_ref):
    route = route_ref[...]
    lane = lax.broadcasted_iota(jnp.int32, route.shape, 1)
    w0 = jnp.sum(jnp.where(lane == TOP_K, route, 0.0), axis=-1, keepdims=True)
    w1 = jnp.sum(jnp.where(lane == TOP_K + 1, route, 0.0), axis=-1, keepdims=True)
    y0 = _unpack_pairs(y0_ref[...]).astype(F32)
    y1 = _unpack_pairs(y1_ref[...]).astype(F32)
    x = x1_ref[...] + (w0 * y0 + w1 * y1)
    o_ref[...] = x * lax.rsqrt(jnp.mean(x * x, axis=-1, keepdims=True) + EPS) * g_ref[...]


def _final(x1, ys_tok, route, g_final, tm):
    T, D = x1.shape
    row = pl.BlockSpec((tm, D), lambda i: (i, 0))
    y_spec = lambda k: pl.BlockSpec((None, tm, D // 2), lambda i: (k, i, 0))
    return pl.pallas_call(
        _final_kernel,
        grid=(T // tm,),
        in_specs=[row, y_spec(0), y_spec(1), pl.BlockSpec((tm, LANE), lambda i: (i, 0)), _const_spec((1, D))],
        out_specs=row,
        out_shape=jax.ShapeDtypeStruct((T, D), F32),
        compiler_params=_cparams("parallel"),
        name="final",
    )(x1, ys_tok, ys_tok, route, g_final)


def _route_tables(ids, rank, counts, tm):
    T = ids.shape[0]
    A = T * TOP_K
    n_experts = counts.shape[0]
    padded = ((counts + tm - 1) // tm) * tm
    ends = jnp.cumsum(padded)
    starts = ends - padded
    onehot = ids[:, :, None] == jnp.arange(n_experts, dtype=jnp.int32)[None, None, :]
    pos = jnp.sum(jnp.where(onehot, starts[None, None, :], 0), axis=-1) + rank
    n_tiles = A // tm + n_experts
    tile_start = jnp.arange(n_tiles, dtype=jnp.int32) * tm
    tile_expert_raw = jnp.sum((ends[None, :] <= tile_start[:, None]).astype(jnp.int32), axis=1)
    valid = tile_start < ends[-1]
    last_expert = jnp.max(jnp.where(counts > 0, jnp.arange(n_experts, dtype=jnp.int32), 0))
    tile_expert = jnp.where(valid, jnp.minimum(tile_expert_raw, n_experts - 1), last_expert)
    first = jnp.concatenate([jnp.ones((1,), bool), tile_expert[1:] != tile_expert[:-1]])
    tile_flag = jnp.where(valid, jnp.where(first, 2, 1), 0).astype(jnp.int32)
    tile_rows = jnp.clip((starts + counts)[tile_expert] - tile_start, 0, tm)
    tile_rows = jnp.where(valid, tile_rows, 0).astype(jnp.int32)
    e_idx = jnp.arange(n_experts, dtype=jnp.int32)
    nonempty = counts > 0
    later = (e_idx[None, :] > e_idx[:, None]) & nonempty[None, :]
    next_nonempty = jnp.min(jnp.where(later, e_idx[None, :], n_experts), axis=1)
    next_nonempty = jnp.where(next_nonempty < n_experts, next_nonempty, -1).astype(jnp.int32)
    order = jnp.sum(((e_idx[None, :] < e_idx[:, None]) & nonempty[None, :]).astype(jnp.int32), axis=1)
    next_expert = next_nonempty[tile_expert]
    slot = (order[tile_expert] % 2).astype(jnp.int32)
    return pos, tile_expert, tile_flag, tile_rows, next_expert, slot


def _largest_tile(n, cap):
    t = min(n, cap)
    while n % t:
        t //= 2
    return t


def kernel(x, g_mix, w_in, conv_w, a_log, dt_bias, gdn_norm_w, diff_lambda, diff_norm_w, w_branch_a, w_branch_d,
           w_out, g_ffn, w_group, b_group, w_router, b_router, w_exp_gate, w_exp_up, w_exp_down, g_final):
    B, S, D = x.shape
    T = B * S
    depth = g_mix.shape[0]
    Hg = a_log.shape[-1]
    gdn_qk = Hg * GDN_DK
    gdn_v = Hg * GDN_DV
    diff_v = w_branch_d.shape[1]
    Hd = diff_v // DIFF_DV
    diff_qk = Hd * 2 * DIFF_DH
    n_groups = w_group.shape[-1]
    n_experts = w_router.shape[-1]
    per_group = n_experts // n_groups
    assert 4 * Hg <= LANE and n_groups + n_experts <= LANE
    assert S % GDN_CHUNK == 0 and D % LANE == 0

    ab_lo = 2 * gdn_qk + 2 * gdn_v
    ab_hi = ab_lo + 4 * Hg
    gates_lo = ab_hi + 2 * diff_qk + diff_v
    main_cols = 2 * D + ab_lo + 2 * diff_qk + diff_v
    gdn_off = 2 * D
    q_d_off = gdn_off + ab_lo
    k_d_off = q_d_off + diff_qk
    v_d_off = k_d_off + diff_qk

    tm_norm = _largest_tile(T, 256)
    tm_proj = _largest_tile(T, 2048)
    tn_proj = _largest_tile(main_cols, 1024)
    tm_merge = _largest_tile(T, 512)
    tm_exp = _largest_tile(T * TOP_K, 256)
    tm_final = _largest_tile(T, 512)

    slopes = jnp.exp2(-8.0 * (jnp.arange(Hd, dtype=F32) + 1.0) / Hd)
    x2 = x.reshape(T, D)
    for layer in range(depth):
        lam_init = 0.8 - 0.6 * math.exp(-0.3 * layer)
        w_in_l = w_in[layer]
        w_main, w_ab = _reorder_cast(w_in_l.T, ((gates_lo, 2 * D), (0, ab_lo), (ab_hi, gates_lo - ab_hi)),
                                     (ab_lo, 4 * Hg), _largest_tile(D, 256))
        alog_pad = jnp.pad(a_log[layer].reshape(1, 2 * Hg), ((0, 0), (0, LANE - 2 * Hg)))
        dtb_pad = jnp.pad(dt_bias[layer].reshape(1, 2 * Hg), ((0, 0), (0, LANE - 2 * Hg)))

        h, gb = _norm_proj(x2, g_mix[layer].reshape(1, D), w_ab, alog_pad, dtb_pad, 2 * Hg, tm_norm)
        p = _matmul_nt(h, w_main, tm_proj, tn_proj, BF16)

        grow = gb[:, :2 * Hg].reshape(B, S // GDN_CHUNK, GDN_CHUNK, 2 * Hg).transpose(0, 3, 1, 2)
        conv_w8 = jnp.pad(conv_w[layer], ((0, SUBLANE - GDN_CONV), (0, 0)))
        y_a = _gdn(p, gdn_off // LANE, conv_w8, gb, grow, gdn_norm_w[layer].reshape(1, GDN_DV), B, S, Hg)
        y_d = _diff_attn(p, slopes, diff_lambda[layer], diff_norm_w[layer].reshape(DIFF_DV, 1), B, S, Hd,
                         q_d_off // LANE, k_d_off // LANE, v_d_off // LANE, lam_init)

        w_r = jnp.pad(jnp.concatenate([w_group[layer], w_router[layer]], axis=1),
                      ((0, 0), (0, LANE - n_groups - n_experts)))
        wr_hi = w_r.astype(BF16)
        wr_lo = (w_r - wr_hi.astype(F32)).astype(BF16)
        b_r = jnp.pad(jnp.concatenate([b_group[layer], b_router[layer]]).reshape(1, -1),
                      ((0, 0), (0, LANE - n_groups - n_experts)))
        x1, h2, route, counts = _merge(y_a, y_d, p, x2,
                                       w_branch_a[layer].astype(BF16), w_branch_d[layer].astype(BF16),
                                       w_out[layer].astype(BF16), g_ffn[layer].reshape(1, D),
                                       wr_hi, wr_lo, b_r, n_groups, per_group, tm_merge)

        ids = route[:, 0:TOP_K].astype(jnp.int32)
        rank = route[:, 2 * TOP_K:3 * TOP_K].astype(jnp.int32)
        counts = counts[0, n_groups:n_groups + n_experts].astype(jnp.int32)
        pos, *tile_plan = _route_tables(ids, rank, counts, tm_exp)
        xs = _dispatch_rows(h2, pos, tile_plan[0].shape[0] * tm_exp)
        ys = _experts(tile_plan, xs, w_exp_gate[layer], w_exp_up[layer], w_exp_down[layer], tm_exp)
        ys_tok = _collect_rows(ys, pos)
        if layer + 1 < depth:
            wts = route[:, TOP_K:2 * TOP_K]
            x2 = x1 + (wts[:, 0:1] * _unpack_pairs(ys_tok[0]).astype(F32)
                       + wts[:, 1:2] * _unpack_pairs(ys_tok[1]).astype(F32))
    out = _final(x1, ys_tok, route, g_final.reshape(1, D), tm_final)
    return out.reshape(B, S, D)
```
